```python
import math
import jax, jax.numpy as jnp
from jax import lax
import numpy as np

D_MODEL = 1024
BATCH = 4
SEQ = 4096
DEPTH = 4
DEC_BATCH = 128
DEC_SEQ = 1
PAST_LEN = 8192
PAGE_SIZE = 128

N_MIXERS = 3
HEAD_DIM = 64
N_HEADS = D_MODEL // HEAD_DIM
N_KV_HEADS = 4
Q_PER_KV = N_HEADS // N_KV_HEADS
ATTN_DIM = N_HEADS * HEAD_DIM
QKV_DIM = (N_HEADS + 2 * N_KV_HEADS) * HEAD_DIM
WINDOW = 128
ATTN_BLOCK = WINDOW
ATTN_SCALE = HEAD_DIM ** -0.5
N_BUCKETS = 32
MAX_DISTANCE = 128
CHUNK = 128
SGU_WIDTH = D_MODEL
SGU_GROUPS = 8
SGU_GROUP_DIM = SGU_WIDTH // SGU_GROUPS
CONV_WIDTH = 31
D_FF = 4 * D_MODEL
EPS = 1e-6
NEG_INF = -1e30

MIXER_OF_LAYER = tuple(i % N_MIXERS for i in range(DEPTH))
SLOT_OF_LAYER = tuple(MIXER_OF_LAYER[:i].count(MIXER_OF_LAYER[i]) for i in range(DEPTH))
N_ATTN_LAYERS = MIXER_OF_LAYER.count(0)
N_SGU_LAYERS = MIXER_OF_LAYER.count(1)
N_CONV_LAYERS = MIXER_OF_LAYER.count(2)

kernel_name = 'hybrid_swa_sgu_conformer_decode_step'


def rms_norm(x, g):
    xf = x.astype(jnp.float32)
    y = xf * lax.rsqrt(jnp.mean(xf * xf, axis=-1, keepdims=True) + EPS)
    return (y * g.astype(jnp.float32)).astype(x.dtype)


def layer_norm(x, g, b):
    xf = x.astype(jnp.float32)
    mu = jnp.mean(xf, axis=-1, keepdims=True)
    var = jnp.mean(jnp.square(xf - mu), axis=-1, keepdims=True)
    y = (xf - mu) * lax.rsqrt(var + EPS)
    return (y * g.astype(jnp.float32) + b.astype(jnp.float32)).astype(x.dtype)


def t5_bucket(dist):
    n = jnp.maximum(dist, 0)
    max_exact = N_BUCKETS // 2
    nf = jnp.maximum(n, 1).astype(jnp.float32)
    large = max_exact + (jnp.log(nf / max_exact) / math.log(MAX_DISTANCE / max_exact)
                         * (N_BUCKETS - max_exact)).astype(jnp.int32)
    large = jnp.minimum(large, N_BUCKETS - 1)
    return jnp.where(n < max_exact, n, large)


def band_attention(q, k, v, sinks, rel_bias, key_valid):
    lq, lk = q.shape[-4], k.shape[-3]
    dist = jnp.arange(lq, dtype=jnp.int32)[:, None] - jnp.arange(lk, dtype=jnp.int32)[None, :] + WINDOW
    allowed = (dist >= 0) & (dist <= WINDOW)
    if key_valid is not None:
        allowed = allowed & key_valid
    bias = rel_bias[t5_bucket(dist)].astype(jnp.float32)
    bias = bias.reshape(lq, lk, N_KV_HEADS, Q_PER_KV).transpose(2, 3, 0, 1)
    s = jnp.einsum('...qhgd,...khd->...hgqk', q, k).astype(jnp.float32) * ATTN_SCALE + bias
    s = jnp.where(allowed, s, NEG_INF)
    sink = jnp.broadcast_to(sinks.astype(jnp.float32).reshape(N_KV_HEADS, Q_PER_KV, 1, 1),
                            s.shape[:-1] + (1,))
    p = jax.nn.softmax(jnp.concatenate([s, sink], axis=-1), axis=-1)[..., :lk]
    return jnp.einsum('...hgqk,...khd->...qhgd', p.astype(v.dtype), v)


def swa_project(h, w_qkv):
    lead = h.shape[:-1]
    qkv = h @ w_qkv
    q, k, v = jnp.split(qkv, [ATTN_DIM, ATTN_DIM + N_KV_HEADS * HEAD_DIM], axis=-1)
    q = q.reshape(*lead, N_KV_HEADS, Q_PER_KV, HEAD_DIM)
    k = k.reshape(*lead, N_KV_HEADS, HEAD_DIM)
    v = v.reshape(*lead, N_KV_HEADS, HEAD_DIM)
    return q, k, v


def swa_prompt(h, w_qkv, w_o, sinks, rel_bias):
    b, s, _ = h.shape
    nb = s // ATTN_BLOCK
    q, k, v = swa_project(h, w_qkv)
    qb = q.reshape(b, nb, ATTN_BLOCK, N_KV_HEADS, Q_PER_KV, HEAD_DIM)

    def band(x):
        xb = x.reshape(b, nb, ATTN_BLOCK, N_KV_HEADS, HEAD_DIM)
        prev = jnp.pad(xb[:, :-1], ((0, 0), (1, 0), (0, 0), (0, 0), (0, 0)))
        return jnp.concatenate([prev, xb], axis=2)

    key_pos = (jnp.arange(nb, dtype=jnp.int32)[:, None] * ATTN_BLOCK - WINDOW
               + jnp.arange(WINDOW + ATTN_BLOCK, dtype=jnp.int32)[None, :])
    key_valid = (key_pos >= 0)[:, None, None, None, :]
    o = band_attention(qb, band(k), band(v), sinks, rel_bias, key_valid)
    y = o.reshape(b, s, ATTN_DIM) @ w_o
    return y, k[:, -WINDOW:], v[:, -WINDOW:]


def swa_sample(h, cache_k, cache_v, w_qkv, w_o, sinks, rel_bias):
    db, t, _ = h.shape
    q, k, v = swa_project(h, w_qkv)
    k_all = jnp.concatenate([cache_k.astype(k.dtype), k], axis=1)
    v_all = jnp.concatenate([cache_v.astype(v.dtype), v], axis=1)
    o = band_attention(q, k_all, v_all, sinks, rel_bias, None)
    y = o.reshape(db, t, ATTN_DIM) @ w_o
    return y, k_all[:, -WINDOW:], v_all[:, -WINDOW:]


def sgu_gates(h, w_in, ln_g, ln_b):
    z = jax.nn.gelu(h @ w_in, approximate=False)
    u, v = jnp.split(z, 2, axis=-1)
    return u, layer_norm(v, ln_g, ln_b)


def sgu_prompt(h, w_in, ln_g, ln_b, w_sp, b_sp, w_out):
    b, s, _ = h.shape
    nc = s // CHUNK
    u, v = sgu_gates(h, w_in, ln_g, ln_b)
    w_causal = jnp.tril(w_sp)
    vc = v.reshape(b, nc, CHUNK, SGU_GROUPS, SGU_GROUP_DIM)
    mixed = jnp.einsum('gts,bnsgc->bntgc', w_causal, vc) + b_sp.T[:, :, None]
    return (u * mixed.reshape(b, s, SGU_WIDTH)) @ w_out


def sgu_sample(h, w_in, ln_g, ln_b, w_sp, b_sp, w_out):
    db, t, _ = h.shape
    u, v = sgu_gates(h, w_in, ln_g, ln_b)
    w_causal = jnp.tril(w_sp[:, :t, :t])
    vg = v.reshape(db, t, SGU_GROUPS, SGU_GROUP_DIM)
    mixed = jnp.einsum('gts,bsgc->btgc', w_causal, vg) + b_sp[:, :t].T[:, :, None]
    return (u * mixed.reshape(db, t, SGU_WIDTH)) @ w_out, v


def conv_glu(h, w_in):
    a, gate = jnp.split(h @ w_in, 2, axis=-1)
    return a * jax.nn.sigmoid(gate)


def depthwise_causal(buf, w_dw, b_dw):
    out = lax.conv_general_dilated(buf, w_dw[:, None, :].astype(buf.dtype), window_strides=(1,),
                                   padding='VALID', dimension_numbers=('NWC', 'WIO', 'NWC'),
                                   feature_group_count=D_MODEL)
    return out + b_dw


def conv_tail(c, ln_g, ln_b, w_out):
    return jax.nn.silu(layer_norm(c, ln_g, ln_b)) @ w_out


def conv_prompt(h, w_in, w_dw, b_dw, ln_g, ln_b, w_out):
    a = conv_glu(h, w_in)
    buf = jnp.pad(a, ((0, 0), (CONV_WIDTH - 1, 0), (0, 0)))
    y = conv_tail(depthwise_causal(buf, w_dw, b_dw), ln_g, ln_b, w_out)
    return y, a[:, -(CONV_WIDTH - 1):]


def conv_sample(h, state, w_in, w_dw, b_dw, ln_g, ln_b, w_out):
    a = conv_glu(h, w_in)
    buf = jnp.concatenate([state.astype(a.dtype), a], axis=1)
    y = conv_tail(depthwise_causal(buf, w_dw, b_dw), ln_g, ln_b, w_out)
    return y, buf[:, -(CONV_WIDTH - 1):]


def sqrelu_ffn(h, w_up, w_down):
    return jnp.square(jax.nn.relu(h @ w_up)) @ w_down


def setup_inputs(seed: int = 0) -> dict:
    key = jax.random.key(seed)
    ks = iter(jax.random.split(key, 32))

    def nrm(shape, scale):
        return jax.random.normal(next(ks), shape, jnp.float32) * scale

    def gain(shape):
        return 1.0 + nrm(shape, 0.02)

    return {
        'x_prompt': nrm((BATCH, SEQ, D_MODEL), 1.0),
        'x_sample': nrm((DEC_BATCH, DEC_SEQ, D_MODEL), 1.0),
        'cache_swa_k': nrm((N_ATTN_LAYERS, DEC_BATCH, WINDOW, N_KV_HEADS, HEAD_DIM), 1.0),
        'cache_swa_v': nrm((N_ATTN_LAYERS, DEC_BATCH, WINDOW, N_KV_HEADS, HEAD_DIM), 1.0),
        'state_conv': nrm((N_CONV_LAYERS, DEC_BATCH, CONV_WIDTH - 1, D_MODEL), 0.5),
        'rel_bias': nrm((N_BUCKETS, N_HEADS), 0.5),
        'norm_mix': gain((DEPTH, D_MODEL)),
        'norm_ffn': gain((DEPTH, D_MODEL)),
        'norm_final': gain((D_MODEL,)),
        'attn_w_qkv': nrm((N_ATTN_LAYERS, D_MODEL, QKV_DIM), D_MODEL ** -0.5),
        'attn_w_o': nrm((N_ATTN_LAYERS, ATTN_DIM, D_MODEL), ATTN_DIM ** -0.5),
        'attn_sinks': nrm((N_ATTN_LAYERS, N_HEADS), 0.5),
        'sgu_w_in': nrm((N_SGU_LAYERS, D_MODEL, 2 * SGU_WIDTH), D_MODEL ** -0.5),
        'sgu_ln_g': gain((N_SGU_LAYERS, SGU_WIDTH)),
        'sgu_ln_b': nrm((N_SGU_LAYERS, SGU_WIDTH), 0.02),
        'sgu_w_spatial': nrm((N_SGU_LAYERS, SGU_GROUPS, CHUNK, CHUNK), CHUNK ** -0.5),
        'sgu_b_spatial': gain((N_SGU_LAYERS, SGU_GROUPS, CHUNK)),
        'sgu_w_out': nrm((N_SGU_LAYERS, SGU_WIDTH, D_MODEL), SGU_WIDTH ** -0.5),
        'conv_w_in': nrm((N_CONV_LAYERS, D_MODEL, 2 * D_MODEL), D_MODEL ** -0.5),
        'conv_w_dw': nrm((N_CONV_LAYERS, CONV_WIDTH, D_MODEL), CONV_WIDTH ** -0.5),
        'conv_b_dw': nrm((N_CONV_LAYERS, D_MODEL), 0.02),
        'conv_ln_g': gain((N_CONV_LAYERS, D_MODEL)),
        'conv_ln_b': nrm((N_CONV_LAYERS, D_MODEL), 0.02),
        'conv_w_out': nrm((N_CONV_LAYERS, D_MODEL, D_MODEL), D_MODEL ** -0.5),
        'ffn_w_up': nrm((DEPTH, D_MODEL, D_FF), D_MODEL ** -0.5),
        'ffn_w_down': nrm((DEPTH, D_FF, D_MODEL), D_FF ** -0.5),
    }


def reference(x_prompt, x_sample, cache_swa_k, cache_swa_v, state_conv, rel_bias,
              norm_mix, norm_ffn, norm_final, attn_w_qkv, attn_w_o, attn_sinks,
              sgu_w_in, sgu_ln_g, sgu_ln_b, sgu_w_spatial, sgu_b_spatial, sgu_w_out,
              conv_w_in, conv_w_dw, conv_b_dw, conv_ln_g, conv_ln_b, conv_w_out,
              ffn_w_up, ffn_w_down):
    xp, xs = x_prompt, x_sample
    kp, vp, ksm, vsm, sgu_v_new, convp, convs = [], [], [], [], [], [], []
    for i in range(DEPTH):
        m, j = MIXER_OF_LAYER[i], SLOT_OF_LAYER[i]
        hp = rms_norm(xp, norm_mix[i])
        hs = rms_norm(xs, norm_mix[i])
        if m == 0:
            op, k1, v1 = swa_prompt(hp, attn_w_qkv[j], attn_w_o[j], attn_sinks[j], rel_bias)
            osm, k2, v2 = swa_sample(hs, cache_swa_k[j], cache_swa_v[j], attn_w_qkv[j],
                                     attn_w_o[j], attn_sinks[j], rel_bias)
            kp.append(k1); vp.append(v1); ksm.append(k2); vsm.append(v2)
        elif m == 1:
            op = sgu_prompt(hp, sgu_w_in[j], sgu_ln_g[j], sgu_ln_b[j], sgu_w_spatial[j],
                            sgu_b_spatial[j], sgu_w_out[j])
            osm, v_rows = sgu_sample(hs, sgu_w_in[j], sgu_ln_g[j], sgu_ln_b[j], sgu_w_spatial[j],
                                     sgu_b_spatial[j], sgu_w_out[j])
            sgu_v_new.append(v_rows)
        else:
            op, c1 = conv_prompt(hp, conv_w_in[j], conv_w_dw[j], conv_b_dw[j], conv_ln_g[j],
                                 conv_ln_b[j], conv_w_out[j])
            osm, c2 = conv_sample(hs, state_conv[j], conv_w_in[j], conv_w_dw[j], conv_b_dw[j],
                                  conv_ln_g[j], conv_ln_b[j], conv_w_out[j])
            convp.append(c1); convs.append(c2)
        xp = xp + op
        xs = xs + osm
        xp = xp + sqrelu_ffn(rms_norm(xp, norm_ffn[i]), ffn_w_up[i], ffn_w_down[i])
        xs = xs + sqrelu_ffn(rms_norm(xs, norm_ffn[i]), ffn_w_up[i], ffn_w_down[i])
    y_prompt = rms_norm(xp, norm_final)
    y_sample = rms_norm(xs, norm_final)
    return (y_prompt, y_sample, jnp.stack(kp), jnp.stack(vp), jnp.stack(ksm), jnp.stack(vsm),
            jnp.stack(sgu_v_new), jnp.stack(convp), jnp.stack(convs))
```

```python
import functools
import math

import jax
import jax.numpy as jnp
from jax import lax
from jax.experimental import pallas as pl
from jax.experimental.pallas import tpu as pltpu

D_MODEL = 1024
HEAD_DIM = 64
N_HEADS = 16
N_KV_HEADS = 4
Q_PER_KV = 4
KV_DIM = N_KV_HEADS * HEAD_DIM
QKV_DIM = D_MODEL + 2 * KV_DIM
WINDOW = 128
ATTN_SCALE = HEAD_DIM ** -0.5
N_BUCKETS = 32
MAX_DISTANCE = 128
CHUNK = 128
SGU_GROUPS = 8
CONV_WIDTH = 31
CONV_HALO = 32
D_FF = 4 * D_MODEL
EPS = 1e-6
NEG_INF = -1e30
F32_MAX = float(jnp.finfo(jnp.float32).max)
INV_SQRT2 = 1.0 / math.sqrt(2.0)

LANES = 128
SUBLANES = 8
VMEM_LIMIT = 56 * 1024 * 1024

F32 = jnp.float32
BF16 = jnp.bfloat16


def _const_spec(shape):
    n = len(shape)
    return pl.BlockSpec(shape, lambda *_: (0,) * n, pipeline_mode=pl.Buffered(1))


def _params(*sem):
    return pltpu.CompilerParams(dimension_semantics=sem, vmem_limit_bytes=VMEM_LIMIT)


def _rms(x, g):
    return x * lax.rsqrt(jnp.mean(x * x, axis=-1, keepdims=True) + EPS) * g


def _layer_norm(x, g, b):
    mu = jnp.mean(x, axis=-1, keepdims=True)
    xc = x - mu
    var = jnp.mean(xc * xc, axis=-1, keepdims=True)
    return xc * lax.rsqrt(var + EPS) * g + b


def _dot(a, b):
    return jnp.dot(a, b, preferred_element_type=F32)


def _dot_nt(a, b):
    return lax.dot_general(a, b, (((1,), (1,)), ((), ())), preferred_element_type=F32)


FFN_CHUNK = 1024


def _ffn_kernel(x_ref, g_ref, wu_ref, wd_ref, *rest, final):
    x = x_ref[...]
    h = _rms(x, g_ref[...]).astype(BF16)
    y = x
    for c in range(D_FF // FFN_CHUNK):
        cols = slice(c * FFN_CHUNK, (c + 1) * FFN_CHUNK)
        u = _dot(h, wu_ref[:, cols])
        u = jnp.square(jnp.maximum(u, 0.0)).astype(BF16)
        y = y + _dot(u, wd_ref[cols, :])
    if final:
        gf_ref, o_ref = rest
        o_ref[...] = _rms(y, gf_ref[...])
    else:
        (o_ref,) = rest
        o_ref[...] = y


def _ffn(x, g, w_up, w_down, g_final=None, *, tm):
    n = x.shape[0]
    row = pl.BlockSpec((tm, D_MODEL), lambda i: (i, 0))
    in_specs = [row, _const_spec((1, D_MODEL)), _const_spec((D_MODEL, D_FF)),
                _const_spec((D_FF, D_MODEL))]
    args = [x, g, w_up, w_down]
    if g_final is not None:
        in_specs.append(_const_spec((1, D_MODEL)))
        args.append(g_final)
    return pl.pallas_call(
        functools.partial(_ffn_kernel, final=g_final is not None),
        out_shape=jax.ShapeDtypeStruct((n, D_MODEL), F32),
        grid=(n // tm,),
        in_specs=in_specs,
        out_specs=row,
        compiler_params=_params("parallel"),
        name="ffn_final" if g_final is not None else "ffn",
    )(*args)


def _norm_proj_kernel(x_ref, g_ref, w_ref, o_ref, *, glu):
    h = _rms(x_ref[...], g_ref[...]).astype(BF16)
    y = _dot(h, w_ref[...])
    if glu:
        half = y.shape[1] // 2
        y = y[:, :half] * jax.nn.sigmoid(y[:, half:])
    o_ref[...] = y


def _norm_proj(x, g, w, *, glu=False):
    n = x.shape[0]
    n_out = w.shape[1] // 2 if glu else w.shape[1]
    return pl.pallas_call(
        functools.partial(_norm_proj_kernel, glu=glu),
        out_shape=jax.ShapeDtypeStruct((n, n_out), F32),
        grid=(1,),
        in_specs=[_const_spec(x.shape), _const_spec(g.shape), _const_spec(w.shape)],
        out_specs=_const_spec((n, n_out)),
        compiler_params=_params("arbitrary"),
        name="norm_proj_glu" if glu else "norm_proj",
    )(x, g, w)


def _proj_res_kernel(x_ref, a_ref, w_ref, *rest, conv_tail):
    a = a_ref[...]
    if conv_tail:
        lg_ref, lb_ref, o_ref = rest
        a = _layer_norm(a, lg_ref[...], lb_ref[...])
        a = a * jax.nn.sigmoid(a)
    else:
        (o_ref,) = rest
    o_ref[...] = x_ref[...] + _dot(a.astype(BF16), w_ref[...])


def _proj_res(x, a, w, ln=None):
    args = [x, a, w] + (list(ln) if ln is not None else [])
    return pl.pallas_call(
        functools.partial(_proj_res_kernel, conv_tail=ln is not None),
        out_shape=jax.ShapeDtypeStruct(x.shape, F32),
        grid=(1,),
        in_specs=[_const_spec(t.shape) for t in args],
        out_specs=_const_spec(x.shape),
        compiler_params=_params("arbitrary"),
        name="proj_res_ln" if ln is not None else "proj_res",
    )(*args)


def _t5_bucket(dist):
    n = jnp.maximum(dist, 0)
    max_exact = N_BUCKETS // 2
    nf = jnp.maximum(n, 1).astype(F32)
    large = max_exact + (jnp.log(nf / max_exact) / math.log(MAX_DISTANCE / max_exact)
                         * (N_BUCKETS - max_exact)).astype(jnp.int32)
    large = jnp.minimum(large, N_BUCKETS - 1)
    return jnp.where(n < max_exact, n, large)


def _prompt_bias_tables(rel_bias):
    qi = jnp.arange(WINDOW, dtype=jnp.int32)[:, None]
    kj = jnp.arange(2 * WINDOW, dtype=jnp.int32)[None, :]
    dist = qi - kj + WINDOW
    per_head = rel_bias[_t5_bucket(dist)].astype(F32).transpose(2, 0, 1)
    per_head = per_head.reshape(N_KV_HEADS, 2, 2, WINDOW, 2 * WINDOW)
    bias = per_head.transpose(0, 2, 1, 3, 4).reshape(N_KV_HEADS, 2, 2 * WINDOW, 2 * WINDOW)
    allowed = (dist >= 0) & (dist <= WINDOW)
    first = allowed & (kj >= WINDOW)
    cap = jnp.stack([jnp.where(allowed, F32_MAX, NEG_INF), jnp.where(first, F32_MAX, NEG_INF)])
    cap = jnp.concatenate([cap, cap], axis=1).astype(F32)
    return bias, cap


def _softmax_with_sink(s, sink):
    m = jnp.maximum(jnp.max(s, axis=-1, keepdims=True), sink)
    p = jnp.exp(s - m)
    denom = jnp.sum(p, axis=-1, keepdims=True) + jnp.exp(sink - m)
    return p, 1.0 / denom


def _attn_prompt_kernel(sink_ref, x_ref, g_ref, wqkv_ref, wo_ref, bias_ref, cap_ref,
                        o_ref, kc_ref, vc_ref,
                        q_scr, o_scr, klo_scr, khi_scr, vlo_scr, vhi_scr, *, tq):
    i = pl.program_id(1)
    n_i = pl.num_programs(1)
    x = x_ref[...]
    h = _rms(x, g_ref[...]).astype(BF16)
    qkv = _dot(h, wqkv_ref[...])
    q_scr[...] = (qkv[:, :D_MODEL] * ATTN_SCALE).astype(BF16)
    k = qkv[:, D_MODEL:D_MODEL + KV_DIM]
    v = qkv[:, D_MODEL + KV_DIM:]

    @pl.when(i == n_i - 1)
    def _():
        kc_ref[...] = k[tq - WINDOW:, :]
        vc_ref[...] = v[tq - WINDOW:, :]

    @pl.when(i == 0)
    def _():
        for scr in (klo_scr, khi_scr, vlo_scr, vhi_scr):
            scr[:, 0:WINDOW, :] = jnp.zeros((N_KV_HEADS, WINDOW, LANES), BF16)

    @pl.when(i > 0)
    def _():
        for scr in (klo_scr, khi_scr, vlo_scr, vhi_scr):
            scr[:, 0:WINDOW, :] = scr[:, tq:tq + WINDOW, :]

    low = lax.broadcasted_iota(jnp.int32, (tq, LANES), 1) < HEAD_DIM
    for t, lo_scr, hi_scr in ((k, klo_scr, khi_scr), (v, vlo_scr, vhi_scr)):
        for c in range(KV_DIM // LANES):
            tc = t[:, c * LANES:(c + 1) * LANES]
            tr = pltpu.roll(tc, HEAD_DIM, axis=1)
            lo_scr[2 * c, WINDOW:, :] = jnp.where(low, tc, 0.0).astype(BF16)
            hi_scr[2 * c, WINDOW:, :] = jnp.where(low, 0.0, tr).astype(BF16)
            lo_scr[2 * c + 1, WINDOW:, :] = jnp.where(low, tr, 0.0).astype(BF16)
            hi_scr[2 * c + 1, WINDOW:, :] = jnp.where(low, 0.0, tc).astype(BF16)

    low_o = lax.broadcasted_iota(jnp.int32, (2 * WINDOW, LANES), 1) < HEAD_DIM

    def block(jb, carry):
        r0 = pl.multiple_of(jb * WINDOW, WINDOW)
        first = jnp.where(jnp.logical_and(i == 0, jb == 0), 1, 0)
        cap = cap_ref[first]
        for kh in range(N_KV_HEADS):
            qst = jnp.concatenate(
                [q_scr[pl.ds(r0, WINDOW), (2 * kh) * LANES:(2 * kh + 1) * LANES],
                 q_scr[pl.ds(r0, WINDOW), (2 * kh + 1) * LANES:(2 * kh + 2) * LANES]], axis=0)
            probs, recips = [], []
            for half, k_scr in ((0, klo_scr), (1, khi_scr)):
                s = _dot_nt(qst, k_scr[kh, pl.ds(r0, 2 * WINDOW), :])
                s = jnp.minimum(s + bias_ref[kh, half], cap)
                p0, r0_ = _softmax_with_sink(s[:WINDOW], sink_ref[4 * kh + half])
                p1, r1_ = _softmax_with_sink(s[WINDOW:], sink_ref[4 * kh + 2 + half])
                probs.append(jnp.concatenate([p0, p1], axis=0).astype(BF16))
                recips.append(jnp.concatenate([r0_, r1_], axis=0))
            o = (_dot(probs[0], vlo_scr[kh, pl.ds(r0, 2 * WINDOW), :])
                 + _dot(probs[1], vhi_scr[kh, pl.ds(r0, 2 * WINDOW), :]))
            o = (o * jnp.where(low_o, recips[0], recips[1])).astype(BF16)
            o_scr[pl.ds(r0, WINDOW), (2 * kh) * LANES:(2 * kh + 1) * LANES] = o[:WINDOW]
            o_scr[pl.ds(r0, WINDOW), (2 * kh + 1) * LANES:(2 * kh + 2) * LANES] = o[WINDOW:]
        return carry

    lax.fori_loop(0, tq // WINDOW, block, 0)
    o_ref[...] = x + _dot(o_scr[...], wo_ref[...])


def _attn_prompt(x, g, w_qkv, w_o, sinks, bias, cap, *, tq):
    b, s, _ = x.shape
    row = pl.BlockSpec((None, tq, D_MODEL), lambda bi, i: (bi, i, 0))
    cache = pl.BlockSpec((None, WINDOW, KV_DIM), lambda bi, i: (bi, 0, 0))
    kv_scr = pltpu.VMEM((N_KV_HEADS, WINDOW + tq, LANES), BF16)
    return pl.pallas_call(
        functools.partial(_attn_prompt_kernel, tq=tq),
        out_shape=(jax.ShapeDtypeStruct(x.shape, F32),
                   jax.ShapeDtypeStruct((b, WINDOW, KV_DIM), F32),
                   jax.ShapeDtypeStruct((b, WINDOW, KV_DIM), F32)),
        grid=(b, s // tq),
        in_specs=[pl.BlockSpec(memory_space=pltpu.SMEM), row, _const_spec((1, D_MODEL)),
                  _const_spec((D_MODEL, QKV_DIM)), _const_spec((D_MODEL, D_MODEL)),
                  _const_spec(bias.shape), _const_spec(cap.shape)],
        out_specs=(row, cache, cache),
        scratch_shapes=[pltpu.VMEM((tq, D_MODEL), BF16), pltpu.VMEM((tq, D_MODEL), BF16),
                        kv_scr, kv_scr, kv_scr, kv_scr],
        compiler_params=_params("parallel", "arbitrary"),
        name="attn_prompt",
    )(sinks, x, g, w_qkv, w_o, bias, cap)


KEYS_PAD = WINDOW + 8
ROWS_PAD = 8 * Q_PER_KV


def _sample_bias_table(rel_bias):
    j = jnp.arange(KEYS_PAD, dtype=jnp.int32)
    dist = WINDOW - j
    tab = rel_bias[_t5_bucket(dist)].astype(F32).T
    tab = jnp.where(j[None, :] <= WINDOW, tab, NEG_INF)
    tab = tab.reshape(N_KV_HEADS, Q_PER_KV, KEYS_PAD).transpose(1, 0, 2)
    tab = jnp.pad(tab, ((0, 0), (0, 8 - N_KV_HEADS), (0, 0)))
    return tab.reshape(ROWS_PAD, KEYS_PAD)


def _attn_sample_kernel(q_ref, kn_ref, vn_ref, kc_ref, vc_ref, bias_ref, sink_ref,
                        o_ref, ko_ref, vo_ref, *, tb):
    ko_ref[:, 0:WINDOW - 1, :] = kc_ref[:, 1:WINDOW, :]
    vo_ref[:, 0:WINDOW - 1, :] = vc_ref[:, 1:WINDOW, :]
    sub = lax.broadcasted_iota(jnp.int32, (8, KV_DIM), 0)
    lane_head = lax.broadcasted_iota(jnp.int32, (8, KV_DIM), 1) // HEAD_DIM
    own = jnp.logical_and(sub < N_KV_HEADS, lane_head == sub)
    first_row = sub == 0
    bias = bias_ref[...]
    sink = sink_ref[...]

    def one(b, carry):
        k_new = kn_ref[pl.ds(b, 1), :]
        v_new = vn_ref[pl.ds(b, 1), :]
        ko_ref[b, WINDOW - 1:WINDOW, :] = k_new
        vo_ref[b, WINDOW - 1:WINDOW, :] = v_new
        q_rows = []
        for gq in range(Q_PER_KV):
            q_g = q_ref[pl.ds(b, 1), gq * KV_DIM:(gq + 1) * KV_DIM] * ATTN_SCALE
            q_rows.append(jnp.where(own, jnp.broadcast_to(q_g, (8, KV_DIM)), 0.0))
        q_blk = jnp.concatenate(q_rows, axis=0).astype(BF16)
        k_ext = jnp.concatenate(
            [kc_ref[b], jnp.where(first_row, jnp.broadcast_to(k_new, (8, KV_DIM)), 0.0)],
            axis=0).astype(BF16)
        v_ext = jnp.concatenate(
            [vc_ref[b], jnp.where(first_row, jnp.broadcast_to(v_new, (8, KV_DIM)), 0.0)],
            axis=0).astype(BF16)
        s = _dot_nt(q_blk, k_ext) + bias
        m = jnp.maximum(jnp.max(s, axis=-1, keepdims=True), sink)
        p = jnp.exp(s - m)
        denom = jnp.sum(p, axis=-1, keepdims=True) + jnp.exp(sink - m)
        o = _dot(p.astype(BF16), v_ext) / denom
        for gq in range(Q_PER_KV):
            o_g = jnp.sum(jnp.where(own, o[8 * gq:8 * gq + 8], 0.0), axis=0, keepdims=True)
            o_ref[pl.ds(b, 1), gq * KV_DIM:(gq + 1) * KV_DIM] = o_g
        return carry

    lax.fori_loop(0, tb, one, 0)


def _attn_sample_core(qkv, k_cache, v_cache, bias, sink_col, *, tb):
    n = qkv.shape[0]
    cache = pl.BlockSpec((tb, WINDOW, KV_DIM), lambda t: (t, 0, 0))
    kv_col = D_MODEL // KV_DIM
    return pl.pallas_call(
        functools.partial(_attn_sample_kernel, tb=tb),
        out_shape=(jax.ShapeDtypeStruct((n, D_MODEL), F32),
                   jax.ShapeDtypeStruct(k_cache.shape, F32),
                   jax.ShapeDtypeStruct(v_cache.shape, F32)),
        grid=(n // tb,),
        in_specs=[pl.BlockSpec((tb, D_MODEL), lambda t: (t, 0)),
                  pl.BlockSpec((tb, KV_DIM), lambda t: (t, kv_col)),
                  pl.BlockSpec((tb, KV_DIM), lambda t: (t, kv_col + 1)),
                  cache, cache, _const_spec(bias.shape), _const_spec(sink_col.shape)],
        out_specs=(pl.BlockSpec((tb, D_MODEL), lambda t: (t, 0)), cache, cache),
        compiler_params=_params("parallel"),
        name="attn_sample",
    )(qkv, qkv, qkv, k_cache, v_cache, bias, sink_col)


def _sgu_kernel(x_ref, g_ref, win_ref, lg_ref, lb_ref, sp_ref, bsp_ref, wout_ref, *rest,
                tm, sample):
    x = x_ref[...]
    h = _rms(x, g_ref[...]).astype(BF16)
    z = _dot(h, win_ref[...])
    z = 0.5 * z * (1.0 + lax.erf(z * INV_SQRT2))
    u = z[:, :D_MODEL]
    v = _layer_norm(z[:, D_MODEL:], lg_ref[...], lb_ref[...])
    if sample:
        o_ref, v_ref = rest
        v_ref[...] = v
        gated = (u * (v * sp_ref[...] + bsp_ref[...])).astype(BF16)
    else:
        o_ref, gated_scr = rest
        vb = v.astype(BF16)
        for c in range(tm // CHUNK):
            rows = slice(c * CHUNK, (c + 1) * CHUNK)
            for gi in range(SGU_GROUPS):
                cols = slice(gi * LANES, (gi + 1) * LANES)
                mixed = _dot(sp_ref[gi], vb[rows, cols]) + bsp_ref[gi]
                gated_scr[rows, cols] = (u[rows, cols] * mixed).astype(BF16)
        gated = gated_scr[...]
    o_ref[...] = x + _dot(gated, wout_ref[...])


def _sgu(x, g, w_in, ln_g, ln_b, sp, bsp, w_out, *, tm, sample):
    n = x.shape[0]
    row = pl.BlockSpec((tm, D_MODEL), lambda i: (i, 0))
    args = [x, g, w_in, ln_g, ln_b, sp, bsp, w_out]
    in_specs = [row] + [_const_spec(t.shape) for t in args[1:]]
    if sample:
        out_shape = (jax.ShapeDtypeStruct((n, D_MODEL), F32),) * 2
        out_specs = (row, row)
        scratch = []
    else:
        out_shape = jax.ShapeDtypeStruct((n, D_MODEL), F32)
        out_specs = row
        scratch = [pltpu.VMEM((tm, D_MODEL), BF16)]
    return pl.pallas_call(
        functools.partial(_sgu_kernel, tm=tm, sample=sample),
        out_shape=out_shape,
        grid=(n // tm,),
        in_specs=in_specs,
        out_specs=out_specs,
        scratch_shapes=scratch,
        compiler_params=_params("parallel"),
        name="sgu_sample" if sample else "sgu_prompt",
    )(*args)


CONV_ROWS = 32
CONV_LANES = 512


def _conv_prompt_kernel(x_ref, g_ref, win_ref, wdw_ref, bdw_ref, lg_ref, lb_ref, wout_ref,
                        o_ref, tail_ref, a_scr, sh_scr, c_scr, *, tm):
    i = pl.program_id(1)
    x = x_ref[...]
    h = _rms(x, g_ref[...]).astype(BF16)
    ag = _dot(h, win_ref[...])
    a = ag[:, :D_MODEL] * jax.nn.sigmoid(ag[:, D_MODEL:])

    @pl.when(i == 0)
    def _():
        a_scr[0:CONV_HALO, :] = jnp.zeros((CONV_HALO, D_MODEL), F32)

    @pl.when(i > 0)
    def _():
        a_scr[0:CONV_HALO, :] = a_scr[tm:tm + CONV_HALO, :]

    a_scr[CONV_HALO:, :] = a
    tail_ref[...] = a[tm - CONV_HALO:, :]

    for r in range(1, SUBLANES):
        sh_scr[r - 1] = a_scr[r:r + tm + CONV_HALO - SUBLANES, :]

    first_tap = CONV_HALO - (CONV_WIDTH - 1)

    def rows_block(rb, carry):
        r0 = pl.multiple_of(rb * CONV_ROWS, CONV_ROWS)
        for lc in range(D_MODEL // CONV_LANES):
            cols = slice(lc * CONV_LANES, (lc + 1) * CONV_LANES)
            acc = jnp.broadcast_to(bdw_ref[:, cols], (CONV_ROWS, CONV_LANES))
            for kk in range(CONV_WIDTH):
                whole, r = divmod(first_tap + kk, SUBLANES)
                start = pl.multiple_of(r0 + whole * SUBLANES, SUBLANES)
                src = a_scr if r == 0 else sh_scr.at[r - 1]
                acc = acc + src[pl.ds(start, CONV_ROWS), cols] * wdw_ref[kk:kk + 1, cols]
            c_scr[pl.ds(r0, CONV_ROWS), cols] = acc
        return carry

    lax.fori_loop(0, tm // CONV_ROWS, rows_block, 0)
    c = _layer_norm(c_scr[...], lg_ref[...], lb_ref[...])
    c = (c * jax.nn.sigmoid(c)).astype(BF16)
    o_ref[...] = x + _dot(c, wout_ref[...])


def _conv_prompt(x, g, w_in, w_dw, b_dw, ln_g, ln_b, w_out, *, tm):
    b, s, _ = x.shape
    row = pl.BlockSpec((None, tm, D_MODEL), lambda bi, i: (bi, i, 0))
    consts = [g, w_in, w_dw, b_dw, ln_g, ln_b, w_out]
    return pl.pallas_call(
        functools.partial(_conv_prompt_kernel, tm=tm),
        out_shape=(jax.ShapeDtypeStruct(x.shape, F32),
                   jax.ShapeDtypeStruct((b, CONV_HALO, D_MODEL), F32)),
        grid=(b, s // tm),
        in_specs=[row] + [_const_spec(t.shape) for t in consts],
        out_specs=(row, pl.BlockSpec((None, CONV_HALO, D_MODEL), lambda bi, i: (bi, 0, 0))),
        scratch_shapes=[pltpu.VMEM((CONV_HALO + tm, D_MODEL), F32),
                        pltpu.VMEM((SUBLANES - 1, CONV_HALO + tm - SUBLANES, D_MODEL), F32),
                        pltpu.VMEM((tm, D_MODEL), F32)],
        compiler_params=_params("parallel", "arbitrary"),
        name="conv_prompt",
    )(x, *consts)


def _conv_sample_kernel(a_ref, st_ref, wdw_ref, bdw_ref, c_ref, so_ref, *, tb):
    n_hist = CONV_WIDTH - 1
    st = st_ref[...]
    hist = jnp.sum(st * wdw_ref[0:n_hist, :][None], axis=1)
    c_ref[...] = hist + a_ref[...] * wdw_ref[n_hist:CONV_WIDTH, :] + bdw_ref[...]
    so_ref[:, 0:n_hist - 1, :] = st_ref[:, 1:n_hist, :]

    def one(b, carry):
        so_ref[b, n_hist - 1:n_hist, :] = a_ref[pl.ds(b, 1), :]
        return carry

    lax.fori_loop(0, tb, one, 0)


def _conv_sample_core(a, state, w_dw, b_dw, *, tb):
    n = a.shape[0]
    row = pl.BlockSpec((tb, D_MODEL), lambda t: (t, 0))
    st = pl.BlockSpec((tb, CONV_WIDTH - 1, D_MODEL), lambda t: (t, 0, 0))
    return pl.pallas_call(
        functools.partial(_conv_sample_kernel, tb=tb),
        out_shape=(jax.ShapeDtypeStruct((n, D_MODEL), F32),
                   jax.ShapeDtypeStruct(state.shape, F32)),
        grid=(n // tb,),
        in_specs=[row, st, _const_spec(w_dw.shape), _const_spec(b_dw.shape)],
        out_specs=(row, st),
        compiler_params=_params("parallel"),
        name="conv_sample",
    )(a, state, w_dw, b_dw)


PROMPT_TILE = 512
SAMPLE_ATTN_TILE = 16
SAMPLE_CONV_TILE = 16


def kernel(x_prompt, x_sample, cache_swa_k, cache_swa_v, state_conv, rel_bias, norm_mix, norm_ffn, norm_final, attn_w_qkv, attn_w_o, attn_sinks, sgu_w_in, sgu_ln_g, sgu_ln_b, sgu_w_spatial, sgu_b_spatial, sgu_w_out, conv_w_in, conv_w_dw, conv_b_dw, conv_ln_g, conv_ln_b, conv_w_out, ffn_w_up, ffn_w_down):
    batch, seq, _ = x_prompt.shape
    dec = x_sample.shape[0]
    depth = norm_mix.shape[0]
    mixer_of_layer = tuple(i % 3 for i in range(depth))
    slot_of_layer = tuple(mixer_of_layer[:i].count(mixer_of_layer[i]) for i in range(depth))

    def row(v):
        return v.reshape(1, -1).astype(F32)

    xp = x_prompt
    xs = x_sample.reshape(dec, D_MODEL)
    bias_p, cap_p = _prompt_bias_tables(rel_bias)
    bias_s = _sample_bias_table(rel_bias)
    perm = jnp.arange(D_MODEL).reshape(N_KV_HEADS, Q_PER_KV, HEAD_DIM).transpose(1, 0, 2).reshape(-1)

    kp, vp, ksm, vsm, sgu_v_new, convp, convs = [], [], [], [], [], [], []
    y_prompt = y_sample = None
    for i in range(depth):
        m, j = mixer_of_layer[i], slot_of_layer[i]
        g_mix = row(norm_mix[i])
        if m == 0:
            w_qkv = attn_w_qkv[j].astype(BF16)
            w_o = attn_w_o[j].astype(BF16)
            xp, k1, v1 = _attn_prompt(xp, g_mix, w_qkv, w_o, attn_sinks[j].astype(F32),
                                      bias_p, cap_p, tq=PROMPT_TILE)
            kp.append(k1.reshape(batch, WINDOW, N_KV_HEADS, HEAD_DIM))
            vp.append(v1.reshape(batch, WINDOW, N_KV_HEADS, HEAD_DIM))

            w_qkv_s = jnp.concatenate([w_qkv[:, :D_MODEL][:, perm], w_qkv[:, D_MODEL:]], axis=1)
            sink_col = attn_sinks[j].astype(F32).reshape(N_KV_HEADS, Q_PER_KV).T
            sink_col = jnp.pad(sink_col, ((0, 0), (0, 8 - N_KV_HEADS))).reshape(ROWS_PAD, 1)
            qkv_s = _norm_proj(xs, g_mix, w_qkv_s)
            o_s, k2, v2 = _attn_sample_core(
                qkv_s, cache_swa_k[j].reshape(dec, WINDOW, KV_DIM),
                cache_swa_v[j].reshape(dec, WINDOW, KV_DIM), bias_s, sink_col, tb=SAMPLE_ATTN_TILE)
            xs = _proj_res(xs, o_s, w_o[perm, :])
            ksm.append(k2.reshape(dec, WINDOW, N_KV_HEADS, HEAD_DIM))
            vsm.append(v2.reshape(dec, WINDOW, N_KV_HEADS, HEAD_DIM))
        elif m == 1:
            w_in = sgu_w_in[j].astype(BF16)
            w_out = sgu_w_out[j].astype(BF16)
            ln_g, ln_b = row(sgu_ln_g[j]), row(sgu_ln_b[j])
            sp = jnp.tril(sgu_w_spatial[j]).astype(BF16)
            bsp = jnp.broadcast_to(sgu_b_spatial[j].astype(F32)[:, :, None],
                                   (SGU_GROUPS, CHUNK, LANES))
            xp = _sgu(xp.reshape(batch * seq, D_MODEL), g_mix, w_in, ln_g, ln_b, sp, bsp, w_out,
                      tm=PROMPT_TILE, sample=False).reshape(batch, seq, D_MODEL)
            sp0 = row(jnp.repeat(sgu_w_spatial[j][:, 0, 0], LANES))
            bsp0 = row(jnp.repeat(sgu_b_spatial[j][:, 0], LANES))
            xs, v_rows = _sgu(xs, g_mix, w_in, ln_g, ln_b, sp0, bsp0, w_out, tm=dec, sample=True)
            sgu_v_new.append(v_rows.reshape(dec, 1, D_MODEL))
        else:
            w_in = conv_w_in[j].astype(BF16)
            w_out = conv_w_out[j].astype(BF16)
            w_dw = conv_w_dw[j].astype(F32)
            b_dw = row(conv_b_dw[j])
            ln_g, ln_b = row(conv_ln_g[j]), row(conv_ln_b[j])
            xp, tail = _conv_prompt(xp, g_mix, w_in, w_dw, b_dw, ln_g, ln_b, w_out, tm=PROMPT_TILE)
            convp.append(tail[:, CONV_HALO - (CONV_WIDTH - 1):, :])
            a_s = _norm_proj(xs, g_mix, w_in, glu=True)
            c_s, st2 = _conv_sample_core(a_s, state_conv[j].astype(F32), w_dw, b_dw,
                                         tb=SAMPLE_CONV_TILE)
            xs = _proj_res(xs, c_s, w_out, ln=(ln_g, ln_b))
            convs.append(st2)

        g_ffn = row(norm_ffn[i])
        w_up = ffn_w_up[i].astype(BF16)
        w_down = ffn_w_down[i].astype(BF16)
        g_fin = row(norm_final) if i == depth - 1 else None
        xp = _ffn(xp.reshape(batch * seq, D_MODEL), g_ffn, w_up, w_down, g_fin,
                  tm=PROMPT_TILE).reshape(batch, seq, D_MODEL)
        xs = _ffn(xs, g_ffn, w_up, w_down, g_fin, tm=dec)

    y_prompt = xp
    y_sample = xs.reshape(dec, 1, D_MODEL)
    return (y_prompt, y_sample, jnp.stack(kp), jnp.stack(vp), jnp.stack(ksm), jnp.stack(vsm),
            jnp.stack(sgu_v_new), jnp.stack(convp), jnp.stack(convs))
```

```python
import functools
import math

import jax
import jax.numpy as jnp
from jax import lax
from jax.experimental import pallas as pl
from jax.experimental.pallas import tpu as pltpu

D_MODEL = 1024
HEAD_DIM = 64
N_HEADS = 16
N_KV_HEADS = 4
Q_PER_KV = 4
KV_DIM = N_KV_HEADS * HEAD_DIM
QKV_DIM = D_MODEL + 2 * KV_DIM
WINDOW = 128
ATTN_SCALE = HEAD_DIM ** -0.5
N_BUCKETS = 32
MAX_DISTANCE = 128
CHUNK = 128
SGU_GROUPS = 8
CONV_WIDTH = 31
CONV_HALO = 32
D_FF = 4 * D_MODEL
EPS = 1e-6
NEG_INF = -1e30
F32_MAX = float(jnp.finfo(jnp.float32).max)
INV_SQRT2 = 1.0 / math.sqrt(2.0)

LANES = 128
SUBLANES = 8
VMEM_LIMIT = 56 * 1024 * 1024

F32 = jnp.float32
BF16 = jnp.bfloat16


def _const_spec(shape):
    n = len(shape)
    return pl.BlockSpec(shape, lambda *_: (0,) * n, pipeline_mode=pl.Buffered(1))


def _params(*sem):
    return pltpu.CompilerParams(dimension_semantics=sem, vmem_limit_bytes=VMEM_LIMIT)


def _rms(x, g):
    return x * lax.rsqrt(jnp.mean(x * x, axis=-1, keepdims=True) + EPS) * g


def _layer_norm(x, g, b):
    mu = jnp.mean(x, axis=-1, keepdims=True)
    xc = x - mu
    var = jnp.mean(xc * xc, axis=-1, keepdims=True)
    return xc * lax.rsqrt(var + EPS) * g + b


def _dot(a, b):
    return jnp.dot(a, b, preferred_element_type=F32)


def _dot_nt(a, b):
    return lax.dot_general(a, b, (((1,), (1,)), ((), ())), preferred_element_type=F32)


FFN_CHUNK = 1024


def _ffn_kernel(x_ref, g_ref, wu_ref, wd_ref, *rest, final):
    x = x_ref[...]
    h = _rms(x, g_ref[...]).astype(BF16)
    y = x
    for c in range(D_FF // FFN_CHUNK):
        cols = slice(c * FFN_CHUNK, (c + 1) * FFN_CHUNK)
        u = _dot(h, wu_ref[:, cols])
        u = jnp.square(jnp.maximum(u, 0.0)).astype(BF16)
        y = y + _dot(u, wd_ref[cols, :])
    if final:
        gf_ref, o_ref = rest
        o_ref[...] = _rms(y, gf_ref[...])
    else:
        (o_ref,) = rest
        o_ref[...] = y


def _ffn(x, g, w_up, w_down, g_final=None, *, tm):
    n = x.shape[0]
    row = pl.BlockSpec((tm, D_MODEL), lambda i: (i, 0))
    in_specs = [row, _const_spec((1, D_MODEL)), _const_spec((D_MODEL, D_FF)),
                _const_spec((D_FF, D_MODEL))]
    args = [x, g, w_up, w_down]
    if g_final is not None:
        in_specs.append(_const_spec((1, D_MODEL)))
        args.append(g_final)
    return pl.pallas_call(
        functools.partial(_ffn_kernel, final=g_final is not None),
        out_shape=jax.ShapeDtypeStruct((n, D_MODEL), F32),
        grid=(n // tm,),
        in_specs=in_specs,
        out_specs=row,
        compiler_params=_params("parallel"),
        name="ffn_final" if g_final is not None else "ffn",
    )(*args)


def _norm_proj_kernel(x_ref, g_ref, w_ref, o_ref, *, glu):
    h = _rms(x_ref[...], g_ref[...]).astype(BF16)
    y = _dot(h, w_ref[...])
    if glu:
        half = y.shape[1] // 2
        y = y[:, :half] * jax.nn.sigmoid(y[:, half:])
    o_ref[...] = y


def _norm_proj(x, g, w, *, glu=False):
    n = x.shape[0]
    n_out = w.shape[1] // 2 if glu else w.shape[1]
    return pl.pallas_call(
        functools.partial(_norm_proj_kernel, glu=glu),
        out_shape=jax.ShapeDtypeStruct((n, n_out), F32),
        grid=(1,),
        in_specs=[_const_spec(x.shape), _const_spec(g.shape), _const_spec(w.shape)],
        out_specs=_const_spec((n, n_out)),
        compiler_params=_params("arbitrary"),
        name="norm_proj_glu" if glu else "norm_proj",
    )(x, g, w)


def _proj_res_kernel(x_ref, a_ref, w_ref, *rest, conv_tail):
    a = a_ref[...]
    if conv_tail:
        lg_ref, lb_ref, o_ref = rest
        a = _layer_norm(a, lg_ref[...], lb_ref[...])
        a = a * jax.nn.sigmoid(a)
    else:
        (o_ref,) = rest
    o_ref[...] = x_ref[...] + _dot(a.astype(BF16), w_ref[...])


def _proj_res(x, a, w, ln=None):
    args = [x, a, w] + (list(ln) if ln is not None else [])
    return pl.pallas_call(
        functools.partial(_proj_res_kernel, conv_tail=ln is not None),
        out_shape=jax.ShapeDtypeStruct(x.shape, F32),
        grid=(1,),
        in_specs=[_const_spec(t.shape) for t in args],
        out_specs=_const_spec(x.shape),
        compiler_params=_params("arbitrary"),
        name="proj_res_ln" if ln is not None else "proj_res",
    )(*args)


def _t5_bucket(dist):
    n = jnp.maximum(dist, 0)
    max_exact = N_BUCKETS // 2
    nf = jnp.maximum(n, 1).astype(F32)
    large = max_exact + (jnp.log(nf / max_exact) / math.log(MAX_DISTANCE / max_exact)
                         * (N_BUCKETS - max_exact)).astype(jnp.int32)
    large = jnp.minimum(large, N_BUCKETS - 1)
    return jnp.where(n < max_exact, n, large)


def _distance_bias(rel_bias):
    buckets = _t5_bucket(jnp.arange(WINDOW + 1, dtype=jnp.int32))
    onehot = (buckets[:, None] == jnp.arange(N_BUCKETS, dtype=jnp.int32)[None, :]).astype(F32)
    return jnp.dot(onehot, rel_bias.astype(F32), precision=lax.Precision.HIGHEST)


def _prompt_bias_tables(dist_bias, sinks):
    period = 3 * WINDOW
    line = jnp.concatenate([jnp.broadcast_to(dist_bias[WINDOW:], (WINDOW - 1, N_HEADS)),
                            dist_bias[::-1],
                            jnp.broadcast_to(dist_bias[:1], (WINDOW, N_HEADS))]).T
    skew = jnp.tile(line, (1, WINDOW))[:, :WINDOW * (period - 1)]
    per_head = skew.reshape(N_HEADS, WINDOW, period - 1)[:, :, WINDOW - 1:period - 1]
    per_head = per_head.reshape(N_KV_HEADS, 2, 2, WINDOW, 2 * WINDOW)
    bias = per_head.transpose(0, 2, 4, 1, 3).reshape(N_KV_HEADS, 4 * WINDOW, 2 * WINDOW)

    qi = jnp.arange(WINDOW, dtype=jnp.int32)[None, :]
    kj = jnp.arange(2 * WINDOW, dtype=jnp.int32)[:, None]
    dist = qi - kj + WINDOW
    allowed = (dist >= 0) & (dist <= WINDOW)
    first = allowed & (kj >= WINDOW)
    cap = jnp.stack([jnp.where(allowed, F32_MAX, NEG_INF), jnp.where(first, F32_MAX, NEG_INF)])
    cap = jnp.tile(cap.astype(F32), (1, 2, 2))

    sink_rows = sinks.astype(F32).reshape(N_KV_HEADS, 2, 2).transpose(0, 2, 1).reshape(2 * N_KV_HEADS, 2)
    sink_rows = jnp.repeat(sink_rows, WINDOW, axis=1)
    return bias, cap, sink_rows


def _attn_prompt_kernel(x_ref, g_ref, wqkv_ref, wo_ref, bias_ref, cap_ref, sink_ref,
                        o_ref, kc_ref, vc_ref,
                        q_scr, o_scr, klo_scr, khi_scr, vt_scr, *, tq):
    i = pl.program_id(1)
    n_i = pl.num_programs(1)
    x = x_ref[...]
    h = _rms(x, g_ref[...]).astype(BF16)
    qkv = _dot(h, wqkv_ref[...])
    q_scr[...] = (qkv[:, :D_MODEL] * ATTN_SCALE).astype(BF16)
    k = qkv[:, D_MODEL:D_MODEL + KV_DIM]
    v = qkv[:, D_MODEL + KV_DIM:]

    @pl.when(i == n_i - 1)
    def _():
        kc_ref[...] = k[tq - WINDOW:, :]
        vc_ref[...] = v[tq - WINDOW:, :]

    @pl.when(i == 0)
    def _():
        klo_scr[:, 0:WINDOW, :] = jnp.zeros((N_KV_HEADS, WINDOW, LANES), BF16)
        khi_scr[:, 0:WINDOW, :] = jnp.zeros((N_KV_HEADS, WINDOW, LANES), BF16)
        vt_scr[:, :, 0:WINDOW] = jnp.zeros((N_KV_HEADS, HEAD_DIM, WINDOW), BF16)

    @pl.when(i > 0)
    def _():
        klo_scr[:, 0:WINDOW, :] = klo_scr[:, tq:tq + WINDOW, :]
        khi_scr[:, 0:WINDOW, :] = khi_scr[:, tq:tq + WINDOW, :]
        vt_scr[:, :, 0:WINDOW] = vt_scr[:, :, tq:tq + WINDOW]

    low = lax.broadcasted_iota(jnp.int32, (tq, LANES), 1) < HEAD_DIM
    for c in range(KV_DIM // LANES):
        kc = k[:, c * LANES:(c + 1) * LANES]
        kr = pltpu.roll(kc, HEAD_DIM, axis=1)
        klo_scr[2 * c, WINDOW:, :] = jnp.where(low, kc, 0.0).astype(BF16)
        khi_scr[2 * c, WINDOW:, :] = jnp.where(low, 0.0, kr).astype(BF16)
        klo_scr[2 * c + 1, WINDOW:, :] = jnp.where(low, kr, 0.0).astype(BF16)
        khi_scr[2 * c + 1, WINDOW:, :] = jnp.where(low, 0.0, kc).astype(BF16)
        vt = v[:, c * LANES:(c + 1) * LANES].T.astype(BF16)
        vt_scr[2 * c, :, WINDOW:] = vt[:HEAD_DIM]
        vt_scr[2 * c + 1, :, WINDOW:] = vt[HEAD_DIM:]

    is_first = jnp.where(i == 0, 1, 0)

    def scores(jb, kh):
        rows = slice(jb * WINDOW, (jb + 1) * WINDOW)
        band = slice(jb * WINDOW, (jb + 2) * WINDOW)
        cap = cap_ref[is_first] if jb == 0 else cap_ref[0]
        qst = jnp.concatenate([q_scr[rows, (2 * kh) * LANES:(2 * kh + 1) * LANES],
                               q_scr[rows, (2 * kh + 1) * LANES:(2 * kh + 2) * LANES]], axis=0)
        kst = jnp.concatenate([klo_scr[kh, band, :], khi_scr[kh, band, :]], axis=0)
        return jnp.minimum(_dot_nt(kst, qst) + bias_ref[kh], cap)

    def attend(jb, kh, s):
        rows = slice(jb * WINDOW, (jb + 1) * WINDOW)
        vt = vt_scr[kh, :, jb * WINDOW:(jb + 2) * WINDOW]
        halves = []
        for half in range(2):
            sh = s[half * 2 * WINDOW:(half + 1) * 2 * WINDOW]
            sink = sink_ref[2 * kh + half:2 * kh + half + 1, :]
            m = jnp.maximum(jnp.max(sh, axis=0, keepdims=True), sink)
            p = jnp.exp(sh - m)
            denom = jnp.sum(p, axis=0, keepdims=True) + jnp.exp(sink - m)
            halves.append(_dot(vt, p.astype(BF16)) * (1.0 / denom))
        o = jnp.concatenate(halves, axis=0).T.astype(BF16)
        o_scr[rows, (2 * kh) * LANES:(2 * kh + 1) * LANES] = o[:WINDOW]
        o_scr[rows, (2 * kh + 1) * LANES:(2 * kh + 2) * LANES] = o[WINDOW:]

    steps = [(jb, kh) for jb in range(tq // WINDOW) for kh in range(N_KV_HEADS)]
    s_next = scores(*steps[0])
    for n, step in enumerate(steps):
        s_cur = s_next
        if n + 1 < len(steps):
            s_next = scores(*steps[n + 1])
        attend(*step, s_cur)

    o_ref[...] = x + _dot(o_scr[...], wo_ref[...])


def _attn_prompt(x, g, w_qkv, w_o, bias, cap, sink_rows, *, tq):
    b, s, _ = x.shape
    row = pl.BlockSpec((None, tq, D_MODEL), lambda bi, i: (bi, i, 0))
    cache = pl.BlockSpec((None, WINDOW, KV_DIM), lambda bi, i: (bi, 0, 0))
    k_scr = pltpu.VMEM((N_KV_HEADS, WINDOW + tq, LANES), BF16)
    return pl.pallas_call(
        functools.partial(_attn_prompt_kernel, tq=tq),
        out_shape=(jax.ShapeDtypeStruct(x.shape, F32),
                   jax.ShapeDtypeStruct((b, WINDOW, KV_DIM), F32),
                   jax.ShapeDtypeStruct((b, WINDOW, KV_DIM), F32)),
        grid=(b, s // tq),
        in_specs=[row, _const_spec((1, D_MODEL)),
                  _const_spec((D_MODEL, QKV_DIM)), _const_spec((D_MODEL, D_MODEL)),
                  _const_spec(bias.shape), _const_spec(cap.shape), _const_spec(sink_rows.shape)],
        out_specs=(row, cache, cache),
        scratch_shapes=[pltpu.VMEM((tq, D_MODEL), BF16), pltpu.VMEM((tq, D_MODEL), BF16),
                        k_scr, k_scr, pltpu.VMEM((N_KV_HEADS, HEAD_DIM, WINDOW + tq), BF16)],
        compiler_params=_params("parallel", "arbitrary"),
        name="attn_prompt",
    )(x, g, w_qkv, w_o, bias, cap, sink_rows)


KEYS_PAD = WINDOW + 8
ROWS_PAD = 8 * Q_PER_KV


def _sample_bias_table(dist_bias):
    tab = jnp.concatenate([dist_bias[::-1].T,
                           jnp.full((N_HEADS, KEYS_PAD - WINDOW - 1), NEG_INF, F32)], axis=1)
    tab = tab.reshape(N_KV_HEADS, Q_PER_KV, KEYS_PAD).transpose(1, 0, 2)
    tab = jnp.pad(tab, ((0, 0), (0, 8 - N_KV_HEADS), (0, 0)))
    return tab.reshape(ROWS_PAD, KEYS_PAD)


def _attn_sample_kernel(q_ref, kn_ref, vn_ref, kc_ref, vc_ref, bias_ref, sink_ref,
                        o_ref, ko_ref, vo_ref, *, tb):
    ko_ref[:, 0:WINDOW - 1, :] = kc_ref[:, 1:WINDOW, :]
    vo_ref[:, 0:WINDOW - 1, :] = vc_ref[:, 1:WINDOW, :]
    sub = lax.broadcasted_iota(jnp.int32, (8, KV_DIM), 0)
    lane_head = lax.broadcasted_iota(jnp.int32, (8, KV_DIM), 1) // HEAD_DIM
    own = jnp.logical_and(sub < N_KV_HEADS, lane_head == sub)
    first_row = sub == 0
    bias = bias_ref[...]
    sink = sink_ref[...]

    def one(b, carry):
        k_new = kn_ref[pl.ds(b, 1), :]
        v_new = vn_ref[pl.ds(b, 1), :]
        ko_ref[b, WINDOW - 1:WINDOW, :] = k_new
        vo_ref[b, WINDOW - 1:WINDOW, :] = v_new
        q_rows = []
        for gq in range(Q_PER_KV):
            q_g = q_ref[pl.ds(b, 1), gq * KV_DIM:(gq + 1) * KV_DIM] * ATTN_SCALE
            q_rows.append(jnp.where(own, jnp.broadcast_to(q_g, (8, KV_DIM)), 0.0))
        q_blk = jnp.concatenate(q_rows, axis=0).astype(BF16)
        k_ext = jnp.concatenate(
            [kc_ref[b], jnp.where(first_row, jnp.broadcast_to(k_new, (8, KV_DIM)), 0.0)],
            axis=0).astype(BF16)
        v_ext = jnp.concatenate(
            [vc_ref[b], jnp.where(first_row, jnp.broadcast_to(v_new, (8, KV_DIM)), 0.0)],
            axis=0).astype(BF16)
        s = _dot_nt(q_blk, k_ext) + bias
        m = jnp.maximum(jnp.max(s, axis=-1, keepdims=True), sink)
        p = jnp.exp(s - m)
        denom = jnp.sum(p, axis=-1, keepdims=True) + jnp.exp(sink - m)
        o = _dot(p.astype(BF16), v_ext) / denom
        for gq in range(Q_PER_KV):
            o_g = jnp.sum(jnp.where(own, o[8 * gq:8 * gq + 8], 0.0), axis=0, keepdims=True)
            o_ref[pl.ds(b, 1), gq * KV_DIM:(gq + 1) * KV_DIM] = o_g
        return carry

    lax.fori_loop(0, tb, one, 0)


def _attn_sample_core(qkv, k_cache, v_cache, bias, sink_col, *, tb):
    n = qkv.shape[0]
    cache = pl.BlockSpec((tb, WINDOW, KV_DIM), lambda t: (t, 0, 0))
    kv_col = D_MODEL // KV_DIM
    return pl.pallas_call(
        functools.partial(_attn_sample_kernel, tb=tb),
        out_shape=(jax.ShapeDtypeStruct((n, D_MODEL), F32),
                   jax.ShapeDtypeStruct(k_cache.shape, F32),
                   jax.ShapeDtypeStruct(v_cache.shape, F32)),
        grid=(n // tb,),
        in_specs=[pl.BlockSpec((tb, D_MODEL), lambda t: (t, 0)),
                  pl.BlockSpec((tb, KV_DIM), lambda t: (t, kv_col)),
                  pl.BlockSpec((tb, KV_DIM), lambda t: (t, kv_col + 1)),
                  cache, cache, _const_spec(bias.shape), _const_spec(sink_col.shape)],
        out_specs=(pl.BlockSpec((tb, D_MODEL), lambda t: (t, 0)), cache, cache),
        compiler_params=_params("parallel"),
        name="attn_sample",
    )(qkv, qkv, qkv, k_cache, v_cache, bias, sink_col)


def _sgu_kernel(x_ref, g_ref, win_ref, lg_ref, lb_ref, sp_ref, bsp_ref, wout_ref, *rest,
                tm, sample):
    x = x_ref[...]
    h = _rms(x, g_ref[...]).astype(BF16)
    z = _dot(h, win_ref[...])
    z = 0.5 * z * (1.0 + lax.erf(z * INV_SQRT2))
    u = z[:, :D_MODEL]
    v = _layer_norm(z[:, D_MODEL:], lg_ref[...], lb_ref[...])
    if sample:
        o_ref, v_ref = rest
        v_ref[...] = v
        gated = (u * (v * sp_ref[...] + bsp_ref[...])).astype(BF16)
    else:
        o_ref, gated_scr = rest
        vb = v.astype(BF16)
        for c in range(tm // CHUNK):
            rows = slice(c * CHUNK, (c + 1) * CHUNK)
            for gi in range(SGU_GROUPS):
                cols = slice(gi * LANES, (gi + 1) * LANES)
                mixed = _dot(sp_ref[gi], vb[rows, cols]) + bsp_ref[gi]
                gated_scr[rows, cols] = (u[rows, cols] * mixed).astype(BF16)
        gated = gated_scr[...]
    o_ref[...] = x + _dot(gated, wout_ref[...])


def _sgu(x, g, w_in, ln_g, ln_b, sp, bsp, w_out, *, tm, sample):
    n = x.shape[0]
    row = pl.BlockSpec((tm, D_MODEL), lambda i: (i, 0))
    args = [x, g, w_in, ln_g, ln_b, sp, bsp, w_out]
    in_specs = [row] + [_const_spec(t.shape) for t in args[1:]]
    if sample:
        out_shape = (jax.ShapeDtypeStruct((n, D_MODEL), F32),) * 2
        out_specs = (row, row)
        scratch = []
    else:
        out_shape = jax.ShapeDtypeStruct((n, D_MODEL), F32)
        out_specs = row
        scratch = [pltpu.VMEM((tm, D_MODEL), BF16)]
    return pl.pallas_call(
        functools.partial(_sgu_kernel, tm=tm, sample=sample),
        out_shape=out_shape,
        grid=(n // tm,),
        in_specs=in_specs,
        out_specs=out_specs,
        scratch_shapes=scratch,
        compiler_params=_params("parallel"),
        name="sgu_sample" if sample else "sgu_prompt",
    )(*args)


CONV_ROWS = 64
CONV_LANES = 256


def _conv_prompt_kernel(x_ref, g_ref, win_ref, wdw_ref, bdw_ref, lg_ref, lb_ref, wout_ref,
                        o_ref, tail_ref, a_scr, sh_scr, c_scr, *, tm):
    i = pl.program_id(1)
    x = x_ref[...]
    h = _rms(x, g_ref[...]).astype(BF16)
    ag = _dot(h, win_ref[...])
    a = ag[:, :D_MODEL] * jax.nn.sigmoid(ag[:, D_MODEL:])

    @pl.when(i == 0)
    def _():
        a_scr[0:CONV_HALO, :] = jnp.zeros((CONV_HALO, D_MODEL), F32)

    @pl.when(i > 0)
    def _():
        a_scr[0:CONV_HALO, :] = a_scr[tm:tm + CONV_HALO, :]

    a_scr[CONV_HALO:, :] = a
    tail_ref[...] = a[tm - CONV_HALO:, :]

    for r in range(1, SUBLANES):
        sh_scr[r - 1] = a_scr[r:r + tm + CONV_HALO - SUBLANES, :]

    first_tap = CONV_HALO - (CONV_WIDTH - 1)
    reps = CONV_ROWS // SUBLANES

    def rows_block(rb, carry):
        r0 = pl.multiple_of(rb * CONV_ROWS, CONV_ROWS)
        for lc in range(D_MODEL // CONV_LANES):
            cols = slice(lc * CONV_LANES, (lc + 1) * CONV_LANES)
            acc = jnp.concatenate([bdw_ref[:, cols]] * reps, axis=0)
            for kk in range(CONV_WIDTH):
                whole, r = divmod(first_tap + kk, SUBLANES)
                start = pl.multiple_of(r0 + whole * SUBLANES, SUBLANES)
                src = a_scr if r == 0 else sh_scr.at[r - 1]
                w = jnp.concatenate([wdw_ref[kk, :, cols]] * reps, axis=0)
                acc = acc + src[pl.ds(start, CONV_ROWS), cols] * w
            c_scr[pl.ds(r0, CONV_ROWS), cols] = acc
        return carry

    lax.fori_loop(0, tm // CONV_ROWS, rows_block, 0)
    c = _layer_norm(c_scr[...], lg_ref[...], lb_ref[...])
    c = (c * jax.nn.sigmoid(c)).astype(BF16)
    o_ref[...] = x + _dot(c, wout_ref[...])


def _conv_prompt(x, g, w_in, w_dw8, b_dw8, ln_g, ln_b, w_out, *, tm):
    b, s, _ = x.shape
    row = pl.BlockSpec((None, tm, D_MODEL), lambda bi, i: (bi, i, 0))
    consts = [g, w_in, w_dw8, b_dw8, ln_g, ln_b, w_out]
    return pl.pallas_call(
        functools.partial(_conv_prompt_kernel, tm=tm),
        out_shape=(jax.ShapeDtypeStruct(x.shape, F32),
                   jax.ShapeDtypeStruct((b, CONV_HALO, D_MODEL), F32)),
        grid=(b, s // tm),
        in_specs=[row] + [_const_spec(t.shape) for t in consts],
        out_specs=(row, pl.BlockSpec((None, CONV_HALO, D_MODEL), lambda bi, i: (bi, 0, 0))),
        scratch_shapes=[pltpu.VMEM((CONV_HALO + tm, D_MODEL), F32),
                        pltpu.VMEM((SUBLANES - 1, CONV_HALO + tm - SUBLANES, D_MODEL), F32),
                        pltpu.VMEM((tm, D_MODEL), F32)],
        compiler_params=_params("parallel", "arbitrary"),
        name="conv_prompt",
    )(x, *consts)


def _conv_sample_kernel(a_ref, st_ref, wdw_ref, bdw_ref, c_ref, so_ref, *, tb):
    n_hist = CONV_WIDTH - 1
    st = st_ref[...]
    hist = jnp.sum(st * wdw_ref[0:n_hist, :][None], axis=1)
    c_ref[...] = hist + a_ref[...] * wdw_ref[n_hist:CONV_WIDTH, :] + bdw_ref[...]
    so_ref[:, 0:n_hist - 1, :] = st_ref[:, 1:n_hist, :]

    def one(b, carry):
        so_ref[b, n_hist - 1:n_hist, :] = a_ref[pl.ds(b, 1), :]
        return carry

    lax.fori_loop(0, tb, one, 0)


def _conv_sample_core(a, state, w_dw, b_dw, *, tb):
    n = a.shape[0]
    row = pl.BlockSpec((tb, D_MODEL), lambda t: (t, 0))
    st = pl.BlockSpec((tb, CONV_WIDTH - 1, D_MODEL), lambda t: (t, 0, 0))
    return pl.pallas_call(
        functools.partial(_conv_sample_kernel, tb=tb),
        out_shape=(jax.ShapeDtypeStruct((n, D_MODEL), F32),
                   jax.ShapeDtypeStruct(state.shape, F32)),
        grid=(n // tb,),
        in_specs=[row, st, _const_spec(w_dw.shape), _const_spec(b_dw.shape)],
        out_specs=(row, st),
        compiler_params=_params("parallel"),
        name="conv_sample",
    )(a, state, w_dw, b_dw)


PROMPT_TILE = 512
SAMPLE_ATTN_TILE = 16
SAMPLE_CONV_TILE = 16


def kernel(x_prompt, x_sample, cache_swa_k, cache_swa_v, state_conv, rel_bias, norm_mix, norm_ffn, norm_final, attn_w_qkv, attn_w_o, attn_sinks, sgu_w_in, sgu_ln_g, sgu_ln_b, sgu_w_spatial, sgu_b_spatial, sgu_w_out, conv_w_in, conv_w_dw, conv_b_dw, conv_ln_g, conv_ln_b, conv_w_out, ffn_w_up, ffn_w_down):
    batch, seq, _ = x_prompt.shape
    dec = x_sample.shape[0]
    depth = norm_mix.shape[0]
    mixer_of_layer = tuple(i % 3 for i in range(depth))
    slot_of_layer = tuple(mixer_of_layer[:i].count(mixer_of_layer[i]) for i in range(depth))

    def row(v):
        return v.reshape(1, -1).astype(F32)

    def regroup(w, axis):
        shape = w.shape[:axis] + (N_KV_HEADS, Q_PER_KV, HEAD_DIM) + w.shape[axis + 1:]
        return jnp.swapaxes(w.reshape(shape), axis, axis + 1).reshape(w.shape)

    xp = x_prompt
    xs = x_sample.reshape(dec, D_MODEL)
    dist_bias = _distance_bias(rel_bias)
    bias_s = _sample_bias_table(dist_bias)

    kp, vp, ksm, vsm, sgu_v_new, convp, convs = [], [], [], [], [], [], []
    for i in range(depth):
        m, j = mixer_of_layer[i], slot_of_layer[i]
        g_mix = row(norm_mix[i])
        if m == 0:
            w_qkv = attn_w_qkv[j].astype(BF16)
            w_o = attn_w_o[j].astype(BF16)
            bias_p, cap_p, sink_rows = _prompt_bias_tables(dist_bias, attn_sinks[j])
            xp, k1, v1 = _attn_prompt(xp, g_mix, w_qkv, w_o, bias_p, cap_p, sink_rows,
                                      tq=PROMPT_TILE)
            kp.append(k1.reshape(batch, WINDOW, N_KV_HEADS, HEAD_DIM))
            vp.append(v1.reshape(batch, WINDOW, N_KV_HEADS, HEAD_DIM))

            w_qkv_s = jnp.concatenate([regroup(w_qkv[:, :D_MODEL], 1), w_qkv[:, D_MODEL:]], axis=1)
            sink_col = attn_sinks[j].astype(F32).reshape(N_KV_HEADS, Q_PER_KV).T
            sink_col = jnp.pad(sink_col, ((0, 0), (0, 8 - N_KV_HEADS))).reshape(ROWS_PAD, 1)
            qkv_s = _norm_proj(xs, g_mix, w_qkv_s)
            o_s, k2, v2 = _attn_sample_core(
                qkv_s, cache_swa_k[j].reshape(dec, WINDOW, KV_DIM),
                cache_swa_v[j].reshape(dec, WINDOW, KV_DIM), bias_s, sink_col, tb=SAMPLE_ATTN_TILE)
            xs = _proj_res(xs, o_s, regroup(w_o, 0))
            ksm.append(k2.reshape(dec, WINDOW, N_KV_HEADS, HEAD_DIM))
            vsm.append(v2.reshape(dec, WINDOW, N_KV_HEADS, HEAD_DIM))
        elif m == 1:
            w_in = sgu_w_in[j].astype(BF16)
            w_out = sgu_w_out[j].astype(BF16)
            ln_g, ln_b = row(sgu_ln_g[j]), row(sgu_ln_b[j])
            sp = jnp.tril(sgu_w_spatial[j]).astype(BF16)
            bsp = jnp.broadcast_to(sgu_b_spatial[j].astype(F32)[:, :, None],
                                   (SGU_GROUPS, CHUNK, LANES))
            xp = _sgu(xp.reshape(batch * seq, D_MODEL), g_mix, w_in, ln_g, ln_b, sp, bsp, w_out,
                      tm=PROMPT_TILE, sample=False).reshape(batch, seq, D_MODEL)
            sp0 = row(jnp.repeat(sgu_w_spatial[j][:, 0, 0], LANES))
            bsp0 = row(jnp.repeat(sgu_b_spatial[j][:, 0], LANES))
            xs, v_rows = _sgu(xs, g_mix, w_in, ln_g, ln_b, sp0, bsp0, w_out, tm=dec, sample=True)
            sgu_v_new.append(v_rows.reshape(dec, 1, D_MODEL))
        else:
            w_in = conv_w_in[j].astype(BF16)
            w_out = conv_w_out[j].astype(BF16)
            w_dw = conv_w_dw[j].astype(F32)
            b_dw = row(conv_b_dw[j])
            ln_g, ln_b = row(conv_ln_g[j]), row(conv_ln_b[j])
            w_dw8 = jnp.broadcast_to(w_dw[:, None, :], (CONV_WIDTH, SUBLANES, D_MODEL))
            b_dw8 = jnp.broadcast_to(b_dw, (SUBLANES, D_MODEL))
            xp, tail = _conv_prompt(xp, g_mix, w_in, w_dw8, b_dw8, ln_g, ln_b, w_out,
                                    tm=PROMPT_TILE)
            convp.append(tail[:, CONV_HALO - (CONV_WIDTH - 1):, :])
            a_s = _norm_proj(xs, g_mix, w_in, glu=True)
            c_s, st2 = _conv_sample_core(a_s, state_conv[j].astype(F32), w_dw, b_dw,
                                         tb=SAMPLE_CONV_TILE)
            xs = _proj_res(xs, c_s, w_out, ln=(ln_g, ln_b))
            convs.append(st2)

        g_ffn = row(norm_ffn[i])
        w_up = ffn_w_up[i].astype(BF16)
        w_down = ffn_w_down[i].astype(BF16)
        g_fin = row(norm_final) if i == depth - 1 else None
        xp = _ffn(xp.reshape(batch * seq, D_MODEL), g_ffn, w_up, w_down, g_fin,
                  tm=PROMPT_TILE).reshape(batch, seq, D_MODEL)
        xs = _ffn(xs, g_ffn, w_up, w_down, g_fin, tm=dec)

    y_prompt = xp
    y_sample = xs.reshape(dec, 1, D_MODEL)
    return (y_prompt, y_sample, jnp.stack(kp), jnp.stack(vp), jnp.stack(ksm), jnp.stack(vsm),
            jnp.stack(sgu_v_new), jnp.stack(convp), jnp.stack(convs))
```

```python
import functools
import math

import jax
import jax.numpy as jnp
from jax import lax
from jax.experimental import pallas as pl
from jax.experimental.pallas import tpu as pltpu

D_MODEL = 1024
HEAD_DIM = 64
N_HEADS = 16
N_KV_HEADS = 4
Q_PER_KV = 4
KV_DIM = N_KV_HEADS * HEAD_DIM
QKV_DIM = D_MODEL + 2 * KV_DIM
WINDOW = 128
ATTN_SCALE = HEAD_DIM ** -0.5
N_BUCKETS = 32
MAX_DISTANCE = 128
CHUNK = 128
SGU_GROUPS = 8
CONV_WIDTH = 31
CONV_HALO = 32
D_FF = 4 * D_MODEL
EPS = 1e-6
NEG_INF = -1e30
F32_MAX = float(jnp.finfo(jnp.float32).max)
INV_SQRT2 = 1.0 / math.sqrt(2.0)

LANES = 128
SUBLANES = 8
VMEM_LIMIT = 56 * 1024 * 1024

F32 = jnp.float32
BF16 = jnp.bfloat16


def _const_spec(shape):
    n = len(shape)
    return pl.BlockSpec(shape, lambda *_: (0,) * n, pipeline_mode=pl.Buffered(1))


def _params(*sem):
    return pltpu.CompilerParams(dimension_semantics=sem, vmem_limit_bytes=VMEM_LIMIT)


def _rms(x, g):
    return x * lax.rsqrt(jnp.mean(x * x, axis=-1, keepdims=True) + EPS) * g


def _layer_norm(x, g, b):
    mu = jnp.mean(x, axis=-1, keepdims=True)
    xc = x - mu
    var = jnp.mean(xc * xc, axis=-1, keepdims=True)
    return xc * lax.rsqrt(var + EPS) * g + b


def _dot(a, b):
    return jnp.dot(a, b, preferred_element_type=F32)


def _dot_nt(a, b):
    return lax.dot_general(a, b, (((1,), (1,)), ((), ())), preferred_element_type=F32)


FFN_CHUNK = 1024


def _ffn_kernel(x_ref, g_ref, wu_ref, wd_ref, *rest, final):
    x = x_ref[...]
    h = _rms(x, g_ref[...]).astype(BF16)
    y = x
    for c in range(D_FF // FFN_CHUNK):
        cols = slice(c * FFN_CHUNK, (c + 1) * FFN_CHUNK)
        u = _dot(h, wu_ref[:, cols])
        u = jnp.square(jnp.maximum(u, 0.0)).astype(BF16)
        y = y + _dot(u, wd_ref[cols, :])
    if final:
        gf_ref, o_ref = rest
        o_ref[...] = _rms(y, gf_ref[...])
    else:
        (o_ref,) = rest
        o_ref[...] = y


def _ffn(x, g, w_up, w_down, g_final=None, *, tm):
    n = x.shape[0]
    row = pl.BlockSpec((tm, D_MODEL), lambda i: (i, 0))
    in_specs = [row, _const_spec((1, D_MODEL)), _const_spec((D_MODEL, D_FF)),
                _const_spec((D_FF, D_MODEL))]
    args = [x, g, w_up, w_down]
    if g_final is not None:
        in_specs.append(_const_spec((1, D_MODEL)))
        args.append(g_final)
    return pl.pallas_call(
        functools.partial(_ffn_kernel, final=g_final is not None),
        out_shape=jax.ShapeDtypeStruct((n, D_MODEL), F32),
        grid=(n // tm,),
        in_specs=in_specs,
        out_specs=row,
        compiler_params=_params("parallel"),
        name="ffn_final" if g_final is not None else "ffn",
    )(*args)


def _norm_proj_kernel(x_ref, g_ref, w_ref, o_ref, *, glu):
    h = _rms(x_ref[...], g_ref[...]).astype(BF16)
    y = _dot(h, w_ref[...])
    if glu:
        half = y.shape[1] // 2
        y = y[:, :half] * jax.nn.sigmoid(y[:, half:])
    o_ref[...] = y


def _norm_proj(x, g, w, *, glu=False):
    n = x.shape[0]
    n_out = w.shape[1] // 2 if glu else w.shape[1]
    return pl.pallas_call(
        functools.partial(_norm_proj_kernel, glu=glu),
        out_shape=jax.ShapeDtypeStruct((n, n_out), F32),
        grid=(1,),
        in_specs=[_const_spec(x.shape), _const_spec(g.shape), _const_spec(w.shape)],
        out_specs=_const_spec((n, n_out)),
        compiler_params=_params("arbitrary"),
        name="norm_proj_glu" if glu else "norm_proj",
    )(x, g, w)


def _proj_res_kernel(x_ref, a_ref, w_ref, *rest, conv_tail):
    a = a_ref[...]
    if conv_tail:
        lg_ref, lb_ref, o_ref = rest
        a = _layer_norm(a, lg_ref[...], lb_ref[...])
        a = a * jax.nn.sigmoid(a)
    else:
        (o_ref,) = rest
    o_ref[...] = x_ref[...] + _dot(a.astype(BF16), w_ref[...])


def _proj_res(x, a, w, ln=None):
    args = [x, a, w] + (list(ln) if ln is not None else [])
    return pl.pallas_call(
        functools.partial(_proj_res_kernel, conv_tail=ln is not None),
        out_shape=jax.ShapeDtypeStruct(x.shape, F32),
        grid=(1,),
        in_specs=[_const_spec(t.shape) for t in args],
        out_specs=_const_spec(x.shape),
        compiler_params=_params("arbitrary"),
        name="proj_res_ln" if ln is not None else "proj_res",
    )(*args)


def _t5_bucket(dist):
    n = jnp.maximum(dist, 0)
    max_exact = N_BUCKETS // 2
    nf = jnp.maximum(n, 1).astype(F32)
    large = max_exact + (jnp.log(nf / max_exact) / math.log(MAX_DISTANCE / max_exact)
                         * (N_BUCKETS - max_exact)).astype(jnp.int32)
    large = jnp.minimum(large, N_BUCKETS - 1)
    return jnp.where(n < max_exact, n, large)


def _distance_bias(rel_bias):
    buckets = _t5_bucket(jnp.arange(WINDOW + 1, dtype=jnp.int32))
    onehot = (buckets[:, None] == jnp.arange(N_BUCKETS, dtype=jnp.int32)[None, :]).astype(F32)
    return jnp.dot(onehot, rel_bias.astype(F32), precision=lax.Precision.HIGHEST)


def _prompt_bias_tables(dist_bias, sinks):
    period = 3 * WINDOW
    line = jnp.concatenate([jnp.broadcast_to(dist_bias[WINDOW:], (WINDOW - 1, N_HEADS)),
                            dist_bias[::-1],
                            jnp.broadcast_to(dist_bias[:1], (WINDOW, N_HEADS))]).T
    skew = jnp.tile(line, (1, WINDOW))[:, :WINDOW * (period - 1)]
    per_head = skew.reshape(N_HEADS, WINDOW, period - 1)[:, :, WINDOW - 1:period - 1]
    per_head = per_head.reshape(N_KV_HEADS, 2, 2, WINDOW, 2 * WINDOW)
    bias = per_head.transpose(0, 2, 4, 1, 3).reshape(N_KV_HEADS, 4 * WINDOW, 2 * WINDOW)

    qi = jnp.arange(WINDOW, dtype=jnp.int32)[None, :]
    kj = jnp.arange(2 * WINDOW, dtype=jnp.int32)[:, None]
    dist = qi - kj + WINDOW
    allowed = (dist >= 0) & (dist <= WINDOW)
    first = allowed & (kj >= WINDOW)
    cap = jnp.stack([jnp.where(allowed, F32_MAX, NEG_INF), jnp.where(first, F32_MAX, NEG_INF)])
    cap = jnp.tile(cap.astype(F32), (1, 2, 2))

    sink_rows = sinks.astype(F32).reshape(N_KV_HEADS, 2, 2).transpose(0, 2, 1).reshape(2 * N_KV_HEADS, 2)
    sink_rows = jnp.repeat(sink_rows, WINDOW, axis=1)
    return bias, cap, sink_rows


def _attn_prompt_kernel(x_ref, g_ref, wqkv_ref, wo_ref, bias_ref, cap_ref, sink_ref,
                        o_ref, kc_ref, vc_ref,
                        q_scr, o_scr, klo_scr, khi_scr, vt_scr, *, tq):
    i = pl.program_id(1)
    n_i = pl.num_programs(1)
    x = x_ref[...]
    h = _rms(x, g_ref[...]).astype(BF16)
    qkv = _dot(h, wqkv_ref[...])
    q_scr[...] = (qkv[:, :D_MODEL] * ATTN_SCALE).astype(BF16)
    k = qkv[:, D_MODEL:D_MODEL + KV_DIM]
    v = qkv[:, D_MODEL + KV_DIM:]

    @pl.when(i == n_i - 1)
    def _():
        kc_ref[...] = k[tq - WINDOW:, :]
        vc_ref[...] = v[tq - WINDOW:, :]

    @pl.when(i == 0)
    def _():
        klo_scr[:, 0:WINDOW, :] = jnp.zeros((N_KV_HEADS, WINDOW, LANES), BF16)
        khi_scr[:, 0:WINDOW, :] = jnp.zeros((N_KV_HEADS, WINDOW, LANES), BF16)
        vt_scr[:, :, 0:WINDOW] = jnp.zeros((N_KV_HEADS, HEAD_DIM, WINDOW), BF16)

    @pl.when(i > 0)
    def _():
        klo_scr[:, 0:WINDOW, :] = klo_scr[:, tq:tq + WINDOW, :]
        khi_scr[:, 0:WINDOW, :] = khi_scr[:, tq:tq + WINDOW, :]
        vt_scr[:, :, 0:WINDOW] = vt_scr[:, :, tq:tq + WINDOW]

    low = lax.broadcasted_iota(jnp.int32, (tq, LANES), 1) < HEAD_DIM
    for c in range(KV_DIM // LANES):
        kc = k[:, c * LANES:(c + 1) * LANES]
        kr = pltpu.roll(kc, HEAD_DIM, axis=1)
        klo_scr[2 * c, WINDOW:, :] = jnp.where(low, kc, 0.0).astype(BF16)
        khi_scr[2 * c, WINDOW:, :] = jnp.where(low, 0.0, kr).astype(BF16)
        klo_scr[2 * c + 1, WINDOW:, :] = jnp.where(low, kr, 0.0).astype(BF16)
        khi_scr[2 * c + 1, WINDOW:, :] = jnp.where(low, 0.0, kc).astype(BF16)
        vt = v[:, c * LANES:(c + 1) * LANES].T.astype(BF16)
        vt_scr[2 * c, :, WINDOW:] = vt[:HEAD_DIM]
        vt_scr[2 * c + 1, :, WINDOW:] = vt[HEAD_DIM:]

    is_first = jnp.where(i == 0, 1, 0)

    def scores(jb, kh):
        rows = slice(jb * WINDOW, (jb + 1) * WINDOW)
        band = slice(jb * WINDOW, (jb + 2) * WINDOW)
        cap = cap_ref[is_first] if jb == 0 else cap_ref[0]
        qst = jnp.concatenate([q_scr[rows, (2 * kh) * LANES:(2 * kh + 1) * LANES],
                               q_scr[rows, (2 * kh + 1) * LANES:(2 * kh + 2) * LANES]], axis=0)
        kst = jnp.concatenate([klo_scr[kh, band, :], khi_scr[kh, band, :]], axis=0)
        return jnp.minimum(_dot_nt(kst, qst) + bias_ref[kh], cap)

    def attend(jb, kh, s):
        rows = slice(jb * WINDOW, (jb + 1) * WINDOW)
        vt = vt_scr[kh, :, jb * WINDOW:(jb + 2) * WINDOW]
        halves = []
        for half in range(2):
            sh = s[half * 2 * WINDOW:(half + 1) * 2 * WINDOW]
            sink = sink_ref[2 * kh + half:2 * kh + half + 1, :]
            m = jnp.maximum(jnp.max(sh, axis=0, keepdims=True), sink)
            p = jnp.exp(sh - m)
            denom = jnp.sum(p, axis=0, keepdims=True) + jnp.exp(sink - m)
            halves.append(_dot(vt, p.astype(BF16)) * (1.0 / denom))
        o = jnp.concatenate(halves, axis=0).T.astype(BF16)
        o_scr[rows, (2 * kh) * LANES:(2 * kh + 1) * LANES] = o[:WINDOW]
        o_scr[rows, (2 * kh + 1) * LANES:(2 * kh + 2) * LANES] = o[WINDOW:]

    steps = [(jb, kh) for jb in range(tq // WINDOW) for kh in range(N_KV_HEADS)]
    s_next = scores(*steps[0])
    for n, step in enumerate(steps):
        s_cur = s_next
        if n + 1 < len(steps):
            s_next = scores(*steps[n + 1])
        attend(*step, s_cur)

    o_ref[...] = x + _dot(o_scr[...], wo_ref[...])


def _attn_prompt(x, g, w_qkv, w_o, bias, cap, sink_rows, *, tq):
    b, s, _ = x.shape
    row = pl.BlockSpec((None, tq, D_MODEL), lambda bi, i: (bi, i, 0))
    cache = pl.BlockSpec((None, WINDOW, KV_DIM), lambda bi, i: (bi, 0, 0))
    k_scr = pltpu.VMEM((N_KV_HEADS, WINDOW + tq, LANES), BF16)
    return pl.pallas_call(
        functools.partial(_attn_prompt_kernel, tq=tq),
        out_shape=(jax.ShapeDtypeStruct(x.shape, F32),
                   jax.ShapeDtypeStruct((b, WINDOW, KV_DIM), F32),
                   jax.ShapeDtypeStruct((b, WINDOW, KV_DIM), F32)),
        grid=(b, s // tq),
        in_specs=[row, _const_spec((1, D_MODEL)),
                  _const_spec((D_MODEL, QKV_DIM)), _const_spec((D_MODEL, D_MODEL)),
                  _const_spec(bias.shape), _const_spec(cap.shape), _const_spec(sink_rows.shape)],
        out_specs=(row, cache, cache),
        scratch_shapes=[pltpu.VMEM((tq, D_MODEL), BF16), pltpu.VMEM((tq, D_MODEL), BF16),
                        k_scr, k_scr, pltpu.VMEM((N_KV_HEADS, HEAD_DIM, WINDOW + tq), BF16)],
        compiler_params=_params("parallel", "arbitrary"),
        name="attn_prompt",
    )(x, g, w_qkv, w_o, bias, cap, sink_rows)


KEYS_PAD = WINDOW + 8
ROWS_PAD = 8 * Q_PER_KV


def _sample_bias_table(dist_bias):
    tab = jnp.concatenate([dist_bias[::-1].T,
                           jnp.full((N_HEADS, KEYS_PAD - WINDOW - 1), NEG_INF, F32)], axis=1)
    tab = tab.reshape(N_KV_HEADS, Q_PER_KV, KEYS_PAD).transpose(1, 0, 2)
    tab = jnp.pad(tab, ((0, 0), (0, 8 - N_KV_HEADS), (0, 0)))
    return tab.reshape(ROWS_PAD, KEYS_PAD)


def _attn_sample_kernel(q_ref, kn_ref, vn_ref, kc_ref, vc_ref, bias_ref, sink_ref,
                        o_ref, ko_ref, vo_ref, *, tb):
    ko_ref[:, 0:WINDOW - 1, :] = kc_ref[:, 1:WINDOW, :]
    vo_ref[:, 0:WINDOW - 1, :] = vc_ref[:, 1:WINDOW, :]
    sub = lax.broadcasted_iota(jnp.int32, (8, KV_DIM), 0)
    lane_head = lax.broadcasted_iota(jnp.int32, (8, KV_DIM), 1) // HEAD_DIM
    own = jnp.logical_and(sub < N_KV_HEADS, lane_head == sub)
    first_row = sub == 0
    bias = bias_ref[...]
    sink = sink_ref[...]

    def one(b, carry):
        k_new = kn_ref[pl.ds(b, 1), :]
        v_new = vn_ref[pl.ds(b, 1), :]
        ko_ref[b, WINDOW - 1:WINDOW, :] = k_new
        vo_ref[b, WINDOW - 1:WINDOW, :] = v_new
        q_rows = []
        for gq in range(Q_PER_KV):
            q_g = q_ref[pl.ds(b, 1), gq * KV_DIM:(gq + 1) * KV_DIM] * ATTN_SCALE
            q_rows.append(jnp.where(own, jnp.broadcast_to(q_g, (8, KV_DIM)), 0.0))
        q_blk = jnp.concatenate(q_rows, axis=0).astype(BF16)
        k_ext = jnp.concatenate(
            [kc_ref[b], jnp.where(first_row, jnp.broadcast_to(k_new, (8, KV_DIM)), 0.0)],
            axis=0).astype(BF16)
        v_ext = jnp.concatenate(
            [vc_ref[b], jnp.where(first_row, jnp.broadcast_to(v_new, (8, KV_DIM)), 0.0)],
            axis=0).astype(BF16)
        s = _dot_nt(q_blk, k_ext) + bias
        m = jnp.maximum(jnp.max(s, axis=-1, keepdims=True), sink)
        p = jnp.exp(s - m)
        denom = jnp.sum(p, axis=-1, keepdims=True) + jnp.exp(sink - m)
        o = _dot(p.astype(BF16), v_ext) / denom
        for gq in range(Q_PER_KV):
            o_g = jnp.sum(jnp.where(own, o[8 * gq:8 * gq + 8], 0.0), axis=0, keepdims=True)
            o_ref[pl.ds(b, 1), gq * KV_DIM:(gq + 1) * KV_DIM] = o_g
        return carry

    lax.fori_loop(0, tb, one, 0)


def _attn_sample_core(qkv, k_cache, v_cache, bias, sink_col, *, tb):
    n = qkv.shape[0]
    cache = pl.BlockSpec((tb, WINDOW, KV_DIM), lambda t: (t, 0, 0))
    kv_col = D_MODEL // KV_DIM
    return pl.pallas_call(
        functools.partial(_attn_sample_kernel, tb=tb),
        out_shape=(jax.ShapeDtypeStruct((n, D_MODEL), F32),
                   jax.ShapeDtypeStruct(k_cache.shape, F32),
                   jax.ShapeDtypeStruct(v_cache.shape, F32)),
        grid=(n // tb,),
        in_specs=[pl.BlockSpec((tb, D_MODEL), lambda t: (t, 0)),
                  pl.BlockSpec((tb, KV_DIM), lambda t: (t, kv_col)),
                  pl.BlockSpec((tb, KV_DIM), lambda t: (t, kv_col + 1)),
                  cache, cache, _const_spec(bias.shape), _const_spec(sink_col.shape)],
        out_specs=(pl.BlockSpec((tb, D_MODEL), lambda t: (t, 0)), cache, cache),
        compiler_params=_params("parallel"),
        name="attn_sample",
    )(qkv, qkv, qkv, k_cache, v_cache, bias, sink_col)


def _sgu_kernel(x_ref, g_ref, win_ref, lg_ref, lb_ref, sp_ref, bsp_ref, wout_ref, *rest,
                tm, sample):
    x = x_ref[...]
    h = _rms(x, g_ref[...]).astype(BF16)
    z = _dot(h, win_ref[...])
    z = 0.5 * z * (1.0 + lax.erf(z * INV_SQRT2))
    u = z[:, :D_MODEL]
    v = _layer_norm(z[:, D_MODEL:], lg_ref[...], lb_ref[...])
    if sample:
        o_ref, v_ref = rest
        v_ref[...] = v
        gated = (u * (v * sp_ref[...] + bsp_ref[...])).astype(BF16)
    else:
        o_ref, gated_scr = rest
        vb = v.astype(BF16)
        for c in range(tm // CHUNK):
            rows = slice(c * CHUNK, (c + 1) * CHUNK)
            for gi in range(SGU_GROUPS):
                cols = slice(gi * LANES, (gi + 1) * LANES)
                mixed = _dot(sp_ref[gi], vb[rows, cols]) + bsp_ref[gi]
                gated_scr[rows, cols] = (u[rows, cols] * mixed).astype(BF16)
        gated = gated_scr[...]
    o_ref[...] = x + _dot(gated, wout_ref[...])


def _sgu(x, g, w_in, ln_g, ln_b, sp, bsp, w_out, *, tm, sample):
    n = x.shape[0]
    row = pl.BlockSpec((tm, D_MODEL), lambda i: (i, 0))
    args = [x, g, w_in, ln_g, ln_b, sp, bsp, w_out]
    in_specs = [row] + [_const_spec(t.shape) for t in args[1:]]
    if sample:
        out_shape = (jax.ShapeDtypeStruct((n, D_MODEL), F32),) * 2
        out_specs = (row, row)
        scratch = []
    else:
        out_shape = jax.ShapeDtypeStruct((n, D_MODEL), F32)
        out_specs = row
        scratch = [pltpu.VMEM((tm, D_MODEL), BF16)]
    return pl.pallas_call(
        functools.partial(_sgu_kernel, tm=tm, sample=sample),
        out_shape=out_shape,
        grid=(n // tm,),
        in_specs=in_specs,
        out_specs=out_specs,
        scratch_shapes=scratch,
        compiler_params=_params("parallel"),
        name="sgu_sample" if sample else "sgu_prompt",
    )(*args)


CONV_ROWS = 64
CONV_LANES = 256


def _conv_prompt_kernel(x_ref, xprev_ref, g_ref, win_ref, wdw_ref, bdw_ref, lg_ref, lb_ref,
                        wout_ref, o_ref, tail_ref, a0_scr, a1_scr, sh_scr, c_scr,
                        *, tm, tiles_per_seq):
    s = pl.program_id(0)

    @pl.when(s == 0)
    def _():
        a0_scr[...] = jnp.zeros(a0_scr.shape, F32)
        a1_scr[...] = jnp.zeros(a1_scr.shape, F32)

    first_tap = CONV_HALO - (CONV_WIDTH - 1)
    reps = CONV_ROWS // SUBLANES

    def step(a_new, a_old):
        for r in range(1, SUBLANES):
            sh_scr[r - 1] = a_old[r:r + tm + CONV_HALO - SUBLANES, :]

        h = _rms(x_ref[...], g_ref[...]).astype(BF16)
        starts_sequence = s % tiles_per_seq == 0
        a_new[0:CONV_HALO, :] = jnp.where(starts_sequence, 0.0, a_old[tm:tm + CONV_HALO, :])

        def project(jc):
            cols = slice(jc * CONV_LANES, (jc + 1) * CONV_LANES)
            gate = slice(D_MODEL + jc * CONV_LANES, D_MODEL + (jc + 1) * CONV_LANES)
            a = _dot(h, win_ref[:, cols]) * jax.nn.sigmoid(_dot(h, win_ref[:, gate]))
            a_new[CONV_HALO:, cols] = a
            tail_ref[:, cols] = a[tm - CONV_HALO:, :]

        def convolve(rb):
            for lc in range(D_MODEL // CONV_LANES):
                cols = slice(lc * CONV_LANES, (lc + 1) * CONV_LANES)
                acc = jnp.concatenate([bdw_ref[:, cols]] * reps, axis=0)
                for kk in range(CONV_WIDTH):
                    whole, r = divmod(first_tap + kk, SUBLANES)
                    start = rb * CONV_ROWS + whole * SUBLANES
                    src = a_old if r == 0 else sh_scr.at[r - 1]
                    w = jnp.concatenate([wdw_ref[kk, :, cols]] * reps, axis=0)
                    acc = acc + src[start:start + CONV_ROWS, cols] * w
                c_scr[rb * CONV_ROWS:(rb + 1) * CONV_ROWS, cols] = acc

        n_proj = D_MODEL // CONV_LANES
        n_conv = tm // CONV_ROWS
        for jc in range(n_proj):
            project(jc)
            for rb in range(jc * n_conv // n_proj, (jc + 1) * n_conv // n_proj):
                convolve(rb)
        c = _layer_norm(c_scr[...], lg_ref[...], lb_ref[...])
        c = (c * jax.nn.sigmoid(c)).astype(BF16)
        o_ref[...] = xprev_ref[...] + _dot(c, wout_ref[...])

    @pl.when(s % 2 == 0)
    def _():
        step(a0_scr, a1_scr)

    @pl.when(s % 2 == 1)
    def _():
        step(a1_scr, a0_scr)


def _conv_prompt(x, g, w_in, w_dw8, b_dw8, ln_g, ln_b, w_out, *, tm):
    b, seq, _ = x.shape
    tiles_per_seq = seq // tm
    n = b * tiles_per_seq
    consts = [g, w_in, w_dw8, b_dw8, ln_g, ln_b, w_out]
    a_scr = pltpu.VMEM((CONV_HALO + tm, D_MODEL), F32)
    x2 = x.reshape(b * seq, D_MODEL)
    out, tail = pl.pallas_call(
        functools.partial(_conv_prompt_kernel, tm=tm, tiles_per_seq=tiles_per_seq),
        out_shape=(jax.ShapeDtypeStruct((b * seq, D_MODEL), F32),
                   jax.ShapeDtypeStruct((b, CONV_HALO, D_MODEL), F32)),
        grid=(n + 1,),
        in_specs=[pl.BlockSpec((tm, D_MODEL), lambda s: (jnp.minimum(s, n - 1), 0)),
                  pl.BlockSpec((tm, D_MODEL), lambda s: (jnp.maximum(s - 1, 0), 0))]
                 + [_const_spec(t.shape) for t in consts],
        out_specs=(pl.BlockSpec((tm, D_MODEL), lambda s: (jnp.maximum(s - 1, 0), 0)),
                   pl.BlockSpec((None, CONV_HALO, D_MODEL),
                                lambda s: (jnp.minimum(s, n - 1) // tiles_per_seq, 0, 0))),
        scratch_shapes=[a_scr, a_scr,
                        pltpu.VMEM((SUBLANES - 1, CONV_HALO + tm - SUBLANES, D_MODEL), F32),
                        pltpu.VMEM((tm, D_MODEL), F32)],
        compiler_params=_params("arbitrary"),
        name="conv_prompt",
    )(x2, x2, *consts)
    return out.reshape(b, seq, D_MODEL), tail


def _conv_sample_kernel(a_ref, st_ref, wdw_ref, bdw_ref, c_ref, so_ref, *, tb):
    n_hist = CONV_WIDTH - 1
    st = st_ref[...]
    hist = jnp.sum(st * wdw_ref[0:n_hist, :][None], axis=1)
    c_ref[...] = hist + a_ref[...] * wdw_ref[n_hist:CONV_WIDTH, :] + bdw_ref[...]
    so_ref[:, 0:n_hist - 1, :] = st_ref[:, 1:n_hist, :]

    def one(b, carry):
        so_ref[b, n_hist - 1:n_hist, :] = a_ref[pl.ds(b, 1), :]
        return carry

    lax.fori_loop(0, tb, one, 0)


def _conv_sample_core(a, state, w_dw, b_dw, *, tb):
    n = a.shape[0]
    row = pl.BlockSpec((tb, D_MODEL), lambda t: (t, 0))
    st = pl.BlockSpec((tb, CONV_WIDTH - 1, D_MODEL), lambda t: (t, 0, 0))
    return pl.pallas_call(
        functools.partial(_conv_sample_kernel, tb=tb),
        out_shape=(jax.ShapeDtypeStruct((n, D_MODEL), F32),
                   jax.ShapeDtypeStruct(state.shape, F32)),
        grid=(n // tb,),
        in_specs=[row, st, _const_spec(w_dw.shape), _const_spec(b_dw.shape)],
        out_specs=(row, st),
        compiler_params=_params("parallel"),
        name="conv_sample",
    )(a, state, w_dw, b_dw)


PROMPT_TILE = 512
SAMPLE_ATTN_TILE = 16
SAMPLE_CONV_TILE = 16


def kernel(x_prompt, x_sample, cache_swa_k, cache_swa_v, state_conv, rel_bias, norm_mix, norm_ffn, norm_final, attn_w_qkv, attn_w_o, attn_sinks, sgu_w_in, sgu_ln_g, sgu_ln_b, sgu_w_spatial, sgu_b_spatial, sgu_w_out, conv_w_in, conv_w_dw, conv_b_dw, conv_ln_g, conv_ln_b, conv_w_out, ffn_w_up, ffn_w_down):
    batch, seq, _ = x_prompt.shape
    dec = x_sample.shape[0]
    depth = norm_mix.shape[0]
    mixer_of_layer = tuple(i % 3 for i in range(depth))
    slot_of_layer = tuple(mixer_of_layer[:i].count(mixer_of_layer[i]) for i in range(depth))

    def row(v):
        return v.reshape(1, -1).astype(F32)

    def regroup(w, axis):
        shape = w.shape[:axis] + (N_KV_HEADS, Q_PER_KV, HEAD_DIM) + w.shape[axis + 1:]
        return jnp.swapaxes(w.reshape(shape), axis, axis + 1).reshape(w.shape)

    xp = x_prompt
    xs = x_sample.reshape(dec, D_MODEL)
    dist_bias = _distance_bias(rel_bias)
    bias_s = _sample_bias_table(dist_bias)

    kp, vp, ksm, vsm, sgu_v_new, convp, convs = [], [], [], [], [], [], []
    for i in range(depth):
        m, j = mixer_of_layer[i], slot_of_layer[i]
        g_mix = row(norm_mix[i])
        if m == 0:
            w_qkv = attn_w_qkv[j].astype(BF16)
            w_o = attn_w_o[j].astype(BF16)
            bias_p, cap_p, sink_rows = _prompt_bias_tables(dist_bias, attn_sinks[j])
            xp, k1, v1 = _attn_prompt(xp, g_mix, w_qkv, w_o, bias_p, cap_p, sink_rows,
                                      tq=PROMPT_TILE)
            kp.append(k1.reshape(batch, WINDOW, N_KV_HEADS, HEAD_DIM))
            vp.append(v1.reshape(batch, WINDOW, N_KV_HEADS, HEAD_DIM))

            w_qkv_s = jnp.concatenate([regroup(w_qkv[:, :D_MODEL], 1), w_qkv[:, D_MODEL:]], axis=1)
            sink_col = attn_sinks[j].astype(F32).reshape(N_KV_HEADS, Q_PER_KV).T
            sink_col = jnp.pad(sink_col, ((0, 0), (0, 8 - N_KV_HEADS))).reshape(ROWS_PAD, 1)
            qkv_s = _norm_proj(xs, g_mix, w_qkv_s)
            o_s, k2, v2 = _attn_sample_core(
                qkv_s, cache_swa_k[j].reshape(dec, WINDOW, KV_DIM),
                cache_swa_v[j].reshape(dec, WINDOW, KV_DIM), bias_s, sink_col, tb=SAMPLE_ATTN_TILE)
            xs = _proj_res(xs, o_s, regroup(w_o, 0))
            ksm.append(k2.reshape(dec, WINDOW, N_KV_HEADS, HEAD_DIM))
            vsm.append(v2.reshape(dec, WINDOW, N_KV_HEADS, HEAD_DIM))
        elif m == 1:
            w_in = sgu_w_in[j].astype(BF16)
            w_out = sgu_w_out[j].astype(BF16)
            ln_g, ln_b = row(sgu_ln_g[j]), row(sgu_ln_b[j])
            sp = jnp.tril(sgu_w_spatial[j]).astype(BF16)
            bsp = jnp.broadcast_to(sgu_b_spatial[j].astype(F32)[:, :, None],
                                   (SGU_GROUPS, CHUNK, LANES))
            xp = _sgu(xp.reshape(batch * seq, D_MODEL), g_mix, w_in, ln_g, ln_b, sp, bsp, w_out,
                      tm=PROMPT_TILE, sample=False).reshape(batch, seq, D_MODEL)
            sp0 = row(jnp.repeat(sgu_w_spatial[j][:, 0, 0], LANES))
            bsp0 = row(jnp.repeat(sgu_b_spatial[j][:, 0], LANES))
            xs, v_rows = _sgu(xs, g_mix, w_in, ln_g, ln_b, sp0, bsp0, w_out, tm=dec, sample=True)
            sgu_v_new.append(v_rows.reshape(dec, 1, D_MODEL))
        else:
            w_in = conv_w_in[j].astype(BF16)
            w_out = conv_w_out[j].astype(BF16)
            w_dw = conv_w_dw[j].astype(F32)
            b_dw = row(conv_b_dw[j])
            ln_g, ln_b = row(conv_ln_g[j]), row(conv_ln_b[j])
            w_dw8 = jnp.broadcast_to(w_dw[:, None, :], (CONV_WIDTH, SUBLANES, D_MODEL))
            b_dw8 = jnp.broadcast_to(b_dw, (SUBLANES, D_MODEL))
            xp, tail = _conv_prompt(xp, g_mix, w_in, w_dw8, b_dw8, ln_g, ln_b, w_out,
                                    tm=PROMPT_TILE)
            convp.append(tail[:, CONV_HALO - (CONV_WIDTH - 1):, :])
            a_s = _norm_proj(xs, g_mix, w_in, glu=True)
            c_s, st2 = _conv_sample_core(a_s, state_conv[j].astype(F32), w_dw, b_dw,
                                         tb=SAMPLE_CONV_TILE)
            xs = _proj_res(xs, c_s, w_out, ln=(ln_g, ln_b))
            convs.append(st2)

        g_ffn = row(norm_ffn[i])
        w_up = ffn_w_up[i].astype(BF16)
        w_down = ffn_w_down[i].astype(BF16)
        g_fin = row(norm_final) if i == depth - 1 else None
        xp = _ffn(xp.reshape(batch * seq, D_MODEL), g_ffn, w_up, w_down, g_fin,
                  tm=PROMPT_TILE).reshape(batch, seq, D_MODEL)
        xs = _ffn(xs, g_ffn, w_up, w_down, g_fin, tm=dec)

    y_prompt = xp
    y_sample = xs.reshape(dec, 1, D_MODEL)
    return (y_prompt, y_sample, jnp.stack(kp), jnp.stack(vp), jnp.stack(ksm), jnp.stack(vsm),
            jnp.stack(sgu_v_new), jnp.stack(convp), jnp.stack(convs))
```

```python
import functools
import math

import jax
import jax.numpy as jnp
from jax import lax
from jax.experimental import pallas as pl
from jax.experimental.pallas import tpu as pltpu

D_MODEL = 1024
HEAD_DIM = 64
N_HEADS = 16
N_KV_HEADS = 4
Q_PER_KV = 4
KV_DIM = N_KV_HEADS * HEAD_DIM
QKV_DIM = D_MODEL + 2 * KV_DIM
WINDOW = 128
ATTN_SCALE = HEAD_DIM ** -0.5
N_BUCKETS = 32
MAX_DISTANCE = 128
CHUNK = 128
SGU_GROUPS = 8
CONV_WIDTH = 31
CONV_HALO = 32
D_FF = 4 * D_MODEL
EPS = 1e-6
NEG_INF = -1e30
F32_MAX = float(jnp.finfo(jnp.float32).max)
INV_SQRT2 = 1.0 / math.sqrt(2.0)

LANES = 128
SUBLANES = 8
VMEM_LIMIT = 56 * 1024 * 1024

F32 = jnp.float32
BF16 = jnp.bfloat16


def _const_spec(shape):
    n = len(shape)
    return pl.BlockSpec(shape, lambda *_: (0,) * n, pipeline_mode=pl.Buffered(1))


def _params(*sem):
    return pltpu.CompilerParams(dimension_semantics=sem, vmem_limit_bytes=VMEM_LIMIT)


def _rms(x, g):
    return x * lax.rsqrt(jnp.mean(x * x, axis=-1, keepdims=True) + EPS) * g


def _layer_norm(x, g, b):
    mu = jnp.mean(x, axis=-1, keepdims=True)
    xc = x - mu
    var = jnp.mean(xc * xc, axis=-1, keepdims=True)
    return xc * lax.rsqrt(var + EPS) * g + b


def _dot(a, b):
    return jnp.dot(a, b, preferred_element_type=F32)


def _dot_nt(a, b):
    return lax.dot_general(a, b, (((1,), (1,)), ((), ())), preferred_element_type=F32)


FFN_CHUNK = 1024


def _ffn_kernel(x_ref, g_ref, wu_ref, wd_ref, *rest, final):
    x = x_ref[...]
    h = _rms(x, g_ref[...]).astype(BF16)
    y = x
    for c in range(D_FF // FFN_CHUNK):
        cols = slice(c * FFN_CHUNK, (c + 1) * FFN_CHUNK)
        u = _dot(h, wu_ref[:, cols])
        u = jnp.square(jnp.maximum(u, 0.0)).astype(BF16)
        y = y + _dot(u, wd_ref[cols, :])
    if final:
        gf_ref, o_ref = rest
        o_ref[...] = _rms(y, gf_ref[...])
    else:
        (o_ref,) = rest
        o_ref[...] = y


def _ffn(x, g, w_up, w_down, g_final=None, *, tm):
    n = x.shape[0]
    row = pl.BlockSpec((tm, D_MODEL), lambda i: (i, 0))
    in_specs = [row, _const_spec((1, D_MODEL)), _const_spec((D_MODEL, D_FF)),
                _const_spec((D_FF, D_MODEL))]
    args = [x, g, w_up, w_down]
    if g_final is not None:
        in_specs.append(_const_spec((1, D_MODEL)))
        args.append(g_final)
    return pl.pallas_call(
        functools.partial(_ffn_kernel, final=g_final is not None),
        out_shape=jax.ShapeDtypeStruct((n, D_MODEL), F32),
        grid=(n // tm,),
        in_specs=in_specs,
        out_specs=row,
        compiler_params=_params("parallel"),
        name="ffn_final" if g_final is not None else "ffn",
    )(*args)


def _norm_proj_kernel(x_ref, g_ref, w_ref, o_ref, *rest, glu, kv_t):
    h = _rms(x_ref[...], g_ref[...]).astype(BF16)
    y = _dot(h, w_ref[...])
    if glu:
        half = y.shape[1] // 2
        y = y[:, :half] * jax.nn.sigmoid(y[:, half:])
    o_ref[...] = y
    if kv_t:
        (t_ref,) = rest
        t_ref[...] = y[:, D_MODEL:].T


def _norm_proj(x, g, w, *, glu=False, kv_t=False):
    n = x.shape[0]
    n_out = w.shape[1] // 2 if glu else w.shape[1]
    out_shape = [jax.ShapeDtypeStruct((n, n_out), F32)]
    if kv_t:
        out_shape.append(jax.ShapeDtypeStruct((n_out - D_MODEL, n), F32))
    out = pl.pallas_call(
        functools.partial(_norm_proj_kernel, glu=glu, kv_t=kv_t),
        out_shape=tuple(out_shape),
        grid=(1,),
        in_specs=[_const_spec(x.shape), _const_spec(g.shape), _const_spec(w.shape)],
        out_specs=tuple(_const_spec(o.shape) for o in out_shape),
        compiler_params=_params("arbitrary"),
        name="norm_proj_glu" if glu else "norm_proj",
    )(x, g, w)
    return out if kv_t else out[0]


def _proj_res_kernel(x_ref, a_ref, w_ref, *rest, conv_tail):
    a = a_ref[...]
    if conv_tail:
        lg_ref, lb_ref, o_ref = rest
        a = _layer_norm(a, lg_ref[...], lb_ref[...])
        a = a * jax.nn.sigmoid(a)
    else:
        (o_ref,) = rest
    o_ref[...] = x_ref[...] + _dot(a.astype(BF16), w_ref[...])


def _proj_res(x, a, w, ln=None):
    args = [x, a, w] + (list(ln) if ln is not None else [])
    return pl.pallas_call(
        functools.partial(_proj_res_kernel, conv_tail=ln is not None),
        out_shape=jax.ShapeDtypeStruct(x.shape, F32),
        grid=(1,),
        in_specs=[_const_spec(t.shape) for t in args],
        out_specs=_const_spec(x.shape),
        compiler_params=_params("arbitrary"),
        name="proj_res_ln" if ln is not None else "proj_res",
    )(*args)


def _t5_bucket(dist):
    n = jnp.maximum(dist, 0)
    max_exact = N_BUCKETS // 2
    nf = jnp.maximum(n, 1).astype(F32)
    large = max_exact + (jnp.log(nf / max_exact) / math.log(MAX_DISTANCE / max_exact)
                         * (N_BUCKETS - max_exact)).astype(jnp.int32)
    large = jnp.minimum(large, N_BUCKETS - 1)
    return jnp.where(n < max_exact, n, large)


def _distance_bias(rel_bias):
    buckets = _t5_bucket(jnp.arange(WINDOW + 1, dtype=jnp.int32))
    onehot = (buckets[:, None] == jnp.arange(N_BUCKETS, dtype=jnp.int32)[None, :]).astype(F32)
    return jnp.dot(onehot, rel_bias.astype(F32), precision=lax.Precision.HIGHEST)


def _prompt_bias_tables(dist_bias, sinks):
    period = 3 * WINDOW
    line = jnp.concatenate([jnp.broadcast_to(dist_bias[WINDOW:], (WINDOW - 1, N_HEADS)),
                            dist_bias[::-1],
                            jnp.broadcast_to(dist_bias[:1], (WINDOW, N_HEADS))]).T
    skew = jnp.tile(line, (1, WINDOW))[:, :WINDOW * (period - 1)]
    per_head = skew.reshape(N_HEADS, WINDOW, period - 1)[:, :, WINDOW - 1:period - 1]
    per_head = per_head.reshape(N_KV_HEADS, 2, 2, WINDOW, 2 * WINDOW)
    bias = per_head.transpose(0, 2, 4, 1, 3).reshape(N_KV_HEADS, 4 * WINDOW, 2 * WINDOW)

    qi = jnp.arange(WINDOW, dtype=jnp.int32)[None, :]
    kj = jnp.arange(2 * WINDOW, dtype=jnp.int32)[:, None]
    dist = qi - kj + WINDOW
    allowed = (dist >= 0) & (dist <= WINDOW)
    first = allowed & (kj >= WINDOW)
    cap = jnp.stack([jnp.where(allowed, F32_MAX, NEG_INF), jnp.where(first, F32_MAX, NEG_INF)])
    cap = jnp.tile(cap.astype(F32), (1, 2, 2))

    sink_rows = sinks.astype(F32).reshape(N_KV_HEADS, 2, 2).transpose(0, 2, 1).reshape(2 * N_KV_HEADS, 2)
    sink_rows = jnp.repeat(sink_rows, WINDOW, axis=1)
    return bias, cap, sink_rows


def _attn_prompt_kernel(x_ref, g_ref, wqkv_ref, wo_ref, bias_ref, cap_ref, sink_ref,
                        o_ref, kc_ref, vc_ref,
                        q_scr, o_scr, klo_scr, khi_scr, vt_scr, *, tq):
    i = pl.program_id(1)
    n_i = pl.num_programs(1)
    x = x_ref[...]
    h = _rms(x, g_ref[...]).astype(BF16)
    qkv = _dot(h, wqkv_ref[...])
    q_scr[...] = (qkv[:, :D_MODEL] * ATTN_SCALE).astype(BF16)
    k = qkv[:, D_MODEL:D_MODEL + KV_DIM]
    v = qkv[:, D_MODEL + KV_DIM:]

    @pl.when(i == n_i - 1)
    def _():
        kc_ref[...] = k[tq - WINDOW:, :]
        vc_ref[...] = v[tq - WINDOW:, :]

    @pl.when(i == 0)
    def _():
        klo_scr[:, 0:WINDOW, :] = jnp.zeros((N_KV_HEADS, WINDOW, LANES), BF16)
        khi_scr[:, 0:WINDOW, :] = jnp.zeros((N_KV_HEADS, WINDOW, LANES), BF16)
        vt_scr[:, :, 0:WINDOW] = jnp.zeros((N_KV_HEADS, HEAD_DIM, WINDOW), BF16)

    @pl.when(i > 0)
    def _():
        klo_scr[:, 0:WINDOW, :] = klo_scr[:, tq:tq + WINDOW, :]
        khi_scr[:, 0:WINDOW, :] = khi_scr[:, tq:tq + WINDOW, :]
        vt_scr[:, :, 0:WINDOW] = vt_scr[:, :, tq:tq + WINDOW]

    low = lax.broadcasted_iota(jnp.int32, (tq, LANES), 1) < HEAD_DIM
    for c in range(KV_DIM // LANES):
        kc = k[:, c * LANES:(c + 1) * LANES]
        kr = pltpu.roll(kc, HEAD_DIM, axis=1)
        klo_scr[2 * c, WINDOW:, :] = jnp.where(low, kc, 0.0).astype(BF16)
        khi_scr[2 * c, WINDOW:, :] = jnp.where(low, 0.0, kr).astype(BF16)
        klo_scr[2 * c + 1, WINDOW:, :] = jnp.where(low, kr, 0.0).astype(BF16)
        khi_scr[2 * c + 1, WINDOW:, :] = jnp.where(low, 0.0, kc).astype(BF16)
        vt = v[:, c * LANES:(c + 1) * LANES].T.astype(BF16)
        vt_scr[2 * c, :, WINDOW:] = vt[:HEAD_DIM]
        vt_scr[2 * c + 1, :, WINDOW:] = vt[HEAD_DIM:]

    is_first = jnp.where(i == 0, 1, 0)

    def scores(jb, kh):
        rows = slice(jb * WINDOW, (jb + 1) * WINDOW)
        band = slice(jb * WINDOW, (jb + 2) * WINDOW)
        cap = cap_ref[is_first] if jb == 0 else cap_ref[0]
        qst = jnp.concatenate([q_scr[rows, (2 * kh) * LANES:(2 * kh + 1) * LANES],
                               q_scr[rows, (2 * kh + 1) * LANES:(2 * kh + 2) * LANES]], axis=0)
        kst = jnp.concatenate([klo_scr[kh, band, :], khi_scr[kh, band, :]], axis=0)
        return jnp.minimum(_dot_nt(kst, qst) + bias_ref[kh], cap)

    def attend(jb, kh, s):
        rows = slice(jb * WINDOW, (jb + 1) * WINDOW)
        vt = vt_scr[kh, :, jb * WINDOW:(jb + 2) * WINDOW]
        halves = []
        for half in range(2):
            sh = s[half * 2 * WINDOW:(half + 1) * 2 * WINDOW]
            sink = sink_ref[2 * kh + half:2 * kh + half + 1, :]
            m = jnp.maximum(jnp.max(sh, axis=0, keepdims=True), sink)
            p = jnp.exp(sh - m)
            denom = jnp.sum(p, axis=0, keepdims=True) + jnp.exp(sink - m)
            halves.append(_dot(vt, p.astype(BF16)) * (1.0 / denom))
        o = jnp.concatenate(halves, axis=0).T.astype(BF16)
        o_scr[rows, (2 * kh) * LANES:(2 * kh + 1) * LANES] = o[:WINDOW]
        o_scr[rows, (2 * kh + 1) * LANES:(2 * kh + 2) * LANES] = o[WINDOW:]

    steps = [(jb, kh) for jb in range(tq // WINDOW) for kh in range(N_KV_HEADS)]
    s_next = scores(*steps[0])
    for n, step in enumerate(steps):
        s_cur = s_next
        if n + 1 < len(steps):
            s_next = scores(*steps[n + 1])
        attend(*step, s_cur)

    o_ref[...] = x + _dot(o_scr[...], wo_ref[...])


def _attn_prompt(x, g, w_qkv, w_o, bias, cap, sink_rows, *, tq):
    b, s, _ = x.shape
    row = pl.BlockSpec((None, tq, D_MODEL), lambda bi, i: (bi, i, 0))
    cache = pl.BlockSpec((None, WINDOW, KV_DIM), lambda bi, i: (bi, 0, 0))
    k_scr = pltpu.VMEM((N_KV_HEADS, WINDOW + tq, LANES), BF16)
    return pl.pallas_call(
        functools.partial(_attn_prompt_kernel, tq=tq),
        out_shape=(jax.ShapeDtypeStruct(x.shape, F32),
                   jax.ShapeDtypeStruct((b, WINDOW, KV_DIM), F32),
                   jax.ShapeDtypeStruct((b, WINDOW, KV_DIM), F32)),
        grid=(b, s // tq),
        in_specs=[row, _const_spec((1, D_MODEL)),
                  _const_spec((D_MODEL, QKV_DIM)), _const_spec((D_MODEL, D_MODEL)),
                  _const_spec(bias.shape), _const_spec(cap.shape), _const_spec(sink_rows.shape)],
        out_specs=(row, cache, cache),
        scratch_shapes=[pltpu.VMEM((tq, D_MODEL), BF16), pltpu.VMEM((tq, D_MODEL), BF16),
                        k_scr, k_scr, pltpu.VMEM((N_KV_HEADS, HEAD_DIM, WINDOW + tq), BF16)],
        compiler_params=_params("parallel", "arbitrary"),
        name="attn_prompt",
    )(x, g, w_qkv, w_o, bias, cap, sink_rows)


ROWS_PAD = 8 * Q_PER_KV


def _sample_bias_tables(dist_bias, sinks):
    def rows(per_head):
        t = per_head.reshape(N_KV_HEADS, Q_PER_KV, -1).transpose(1, 0, 2)
        return jnp.pad(t, ((0, 0), (0, 8 - N_KV_HEADS), (0, 0))).reshape(ROWS_PAD, -1)

    bias = rows(dist_bias[:0:-1].T)
    extra = rows(jnp.stack([dist_bias[0], sinks.astype(F32)], axis=1))
    return bias, jnp.pad(extra, ((0, 0), (0, LANES - 2)))


def _attn_sample_kernel(q_ref, kn_ref, vn_ref, kvt_ref, kc_ref, vc_ref, bias_ref, extra_ref,
                        o_ref, ko_ref, vo_ref, *, tb):
    t = pl.program_id(0)
    sub = lax.broadcasted_iota(jnp.int32, (8, KV_DIM), 0)
    lane_head = lax.broadcasted_iota(jnp.int32, (8, KV_DIM), 1) // HEAD_DIM
    own = jnp.logical_and(sub < N_KV_HEADS, lane_head == sub)
    newest = lax.broadcasted_iota(jnp.int32, (KV_DIM, WINDOW), 1) == WINDOW - 1
    bias = bias_ref[...]
    bias_new = extra_ref[:, 0:1]
    sink = extra_ref[:, 1:2]

    def scores(bb):
        q_rows = []
        for gq in range(Q_PER_KV):
            q_g = q_ref[bb:bb + 1, gq * KV_DIM:(gq + 1) * KV_DIM] * ATTN_SCALE
            q_rows.append(jnp.where(own, jnp.broadcast_to(q_g, (8, KV_DIM)), 0.0))
        q_blk = jnp.concatenate(q_rows, axis=0)
        s_old = _dot(q_blk.astype(BF16), kc_ref[bb].astype(BF16)) + bias
        s_new = jnp.sum(q_blk * kn_ref[bb:bb + 1, :], axis=1, keepdims=True) + bias_new
        return s_old, s_new

    def attend(bb, s_old, s_new):
        m = jnp.maximum(jnp.max(s_old, axis=1, keepdims=True), jnp.maximum(s_new, sink))
        p_old = jnp.exp(s_old - m)
        p_new = jnp.exp(s_new - m)
        denom = jnp.sum(p_old, axis=1, keepdims=True) + p_new + jnp.exp(sink - m)
        o = _dot_nt(p_old.astype(BF16), vc_ref[bb].astype(BF16)) + p_new * vn_ref[bb:bb + 1, :]
        o = o / denom
        for gq in range(Q_PER_KV):
            o_g = jnp.sum(jnp.where(own, o[8 * gq:8 * gq + 8], 0.0), axis=0, keepdims=True)
            o_ref[bb:bb + 1, gq * KV_DIM:(gq + 1) * KV_DIM] = o_g

    def shift(bb):
        to_last = (WINDOW - 1) - (t * tb + bb)
        for cache_ref, row0, out_ref in ((kc_ref, 0, ko_ref), (vc_ref, KV_DIM, vo_ref)):
            moved = pltpu.roll(cache_ref[bb], WINDOW - 1, axis=1)
            col = pltpu.roll(kvt_ref[row0:row0 + KV_DIM, :], to_last, axis=1)
            out_ref[bb] = jnp.where(newest, col, moved)

    for bb in range(tb):
        shift(bb)
    s_next = scores(0)
    for bb in range(tb):
        s_cur = s_next
        if bb + 1 < tb:
            s_next = scores(bb + 1)
        attend(bb, *s_cur)


def _attn_sample_core(qkv, kv_t, k_cache, v_cache, bias, extra, *, tb):
    n = qkv.shape[0]
    cache = pl.BlockSpec((tb, KV_DIM, WINDOW), lambda t: (t, 0, 0))
    kv_col = D_MODEL // KV_DIM
    return pl.pallas_call(
        functools.partial(_attn_sample_kernel, tb=tb),
        out_shape=(jax.ShapeDtypeStruct((n, D_MODEL), F32),
                   jax.ShapeDtypeStruct(k_cache.shape, F32),
                   jax.ShapeDtypeStruct(v_cache.shape, F32)),
        grid=(n // tb,),
        in_specs=[pl.BlockSpec((tb, D_MODEL), lambda t: (t, 0)),
                  pl.BlockSpec((tb, KV_DIM), lambda t: (t, kv_col)),
                  pl.BlockSpec((tb, KV_DIM), lambda t: (t, kv_col + 1)),
                  _const_spec(kv_t.shape), cache, cache,
                  _const_spec(bias.shape), _const_spec(extra.shape)],
        out_specs=(pl.BlockSpec((tb, D_MODEL), lambda t: (t, 0)), cache, cache),
        compiler_params=_params("parallel"),
        name="attn_sample",
    )(qkv, qkv, qkv, kv_t, k_cache, v_cache, bias, extra)


def _sgu_kernel(x_ref, g_ref, win_ref, lg_ref, lb_ref, sp_ref, bsp_ref, wout_ref, *rest,
                tm, sample):
    x = x_ref[...]
    h = _rms(x, g_ref[...]).astype(BF16)
    z = _dot(h, win_ref[...])
    z = 0.5 * z * (1.0 + lax.erf(z * INV_SQRT2))
    u = z[:, :D_MODEL]
    v = _layer_norm(z[:, D_MODEL:], lg_ref[...], lb_ref[...])
    if sample:
        o_ref, v_ref = rest
        v_ref[...] = v
        gated = (u * (v * sp_ref[...] + bsp_ref[...])).astype(BF16)
    else:
        o_ref, gated_scr = rest
        vb = v.astype(BF16)
        for c in range(tm // CHUNK):
            rows = slice(c * CHUNK, (c + 1) * CHUNK)
            for gi in range(SGU_GROUPS):
                cols = slice(gi * LANES, (gi + 1) * LANES)
                mixed = _dot(sp_ref[gi], vb[rows, cols]) + bsp_ref[gi]
                gated_scr[rows, cols] = (u[rows, cols] * mixed).astype(BF16)
        gated = gated_scr[...]
    o_ref[...] = x + _dot(gated, wout_ref[...])


def _sgu(x, g, w_in, ln_g, ln_b, sp, bsp, w_out, *, tm, sample):
    n = x.shape[0]
    row = pl.BlockSpec((tm, D_MODEL), lambda i: (i, 0))
    args = [x, g, w_in, ln_g, ln_b, sp, bsp, w_out]
    in_specs = [row] + [_const_spec(t.shape) for t in args[1:]]
    if sample:
        out_shape = (jax.ShapeDtypeStruct((n, D_MODEL), F32),) * 2
        out_specs = (row, row)
        scratch = []
    else:
        out_shape = jax.ShapeDtypeStruct((n, D_MODEL), F32)
        out_specs = row
        scratch = [pltpu.VMEM((tm, D_MODEL), BF16)]
    return pl.pallas_call(
        functools.partial(_sgu_kernel, tm=tm, sample=sample),
        out_shape=out_shape,
        grid=(n // tm,),
        in_specs=in_specs,
        out_specs=out_specs,
        scratch_shapes=scratch,
        compiler_params=_params("parallel"),
        name="sgu_sample" if sample else "sgu_prompt",
    )(*args)


CONV_ROWS = 64
CONV_LANES = 256


def _conv_prompt_kernel(x_ref, xprev_ref, g_ref, win_ref, wdw_ref, bdw_ref, lg_ref, lb_ref,
                        wout_ref, o_ref, tail_ref, a0_scr, a1_scr, sh_scr, c_scr,
                        *, tm, tiles_per_seq):
    s = pl.program_id(0)

    @pl.when(s == 0)
    def _():
        a0_scr[...] = jnp.zeros(a0_scr.shape, F32)
        a1_scr[...] = jnp.zeros(a1_scr.shape, F32)

    first_tap = CONV_HALO - (CONV_WIDTH - 1)
    reps = CONV_ROWS // SUBLANES

    def step(a_new, a_old):
        for r in range(1, SUBLANES):
            sh_scr[r - 1] = a_old[r:r + tm + CONV_HALO - SUBLANES, :]

        h = _rms(x_ref[...], g_ref[...]).astype(BF16)
        starts_sequence = s % tiles_per_seq == 0
        a_new[0:CONV_HALO, :] = jnp.where(starts_sequence, 0.0, a_old[tm:tm + CONV_HALO, :])

        def project(jc):
            cols = slice(jc * CONV_LANES, (jc + 1) * CONV_LANES)
            gate = slice(D_MODEL + jc * CONV_LANES, D_MODEL + (jc + 1) * CONV_LANES)
            a = _dot(h, win_ref[:, cols]) * jax.nn.sigmoid(_dot(h, win_ref[:, gate]))
            a_new[CONV_HALO:, cols] = a
            tail_ref[:, cols] = a[tm - CONV_HALO:, :]

        def convolve(rb):
            for lc in range(D_MODEL // CONV_LANES):
                cols = slice(lc * CONV_LANES, (lc + 1) * CONV_LANES)
                acc = jnp.concatenate([bdw_ref[:, cols]] * reps, axis=0)
                for kk in range(CONV_WIDTH):
                    whole, r = divmod(first_tap + kk, SUBLANES)
                    start = rb * CONV_ROWS + whole * SUBLANES
                    src = a_old if r == 0 else sh_scr.at[r - 1]
                    w = jnp.concatenate([wdw_ref[kk, :, cols]] * reps, axis=0)
                    acc = acc + src[start:start + CONV_ROWS, cols] * w
                c_scr[rb * CONV_ROWS:(rb + 1) * CONV_ROWS, cols] = acc

        n_proj = D_MODEL // CONV_LANES
        n_conv = tm // CONV_ROWS
        for jc in range(n_proj):
            project(jc)
            for rb in range(jc * n_conv // n_proj, (jc + 1) * n_conv // n_proj):
                convolve(rb)
        c = _layer_norm(c_scr[...], lg_ref[...], lb_ref[...])
        c = (c * jax.nn.sigmoid(c)).astype(BF16)
        o_ref[...] = xprev_ref[...] + _dot(c, wout_ref[...])

    @pl.when(s % 2 == 0)
    def _():
        step(a0_scr, a1_scr)

    @pl.when(s % 2 == 1)
    def _():
        step(a1_scr, a0_scr)


def _conv_prompt(x, g, w_in, w_dw8, b_dw8, ln_g, ln_b, w_out, *, tm):
    b, seq, _ = x.shape
    tiles_per_seq = seq // tm
    n = b * tiles_per_seq
    consts = [g, w_in, w_dw8, b_dw8, ln_g, ln_b, w_out]
    a_scr = pltpu.VMEM((CONV_HALO + tm, D_MODEL), F32)
    x2 = x.reshape(b * seq, D_MODEL)
    out, tail = pl.pallas_call(
        functools.partial(_conv_prompt_kernel, tm=tm, tiles_per_seq=tiles_per_seq),
        out_shape=(jax.ShapeDtypeStruct((b * seq, D_MODEL), F32),
                   jax.ShapeDtypeStruct((b, CONV_HALO, D_MODEL), F32)),
        grid=(n + 1,),
        in_specs=[pl.BlockSpec((tm, D_MODEL), lambda s: (jnp.minimum(s, n - 1), 0)),
                  pl.BlockSpec((tm, D_MODEL), lambda s: (jnp.maximum(s - 1, 0), 0))]
                 + [_const_spec(t.shape) for t in consts],
        out_specs=(pl.BlockSpec((tm, D_MODEL), lambda s: (jnp.maximum(s - 1, 0), 0)),
                   pl.BlockSpec((None, CONV_HALO, D_MODEL),
                                lambda s: (jnp.minimum(s, n - 1) // tiles_per_seq, 0, 0))),
        scratch_shapes=[a_scr, a_scr,
                        pltpu.VMEM((SUBLANES - 1, CONV_HALO + tm - SUBLANES, D_MODEL), F32),
                        pltpu.VMEM((tm, D_MODEL), F32)],
        compiler_params=_params("arbitrary"),
        name="conv_prompt",
    )(x2, x2, *consts)
    return out.reshape(b, seq, D_MODEL), tail


def _conv_sample_kernel(a_ref, st_ref, wdw_ref, bdw_ref, c_ref, so_ref, *, tb):
    n_hist = CONV_WIDTH - 1
    st = st_ref[...]
    hist = jnp.sum(st * wdw_ref[0:n_hist, :][None], axis=1)
    c_ref[...] = hist + a_ref[...] * wdw_ref[n_hist:CONV_WIDTH, :] + bdw_ref[...]
    so_ref[:, 0:n_hist - 1, :] = st_ref[:, 1:n_hist, :]

    def one(b, carry):
        so_ref[b, n_hist - 1:n_hist, :] = a_ref[pl.ds(b, 1), :]
        return carry

    lax.fori_loop(0, tb, one, 0)


def _conv_sample_core(a, state, w_dw, b_dw, *, tb):
    n = a.shape[0]
    row = pl.BlockSpec((tb, D_MODEL), lambda t: (t, 0))
    st = pl.BlockSpec((tb, CONV_WIDTH - 1, D_MODEL), lambda t: (t, 0, 0))
    return pl.pallas_call(
        functools.partial(_conv_sample_kernel, tb=tb),
        out_shape=(jax.ShapeDtypeStruct((n, D_MODEL), F32),
                   jax.ShapeDtypeStruct(state.shape, F32)),
        grid=(n // tb,),
        in_specs=[row, st, _const_spec(w_dw.shape), _const_spec(b_dw.shape)],
        out_specs=(row, st),
        compiler_params=_params("parallel"),
        name="conv_sample",
    )(a, state, w_dw, b_dw)


PROMPT_TILE = 512
SAMPLE_ATTN_TILE = 16
SAMPLE_CONV_TILE = 16


def kernel(x_prompt, x_sample, cache_swa_k, cache_swa_v, state_conv, rel_bias, norm_mix, norm_ffn, norm_final, attn_w_qkv, attn_w_o, attn_sinks, sgu_w_in, sgu_ln_g, sgu_ln_b, sgu_w_spatial, sgu_b_spatial, sgu_w_out, conv_w_in, conv_w_dw, conv_b_dw, conv_ln_g, conv_ln_b, conv_w_out, ffn_w_up, ffn_w_down):
    batch, seq, _ = x_prompt.shape
    dec = x_sample.shape[0]
    depth = norm_mix.shape[0]
    mixer_of_layer = tuple(i % 3 for i in range(depth))
    slot_of_layer = tuple(mixer_of_layer[:i].count(mixer_of_layer[i]) for i in range(depth))

    def row(v):
        return v.reshape(1, -1).astype(F32)

    def regroup(w, axis):
        shape = w.shape[:axis] + (N_KV_HEADS, Q_PER_KV, HEAD_DIM) + w.shape[axis + 1:]
        return jnp.swapaxes(w.reshape(shape), axis, axis + 1).reshape(w.shape)

    def cache_t(c):
        return jnp.transpose(c, (0, 2, 3, 1)).reshape(dec, KV_DIM, WINDOW)

    def cache_from_t(c):
        return jnp.transpose(c.reshape(dec, N_KV_HEADS, HEAD_DIM, WINDOW), (0, 3, 1, 2))

    xp = x_prompt
    xs = x_sample.reshape(dec, D_MODEL)
    dist_bias = _distance_bias(rel_bias)

    kp, vp, ksm, vsm, sgu_v_new, convp, convs = [], [], [], [], [], [], []
    for i in range(depth):
        m, j = mixer_of_layer[i], slot_of_layer[i]
        g_mix = row(norm_mix[i])
        if m == 0:
            w_qkv = attn_w_qkv[j].astype(BF16)
            w_o = attn_w_o[j].astype(BF16)
            bias_p, cap_p, sink_rows = _prompt_bias_tables(dist_bias, attn_sinks[j])
            xp, k1, v1 = _attn_prompt(xp, g_mix, w_qkv, w_o, bias_p, cap_p, sink_rows,
                                      tq=PROMPT_TILE)
            kp.append(k1.reshape(batch, WINDOW, N_KV_HEADS, HEAD_DIM))
            vp.append(v1.reshape(batch, WINDOW, N_KV_HEADS, HEAD_DIM))

            w_qkv_s = jnp.concatenate([regroup(w_qkv[:, :D_MODEL], 1), w_qkv[:, D_MODEL:]], axis=1)
            bias_s, extra_s = _sample_bias_tables(dist_bias, attn_sinks[j])
            qkv_s, kv_t = _norm_proj(xs, g_mix, w_qkv_s, kv_t=True)
            o_s, k2, v2 = _attn_sample_core(qkv_s, kv_t, cache_t(cache_swa_k[j]),
                                            cache_t(cache_swa_v[j]), bias_s, extra_s,
                                            tb=SAMPLE_ATTN_TILE)
            xs = _proj_res(xs, o_s, regroup(w_o, 0))
            ksm.append(cache_from_t(k2))
            vsm.append(cache_from_t(v2))
        elif m == 1:
            w_in = sgu_w_in[j].astype(BF16)
            w_out = sgu_w_out[j].astype(BF16)
            ln_g, ln_b = row(sgu_ln_g[j]), row(sgu_ln_b[j])
            sp = jnp.tril(sgu_w_spatial[j]).astype(BF16)
            bsp = jnp.broadcast_to(sgu_b_spatial[j].astype(F32)[:, :, None],
                                   (SGU_GROUPS, CHUNK, LANES))
            xp = _sgu(xp.reshape(batch * seq, D_MODEL), g_mix, w_in, ln_g, ln_b, sp, bsp, w_out,
                      tm=PROMPT_TILE, sample=False).reshape(batch, seq, D_MODEL)
            sp0 = row(jnp.repeat(sgu_w_spatial[j][:, 0, 0], LANES))
            bsp0 = row(jnp.repeat(sgu_b_spatial[j][:, 0], LANES))
            xs, v_rows = _sgu(xs, g_mix, w_in, ln_g, ln_b, sp0, bsp0, w_out, tm=dec, sample=True)
            sgu_v_new.append(v_rows.reshape(dec, 1, D_MODEL))
        else:
            w_in = conv_w_in[j].astype(BF16)
            w_out = conv_w_out[j].astype(BF16)
            w_dw = conv_w_dw[j].astype(F32)
            b_dw = row(conv_b_dw[j])
            ln_g, ln_b = row(conv_ln_g[j]), row(conv_ln_b[j])
            w_dw8 = jnp.broadcast_to(w_dw[:, None, :], (CONV_WIDTH, SUBLANES, D_MODEL))
            b_dw8 = jnp.broadcast_to(b_dw, (SUBLANES, D_MODEL))
            xp, tail = _conv_prompt(xp, g_mix, w_in, w_dw8, b_dw8, ln_g, ln_b, w_out,
                                    tm=PROMPT_TILE)
            convp.append(tail[:, CONV_HALO - (CONV_WIDTH - 1):, :])
            a_s = _norm_proj(xs, g_mix, w_in, glu=True)
            c_s, st2 = _conv_sample_core(a_s, state_conv[j].astype(F32), w_dw, b_dw,
                                         tb=SAMPLE_CONV_TILE)
            xs = _proj_res(xs, c_s, w_out, ln=(ln_g, ln_b))
            convs.append(st2)

        g_ffn = row(norm_ffn[i])
        w_up = ffn_w_up[i].astype(BF16)
        w_down = ffn_w_down[i].astype(BF16)
        g_fin = row(norm_final) if i == depth - 1 else None
        xp = _ffn(xp.reshape(batch * seq, D_MODEL), g_ffn, w_up, w_down, g_fin,
                  tm=PROMPT_TILE).reshape(batch, seq, D_MODEL)
        xs = _ffn(xs, g_ffn, w_up, w_down, g_fin, tm=dec)

    y_prompt = xp
    y_sample = xs.reshape(dec, 1, D_MODEL)
    return (y_prompt, y_sample, jnp.stack(kp), jnp.stack(vp), jnp.stack(ksm), jnp.stack(vsm),
            jnp.stack(sgu_v_new), jnp.stack(convp), jnp.stack(convs))
```

```python
import functools
import math

import jax
import jax.numpy as jnp
from jax import lax
from jax.experimental import pallas as pl
from jax.experimental.pallas import tpu as pltpu

D_MODEL = 1024
HEAD_DIM = 64
N_HEADS = 16
N_KV_HEADS = 4
Q_PER_KV = 4
KV_DIM = N_KV_HEADS * HEAD_DIM
QKV_DIM = D_MODEL + 2 * KV_DIM
WINDOW = 128
ATTN_SCALE = HEAD_DIM ** -0.5
N_BUCKETS = 32
MAX_DISTANCE = 128
CHUNK = 128
SGU_GROUPS = 8
CONV_WIDTH = 31
CONV_HALO = 32
D_FF = 4 * D_MODEL
EPS = 1e-6
NEG_INF = -1e30
F32_MAX = float(jnp.finfo(jnp.float32).max)
INV_SQRT2 = 1.0 / math.sqrt(2.0)

LANES = 128
SUBLANES = 8
VMEM_LIMIT = 56 * 1024 * 1024

F32 = jnp.float32
BF16 = jnp.bfloat16


def _const_spec(shape):
    n = len(shape)
    return pl.BlockSpec(shape, lambda *_: (0,) * n, pipeline_mode=pl.Buffered(1))


def _params(*sem):
    return pltpu.CompilerParams(dimension_semantics=sem, vmem_limit_bytes=VMEM_LIMIT)


def _rms(x, g):
    return x * lax.rsqrt(jnp.mean(x * x, axis=-1, keepdims=True) + EPS) * g


def _layer_norm(x, g, b):
    mu = jnp.mean(x, axis=-1, keepdims=True)
    xc = x - mu
    var = jnp.mean(xc * xc, axis=-1, keepdims=True)
    return xc * lax.rsqrt(var + EPS) * g + b


def _dot(a, b):
    return jnp.dot(a, b, preferred_element_type=F32)


def _dot_nt(a, b):
    return lax.dot_general(a, b, (((1,), (1,)), ((), ())), preferred_element_type=F32)


FFN_CHUNK = 1024


def _layer_spec(shape, layer):
    zeros = (0,) * (len(shape) - 1)
    return pl.BlockSpec((None,) + tuple(shape[1:]), lambda *_: (layer,) + zeros,
                        pipeline_mode=pl.Buffered(1))


def _ffn_kernel(xp_ref, xs_ref, g_ref, wu_ref, wd_ref, *rest, final, n_prompt_tiles):
    if final:
        gf_ref, op_ref, os_ref = rest
    else:
        op_ref, os_ref = rest

    def block(x_ref, o_ref):
        x = x_ref[...]
        h = _rms(x, g_ref[...]).astype(BF16)
        y = x
        for c in range(D_FF // FFN_CHUNK):
            cols = slice(c * FFN_CHUNK, (c + 1) * FFN_CHUNK)
            u = _dot(h, wu_ref[:, cols])
            u = jnp.square(jnp.maximum(u, 0.0)).astype(BF16)
            y = y + _dot(u, wd_ref[cols, :])
        o_ref[...] = _rms(y, gf_ref[...]) if final else y

    i = pl.program_id(0)

    @pl.when(i < n_prompt_tiles)
    def _():
        block(xp_ref, op_ref)

    @pl.when(i == n_prompt_tiles)
    def _():
        block(xs_ref, os_ref)


def _ffn(xp, xs, g, w_up, w_down, layer, g_final=None, *, tm):
    n = xp.shape[0]
    nt = n // tm
    row = pl.BlockSpec((tm, D_MODEL), lambda i: (jnp.minimum(i, nt - 1), 0))
    in_specs = [row, _const_spec(xs.shape), _const_spec((1, D_MODEL)),
                _layer_spec(w_up.shape, layer), _layer_spec(w_down.shape, layer)]
    args = [xp, xs, g, w_up, w_down]
    if g_final is not None:
        in_specs.append(_const_spec((1, D_MODEL)))
        args.append(g_final)
    return pl.pallas_call(
        functools.partial(_ffn_kernel, final=g_final is not None, n_prompt_tiles=nt),
        out_shape=(jax.ShapeDtypeStruct((n, D_MODEL), F32),
                   jax.ShapeDtypeStruct(xs.shape, F32)),
        grid=(nt + 1,),
        in_specs=in_specs,
        out_specs=(row, pl.BlockSpec(xs.shape, lambda i: (0, 0))),
        compiler_params=_params("arbitrary"),
        name="ffn_final" if g_final is not None else "ffn",
    )(*args)


def _norm_proj_kernel(x_ref, g_ref, w_ref, o_ref, *rest, glu, kv_t):
    h = _rms(x_ref[...], g_ref[...]).astype(BF16)
    y = _dot(h, w_ref[...])
    if glu:
        half = y.shape[1] // 2
        y = y[:, :half] * jax.nn.sigmoid(y[:, half:])
    o_ref[...] = y
    if kv_t:
        (t_ref,) = rest
        t_ref[...] = y[:, D_MODEL:].T


def _norm_proj(x, g, w, *, glu=False, kv_t=False):
    n = x.shape[0]
    n_out = w.shape[1] // 2 if glu else w.shape[1]
    out_shape = [jax.ShapeDtypeStruct((n, n_out), F32)]
    if kv_t:
        out_shape.append(jax.ShapeDtypeStruct((n_out - D_MODEL, n), F32))
    out = pl.pallas_call(
        functools.partial(_norm_proj_kernel, glu=glu, kv_t=kv_t),
        out_shape=tuple(out_shape),
        grid=(1,),
        in_specs=[_const_spec(x.shape), _const_spec(g.shape), _const_spec(w.shape)],
        out_specs=tuple(_const_spec(o.shape) for o in out_shape),
        compiler_params=_params("arbitrary"),
        name="norm_proj_glu" if glu else "norm_proj",
    )(x, g, w)
    return out if kv_t else out[0]


def _proj_res_kernel(x_ref, a_ref, w_ref, *rest, conv_tail):
    a = a_ref[...]
    if conv_tail:
        lg_ref, lb_ref, o_ref = rest
        a = _layer_norm(a, lg_ref[...], lb_ref[...])
        a = a * jax.nn.sigmoid(a)
    else:
        (o_ref,) = rest
    o_ref[...] = x_ref[...] + _dot(a.astype(BF16), w_ref[...])


def _proj_res(x, a, w, ln=None):
    args = [x, a, w] + (list(ln) if ln is not None else [])
    return pl.pallas_call(
        functools.partial(_proj_res_kernel, conv_tail=ln is not None),
        out_shape=jax.ShapeDtypeStruct(x.shape, F32),
        grid=(1,),
        in_specs=[_const_spec(t.shape) for t in args],
        out_specs=_const_spec(x.shape),
        compiler_params=_params("arbitrary"),
        name="proj_res_ln" if ln is not None else "proj_res",
    )(*args)


def _t5_bucket(dist):
    n = jnp.maximum(dist, 0)
    max_exact = N_BUCKETS // 2
    nf = jnp.maximum(n, 1).astype(F32)
    large = max_exact + (jnp.log(nf / max_exact) / math.log(MAX_DISTANCE / max_exact)
                         * (N_BUCKETS - max_exact)).astype(jnp.int32)
    large = jnp.minimum(large, N_BUCKETS - 1)
    return jnp.where(n < max_exact, n, large)


def _distance_bias(rel_bias):
    buckets = _t5_bucket(jnp.arange(WINDOW + 1, dtype=jnp.int32))
    onehot = (buckets[:, None] == jnp.arange(N_BUCKETS, dtype=jnp.int32)[None, :]).astype(F32)
    return jnp.dot(onehot, rel_bias.astype(F32), precision=lax.Precision.HIGHEST)


def _prompt_bias_tables(dist_bias, sinks):
    period = 3 * WINDOW
    line = jnp.concatenate([jnp.broadcast_to(dist_bias[WINDOW:], (WINDOW - 1, N_HEADS)),
                            dist_bias[::-1],
                            jnp.broadcast_to(dist_bias[:1], (WINDOW, N_HEADS))]).T
    skew = jnp.tile(line, (1, WINDOW))[:, :WINDOW * (period - 1)]
    per_head = skew.reshape(N_HEADS, WINDOW, period - 1)[:, :, WINDOW - 1:period - 1]
    per_head = per_head.reshape(N_KV_HEADS, 2, 2, WINDOW, 2 * WINDOW)
    bias = per_head.transpose(0, 2, 4, 1, 3).reshape(N_KV_HEADS, 4 * WINDOW, 2 * WINDOW)

    qi = jnp.arange(WINDOW, dtype=jnp.int32)[None, :]
    kj = jnp.arange(2 * WINDOW, dtype=jnp.int32)[:, None]
    dist = qi - kj + WINDOW
    allowed = (dist >= 0) & (dist <= WINDOW)
    first = allowed & (kj >= WINDOW)
    cap = jnp.stack([jnp.where(allowed, F32_MAX, NEG_INF), jnp.where(first, F32_MAX, NEG_INF)])
    cap = jnp.tile(cap.astype(F32), (1, 2, 2))

    sink_rows = sinks.astype(F32).reshape(N_KV_HEADS, 2, 2).transpose(0, 2, 1).reshape(2 * N_KV_HEADS, 2)
    sink_rows = jnp.repeat(sink_rows, WINDOW, axis=1)
    return bias, cap, sink_rows


def _attn_prompt_kernel(x_ref, g_ref, wqkv_ref, wo_ref, bias_ref, cap_ref, sink_ref,
                        o_ref, kc_ref, vc_ref,
                        q_scr, o_scr, klo_scr, khi_scr, vt_scr, *, tq):
    i = pl.program_id(1)
    n_i = pl.num_programs(1)
    x = x_ref[...]
    h = _rms(x, g_ref[...]).astype(BF16)
    qkv = _dot(h, wqkv_ref[...])
    q_scr[...] = (qkv[:, :D_MODEL] * ATTN_SCALE).astype(BF16)
    k = qkv[:, D_MODEL:D_MODEL + KV_DIM]
    v = qkv[:, D_MODEL + KV_DIM:]

    @pl.when(i == n_i - 1)
    def _():
        kc_ref[...] = k[tq - WINDOW:, :]
        vc_ref[...] = v[tq - WINDOW:, :]

    @pl.when(i == 0)
    def _():
        klo_scr[:, 0:WINDOW, :] = jnp.zeros((N_KV_HEADS, WINDOW, LANES), BF16)
        khi_scr[:, 0:WINDOW, :] = jnp.zeros((N_KV_HEADS, WINDOW, LANES), BF16)
        vt_scr[:, :, 0:WINDOW] = jnp.zeros((N_KV_HEADS, HEAD_DIM, WINDOW), BF16)

    @pl.when(i > 0)
    def _():
        klo_scr[:, 0:WINDOW, :] = klo_scr[:, tq:tq + WINDOW, :]
        khi_scr[:, 0:WINDOW, :] = khi_scr[:, tq:tq + WINDOW, :]
        vt_scr[:, :, 0:WINDOW] = vt_scr[:, :, tq:tq + WINDOW]

    low = lax.broadcasted_iota(jnp.int32, (tq, LANES), 1) < HEAD_DIM
    for c in range(KV_DIM // LANES):
        kc = k[:, c * LANES:(c + 1) * LANES]
        kr = pltpu.roll(kc, HEAD_DIM, axis=1)
        klo_scr[2 * c, WINDOW:, :] = jnp.where(low, kc, 0.0).astype(BF16)
        khi_scr[2 * c, WINDOW:, :] = jnp.where(low, 0.0, kr).astype(BF16)
        klo_scr[2 * c + 1, WINDOW:, :] = jnp.where(low, kr, 0.0).astype(BF16)
        khi_scr[2 * c + 1, WINDOW:, :] = jnp.where(low, 0.0, kc).astype(BF16)
        vt = v[:, c * LANES:(c + 1) * LANES].T.astype(BF16)
        vt_scr[2 * c, :, WINDOW:] = vt[:HEAD_DIM]
        vt_scr[2 * c + 1, :, WINDOW:] = vt[HEAD_DIM:]

    is_first = jnp.where(i == 0, 1, 0)

    def scores(jb, kh):
        rows = slice(jb * WINDOW, (jb + 1) * WINDOW)
        band = slice(jb * WINDOW, (jb + 2) * WINDOW)
        cap = cap_ref[is_first] if jb == 0 else cap_ref[0]
        qst = jnp.concatenate([q_scr[rows, (2 * kh) * LANES:(2 * kh + 1) * LANES],
                               q_scr[rows, (2 * kh + 1) * LANES:(2 * kh + 2) * LANES]], axis=0)
        kst = jnp.concatenate([klo_scr[kh, band, :], khi_scr[kh, band, :]], axis=0)
        return jnp.minimum(_dot_nt(kst, qst) + bias_ref[kh], cap)

    def attend(jb, kh, s):
        rows = slice(jb * WINDOW, (jb + 1) * WINDOW)
        vt = vt_scr[kh, :, jb * WINDOW:(jb + 2) * WINDOW]
        halves = []
        for half in range(2):
            sh = s[half * 2 * WINDOW:(half + 1) * 2 * WINDOW]
            sink = sink_ref[2 * kh + half:2 * kh + half + 1, :]
            m = jnp.maximum(jnp.max(sh, axis=0, keepdims=True), sink)
            p = jnp.exp(sh - m)
            denom = jnp.sum(p, axis=0, keepdims=True) + jnp.exp(sink - m)
            halves.append(_dot(vt, p.astype(BF16)) * (1.0 / denom))
        o = jnp.concatenate(halves, axis=0).T.astype(BF16)
        o_scr[rows, (2 * kh) * LANES:(2 * kh + 1) * LANES] = o[:WINDOW]
        o_scr[rows, (2 * kh + 1) * LANES:(2 * kh + 2) * LANES] = o[WINDOW:]

    steps = [(jb, kh) for jb in range(tq // WINDOW) for kh in range(N_KV_HEADS)]
    s_next = scores(*steps[0])
    for n, step in enumerate(steps):
        s_cur = s_next
        if n + 1 < len(steps):
            s_next = scores(*steps[n + 1])
        attend(*step, s_cur)

    o_ref[...] = x + _dot(o_scr[...], wo_ref[...])


def _attn_prompt(x, g, w_qkv, w_o, bias, cap, sink_rows, *, tq):
    b, s, _ = x.shape
    row = pl.BlockSpec((None, tq, D_MODEL), lambda bi, i: (bi, i, 0))
    cache = pl.BlockSpec((None, WINDOW, KV_DIM), lambda bi, i: (bi, 0, 0))
    k_scr = pltpu.VMEM((N_KV_HEADS, WINDOW + tq, LANES), BF16)
    return pl.pallas_call(
        functools.partial(_attn_prompt_kernel, tq=tq),
        out_shape=(jax.ShapeDtypeStruct(x.shape, F32),
                   jax.ShapeDtypeStruct((b, WINDOW, KV_DIM), F32),
                   jax.ShapeDtypeStruct((b, WINDOW, KV_DIM), F32)),
        grid=(b, s // tq),
        in_specs=[row, _const_spec((1, D_MODEL)),
                  _const_spec((D_MODEL, QKV_DIM)), _const_spec((D_MODEL, D_MODEL)),
                  _const_spec(bias.shape), _const_spec(cap.shape), _const_spec(sink_rows.shape)],
        out_specs=(row, cache, cache),
        scratch_shapes=[pltpu.VMEM((tq, D_MODEL), BF16), pltpu.VMEM((tq, D_MODEL), BF16),
                        k_scr, k_scr, pltpu.VMEM((N_KV_HEADS, HEAD_DIM, WINDOW + tq), BF16)],
        compiler_params=_params("parallel", "arbitrary"),
        name="attn_prompt",
    )(x, g, w_qkv, w_o, bias, cap, sink_rows)


ROWS_PAD = 8 * Q_PER_KV


def _sample_bias_tables(dist_bias, sinks):
    def rows(per_head):
        t = per_head.reshape(N_KV_HEADS, Q_PER_KV, -1).transpose(1, 0, 2)
        return jnp.pad(t, ((0, 0), (0, 8 - N_KV_HEADS), (0, 0))).reshape(ROWS_PAD, -1)

    bias = rows(dist_bias[:0:-1].T)
    extra = rows(jnp.stack([dist_bias[0], sinks.astype(F32)], axis=1))
    return bias, jnp.pad(extra, ((0, 0), (0, LANES - 2)))


def _attn_sample_kernel(q_ref, kn_ref, vn_ref, kvt_ref, kc_ref, vc_ref, bias_ref, extra_ref,
                        *rest, tb, n_aliased):
    o_ref, ko_ref, vo_ref = rest[n_aliased:]
    t = pl.program_id(0)
    sub = lax.broadcasted_iota(jnp.int32, (8, KV_DIM), 0)
    lane_head = lax.broadcasted_iota(jnp.int32, (8, KV_DIM), 1) // HEAD_DIM
    own = jnp.logical_and(sub < N_KV_HEADS, lane_head == sub)
    newest = lax.broadcasted_iota(jnp.int32, (KV_DIM, WINDOW), 1) == WINDOW - 1
    bias = bias_ref[...]
    bias_new = extra_ref[:, 0:1]
    sink = extra_ref[:, 1:2]

    def scores(bb):
        q_rows = []
        for gq in range(Q_PER_KV):
            q_g = q_ref[bb:bb + 1, gq * KV_DIM:(gq + 1) * KV_DIM] * ATTN_SCALE
            q_rows.append(jnp.where(own, jnp.broadcast_to(q_g, (8, KV_DIM)), 0.0))
        q_blk = jnp.concatenate(q_rows, axis=0)
        s_old = _dot(q_blk.astype(BF16), kc_ref[bb].astype(BF16)) + bias
        s_new = jnp.sum(q_blk * kn_ref[bb:bb + 1, :], axis=1, keepdims=True) + bias_new
        return s_old, s_new

    def attend(bb, s_old, s_new):
        m = jnp.maximum(jnp.max(s_old, axis=1, keepdims=True), jnp.maximum(s_new, sink))
        p_old = jnp.exp(s_old - m)
        p_new = jnp.exp(s_new - m)
        denom = jnp.sum(p_old, axis=1, keepdims=True) + p_new + jnp.exp(sink - m)
        o = _dot_nt(p_old.astype(BF16), vc_ref[bb].astype(BF16)) + p_new * vn_ref[bb:bb + 1, :]
        o = o / denom
        for gq in range(Q_PER_KV):
            o_g = jnp.sum(jnp.where(own, o[8 * gq:8 * gq + 8], 0.0), axis=0, keepdims=True)
            o_ref[bb:bb + 1, gq * KV_DIM:(gq + 1) * KV_DIM] = o_g

    def shift(bb):
        to_last = (WINDOW - 1) - (t * tb + bb)
        for cache_ref, row0, out_ref in ((kc_ref, 0, ko_ref), (vc_ref, KV_DIM, vo_ref)):
            moved = pltpu.roll(cache_ref[bb], WINDOW - 1, axis=1)
            col = pltpu.roll(kvt_ref[row0:row0 + KV_DIM, :], to_last, axis=1)
            out_ref[bb] = jnp.where(newest, col, moved)

    for bb in range(tb):
        shift(bb)
    s_next = scores(0)
    for bb in range(tb):
        s_cur = s_next
        if bb + 1 < tb:
            s_next = scores(bb + 1)
        attend(bb, *s_cur)


def _attn_sample_core(qkv, kv_t, k_caches, v_caches, bias, extra, layer, new_caches=None, *, tb):
    n = qkv.shape[0]
    cache = pl.BlockSpec((None, tb, KV_DIM, WINDOW), lambda t: (layer, t, 0, 0))
    kv_col = D_MODEL // KV_DIM
    args = [qkv, qkv, qkv, kv_t, k_caches, v_caches, bias, extra]
    in_specs = [pl.BlockSpec((tb, D_MODEL), lambda t: (t, 0)),
                pl.BlockSpec((tb, KV_DIM), lambda t: (t, kv_col)),
                pl.BlockSpec((tb, KV_DIM), lambda t: (t, kv_col + 1)),
                _const_spec(kv_t.shape), cache, cache,
                _const_spec(bias.shape), _const_spec(extra.shape)]
    aliases = {}
    if new_caches is not None:
        aliases = {len(args): 1, len(args) + 1: 2}
        args += list(new_caches)
        in_specs += [pl.BlockSpec(memory_space=pl.ANY)] * 2
    return pl.pallas_call(
        functools.partial(_attn_sample_kernel, tb=tb, n_aliased=len(aliases)),
        out_shape=(jax.ShapeDtypeStruct((n, D_MODEL), F32),
                   jax.ShapeDtypeStruct(k_caches.shape, F32),
                   jax.ShapeDtypeStruct(v_caches.shape, F32)),
        grid=(n // tb,),
        in_specs=in_specs,
        out_specs=(pl.BlockSpec((tb, D_MODEL), lambda t: (t, 0)), cache, cache),
        input_output_aliases=aliases,
        compiler_params=_params("parallel"),
        name="attn_sample",
    )(*args)


def _sgu_kernel(x_ref, g_ref, win_ref, lg_ref, lb_ref, sp_ref, bsp_ref, wout_ref, *rest,
                tm, sample):
    x = x_ref[...]
    h = _rms(x, g_ref[...]).astype(BF16)
    z = _dot(h, win_ref[...])
    z = 0.5 * z * (1.0 + lax.erf(z * INV_SQRT2))
    u = z[:, :D_MODEL]
    v = _layer_norm(z[:, D_MODEL:], lg_ref[...], lb_ref[...])
    if sample:
        o_ref, v_ref = rest
        v_ref[...] = v
        gated = (u * (v * sp_ref[...] + bsp_ref[...])).astype(BF16)
    else:
        o_ref, gated_scr = rest
        vb = v.astype(BF16)
        for c in range(tm // CHUNK):
            rows = slice(c * CHUNK, (c + 1) * CHUNK)
            for gi in range(SGU_GROUPS):
                cols = slice(gi * LANES, (gi + 1) * LANES)
                mixed = _dot(sp_ref[gi], vb[rows, cols]) + bsp_ref[gi]
                gated_scr[rows, cols] = (u[rows, cols] * mixed).astype(BF16)
        gated = gated_scr[...]
    o_ref[...] = x + _dot(gated, wout_ref[...])


def _sgu(x, g, w_in, ln_g, ln_b, sp, bsp, w_out, *, tm, sample):
    n = x.shape[0]
    row = pl.BlockSpec((tm, D_MODEL), lambda i: (i, 0))
    args = [x, g, w_in, ln_g, ln_b, sp, bsp, w_out]
    in_specs = [row] + [_const_spec(t.shape) for t in args[1:]]
    if sample:
        out_shape = (jax.ShapeDtypeStruct((n, D_MODEL), F32),) * 2
        out_specs = (row, row)
        scratch = []
    else:
        out_shape = jax.ShapeDtypeStruct((n, D_MODEL), F32)
        out_specs = row
        scratch = [pltpu.VMEM((tm, D_MODEL), BF16)]
    return pl.pallas_call(
        functools.partial(_sgu_kernel, tm=tm, sample=sample),
        out_shape=out_shape,
        grid=(n // tm,),
        in_specs=in_specs,
        out_specs=out_specs,
        scratch_shapes=scratch,
        compiler_params=_params("parallel"),
        name="sgu_sample" if sample else "sgu_prompt",
    )(*args)


CONV_ROWS = 64
CONV_LANES = 256


def _conv_prompt_kernel(x_ref, xprev_ref, g_ref, win_ref, wdw_ref, bdw_ref, lg_ref, lb_ref,
                        wout_ref, o_ref, tail_ref, a0_scr, a1_scr, sh_scr, c_scr,
                        *, tm, tiles_per_seq):
    s = pl.program_id(0)

    @pl.when(s == 0)
    def _():
        a0_scr[...] = jnp.zeros(a0_scr.shape, F32)
        a1_scr[...] = jnp.zeros(a1_scr.shape, F32)

    first_tap = CONV_HALO - (CONV_WIDTH - 1)
    reps = CONV_ROWS // SUBLANES

    def step(a_new, a_old):
        for r in range(1, SUBLANES):
            sh_scr[r - 1] = a_old[r:r + tm + CONV_HALO - SUBLANES, :]

        h = _rms(x_ref[...], g_ref[...]).astype(BF16)
        starts_sequence = s % tiles_per_seq == 0
        a_new[0:CONV_HALO, :] = jnp.where(starts_sequence, 0.0, a_old[tm:tm + CONV_HALO, :])

        def project(jc):
            cols = slice(jc * CONV_LANES, (jc + 1) * CONV_LANES)
            gate = slice(D_MODEL + jc * CONV_LANES, D_MODEL + (jc + 1) * CONV_LANES)
            a = _dot(h, win_ref[:, cols]) * jax.nn.sigmoid(_dot(h, win_ref[:, gate]))
            a_new[CONV_HALO:, cols] = a
            tail_ref[:, cols] = a[tm - CONV_HALO:, :]

        def convolve(rb):
            for lc in range(D_MODEL // CONV_LANES):
                cols = slice(lc * CONV_LANES, (lc + 1) * CONV_LANES)
                acc = jnp.concatenate([bdw_ref[:, cols]] * reps, axis=0)
                for kk in range(CONV_WIDTH):
                    whole, r = divmod(first_tap + kk, SUBLANES)
                    start = rb * CONV_ROWS + whole * SUBLANES
                    src = a_old if r == 0 else sh_scr.at[r - 1]
                    w = jnp.concatenate([wdw_ref[kk, :, cols]] * reps, axis=0)
                    acc = acc + src[start:start + CONV_ROWS, cols] * w
                c_scr[rb * CONV_ROWS:(rb + 1) * CONV_ROWS, cols] = acc

        n_proj = D_MODEL // CONV_LANES
        n_conv = tm // CONV_ROWS
        for jc in range(n_proj):
            project(jc)
            for rb in range(jc * n_conv // n_proj, (jc + 1) * n_conv // n_proj):
                convolve(rb)
        c = _layer_norm(c_scr[...], lg_ref[...], lb_ref[...])
        c = (c * jax.nn.sigmoid(c)).astype(BF16)
        o_ref[...] = xprev_ref[...] + _dot(c, wout_ref[...])

    @pl.when(s % 2 == 0)
    def _():
        step(a0_scr, a1_scr)

    @pl.when(s % 2 == 1)
    def _():
        step(a1_scr, a0_scr)


def _conv_prompt(x, g, w_in, w_dw8, b_dw8, ln_g, ln_b, w_out, *, tm):
    b, seq, _ = x.shape
    tiles_per_seq = seq // tm
    n = b * tiles_per_seq
    consts = [g, w_in, w_dw8, b_dw8, ln_g, ln_b, w_out]
    a_scr = pltpu.VMEM((CONV_HALO + tm, D_MODEL), F32)
    x2 = x.reshape(b * seq, D_MODEL)
    out, tail = pl.pallas_call(
        functools.partial(_conv_prompt_kernel, tm=tm, tiles_per_seq=tiles_per_seq),
        out_shape=(jax.ShapeDtypeStruct((b * seq, D_MODEL), F32),
                   jax.ShapeDtypeStruct((b, CONV_HALO, D_MODEL), F32)),
        grid=(n + 1,),
        in_specs=[pl.BlockSpec((tm, D_MODEL), lambda s: (jnp.minimum(s, n - 1), 0)),
                  pl.BlockSpec((tm, D_MODEL), lambda s: (jnp.maximum(s - 1, 0), 0))]
                 + [_const_spec(t.shape) for t in consts],
        out_specs=(pl.BlockSpec((tm, D_MODEL), lambda s: (jnp.maximum(s - 1, 0), 0)),
                   pl.BlockSpec((None, CONV_HALO, D_MODEL),
                                lambda s: (jnp.minimum(s, n - 1) // tiles_per_seq, 0, 0))),
        scratch_shapes=[a_scr, a_scr,
                        pltpu.VMEM((SUBLANES - 1, CONV_HALO + tm - SUBLANES, D_MODEL), F32),
                        pltpu.VMEM((tm, D_MODEL), F32)],
        compiler_params=_params("arbitrary"),
        name="conv_prompt",
    )(x2, x2, *consts)
    return out.reshape(b, seq, D_MODEL), tail


def _conv_sample_kernel(a_ref, st_ref, wdw_ref, bdw_ref, c_ref, so_ref, *, tb):
    n_hist = CONV_WIDTH - 1
    st = st_ref[...]
    hist = jnp.sum(st * wdw_ref[0:n_hist, :][None], axis=1)
    c_ref[...] = hist + a_ref[...] * wdw_ref[n_hist:CONV_WIDTH, :] + bdw_ref[...]
    so_ref[:, 0:n_hist - 1, :] = st_ref[:, 1:n_hist, :]

    def one(b, carry):
        so_ref[b, n_hist - 1:n_hist, :] = a_ref[pl.ds(b, 1), :]
        return carry

    lax.fori_loop(0, tb, one, 0)


def _conv_sample_core(a, states, w_dw, b_dw, layer, *, tb):
    n = a.shape[0]
    row = pl.BlockSpec((tb, D_MODEL), lambda t: (t, 0))
    st_in = pl.BlockSpec((None, tb, CONV_WIDTH - 1, D_MODEL), lambda t: (layer, t, 0, 0))
    st_out = pl.BlockSpec((tb, CONV_WIDTH - 1, D_MODEL), lambda t: (t, 0, 0))
    return pl.pallas_call(
        functools.partial(_conv_sample_kernel, tb=tb),
        out_shape=(jax.ShapeDtypeStruct((n, D_MODEL), F32),
                   jax.ShapeDtypeStruct(states.shape[1:], F32)),
        grid=(n // tb,),
        in_specs=[row, st_in, _const_spec(w_dw.shape), _const_spec(b_dw.shape)],
        out_specs=(row, st_out),
        compiler_params=_params("parallel"),
        name="conv_sample",
    )(a, states, w_dw, b_dw)


PROMPT_TILE = 512
SAMPLE_ATTN_TILE = 16
SAMPLE_CONV_TILE = 16


def kernel(x_prompt, x_sample, cache_swa_k, cache_swa_v, state_conv, rel_bias, norm_mix, norm_ffn, norm_final, attn_w_qkv, attn_w_o, attn_sinks, sgu_w_in, sgu_ln_g, sgu_ln_b, sgu_w_spatial, sgu_b_spatial, sgu_w_out, conv_w_in, conv_w_dw, conv_b_dw, conv_ln_g, conv_ln_b, conv_w_out, ffn_w_up, ffn_w_down):
    batch, seq, _ = x_prompt.shape
    dec = x_sample.shape[0]
    depth = norm_mix.shape[0]
    mixer_of_layer = tuple(i % 3 for i in range(depth))
    slot_of_layer = tuple(mixer_of_layer[:i].count(mixer_of_layer[i]) for i in range(depth))

    def row(v):
        return v.reshape(1, -1).astype(F32)

    def regroup(w, axis):
        shape = w.shape[:axis] + (N_KV_HEADS, Q_PER_KV, HEAD_DIM) + w.shape[axis + 1:]
        return jnp.swapaxes(w.reshape(shape), axis, axis + 1).reshape(w.shape)

    n_attn = attn_w_qkv.shape[0]

    def caches_t(c):
        return jnp.transpose(c, (0, 1, 3, 4, 2)).reshape(n_attn, dec, KV_DIM, WINDOW)

    def caches_from_t(c):
        c = c.reshape(n_attn, dec, N_KV_HEADS, HEAD_DIM, WINDOW)
        return jnp.transpose(c, (0, 1, 4, 2, 3))

    xp = x_prompt
    xs = x_sample.reshape(dec, D_MODEL)
    dist_bias = _distance_bias(rel_bias)
    k_caches_t, v_caches_t = caches_t(cache_swa_k), caches_t(cache_swa_v)
    states = state_conv.astype(F32)
    w_up_all = ffn_w_up.astype(BF16)
    w_down_all = ffn_w_down.astype(BF16)

    kp, vp, sgu_v_new, convp, convs = [], [], [], [], []
    new_caches = None
    for i in range(depth):
        m, j = mixer_of_layer[i], slot_of_layer[i]
        g_mix = row(norm_mix[i])
        if m == 0:
            w_qkv = attn_w_qkv[j].astype(BF16)
            w_o = attn_w_o[j].astype(BF16)
            bias_p, cap_p, sink_rows = _prompt_bias_tables(dist_bias, attn_sinks[j])
            xp, k1, v1 = _attn_prompt(xp, g_mix, w_qkv, w_o, bias_p, cap_p, sink_rows,
                                      tq=PROMPT_TILE)
            kp.append(k1.reshape(batch, WINDOW, N_KV_HEADS, HEAD_DIM))
            vp.append(v1.reshape(batch, WINDOW, N_KV_HEADS, HEAD_DIM))

            w_qkv_s = jnp.concatenate([regroup(w_qkv[:, :D_MODEL], 1), w_qkv[:, D_MODEL:]], axis=1)
            bias_s, extra_s = _sample_bias_tables(dist_bias, attn_sinks[j])
            qkv_s, kv_t = _norm_proj(xs, g_mix, w_qkv_s, kv_t=True)
            o_s, *new_caches = _attn_sample_core(qkv_s, kv_t, k_caches_t, v_caches_t, bias_s,
                                                 extra_s, j, new_caches, tb=SAMPLE_ATTN_TILE)
            xs = _proj_res(xs, o_s, regroup(w_o, 0))
        elif m == 1:
            w_in = sgu_w_in[j].astype(BF16)
            w_out = sgu_w_out[j].astype(BF16)
            ln_g, ln_b = row(sgu_ln_g[j]), row(sgu_ln_b[j])
            sp = jnp.tril(sgu_w_spatial[j]).astype(BF16)
            bsp = jnp.broadcast_to(sgu_b_spatial[j].astype(F32)[:, :, None],
                                   (SGU_GROUPS, CHUNK, LANES))
            xp = _sgu(xp.reshape(batch * seq, D_MODEL), g_mix, w_in, ln_g, ln_b, sp, bsp, w_out,
                      tm=PROMPT_TILE, sample=False).reshape(batch, seq, D_MODEL)
            sp0 = row(jnp.repeat(sgu_w_spatial[j][:, 0, 0], LANES))
            bsp0 = row(jnp.repeat(sgu_b_spatial[j][:, 0], LANES))
            xs, v_rows = _sgu(xs, g_mix, w_in, ln_g, ln_b, sp0, bsp0, w_out, tm=dec, sample=True)
            sgu_v_new.append(v_rows.reshape(dec, 1, D_MODEL))
        else:
            w_in = conv_w_in[j].astype(BF16)
            w_out = conv_w_out[j].astype(BF16)
            w_dw = conv_w_dw[j].astype(F32)
            b_dw = row(conv_b_dw[j])
            ln_g, ln_b = row(conv_ln_g[j]), row(conv_ln_b[j])
            w_dw8 = jnp.broadcast_to(w_dw[:, None, :], (CONV_WIDTH, SUBLANES, D_MODEL))
            b_dw8 = jnp.broadcast_to(b_dw, (SUBLANES, D_MODEL))
            xp, tail = _conv_prompt(xp, g_mix, w_in, w_dw8, b_dw8, ln_g, ln_b, w_out,
                                    tm=PROMPT_TILE)
            convp.append(tail[:, CONV_HALO - (CONV_WIDTH - 1):, :])
            a_s = _norm_proj(xs, g_mix, w_in, glu=True)
            c_s, st2 = _conv_sample_core(a_s, states, w_dw, b_dw, j, tb=SAMPLE_CONV_TILE)
            xs = _proj_res(xs, c_s, w_out, ln=(ln_g, ln_b))
            convs.append(st2)

        g_fin = row(norm_final) if i == depth - 1 else None
        xp, xs = _ffn(xp.reshape(batch * seq, D_MODEL), xs, row(norm_ffn[i]), w_up_all, w_down_all,
                      i, g_fin, tm=PROMPT_TILE)
        xp = xp.reshape(batch, seq, D_MODEL)

    y_prompt = xp
    y_sample = xs.reshape(dec, 1, D_MODEL)
    k_new, v_new = (caches_from_t(c) for c in new_caches)
    return (y_prompt, y_sample, jnp.stack(kp), jnp.stack(vp), k_new, v_new,
            jnp.stack(sgu_v_new), jnp.stack(convp), jnp.stack(convs))
```

```python
import functools
import math

import jax
import jax.numpy as jnp
from jax import lax
from jax.experimental import pallas as pl
from jax.experimental.pallas import tpu as pltpu

D_MODEL = 1024
HEAD_DIM = 64
N_HEADS = 16
N_KV_HEADS = 4
Q_PER_KV = 4
KV_DIM = N_KV_HEADS * HEAD_DIM
QKV_DIM = D_MODEL + 2 * KV_DIM
WINDOW = 128
ATTN_SCALE = HEAD_DIM ** -0.5
N_BUCKETS = 32
MAX_DISTANCE = 128
CHUNK = 128
SGU_GROUPS = 8
CONV_WIDTH = 31
CONV_HALO = 32
D_FF = 4 * D_MODEL
EPS = 1e-6
NEG_INF = -1e30
F32_MAX = float(jnp.finfo(jnp.float32).max)
INV_SQRT2 = 1.0 / math.sqrt(2.0)

LANES = 128
SUBLANES = 8
VMEM_LIMIT = 56 * 1024 * 1024

F32 = jnp.float32
BF16 = jnp.bfloat16


def _const_spec(shape):
    n = len(shape)
    return pl.BlockSpec(shape, lambda *_: (0,) * n, pipeline_mode=pl.Buffered(1))


def _params(*sem):
    return pltpu.CompilerParams(dimension_semantics=sem, vmem_limit_bytes=VMEM_LIMIT)


def _rms(x, g):
    return x * lax.rsqrt(jnp.mean(x * x, axis=-1, keepdims=True) + EPS) * g


def _layer_norm(x, g, b):
    mu = jnp.mean(x, axis=-1, keepdims=True)
    xc = x - mu
    var = jnp.mean(xc * xc, axis=-1, keepdims=True)
    return xc * lax.rsqrt(var + EPS) * g + b


def _dot(a, b):
    return jnp.dot(a, b, preferred_element_type=F32)


def _dot_nt(a, b):
    return lax.dot_general(a, b, (((1,), (1,)), ((), ())), preferred_element_type=F32)


FFN_CHUNK = 1024


def _layer_spec(shape, layer):
    zeros = (0,) * (len(shape) - 1)
    return pl.BlockSpec((None,) + tuple(shape[1:]), lambda *_: (layer,) + zeros,
                        pipeline_mode=pl.Buffered(1))


def _ffn_kernel(xp_ref, xs_ref, g_ref, wu_ref, wd_ref, *rest, final, n_prompt_tiles):
    if final:
        gf_ref, op_ref, os_ref = rest
    else:
        op_ref, os_ref = rest

    def block(x_ref, o_ref):
        x = x_ref[...]
        h = _rms(x, g_ref[...]).astype(BF16)
        y = x
        for c in range(D_FF // FFN_CHUNK):
            cols = slice(c * FFN_CHUNK, (c + 1) * FFN_CHUNK)
            u = _dot(h, wu_ref[:, cols])
            u = jnp.square(jnp.maximum(u, 0.0)).astype(BF16)
            y = y + _dot(u, wd_ref[cols, :])
        o_ref[...] = _rms(y, gf_ref[...]) if final else y

    i = pl.program_id(0)

    @pl.when(i < n_prompt_tiles)
    def _():
        block(xp_ref, op_ref)

    @pl.when(i == n_prompt_tiles)
    def _():
        block(xs_ref, os_ref)


def _ffn(xp, xs, g, w_up, w_down, layer, g_final=None, *, tm):
    n = xp.shape[0]
    nt = n // tm
    row = pl.BlockSpec((tm, D_MODEL), lambda i: (jnp.minimum(i, nt - 1), 0))
    in_specs = [row, _const_spec(xs.shape), _const_spec((1, D_MODEL)),
                _layer_spec(w_up.shape, layer), _layer_spec(w_down.shape, layer)]
    args = [xp, xs, g, w_up, w_down]
    if g_final is not None:
        in_specs.append(_const_spec((1, D_MODEL)))
        args.append(g_final)
    return pl.pallas_call(
        functools.partial(_ffn_kernel, final=g_final is not None, n_prompt_tiles=nt),
        out_shape=(jax.ShapeDtypeStruct((n, D_MODEL), F32),
                   jax.ShapeDtypeStruct(xs.shape, F32)),
        grid=(nt + 1,),
        in_specs=in_specs,
        out_specs=(row, pl.BlockSpec(xs.shape, lambda i: (0, 0))),
        compiler_params=_params("arbitrary"),
        name="ffn_final" if g_final is not None else "ffn",
    )(*args)


def _norm_proj_kernel(x_ref, g_ref, w_ref, o_ref, *rest, glu, kv_t):
    h = _rms(x_ref[...], g_ref[...]).astype(BF16)
    y = _dot(h, w_ref[...])
    if glu:
        half = y.shape[1] // 2
        y = y[:, :half] * jax.nn.sigmoid(y[:, half:])
    o_ref[...] = y
    if kv_t:
        (t_ref,) = rest
        t_ref[...] = y[:, D_MODEL:].T


def _norm_proj(x, g, w, *, glu=False, kv_t=False):
    n = x.shape[0]
    n_out = w.shape[1] // 2 if glu else w.shape[1]
    out_shape = [jax.ShapeDtypeStruct((n, n_out), F32)]
    if kv_t:
        out_shape.append(jax.ShapeDtypeStruct((n_out - D_MODEL, n), F32))
    out = pl.pallas_call(
        functools.partial(_norm_proj_kernel, glu=glu, kv_t=kv_t),
        out_shape=tuple(out_shape),
        grid=(1,),
        in_specs=[_const_spec(x.shape), _const_spec(g.shape), _const_spec(w.shape)],
        out_specs=tuple(_const_spec(o.shape) for o in out_shape),
        compiler_params=_params("arbitrary"),
        name="norm_proj_glu" if glu else "norm_proj",
    )(x, g, w)
    return out if kv_t else out[0]


def _proj_res_kernel(x_ref, a_ref, w_ref, *rest, conv_tail):
    a = a_ref[...]
    if conv_tail:
        lg_ref, lb_ref, o_ref = rest
        a = _layer_norm(a, lg_ref[...], lb_ref[...])
        a = a * jax.nn.sigmoid(a)
    else:
        (o_ref,) = rest
    o_ref[...] = x_ref[...] + _dot(a.astype(BF16), w_ref[...])


def _proj_res(x, a, w, ln=None):
    args = [x, a, w] + (list(ln) if ln is not None else [])
    return pl.pallas_call(
        functools.partial(_proj_res_kernel, conv_tail=ln is not None),
        out_shape=jax.ShapeDtypeStruct(x.shape, F32),
        grid=(1,),
        in_specs=[_const_spec(t.shape) for t in args],
        out_specs=_const_spec(x.shape),
        compiler_params=_params("arbitrary"),
        name="proj_res_ln" if ln is not None else "proj_res",
    )(*args)


def _t5_bucket(dist):
    n = jnp.maximum(dist, 0)
    max_exact = N_BUCKETS // 2
    nf = jnp.maximum(n, 1).astype(F32)
    large = max_exact + (jnp.log(nf / max_exact) / math.log(MAX_DISTANCE / max_exact)
                         * (N_BUCKETS - max_exact)).astype(jnp.int32)
    large = jnp.minimum(large, N_BUCKETS - 1)
    return jnp.where(n < max_exact, n, large)


def _distance_bias(rel_bias):
    buckets = _t5_bucket(jnp.arange(WINDOW + 1, dtype=jnp.int32))
    onehot = (buckets[:, None] == jnp.arange(N_BUCKETS, dtype=jnp.int32)[None, :]).astype(F32)
    return jnp.dot(onehot, rel_bias.astype(F32), precision=lax.Precision.HIGHEST)


def _prompt_bias_tables(dist_bias, sinks):
    period = 3 * WINDOW
    line = jnp.concatenate([jnp.broadcast_to(dist_bias[WINDOW:], (WINDOW - 1, N_HEADS)),
                            dist_bias[::-1],
                            jnp.broadcast_to(dist_bias[:1], (WINDOW, N_HEADS))]).T
    skew = jnp.tile(line, (1, WINDOW))[:, :WINDOW * (period - 1)]
    per_head = skew.reshape(N_HEADS, WINDOW, period - 1)[:, :, WINDOW - 1:period - 1]
    per_head = per_head.reshape(N_KV_HEADS, 2, 2, WINDOW, 2 * WINDOW)
    bias = per_head.transpose(0, 2, 4, 1, 3).reshape(N_KV_HEADS, 4 * WINDOW, 2 * WINDOW)

    qi = jnp.arange(WINDOW, dtype=jnp.int32)[None, :]
    kj = jnp.arange(2 * WINDOW, dtype=jnp.int32)[:, None]
    dist = qi - kj + WINDOW
    allowed = (dist >= 0) & (dist <= WINDOW)
    first = allowed & (kj >= WINDOW)
    cap = jnp.stack([jnp.where(allowed, F32_MAX, NEG_INF), jnp.where(first, F32_MAX, NEG_INF)])
    cap = jnp.tile(cap.astype(F32), (1, 2, 2))

    sink_rows = sinks.astype(F32).reshape(N_KV_HEADS, 2, 2).transpose(0, 2, 1).reshape(2 * N_KV_HEADS, 2)
    sink_rows = jnp.repeat(sink_rows, WINDOW, axis=1)
    return bias, cap, sink_rows


ATTN_PARTS = 2


def _attn_prompt_kernel(x_ref, g_ref, wqkv_ref, wo_ref, bias_ref, cap_ref, sink_ref,
                        o_ref, kc_ref, vc_ref,
                        q_scr, o_scr, klo_scr, khi_scr, vt_scr, *, tq):
    i = pl.program_id(1)
    n_i = pl.num_programs(1)
    tp = tq // ATTN_PARTS

    @pl.when(i == 0)
    def _():
        klo_scr[:, 0:WINDOW, :] = jnp.zeros((N_KV_HEADS, WINDOW, LANES), BF16)
        khi_scr[:, 0:WINDOW, :] = jnp.zeros((N_KV_HEADS, WINDOW, LANES), BF16)
        vt_scr[:, :, 0:WINDOW] = jnp.zeros((N_KV_HEADS, HEAD_DIM, WINDOW), BF16)

    @pl.when(i > 0)
    def _():
        klo_scr[:, 0:WINDOW, :] = klo_scr[:, tq:tq + WINDOW, :]
        khi_scr[:, 0:WINDOW, :] = khi_scr[:, tq:tq + WINDOW, :]
        vt_scr[:, :, 0:WINDOW] = vt_scr[:, :, tq:tq + WINDOW]

    low = lax.broadcasted_iota(jnp.int32, (tp, LANES), 1) < HEAD_DIM
    is_first = jnp.where(i == 0, 1, 0)

    def project(part):
        rows = slice(part * tp, (part + 1) * tp)
        held = slice(WINDOW + part * tp, WINDOW + (part + 1) * tp)
        h = _rms(x_ref[rows, :], g_ref[...]).astype(BF16)
        qkv = _dot(h, wqkv_ref[...])
        q_scr[rows, :] = (qkv[:, :D_MODEL] * ATTN_SCALE).astype(BF16)
        k = qkv[:, D_MODEL:D_MODEL + KV_DIM]
        v = qkv[:, D_MODEL + KV_DIM:]
        if part == ATTN_PARTS - 1:
            @pl.when(i == n_i - 1)
            def _():
                kc_ref[...] = k[tp - WINDOW:, :]
                vc_ref[...] = v[tp - WINDOW:, :]
        for c in range(KV_DIM // LANES):
            kc = k[:, c * LANES:(c + 1) * LANES]
            kr = pltpu.roll(kc, HEAD_DIM, axis=1)
            klo_scr[2 * c, held, :] = jnp.where(low, kc, 0.0).astype(BF16)
            khi_scr[2 * c, held, :] = jnp.where(low, 0.0, kr).astype(BF16)
            klo_scr[2 * c + 1, held, :] = jnp.where(low, kr, 0.0).astype(BF16)
            khi_scr[2 * c + 1, held, :] = jnp.where(low, 0.0, kc).astype(BF16)
            vt = v[:, c * LANES:(c + 1) * LANES].T.astype(BF16)
            vt_scr[2 * c, :, held] = vt[:HEAD_DIM]
            vt_scr[2 * c + 1, :, held] = vt[HEAD_DIM:]

    def scores(jb, kh):
        rows = slice(jb * WINDOW, (jb + 1) * WINDOW)
        band = slice(jb * WINDOW, (jb + 2) * WINDOW)
        cap = cap_ref[is_first] if jb == 0 else cap_ref[0]
        qst = jnp.concatenate([q_scr[rows, (2 * kh) * LANES:(2 * kh + 1) * LANES],
                               q_scr[rows, (2 * kh + 1) * LANES:(2 * kh + 2) * LANES]], axis=0)
        kst = jnp.concatenate([klo_scr[kh, band, :], khi_scr[kh, band, :]], axis=0)
        return jnp.minimum(_dot_nt(kst, qst) + bias_ref[kh], cap)

    def attend(jb, kh, s):
        rows = slice(jb * WINDOW, (jb + 1) * WINDOW)
        vt = vt_scr[kh, :, jb * WINDOW:(jb + 2) * WINDOW]
        halves = []
        for half in range(2):
            sh = s[half * 2 * WINDOW:(half + 1) * 2 * WINDOW]
            sink = sink_ref[2 * kh + half:2 * kh + half + 1, :]
            m = jnp.maximum(jnp.max(sh, axis=0, keepdims=True), sink)
            p = jnp.exp(sh - m)
            denom = jnp.sum(p, axis=0, keepdims=True) + jnp.exp(sink - m)
            halves.append(_dot(vt, p.astype(BF16)) * (1.0 / denom))
        o = jnp.concatenate(halves, axis=0).T.astype(BF16)
        o_scr[rows, (2 * kh) * LANES:(2 * kh + 1) * LANES] = o[:WINDOW]
        o_scr[rows, (2 * kh + 1) * LANES:(2 * kh + 2) * LANES] = o[WINDOW:]

    def attention(part):
        blocks = range(part * tp // WINDOW, (part + 1) * tp // WINDOW)
        steps = [(jb, kh) for jb in blocks for kh in range(N_KV_HEADS)]
        s_next = scores(*steps[0])
        for n, step in enumerate(steps):
            s_cur = s_next
            if n + 1 < len(steps):
                s_next = scores(*steps[n + 1])
            attend(*step, s_cur)

    def out_project(part):
        rows = slice(part * tp, (part + 1) * tp)
        o_ref[rows, :] = x_ref[rows, :] + _dot(o_scr[rows, :], wo_ref[...])

    project(0)
    for part in range(ATTN_PARTS):
        if part + 1 < ATTN_PARTS:
            project(part + 1)
        attention(part)
        out_project(part)


def _attn_prompt(x, g, w_qkv, w_o, bias, cap, sink_rows, *, tq):
    b, s, _ = x.shape
    row = pl.BlockSpec((None, tq, D_MODEL), lambda bi, i: (bi, i, 0))
    cache = pl.BlockSpec((None, WINDOW, KV_DIM), lambda bi, i: (bi, 0, 0))
    k_scr = pltpu.VMEM((N_KV_HEADS, WINDOW + tq, LANES), BF16)
    return pl.pallas_call(
        functools.partial(_attn_prompt_kernel, tq=tq),
        out_shape=(jax.ShapeDtypeStruct(x.shape, F32),
                   jax.ShapeDtypeStruct((b, WINDOW, KV_DIM), F32),
                   jax.ShapeDtypeStruct((b, WINDOW, KV_DIM), F32)),
        grid=(b, s // tq),
        in_specs=[row, _const_spec((1, D_MODEL)),
                  _const_spec((D_MODEL, QKV_DIM)), _const_spec((D_MODEL, D_MODEL)),
                  _const_spec(bias.shape), _const_spec(cap.shape), _const_spec(sink_rows.shape)],
        out_specs=(row, cache, cache),
        scratch_shapes=[pltpu.VMEM((tq, D_MODEL), BF16), pltpu.VMEM((tq, D_MODEL), BF16),
                        k_scr, k_scr, pltpu.VMEM((N_KV_HEADS, HEAD_DIM, WINDOW + tq), BF16)],
        compiler_params=_params("parallel", "arbitrary"),
        name="attn_prompt",
    )(x, g, w_qkv, w_o, bias, cap, sink_rows)


ROWS_PAD = 8 * Q_PER_KV


def _sample_bias_tables(dist_bias, sinks):
    def rows(per_head):
        t = per_head.reshape(N_KV_HEADS, Q_PER_KV, -1).transpose(1, 0, 2)
        return jnp.pad(t, ((0, 0), (0, 8 - N_KV_HEADS), (0, 0))).reshape(ROWS_PAD, -1)

    bias = rows(dist_bias[:0:-1].T)
    extra = rows(jnp.stack([dist_bias[0], sinks.astype(F32)], axis=1))
    return bias, jnp.pad(extra, ((0, 0), (0, LANES - 2)))


def _attn_sample_kernel(q_ref, kn_ref, vn_ref, kvt_ref, kc_ref, vc_ref, bias_ref, extra_ref,
                        *rest, tb, n_aliased):
    o_ref, ko_ref, vo_ref = rest[n_aliased:]
    t = pl.program_id(0)
    sub = lax.broadcasted_iota(jnp.int32, (8, KV_DIM), 0)
    lane_head = lax.broadcasted_iota(jnp.int32, (8, KV_DIM), 1) // HEAD_DIM
    own = jnp.logical_and(sub < N_KV_HEADS, lane_head == sub)
    newest = lax.broadcasted_iota(jnp.int32, (KV_DIM, WINDOW), 1) == WINDOW - 1
    bias = bias_ref[...]
    bias_new = extra_ref[:, 0:1]
    sink = extra_ref[:, 1:2]

    def scores(bb):
        q_rows = []
        for gq in range(Q_PER_KV):
            q_g = q_ref[bb:bb + 1, gq * KV_DIM:(gq + 1) * KV_DIM] * ATTN_SCALE
            q_rows.append(jnp.where(own, jnp.broadcast_to(q_g, (8, KV_DIM)), 0.0))
        q_blk = jnp.concatenate(q_rows, axis=0)
        s_old = _dot(q_blk.astype(BF16), kc_ref[bb].astype(BF16)) + bias
        s_new = jnp.sum(q_blk * kn_ref[bb:bb + 1, :], axis=1, keepdims=True) + bias_new
        return s_old, s_new

    def attend(bb, s_old, s_new):
        m = jnp.maximum(jnp.max(s_old, axis=1, keepdims=True), jnp.maximum(s_new, sink))
        p_old = jnp.exp(s_old - m)
        p_new = jnp.exp(s_new - m)
        denom = jnp.sum(p_old, axis=1, keepdims=True) + p_new + jnp.exp(sink - m)
        o = _dot_nt(p_old.astype(BF16), vc_ref[bb].astype(BF16)) + p_new * vn_ref[bb:bb + 1, :]
        o = o / denom
        for gq in range(Q_PER_KV):
            o_g = jnp.sum(jnp.where(own, o[8 * gq:8 * gq + 8], 0.0), axis=0, keepdims=True)
            o_ref[bb:bb + 1, gq * KV_DIM:(gq + 1) * KV_DIM] = o_g

    def shift(bb):
        to_last = (WINDOW - 1) - (t * tb + bb)
        for cache_ref, row0, out_ref in ((kc_ref, 0, ko_ref), (vc_ref, KV_DIM, vo_ref)):
            moved = pltpu.roll(cache_ref[bb], WINDOW - 1, axis=1)
            col = pltpu.roll(kvt_ref[row0:row0 + KV_DIM, :], to_last, axis=1)
            out_ref[bb] = jnp.where(newest, col, moved)

    for bb in range(tb):
        shift(bb)
    s_next = scores(0)
    for bb in range(tb):
        s_cur = s_next
        if bb + 1 < tb:
            s_next = scores(bb + 1)
        attend(bb, *s_cur)


def _attn_sample_core(qkv, kv_t, k_caches, v_caches, bias, extra, layer, new_caches=None, *, tb):
    n = qkv.shape[0]
    cache = pl.BlockSpec((None, tb, KV_DIM, WINDOW), lambda t: (layer, t, 0, 0))
    kv_col = D_MODEL // KV_DIM
    args = [qkv, qkv, qkv, kv_t, k_caches, v_caches, bias, extra]
    in_specs = [pl.BlockSpec((tb, D_MODEL), lambda t: (t, 0)),
                pl.BlockSpec((tb, KV_DIM), lambda t: (t, kv_col)),
                pl.BlockSpec((tb, KV_DIM), lambda t: (t, kv_col + 1)),
                _const_spec(kv_t.shape), cache, cache,
                _const_spec(bias.shape), _const_spec(extra.shape)]
    aliases = {}
    if new_caches is not None:
        aliases = {len(args): 1, len(args) + 1: 2}
        args += list(new_caches)
        in_specs += [pl.BlockSpec(memory_space=pl.ANY)] * 2
    return pl.pallas_call(
        functools.partial(_attn_sample_kernel, tb=tb, n_aliased=len(aliases)),
        out_shape=(jax.ShapeDtypeStruct((n, D_MODEL), F32),
                   jax.ShapeDtypeStruct(k_caches.shape, F32),
                   jax.ShapeDtypeStruct(v_caches.shape, F32)),
        grid=(n // tb,),
        in_specs=in_specs,
        out_specs=(pl.BlockSpec((tb, D_MODEL), lambda t: (t, 0)), cache, cache),
        input_output_aliases=aliases,
        compiler_params=_params("parallel"),
        name="attn_sample",
    )(*args)


def _sgu_kernel(x_ref, g_ref, win_ref, lg_ref, lb_ref, sp_ref, bsp_ref, wout_ref, *rest,
                tm, sample):
    x = x_ref[...]
    h = _rms(x, g_ref[...]).astype(BF16)
    z = _dot(h, win_ref[...])
    z = 0.5 * z * (1.0 + lax.erf(z * INV_SQRT2))
    u = z[:, :D_MODEL]
    v = _layer_norm(z[:, D_MODEL:], lg_ref[...], lb_ref[...])
    if sample:
        o_ref, v_ref = rest
        v_ref[...] = v
        gated = (u * (v * sp_ref[...] + bsp_ref[...])).astype(BF16)
    else:
        o_ref, gated_scr = rest
        vb = v.astype(BF16)
        for c in range(tm // CHUNK):
            rows = slice(c * CHUNK, (c + 1) * CHUNK)
            for gi in range(SGU_GROUPS):
                cols = slice(gi * LANES, (gi + 1) * LANES)
                mixed = _dot(sp_ref[gi], vb[rows, cols]) + bsp_ref[gi]
                gated_scr[rows, cols] = (u[rows, cols] * mixed).astype(BF16)
        gated = gated_scr[...]
    o_ref[...] = x + _dot(gated, wout_ref[...])


def _sgu(x, g, w_in, ln_g, ln_b, sp, bsp, w_out, *, tm, sample):
    n = x.shape[0]
    row = pl.BlockSpec((tm, D_MODEL), lambda i: (i, 0))
    args = [x, g, w_in, ln_g, ln_b, sp, bsp, w_out]
    in_specs = [row] + [_const_spec(t.shape) for t in args[1:]]
    if sample:
        out_shape = (jax.ShapeDtypeStruct((n, D_MODEL), F32),) * 2
        out_specs = (row, row)
        scratch = []
    else:
        out_shape = jax.ShapeDtypeStruct((n, D_MODEL), F32)
        out_specs = row
        scratch = [pltpu.VMEM((tm, D_MODEL), BF16)]
    return pl.pallas_call(
        functools.partial(_sgu_kernel, tm=tm, sample=sample),
        out_shape=out_shape,
        grid=(n // tm,),
        in_specs=in_specs,
        out_specs=out_specs,
        scratch_shapes=scratch,
        compiler_params=_params("parallel"),
        name="sgu_sample" if sample else "sgu_prompt",
    )(*args)


CONV_ROWS = 64
CONV_LANES = 256
PROJ_LANES = 256


def _conv_prompt_kernel(x_ref, xprev_ref, g_ref, win_ref, wdw_ref, bdw_ref, lg_ref, lb_ref,
                        wout_ref, o_ref, tail_ref, a0_scr, a1_scr, sh_scr, c_scr,
                        *, tm, tiles_per_seq):
    s = pl.program_id(0)

    @pl.when(s == 0)
    def _():
        a0_scr[...] = jnp.zeros(a0_scr.shape, F32)
        a1_scr[...] = jnp.zeros(a1_scr.shape, F32)

    first_tap = CONV_HALO - (CONV_WIDTH - 1)
    reps = CONV_ROWS // SUBLANES

    def step(a_new, a_old):
        for r in range(1, SUBLANES):
            sh_scr[r - 1] = a_old[r:r + tm + CONV_HALO - SUBLANES, :]

        h = _rms(x_ref[...], g_ref[...]).astype(BF16)
        starts_sequence = s % tiles_per_seq == 0
        a_new[0:CONV_HALO, :] = jnp.where(starts_sequence, 0.0, a_old[tm:tm + CONV_HALO, :])

        def project(jc):
            cols = slice(jc * PROJ_LANES, (jc + 1) * PROJ_LANES)
            gate = slice(D_MODEL + jc * PROJ_LANES, D_MODEL + (jc + 1) * PROJ_LANES)
            a = _dot(h, win_ref[:, cols]) * jax.nn.sigmoid(_dot(h, win_ref[:, gate]))
            a_new[CONV_HALO:, cols] = a
            tail_ref[:, cols] = a[tm - CONV_HALO:, :]

        def convolve(rb):
            for lc in range(D_MODEL // CONV_LANES):
                cols = slice(lc * CONV_LANES, (lc + 1) * CONV_LANES)
                acc = jnp.concatenate([bdw_ref[:, cols]] * reps, axis=0)
                for kk in range(CONV_WIDTH):
                    whole, r = divmod(first_tap + kk, SUBLANES)
                    start = rb * CONV_ROWS + whole * SUBLANES
                    src = a_old if r == 0 else sh_scr.at[r - 1]
                    w = jnp.concatenate([wdw_ref[kk, :, cols]] * reps, axis=0)
                    acc = acc + src[start:start + CONV_ROWS, cols] * w
                c_scr[rb * CONV_ROWS:(rb + 1) * CONV_ROWS, cols] = acc

        n_proj = D_MODEL // PROJ_LANES
        n_conv = tm // CONV_ROWS
        for jc in range(n_proj):
            project(jc)
            for rb in range(jc * n_conv // n_proj, (jc + 1) * n_conv // n_proj):
                convolve(rb)
        c = _layer_norm(c_scr[...], lg_ref[...], lb_ref[...])
        c = (c * jax.nn.sigmoid(c)).astype(BF16)
        o_ref[...] = xprev_ref[...] + _dot(c, wout_ref[...])

    @pl.when(s % 2 == 0)
    def _():
        step(a0_scr, a1_scr)

    @pl.when(s % 2 == 1)
    def _():
        step(a1_scr, a0_scr)


def _conv_prompt(x, g, w_in, w_dw8, b_dw8, ln_g, ln_b, w_out, *, tm):
    b, seq, _ = x.shape
    tiles_per_seq = seq // tm
    n = b * tiles_per_seq
    consts = [g, w_in, w_dw8, b_dw8, ln_g, ln_b, w_out]
    a_scr = pltpu.VMEM((CONV_HALO + tm, D_MODEL), F32)
    x2 = x.reshape(b * seq, D_MODEL)
    out, tail = pl.pallas_call(
        functools.partial(_conv_prompt_kernel, tm=tm, tiles_per_seq=tiles_per_seq),
        out_shape=(jax.ShapeDtypeStruct((b * seq, D_MODEL), F32),
                   jax.ShapeDtypeStruct((b, CONV_HALO, D_MODEL), F32)),
        grid=(n + 1,),
        in_specs=[pl.BlockSpec((tm, D_MODEL), lambda s: (jnp.minimum(s, n - 1), 0)),
                  pl.BlockSpec((tm, D_MODEL), lambda s: (jnp.maximum(s - 1, 0), 0))]
                 + [_const_spec(t.shape) for t in consts],
        out_specs=(pl.BlockSpec((tm, D_MODEL), lambda s: (jnp.maximum(s - 1, 0), 0)),
                   pl.BlockSpec((None, CONV_HALO, D_MODEL),
                                lambda s: (jnp.minimum(s, n - 1) // tiles_per_seq, 0, 0))),
        scratch_shapes=[a_scr, a_scr,
                        pltpu.VMEM((SUBLANES - 1, CONV_HALO + tm - SUBLANES, D_MODEL), F32),
                        pltpu.VMEM((tm, D_MODEL), F32)],
        compiler_params=_params("arbitrary"),
        name="conv_prompt",
    )(x2, x2, *consts)
    return out.reshape(b, seq, D_MODEL), tail


def _conv_sample_kernel(a_ref, st_ref, wdw_ref, bdw_ref, c_ref, so_ref, *, tb):
    n_hist = CONV_WIDTH - 1
    st = st_ref[...]
    hist = jnp.sum(st * wdw_ref[0:n_hist, :][None], axis=1)
    c_ref[...] = hist + a_ref[...] * wdw_ref[n_hist:CONV_WIDTH, :] + bdw_ref[...]
    so_ref[:, 0:n_hist - 1, :] = st_ref[:, 1:n_hist, :]

    def one(b, carry):
        so_ref[b, n_hist - 1:n_hist, :] = a_ref[pl.ds(b, 1), :]
        return carry

    lax.fori_loop(0, tb, one, 0)


def _conv_sample_core(a, states, w_dw, b_dw, layer, *, tb):
    n = a.shape[0]
    row = pl.BlockSpec((tb, D_MODEL), lambda t: (t, 0))
    st_in = pl.BlockSpec((None, tb, CONV_WIDTH - 1, D_MODEL), lambda t: (layer, t, 0, 0))
    st_out = pl.BlockSpec((tb, CONV_WIDTH - 1, D_MODEL), lambda t: (t, 0, 0))
    return pl.pallas_call(
        functools.partial(_conv_sample_kernel, tb=tb),
        out_shape=(jax.ShapeDtypeStruct((n, D_MODEL), F32),
                   jax.ShapeDtypeStruct(states.shape[1:], F32)),
        grid=(n // tb,),
        in_specs=[row, st_in, _const_spec(w_dw.shape), _const_spec(b_dw.shape)],
        out_specs=(row, st_out),
        compiler_params=_params("parallel"),
        name="conv_sample",
    )(a, states, w_dw, b_dw)


PROMPT_TILE = 512
FFN_TILE = 1024
SAMPLE_ATTN_TILE = 16
SAMPLE_CONV_TILE = 16


def kernel(x_prompt, x_sample, cache_swa_k, cache_swa_v, state_conv, rel_bias, norm_mix, norm_ffn, norm_final, attn_w_qkv, attn_w_o, attn_sinks, sgu_w_in, sgu_ln_g, sgu_ln_b, sgu_w_spatial, sgu_b_spatial, sgu_w_out, conv_w_in, conv_w_dw, conv_b_dw, conv_ln_g, conv_ln_b, conv_w_out, ffn_w_up, ffn_w_down):
    batch, seq, _ = x_prompt.shape
    dec = x_sample.shape[0]
    depth = norm_mix.shape[0]
    mixer_of_layer = tuple(i % 3 for i in range(depth))
    slot_of_layer = tuple(mixer_of_layer[:i].count(mixer_of_layer[i]) for i in range(depth))

    def row(v):
        return v.reshape(1, -1).astype(F32)

    def regroup(w, axis):
        shape = w.shape[:axis] + (N_KV_HEADS, Q_PER_KV, HEAD_DIM) + w.shape[axis + 1:]
        return jnp.swapaxes(w.reshape(shape), axis, axis + 1).reshape(w.shape)

    n_attn = attn_w_qkv.shape[0]

    def caches_t(c):
        return jnp.transpose(c, (0, 1, 3, 4, 2)).reshape(n_attn, dec, KV_DIM, WINDOW)

    def caches_from_t(c):
        c = c.reshape(n_attn, dec, N_KV_HEADS, HEAD_DIM, WINDOW)
        return jnp.transpose(c, (0, 1, 4, 2, 3))

    xp = x_prompt
    xs = x_sample.reshape(dec, D_MODEL)
    dist_bias = _distance_bias(rel_bias)
    k_caches_t, v_caches_t = caches_t(cache_swa_k), caches_t(cache_swa_v)
    states = state_conv.astype(F32)
    w_up_all = ffn_w_up.astype(BF16)
    w_down_all = ffn_w_down.astype(BF16)

    kp, vp, sgu_v_new, convp, convs = [], [], [], [], []
    new_caches = None
    for i in range(depth):
        m, j = mixer_of_layer[i], slot_of_layer[i]
        g_mix = row(norm_mix[i])
        if m == 0:
            w_qkv = attn_w_qkv[j].astype(BF16)
            w_o = attn_w_o[j].astype(BF16)
            bias_p, cap_p, sink_rows = _prompt_bias_tables(dist_bias, attn_sinks[j])
            xp, k1, v1 = _attn_prompt(xp, g_mix, w_qkv, w_o, bias_p, cap_p, sink_rows,
                                      tq=PROMPT_TILE)
            kp.append(k1.reshape(batch, WINDOW, N_KV_HEADS, HEAD_DIM))
            vp.append(v1.reshape(batch, WINDOW, N_KV_HEADS, HEAD_DIM))

            w_qkv_s = jnp.concatenate([regroup(w_qkv[:, :D_MODEL], 1), w_qkv[:, D_MODEL:]], axis=1)
            bias_s, extra_s = _sample_bias_tables(dist_bias, attn_sinks[j])
            qkv_s, kv_t = _norm_proj(xs, g_mix, w_qkv_s, kv_t=True)
            o_s, *new_caches = _attn_sample_core(qkv_s, kv_t, k_caches_t, v_caches_t, bias_s,
                                                 extra_s, j, new_caches, tb=SAMPLE_ATTN_TILE)
            xs = _proj_res(xs, o_s, regroup(w_o, 0))
        elif m == 1:
            w_in = sgu_w_in[j].astype(BF16)
            w_out = sgu_w_out[j].astype(BF16)
            ln_g, ln_b = row(sgu_ln_g[j]), row(sgu_ln_b[j])
            sp = jnp.tril(sgu_w_spatial[j]).astype(BF16)
            bsp = jnp.broadcast_to(sgu_b_spatial[j].astype(F32)[:, :, None],
                                   (SGU_GROUPS, CHUNK, LANES))
            xp = _sgu(xp.reshape(batch * seq, D_MODEL), g_mix, w_in, ln_g, ln_b, sp, bsp, w_out,
                      tm=PROMPT_TILE, sample=False).reshape(batch, seq, D_MODEL)
            sp0 = row(jnp.repeat(sgu_w_spatial[j][:, 0, 0], LANES))
            bsp0 = row(jnp.repeat(sgu_b_spatial[j][:, 0], LANES))
            xs, v_rows = _sgu(xs, g_mix, w_in, ln_g, ln_b, sp0, bsp0, w_out, tm=dec, sample=True)
            sgu_v_new.append(v_rows.reshape(dec, 1, D_MODEL))
        else:
            w_in = conv_w_in[j].astype(BF16)
            w_out = conv_w_out[j].astype(BF16)
            w_dw = conv_w_dw[j].astype(F32)
            b_dw = row(conv_b_dw[j])
            ln_g, ln_b = row(conv_ln_g[j]), row(conv_ln_b[j])
            w_dw8 = jnp.broadcast_to(w_dw[:, None, :], (CONV_WIDTH, SUBLANES, D_MODEL))
            b_dw8 = jnp.broadcast_to(b_dw, (SUBLANES, D_MODEL))
            xp, tail = _conv_prompt(xp, g_mix, w_in, w_dw8, b_dw8, ln_g, ln_b, w_out,
                                    tm=PROMPT_TILE)
            convp.append(tail[:, CONV_HALO - (CONV_WIDTH - 1):, :])
            a_s = _norm_proj(xs, g_mix, w_in, glu=True)
            c_s, st2 = _conv_sample_core(a_s, states, w_dw, b_dw, j, tb=SAMPLE_CONV_TILE)
            xs = _proj_res(xs, c_s, w_out, ln=(ln_g, ln_b))
            convs.append(st2)

        g_fin = row(norm_final) if i == depth - 1 else None
        xp, xs = _ffn(xp.reshape(batch * seq, D_MODEL), xs, row(norm_ffn[i]), w_up_all, w_down_all,
                      i, g_fin, tm=FFN_TILE)
        xp = xp.reshape(batch, seq, D_MODEL)

    y_prompt = xp
    y_sample = xs.reshape(dec, 1, D_MODEL)
    k_new, v_new = (caches_from_t(c) for c in new_caches)
    return (y_prompt, y_sample, jnp.stack(kp), jnp.stack(vp), k_new, v_new,
            jnp.stack(sgu_v_new), jnp.stack(convp), jnp.stack(convs))
```

```python
import functools
import math

import jax
import jax.numpy as jnp
from jax import lax
from jax.experimental import pallas as pl
from jax.experimental.pallas import tpu as pltpu

D_MODEL = 1024
HEAD_DIM = 64
N_HEADS = 16
N_KV_HEADS = 4
Q_PER_KV = 4
KV_DIM = N_KV_HEADS * HEAD_DIM
QKV_DIM = D_MODEL + 2 * KV_DIM
WINDOW = 128
ATTN_SCALE = HEAD_DIM ** -0.5
N_BUCKETS = 32
MAX_DISTANCE = 128
CHUNK = 128
SGU_GROUPS = 8
CONV_WIDTH = 31
CONV_HALO = 32
D_FF = 4 * D_MODEL
EPS = 1e-6
NEG_INF = -1e30
F32_MAX = float(jnp.finfo(jnp.float32).max)
INV_SQRT2 = 1.0 / math.sqrt(2.0)

LANES = 128
SUBLANES = 8
VMEM_LIMIT = 56 * 1024 * 1024

F32 = jnp.float32
BF16 = jnp.bfloat16


def _const_spec(shape):
    n = len(shape)
    return pl.BlockSpec(shape, lambda *_: (0,) * n, pipeline_mode=pl.Buffered(1))


def _params(*sem):
    return pltpu.CompilerParams(dimension_semantics=sem, vmem_limit_bytes=VMEM_LIMIT)


def _rms(x, g):
    return x * lax.rsqrt(jnp.mean(x * x, axis=-1, keepdims=True) + EPS) * g


def _layer_norm(x, g, b):
    mu = jnp.mean(x, axis=-1, keepdims=True)
    xc = x - mu
    var = jnp.mean(xc * xc, axis=-1, keepdims=True)
    return xc * lax.rsqrt(var + EPS) * g + b


def _dot(a, b):
    return jnp.dot(a, b, preferred_element_type=F32)


def _dot_nt(a, b):
    return lax.dot_general(a, b, (((1,), (1,)), ((), ())), preferred_element_type=F32)


FFN_CHUNK = 1024


def _layer_spec(shape, layer):
    zeros = (0,) * (len(shape) - 1)
    return pl.BlockSpec((None,) + tuple(shape[1:]), lambda *_: (layer,) + zeros,
                        pipeline_mode=pl.Buffered(1))


def _ffn_kernel(xp_ref, xs_ref, g_ref, wu_ref, wd_ref, *rest, final, n_prompt_tiles):
    if final:
        gf_ref, op_ref, os_ref = rest
    else:
        op_ref, os_ref = rest

    def block(x_ref, o_ref):
        x = x_ref[...]
        h = _rms(x, g_ref[...]).astype(BF16)
        y = x
        for c in range(D_FF // FFN_CHUNK):
            cols = slice(c * FFN_CHUNK, (c + 1) * FFN_CHUNK)
            u = _dot(h, wu_ref[:, cols])
            u = jnp.square(jnp.maximum(u, 0.0)).astype(BF16)
            y = y + _dot(u, wd_ref[cols, :])
        o_ref[...] = _rms(y, gf_ref[...]) if final else y

    i = pl.program_id(0)

    @pl.when(i < n_prompt_tiles)
    def _():
        block(xp_ref, op_ref)

    @pl.when(i == n_prompt_tiles)
    def _():
        block(xs_ref, os_ref)


def _ffn(xp, xs, g, w_up, w_down, layer, g_final=None, *, tm):
    n = xp.shape[0]
    nt = n // tm
    row = pl.BlockSpec((tm, D_MODEL), lambda i: (jnp.minimum(i, nt - 1), 0))
    in_specs = [row, _const_spec(xs.shape), _const_spec((1, D_MODEL)),
                _layer_spec(w_up.shape, layer), _layer_spec(w_down.shape, layer)]
    args = [xp, xs, g, w_up, w_down]
    if g_final is not None:
        in_specs.append(_const_spec((1, D_MODEL)))
        args.append(g_final)
    return pl.pallas_call(
        functools.partial(_ffn_kernel, final=g_final is not None, n_prompt_tiles=nt),
        out_shape=(jax.ShapeDtypeStruct((n, D_MODEL), F32),
                   jax.ShapeDtypeStruct(xs.shape, F32)),
        grid=(nt + 1,),
        in_specs=in_specs,
        out_specs=(row, pl.BlockSpec(xs.shape, lambda i: (0, 0))),
        compiler_params=_params("arbitrary"),
        name="ffn_final" if g_final is not None else "ffn",
    )(*args)


def _norm_proj_kernel(x_ref, g_ref, w_ref, o_ref, *rest, glu, kv_t):
    h = _rms(x_ref[...], g_ref[...]).astype(BF16)
    y = _dot(h, w_ref[...])
    if glu:
        half = y.shape[1] // 2
        y = y[:, :half] * jax.nn.sigmoid(y[:, half:])
    o_ref[...] = y
    if kv_t:
        (t_ref,) = rest
        t_ref[...] = y[:, D_MODEL:].T


def _norm_proj(x, g, w, *, glu=False, kv_t=False):
    n = x.shape[0]
    n_out = w.shape[1] // 2 if glu else w.shape[1]
    out_shape = [jax.ShapeDtypeStruct((n, n_out), F32)]
    if kv_t:
        out_shape.append(jax.ShapeDtypeStruct((n_out - D_MODEL, n), F32))
    out = pl.pallas_call(
        functools.partial(_norm_proj_kernel, glu=glu, kv_t=kv_t),
        out_shape=tuple(out_shape),
        grid=(1,),
        in_specs=[_const_spec(x.shape), _const_spec(g.shape), _const_spec(w.shape)],
        out_specs=tuple(_const_spec(o.shape) for o in out_shape),
        compiler_params=_params("arbitrary"),
        name="norm_proj_glu" if glu else "norm_proj",
    )(x, g, w)
    return out if kv_t else out[0]


def _proj_res_kernel(x_ref, a_ref, w_ref, *rest, conv_tail):
    a = a_ref[...]
    if conv_tail:
        lg_ref, lb_ref, o_ref = rest
        a = _layer_norm(a, lg_ref[...], lb_ref[...])
        a = a * jax.nn.sigmoid(a)
    else:
        (o_ref,) = rest
    o_ref[...] = x_ref[...] + _dot(a.astype(BF16), w_ref[...])


def _proj_res(x, a, w, ln=None):
    args = [x, a, w] + (list(ln) if ln is not None else [])
    return pl.pallas_call(
        functools.partial(_proj_res_kernel, conv_tail=ln is not None),
        out_shape=jax.ShapeDtypeStruct(x.shape, F32),
        grid=(1,),
        in_specs=[_const_spec(t.shape) for t in args],
        out_specs=_const_spec(x.shape),
        compiler_params=_params("arbitrary"),
        name="proj_res_ln" if ln is not None else "proj_res",
    )(*args)


def _t5_bucket(dist):
    n = jnp.maximum(dist, 0)
    max_exact = N_BUCKETS // 2
    nf = jnp.maximum(n, 1).astype(F32)
    large = max_exact + (jnp.log(nf / max_exact) / math.log(MAX_DISTANCE / max_exact)
                         * (N_BUCKETS - max_exact)).astype(jnp.int32)
    large = jnp.minimum(large, N_BUCKETS - 1)
    return jnp.where(n < max_exact, n, large)


def _distance_bias(rel_bias):
    buckets = _t5_bucket(jnp.arange(WINDOW + 1, dtype=jnp.int32))
    onehot = (buckets[:, None] == jnp.arange(N_BUCKETS, dtype=jnp.int32)[None, :]).astype(F32)
    return jnp.dot(onehot, rel_bias.astype(F32), precision=lax.Precision.HIGHEST)


def _prompt_bias_tables(dist_bias, sinks):
    period = 3 * WINDOW
    line = jnp.concatenate([jnp.broadcast_to(dist_bias[WINDOW:], (WINDOW - 1, N_HEADS)),
                            dist_bias[::-1],
                            jnp.broadcast_to(dist_bias[:1], (WINDOW, N_HEADS))]).T
    skew = jnp.tile(line, (1, WINDOW))[:, :WINDOW * (period - 1)]
    per_head = skew.reshape(N_HEADS, WINDOW, period - 1)[:, :, WINDOW - 1:period - 1]
    per_head = per_head.reshape(N_KV_HEADS, 2, 2, WINDOW, 2 * WINDOW)
    bias = per_head.transpose(0, 2, 4, 1, 3).reshape(N_KV_HEADS, 4 * WINDOW, 2 * WINDOW)

    qi = jnp.arange(WINDOW, dtype=jnp.int32)[None, :]
    kj = jnp.arange(2 * WINDOW, dtype=jnp.int32)[:, None]
    dist = qi - kj + WINDOW
    allowed = (dist >= 0) & (dist <= WINDOW)
    first = allowed & (kj >= WINDOW)
    cap = jnp.stack([jnp.where(allowed, F32_MAX, NEG_INF), jnp.where(first, F32_MAX, NEG_INF)])
    cap = jnp.tile(cap.astype(F32), (1, 2, 2))

    sink_rows = sinks.astype(F32).reshape(N_KV_HEADS, 2, 2).transpose(0, 2, 1).reshape(2 * N_KV_HEADS, 2)
    sink_rows = jnp.repeat(sink_rows, WINDOW, axis=1)
    return bias, cap, sink_rows


def _attn_prompt_kernel(x_ref, xres_ref, g_ref, wqkv_ref, wo_ref, bias_ref, cap_ref, sink_ref,
                        o_ref, kc_ref, vc_ref,
                        q0_scr, q1_scr, klo0_scr, klo1_scr, khi0_scr, khi1_scr, vt0_scr, vt1_scr,
                        a0_scr, a1_scr, *, tq, tiles_per_seq):
    s = pl.program_id(0)
    sets = ((q0_scr, klo0_scr, khi0_scr, vt0_scr, a0_scr),
            (q1_scr, klo1_scr, khi1_scr, vt1_scr, a1_scr))

    @pl.when(s == 0)
    def _():
        for scr in sets[0] + sets[1]:
            scr[...] = jnp.zeros(scr.shape, BF16)

    low = lax.broadcasted_iota(jnp.int32, (tq, LANES), 1) < HEAD_DIM
    n_col = D_MODEL // KV_DIM

    def step(new, old):
        q_new, klo_new, khi_new, vt_new, att_done = new
        q_old, klo_old, khi_old, vt_old, att_out = old

        starts_sequence = s % tiles_per_seq == 0
        klo_new[:, 0:WINDOW, :] = jnp.where(starts_sequence, 0.0, klo_old[:, tq:tq + WINDOW, :])
        khi_new[:, 0:WINDOW, :] = jnp.where(starts_sequence, 0.0, khi_old[:, tq:tq + WINDOW, :])
        vt_new[:, :, 0:WINDOW] = jnp.where(starts_sequence, 0.0, vt_old[:, :, tq:tq + WINDOW])
        h = _rms(x_ref[...], g_ref[...]).astype(BF16)

        def project_q(c):
            cols = slice(c * KV_DIM, (c + 1) * KV_DIM)
            q_new[:, cols] = (_dot(h, wqkv_ref[:, cols]) * ATTN_SCALE).astype(BF16)

        def project_k():
            k = _dot(h, wqkv_ref[:, D_MODEL:D_MODEL + KV_DIM])
            kc_ref[...] = k[tq - WINDOW:, :]
            for c in range(KV_DIM // LANES):
                kc = k[:, c * LANES:(c + 1) * LANES]
                kr = pltpu.roll(kc, HEAD_DIM, axis=1)
                klo_new[2 * c, WINDOW:, :] = jnp.where(low, kc, 0.0).astype(BF16)
                khi_new[2 * c, WINDOW:, :] = jnp.where(low, 0.0, kr).astype(BF16)
                klo_new[2 * c + 1, WINDOW:, :] = jnp.where(low, kr, 0.0).astype(BF16)
                khi_new[2 * c + 1, WINDOW:, :] = jnp.where(low, 0.0, kc).astype(BF16)

        def project_v():
            v = _dot(h, wqkv_ref[:, D_MODEL + KV_DIM:])
            vc_ref[...] = v[tq - WINDOW:, :]
            for c in range(KV_DIM // LANES):
                vt = v[:, c * LANES:(c + 1) * LANES].T.astype(BF16)
                vt_new[2 * c, :, WINDOW:] = vt[:HEAD_DIM]
                vt_new[2 * c + 1, :, WINDOW:] = vt[HEAD_DIM:]

        def project_out(c):
            cols = slice(c * KV_DIM, (c + 1) * KV_DIM)
            o_ref[:, cols] = xres_ref[:, cols] + _dot(att_done[...], wo_ref[:, cols])

        is_first = jnp.where((s - 1) % tiles_per_seq == 0, 1, 0)

        def scores(jb, kh):
            rows = slice(jb * WINDOW, (jb + 1) * WINDOW)
            band = slice(jb * WINDOW, (jb + 2) * WINDOW)
            cap = cap_ref[is_first] if jb == 0 else cap_ref[0]
            qst = jnp.concatenate([q_old[rows, (2 * kh) * LANES:(2 * kh + 1) * LANES],
                                   q_old[rows, (2 * kh + 1) * LANES:(2 * kh + 2) * LANES]], axis=0)
            kst = jnp.concatenate([klo_old[kh, band, :], khi_old[kh, band, :]], axis=0)
            return jnp.minimum(_dot_nt(kst, qst) + bias_ref[kh], cap)

        def attend(jb, kh, sc):
            rows = slice(jb * WINDOW, (jb + 1) * WINDOW)
            vt = vt_old[kh, :, jb * WINDOW:(jb + 2) * WINDOW]
            halves = []
            for half in range(2):
                sh = sc[half * 2 * WINDOW:(half + 1) * 2 * WINDOW]
                sink = sink_ref[2 * kh + half:2 * kh + half + 1, :]
                m = jnp.maximum(jnp.max(sh, axis=0, keepdims=True), sink)
                p = jnp.exp(sh - m)
                denom = jnp.sum(p, axis=0, keepdims=True) + jnp.exp(sink - m)
                halves.append(_dot(vt, p.astype(BF16)) * (1.0 / denom))
            o = jnp.concatenate(halves, axis=0).T.astype(BF16)
            att_out[rows, (2 * kh) * LANES:(2 * kh + 1) * LANES] = o[:WINDOW]
            att_out[rows, (2 * kh + 1) * LANES:(2 * kh + 2) * LANES] = o[WINDOW:]

        pieces = [functools.partial(project_q, c) for c in range(n_col)] + [project_k, project_v]
        pieces += [functools.partial(project_out, c) for c in range(n_col)]
        steps = [(jb, kh) for jb in range(tq // WINDOW) for kh in range(N_KV_HEADS)]
        piece_at = {(n * len(steps)) // len(pieces): piece for n, piece in enumerate(pieces)}
        assert len(piece_at) == len(pieces)
        s_next = scores(*steps[0])
        for n, st in enumerate(steps):
            s_cur = s_next
            if n + 1 < len(steps):
                s_next = scores(*steps[n + 1])
            if n in piece_at:
                piece_at[n]()
            attend(*st, s_cur)

    @pl.when(s % 2 == 0)
    def _():
        step(sets[0], sets[1])

    @pl.when(s % 2 == 1)
    def _():
        step(sets[1], sets[0])


def _attn_prompt(x, g, w_qkv, w_o, bias, cap, sink_rows, *, tq):
    b, seq, _ = x.shape
    tiles_per_seq = seq // tq
    n = b * tiles_per_seq
    x2 = x.reshape(b * seq, D_MODEL)
    lagged = pl.BlockSpec((tq, D_MODEL), lambda s: (jnp.maximum(s - 2, 0), 0))
    cache = pl.BlockSpec((None, WINDOW, KV_DIM),
                         lambda s: (jnp.minimum(s, n - 1) // tiles_per_seq, 0, 0))
    q_scr = pltpu.VMEM((tq, D_MODEL), BF16)
    k_scr = pltpu.VMEM((N_KV_HEADS, WINDOW + tq, LANES), BF16)
    vt_scr = pltpu.VMEM((N_KV_HEADS, HEAD_DIM, WINDOW + tq), BF16)
    out, kc, vc = pl.pallas_call(
        functools.partial(_attn_prompt_kernel, tq=tq, tiles_per_seq=tiles_per_seq),
        out_shape=(jax.ShapeDtypeStruct((b * seq, D_MODEL), F32),
                   jax.ShapeDtypeStruct((b, WINDOW, KV_DIM), F32),
                   jax.ShapeDtypeStruct((b, WINDOW, KV_DIM), F32)),
        grid=(n + 2,),
        in_specs=[pl.BlockSpec((tq, D_MODEL), lambda s: (jnp.minimum(s, n - 1), 0)), lagged,
                  _const_spec((1, D_MODEL)),
                  _const_spec((D_MODEL, QKV_DIM)), _const_spec((D_MODEL, D_MODEL)),
                  _const_spec(bias.shape), _const_spec(cap.shape), _const_spec(sink_rows.shape)],
        out_specs=(lagged, cache, cache),
        scratch_shapes=[q_scr, q_scr, k_scr, k_scr, k_scr, k_scr, vt_scr, vt_scr, q_scr, q_scr],
        compiler_params=_params("arbitrary"),
        name="attn_prompt",
    )(x2, x2, g, w_qkv, w_o, bias, cap, sink_rows)
    return out.reshape(b, seq, D_MODEL), kc, vc


ROWS_PAD = 8 * Q_PER_KV


def _sample_bias_tables(dist_bias, sinks):
    def rows(per_head):
        t = per_head.reshape(N_KV_HEADS, Q_PER_KV, -1).transpose(1, 0, 2)
        return jnp.pad(t, ((0, 0), (0, 8 - N_KV_HEADS), (0, 0))).reshape(ROWS_PAD, -1)

    bias = rows(dist_bias[:0:-1].T)
    extra = rows(jnp.stack([dist_bias[0], sinks.astype(F32)], axis=1))
    return bias, jnp.pad(extra, ((0, 0), (0, LANES - 2)))


def _attn_sample_kernel(q_ref, kn_ref, vn_ref, kvt_ref, kc_ref, vc_ref, bias_ref, extra_ref,
                        *rest, tb, n_aliased, out_slot):
    o_ref, ko_ref, vo_ref = rest[n_aliased:]
    if out_slot is not None:
        for out_ref in (ko_ref, vo_ref):
            for slot in range(out_ref.shape[0]):
                if slot != out_slot:
                    out_ref[slot] = jnp.zeros(out_ref.shape[1:], F32)
        ko_ref, vo_ref = ko_ref.at[out_slot], vo_ref.at[out_slot]
    t = pl.program_id(0)
    sub = lax.broadcasted_iota(jnp.int32, (8, KV_DIM), 0)
    lane_head = lax.broadcasted_iota(jnp.int32, (8, KV_DIM), 1) // HEAD_DIM
    own = jnp.logical_and(sub < N_KV_HEADS, lane_head == sub)
    newest = lax.broadcasted_iota(jnp.int32, (KV_DIM, WINDOW), 1) == WINDOW - 1
    bias = bias_ref[...]
    bias_new = extra_ref[:, 0:1]
    sink = extra_ref[:, 1:2]

    def scores(bb):
        q_rows = []
        for gq in range(Q_PER_KV):
            q_g = q_ref[bb:bb + 1, gq * KV_DIM:(gq + 1) * KV_DIM] * ATTN_SCALE
            q_rows.append(jnp.where(own, jnp.broadcast_to(q_g, (8, KV_DIM)), 0.0))
        q_blk = jnp.concatenate(q_rows, axis=0)
        s_old = _dot(q_blk.astype(BF16), kc_ref[bb].astype(BF16)) + bias
        s_new = jnp.sum(q_blk * kn_ref[bb:bb + 1, :], axis=1, keepdims=True) + bias_new
        return s_old, s_new

    def attend(bb, s_old, s_new):
        m = jnp.maximum(jnp.max(s_old, axis=1, keepdims=True), jnp.maximum(s_new, sink))
        p_old = jnp.exp(s_old - m)
        p_new = jnp.exp(s_new - m)
        denom = jnp.sum(p_old, axis=1, keepdims=True) + p_new + jnp.exp(sink - m)
        o = _dot_nt(p_old.astype(BF16), vc_ref[bb].astype(BF16)) + p_new * vn_ref[bb:bb + 1, :]
        o = o / denom
        for gq in range(Q_PER_KV):
            o_g = jnp.sum(jnp.where(own, o[8 * gq:8 * gq + 8], 0.0), axis=0, keepdims=True)
            o_ref[bb:bb + 1, gq * KV_DIM:(gq + 1) * KV_DIM] = o_g

    def shift(bb):
        to_last = (WINDOW - 1) - (t * tb + bb)
        for cache_ref, row0, out_ref in ((kc_ref, 0, ko_ref), (vc_ref, KV_DIM, vo_ref)):
            moved = pltpu.roll(cache_ref[bb], WINDOW - 1, axis=1)
            col = pltpu.roll(kvt_ref[row0:row0 + KV_DIM, :], to_last, axis=1)
            out_ref[bb] = jnp.where(newest, col, moved)

    for bb in range(tb):
        shift(bb)
    s_next = scores(0)
    for bb in range(tb):
        s_cur = s_next
        if bb + 1 < tb:
            s_next = scores(bb + 1)
        attend(bb, *s_cur)


def _attn_sample_core(qkv, kv_t, k_caches, v_caches, bias, extra, layer, new_caches=None, *, tb):
    n = qkv.shape[0]
    cache = pl.BlockSpec((None, tb, KV_DIM, WINDOW), lambda t: (layer, t, 0, 0))
    kv_col = D_MODEL // KV_DIM
    args = [qkv, qkv, qkv, kv_t, k_caches, v_caches, bias, extra]
    in_specs = [pl.BlockSpec((tb, D_MODEL), lambda t: (t, 0)),
                pl.BlockSpec((tb, KV_DIM), lambda t: (t, kv_col)),
                pl.BlockSpec((tb, KV_DIM), lambda t: (t, kv_col + 1)),
                _const_spec(kv_t.shape), cache, cache,
                _const_spec(bias.shape), _const_spec(extra.shape)]
    if new_caches is None:
        aliases, out_slot = {}, layer
        cache_out = pl.BlockSpec((k_caches.shape[0], tb, KV_DIM, WINDOW), lambda t: (0, t, 0, 0))
    else:
        aliases, out_slot = {len(args): 1, len(args) + 1: 2}, None
        cache_out = cache
        args += list(new_caches)
        in_specs += [pl.BlockSpec(memory_space=pl.ANY)] * 2
    return pl.pallas_call(
        functools.partial(_attn_sample_kernel, tb=tb, n_aliased=len(aliases), out_slot=out_slot),
        out_shape=(jax.ShapeDtypeStruct((n, D_MODEL), F32),
                   jax.ShapeDtypeStruct(k_caches.shape, F32),
                   jax.ShapeDtypeStruct(v_caches.shape, F32)),
        grid=(n // tb,),
        in_specs=in_specs,
        out_specs=(pl.BlockSpec((tb, D_MODEL), lambda t: (t, 0)), cache_out, cache_out),
        input_output_aliases=aliases,
        compiler_params=_params("parallel"),
        name="attn_sample",
    )(*args)


def _sgu_kernel(x_ref, g_ref, win_ref, lg_ref, lb_ref, sp_ref, bsp_ref, wout_ref, *rest,
                tm, sample):
    x = x_ref[...]
    h = _rms(x, g_ref[...]).astype(BF16)
    z = _dot(h, win_ref[...])
    z = 0.5 * z * (1.0 + lax.erf(z * INV_SQRT2))
    u = z[:, :D_MODEL]
    v = _layer_norm(z[:, D_MODEL:], lg_ref[...], lb_ref[...])
    if sample:
        o_ref, v_ref = rest
        v_ref[...] = v
        gated = (u * (v * sp_ref[...] + bsp_ref[...])).astype(BF16)
    else:
        o_ref, gated_scr = rest
        vb = v.astype(BF16)
        for c in range(tm // CHUNK):
            rows = slice(c * CHUNK, (c + 1) * CHUNK)
            for gi in range(SGU_GROUPS):
                cols = slice(gi * LANES, (gi + 1) * LANES)
                mixed = _dot(sp_ref[gi], vb[rows, cols]) + bsp_ref[gi]
                gated_scr[rows, cols] = (u[rows, cols] * mixed).astype(BF16)
        gated = gated_scr[...]
    o_ref[...] = x + _dot(gated, wout_ref[...])


def _sgu(x, g, w_in, ln_g, ln_b, sp, bsp, w_out, *, tm, sample):
    n = x.shape[0]
    row = pl.BlockSpec((tm, D_MODEL), lambda i: (i, 0))
    args = [x, g, w_in, ln_g, ln_b, sp, bsp, w_out]
    in_specs = [row] + [_const_spec(t.shape) for t in args[1:]]
    if sample:
        out_shape = (jax.ShapeDtypeStruct((n, D_MODEL), F32),) * 2
        out_specs = (row, row)
        scratch = []
    else:
        out_shape = jax.ShapeDtypeStruct((n, D_MODEL), F32)
        out_specs = row
        scratch = [pltpu.VMEM((tm, D_MODEL), BF16)]
    return pl.pallas_call(
        functools.partial(_sgu_kernel, tm=tm, sample=sample),
        out_shape=out_shape,
        grid=(n // tm,),
        in_specs=in_specs,
        out_specs=out_specs,
        scratch_shapes=scratch,
        compiler_params=_params("parallel"),
        name="sgu_sample" if sample else "sgu_prompt",
    )(*args)


CONV_ROWS = 64
CONV_LANES = 256
PROJ_LANES = 256


def _conv_prompt_kernel(x_ref, xprev_ref, g_ref, win_ref, wdw_ref, bdw_ref, lg_ref, lb_ref,
                        wout_ref, o_ref, tail_ref, a0_scr, a1_scr, sh_scr, c_scr,
                        *, tm, tiles_per_seq):
    s = pl.program_id(0)

    @pl.when(s == 0)
    def _():
        a0_scr[...] = jnp.zeros(a0_scr.shape, F32)
        a1_scr[...] = jnp.zeros(a1_scr.shape, F32)

    first_tap = CONV_HALO - (CONV_WIDTH - 1)
    reps = CONV_ROWS // SUBLANES

    def step(a_new, a_old):
        for r in range(1, SUBLANES):
            sh_scr[r - 1] = a_old[r:r + tm + CONV_HALO - SUBLANES, :]

        h = _rms(x_ref[...], g_ref[...]).astype(BF16)
        starts_sequence = s % tiles_per_seq == 0
        a_new[0:CONV_HALO, :] = jnp.where(starts_sequence, 0.0, a_old[tm:tm + CONV_HALO, :])

        def project(jc):
            cols = slice(jc * PROJ_LANES, (jc + 1) * PROJ_LANES)
            gate = slice(D_MODEL + jc * PROJ_LANES, D_MODEL + (jc + 1) * PROJ_LANES)
            a = _dot(h, win_ref[:, cols]) * jax.nn.sigmoid(_dot(h, win_ref[:, gate]))
            a_new[CONV_HALO:, cols] = a
            tail_ref[:, cols] = a[tm - CONV_HALO:, :]

        def convolve(rb):
            for lc in range(D_MODEL // CONV_LANES):
                cols = slice(lc * CONV_LANES, (lc + 1) * CONV_LANES)
                acc = jnp.concatenate([bdw_ref[:, cols]] * reps, axis=0)
                for kk in range(CONV_WIDTH):
                    whole, r = divmod(first_tap + kk, SUBLANES)
                    start = rb * CONV_ROWS + whole * SUBLANES
                    src = a_old if r == 0 else sh_scr.at[r - 1]
                    w = jnp.concatenate([wdw_ref[kk, :, cols]] * reps, axis=0)
                    acc = acc + src[start:start + CONV_ROWS, cols] * w
                c_scr[rb * CONV_ROWS:(rb + 1) * CONV_ROWS, cols] = acc

        n_proj = D_MODEL // PROJ_LANES
        n_conv = tm // CONV_ROWS
        for jc in range(n_proj):
            project(jc)
            for rb in range(jc * n_conv // n_proj, (jc + 1) * n_conv // n_proj):
                convolve(rb)
        c = _layer_norm(c_scr[...], lg_ref[...], lb_ref[...])
        c = (c * jax.nn.sigmoid(c)).astype(BF16)
        o_ref[...] = xprev_ref[...] + _dot(c, wout_ref[...])

    @pl.when(s % 2 == 0)
    def _():
        step(a0_scr, a1_scr)

    @pl.when(s % 2 == 1)
    def _():
        step(a1_scr, a0_scr)


def _conv_prompt(x, g, w_in, w_dw8, b_dw8, ln_g, ln_b, w_out, *, tm):
    b, seq, _ = x.shape
    tiles_per_seq = seq // tm
    n = b * tiles_per_seq
    consts = [g, w_in, w_dw8, b_dw8, ln_g, ln_b, w_out]
    a_scr = pltpu.VMEM((CONV_HALO + tm, D_MODEL), F32)
    x2 = x.reshape(b * seq, D_MODEL)
    out, tail = pl.pallas_call(
        functools.partial(_conv_prompt_kernel, tm=tm, tiles_per_seq=tiles_per_seq),
        out_shape=(jax.ShapeDtypeStruct((b * seq, D_MODEL), F32),
                   jax.ShapeDtypeStruct((b, CONV_HALO, D_MODEL), F32)),
        grid=(n + 1,),
        in_specs=[pl.BlockSpec((tm, D_MODEL), lambda s: (jnp.minimum(s, n - 1), 0)),
                  pl.BlockSpec((tm, D_MODEL), lambda s: (jnp.maximum(s - 1, 0), 0))]
                 + [_const_spec(t.shape) for t in consts],
        out_specs=(pl.BlockSpec((tm, D_MODEL), lambda s: (jnp.maximum(s - 1, 0), 0)),
                   pl.BlockSpec((None, CONV_HALO, D_MODEL),
                                lambda s: (jnp.minimum(s, n - 1) // tiles_per_seq, 0, 0))),
        scratch_shapes=[a_scr, a_scr,
                        pltpu.VMEM((SUBLANES - 1, CONV_HALO + tm - SUBLANES, D_MODEL), F32),
                        pltpu.VMEM((tm, D_MODEL), F32)],
        compiler_params=_params("arbitrary"),
        name="conv_prompt",
    )(x2, x2, *consts)
    return out.reshape(b, seq, D_MODEL), tail


def _conv_sample_kernel(a_ref, st_ref, wdw_ref, bdw_ref, c_ref, so_ref, *, tb):
    n_hist = CONV_WIDTH - 1
    st = st_ref[...]
    hist = jnp.sum(st * wdw_ref[0:n_hist, :][None], axis=1)
    c_ref[...] = hist + a_ref[...] * wdw_ref[n_hist:CONV_WIDTH, :] + bdw_ref[...]
    so_ref[:, 0:n_hist - 1, :] = st_ref[:, 1:n_hist, :]

    def one(b, carry):
        so_ref[b, n_hist - 1:n_hist, :] = a_ref[pl.ds(b, 1), :]
        return carry

    lax.fori_loop(0, tb, one, 0)


def _conv_sample_core(a, states, w_dw, b_dw, layer, *, tb):
    n = a.shape[0]
    row = pl.BlockSpec((tb, D_MODEL), lambda t: (t, 0))
    st_in = pl.BlockSpec((None, tb, CONV_WIDTH - 1, D_MODEL), lambda t: (layer, t, 0, 0))
    st_out = pl.BlockSpec((tb, CONV_WIDTH - 1, D_MODEL), lambda t: (t, 0, 0))
    return pl.pallas_call(
        functools.partial(_conv_sample_kernel, tb=tb),
        out_shape=(jax.ShapeDtypeStruct((n, D_MODEL), F32),
                   jax.ShapeDtypeStruct(states.shape[1:], F32)),
        grid=(n // tb,),
        in_specs=[row, st_in, _const_spec(w_dw.shape), _const_spec(b_dw.shape)],
        out_specs=(row, st_out),
        compiler_params=_params("parallel"),
        name="conv_sample",
    )(a, states, w_dw, b_dw)


PROMPT_TILE = 512
FFN_TILE = 1024
SAMPLE_ATTN_TILE = 16
SAMPLE_CONV_TILE = 16


def kernel(x_prompt, x_sample, cache_swa_k, cache_swa_v, state_conv, rel_bias, norm_mix, norm_ffn, norm_final, attn_w_qkv, attn_w_o, attn_sinks, sgu_w_in, sgu_ln_g, sgu_ln_b, sgu_w_spatial, sgu_b_spatial, sgu_w_out, conv_w_in, conv_w_dw, conv_b_dw, conv_ln_g, conv_ln_b, conv_w_out, ffn_w_up, ffn_w_down):
    batch, seq, _ = x_prompt.shape
    dec = x_sample.shape[0]
    depth = norm_mix.shape[0]
    mixer_of_layer = tuple(i % 3 for i in range(depth))
    slot_of_layer = tuple(mixer_of_layer[:i].count(mixer_of_layer[i]) for i in range(depth))

    def row(v):
        return v.reshape(1, -1).astype(F32)

    def regroup(w, axis):
        shape = w.shape[:axis] + (N_KV_HEADS, Q_PER_KV, HEAD_DIM) + w.shape[axis + 1:]
        return jnp.swapaxes(w.reshape(shape), axis, axis + 1).reshape(w.shape)

    n_attn = attn_w_qkv.shape[0]

    def caches_t(c):
        return jnp.transpose(c, (0, 1, 3, 4, 2)).reshape(n_attn, dec, KV_DIM, WINDOW)

    def caches_from_t(c):
        c = c.reshape(n_attn, dec, N_KV_HEADS, HEAD_DIM, WINDOW)
        return jnp.transpose(c, (0, 1, 4, 2, 3))

    xp = x_prompt
    xs = x_sample.reshape(dec, D_MODEL)
    dist_bias = _distance_bias(rel_bias)
    k_caches_t, v_caches_t = caches_t(cache_swa_k), caches_t(cache_swa_v)
    states = state_conv.astype(F32)
    w_up_all = ffn_w_up.astype(BF16)
    w_down_all = ffn_w_down.astype(BF16)

    kp, vp, sgu_v_new, convp, convs = [], [], [], [], []
    new_caches = None
    for i in range(depth):
        m, j = mixer_of_layer[i], slot_of_layer[i]
        g_mix = row(norm_mix[i])
        if m == 0:
            w_qkv = attn_w_qkv[j].astype(BF16)
            w_o = attn_w_o[j].astype(BF16)
            bias_p, cap_p, sink_rows = _prompt_bias_tables(dist_bias, attn_sinks[j])
            xp, k1, v1 = _attn_prompt(xp, g_mix, w_qkv, w_o, bias_p, cap_p, sink_rows,
                                      tq=PROMPT_TILE)
            kp.append(k1.reshape(batch, WINDOW, N_KV_HEADS, HEAD_DIM))
            vp.append(v1.reshape(batch, WINDOW, N_KV_HEADS, HEAD_DIM))

            w_qkv_s = jnp.concatenate([regroup(w_qkv[:, :D_MODEL], 1), w_qkv[:, D_MODEL:]], axis=1)
            bias_s, extra_s = _sample_bias_tables(dist_bias, attn_sinks[j])
            qkv_s, kv_t = _norm_proj(xs, g_mix, w_qkv_s, kv_t=True)
            o_s, *new_caches = _attn_sample_core(qkv_s, kv_t, k_caches_t, v_caches_t, bias_s,
                                                 extra_s, j, new_caches, tb=SAMPLE_ATTN_TILE)
            xs = _proj_res(xs, o_s, regroup(w_o, 0))
        elif m == 1:
            w_in = sgu_w_in[j].astype(BF16)
            w_out = sgu_w_out[j].astype(BF16)
            ln_g, ln_b = row(sgu_ln_g[j]), row(sgu_ln_b[j])
            sp = jnp.tril(sgu_w_spatial[j]).astype(BF16)
            bsp = jnp.broadcast_to(sgu_b_spatial[j].astype(F32)[:, :, None],
                                   (SGU_GROUPS, CHUNK, LANES))
            xp = _sgu(xp.reshape(batch * seq, D_MODEL), g_mix, w_in, ln_g, ln_b, sp, bsp, w_out,
                      tm=PROMPT_TILE, sample=False).reshape(batch, seq, D_MODEL)
            sp0 = row(jnp.repeat(sgu_w_spatial[j][:, 0, 0], LANES))
            bsp0 = row(jnp.repeat(sgu_b_spatial[j][:, 0], LANES))
            xs, v_rows = _sgu(xs, g_mix, w_in, ln_g, ln_b, sp0, bsp0, w_out, tm=dec, sample=True)
            sgu_v_new.append(v_rows.reshape(dec, 1, D_MODEL))
        else:
            w_in = conv_w_in[j].astype(BF16)
            w_out = conv_w_out[j].astype(BF16)
            w_dw = conv_w_dw[j].astype(F32)
            b_dw = row(conv_b_dw[j])
            ln_g, ln_b = row(conv_ln_g[j]), row(conv_ln_b[j])
            w_dw8 = jnp.broadcast_to(w_dw[:, None, :], (CONV_WIDTH, SUBLANES, D_MODEL))
            b_dw8 = jnp.broadcast_to(b_dw, (SUBLANES, D_MODEL))
            xp, tail = _conv_prompt(xp, g_mix, w_in, w_dw8, b_dw8, ln_g, ln_b, w_out,
                                    tm=PROMPT_TILE)
            convp.append(tail[:, CONV_HALO - (CONV_WIDTH - 1):, :])
            a_s = _norm_proj(xs, g_mix, w_in, glu=True)
            c_s, st2 = _conv_sample_core(a_s, states, w_dw, b_dw, j, tb=SAMPLE_CONV_TILE)
            xs = _proj_res(xs, c_s, w_out, ln=(ln_g, ln_b))
            convs.append(st2)

        g_fin = row(norm_final) if i == depth - 1 else None
        xp, xs = _ffn(xp.reshape(batch * seq, D_MODEL), xs, row(norm_ffn[i]), w_up_all, w_down_all,
                      i, g_fin, tm=FFN_TILE)
        xp = xp.reshape(batch, seq, D_MODEL)

    y_prompt = xp
    y_sample = xs.reshape(dec, 1, D_MODEL)
    k_new, v_new = (caches_from_t(c) for c in new_caches)
    return (y_prompt, y_sample, jnp.stack(kp), jnp.stack(vp), k_new, v_new,
            jnp.stack(sgu_v_new), jnp.stack(convp), jnp.stack(convs))
```

```python
import functools
import math

import jax
import jax.numpy as jnp
from jax import lax
from jax.experimental import pallas as pl
from jax.experimental.pallas import tpu as pltpu

D_MODEL = 1024
HEAD_DIM = 64
N_HEADS = 16
N_KV_HEADS = 4
Q_PER_KV = 4
KV_DIM = N_KV_HEADS * HEAD_DIM
QKV_DIM = D_MODEL + 2 * KV_DIM
WINDOW = 128
ATTN_SCALE = HEAD_DIM ** -0.5
N_BUCKETS = 32
MAX_DISTANCE = 128
CHUNK = 128
SGU_GROUPS = 8
CONV_WIDTH = 31
CONV_HALO = 32
D_FF = 4 * D_MODEL
EPS = 1e-6
NEG_INF = -1e30
F32_MAX = float(jnp.finfo(jnp.float32).max)
INV_SQRT2 = 1.0 / math.sqrt(2.0)

LANES = 128
SUBLANES = 8
VMEM_LIMIT = 56 * 1024 * 1024

F32 = jnp.float32
BF16 = jnp.bfloat16


def _const_spec(shape):
    n = len(shape)
    return pl.BlockSpec(shape, lambda *_: (0,) * n, pipeline_mode=pl.Buffered(1))


def _params(*sem):
    return pltpu.CompilerParams(dimension_semantics=sem, vmem_limit_bytes=VMEM_LIMIT)


def _rms(x, g):
    return x * lax.rsqrt(jnp.mean(x * x, axis=-1, keepdims=True) + EPS) * g


def _layer_norm(x, g, b):
    mu = jnp.mean(x, axis=-1, keepdims=True)
    xc = x - mu
    var = jnp.mean(xc * xc, axis=-1, keepdims=True)
    return xc * lax.rsqrt(var + EPS) * g + b


def _dot(a, b):
    return jnp.dot(a, b, preferred_element_type=F32)


def _dot_nt(a, b):
    return lax.dot_general(a, b, (((1,), (1,)), ((), ())), preferred_element_type=F32)


FFN_CHUNK = 1024


def _layer_spec(shape, layer):
    zeros = (0,) * (len(shape) - 1)
    return pl.BlockSpec((None,) + tuple(shape[1:]), lambda *_: (layer,) + zeros,
                        pipeline_mode=pl.Buffered(1))


def _ffn_kernel(xp_ref, xs_ref, g_ref, wu_ref, wd_ref, *rest, final, n_prompt_tiles):
    if final:
        gf_ref, op_ref, os_ref = rest
    else:
        op_ref, os_ref = rest

    def block(x_ref, o_ref):
        x = x_ref[...]
        h = _rms(x, g_ref[...]).astype(BF16)
        y = x
        for c in range(D_FF // FFN_CHUNK):
            cols = slice(c * FFN_CHUNK, (c + 1) * FFN_CHUNK)
            u = _dot(h, wu_ref[:, cols])
            u = jnp.square(jnp.maximum(u, 0.0)).astype(BF16)
            y = y + _dot(u, wd_ref[cols, :])
        o_ref[...] = _rms(y, gf_ref[...]) if final else y

    i = pl.program_id(0)

    @pl.when(i < n_prompt_tiles)
    def _():
        block(xp_ref, op_ref)

    @pl.when(i == n_prompt_tiles)
    def _():
        block(xs_ref, os_ref)


def _ffn(xp, xs, g, w_up, w_down, layer, g_final=None, *, tm):
    n = xp.shape[0]
    nt = n // tm
    row = pl.BlockSpec((tm, D_MODEL), lambda i: (jnp.minimum(i, nt - 1), 0))
    in_specs = [row, _const_spec(xs.shape), _const_spec((1, D_MODEL)),
                _layer_spec(w_up.shape, layer), _layer_spec(w_down.shape, layer)]
    args = [xp, xs, g, w_up, w_down]
    if g_final is not None:
        in_specs.append(_const_spec((1, D_MODEL)))
        args.append(g_final)
    return pl.pallas_call(
        functools.partial(_ffn_kernel, final=g_final is not None, n_prompt_tiles=nt),
        out_shape=(jax.ShapeDtypeStruct((n, D_MODEL), F32),
                   jax.ShapeDtypeStruct(xs.shape, F32)),
        grid=(nt + 1,),
        in_specs=in_specs,
        out_specs=(row, pl.BlockSpec(xs.shape, lambda i: (0, 0))),
        compiler_params=_params("arbitrary"),
        name="ffn_final" if g_final is not None else "ffn",
    )(*args)


def _regroup_heads(t, group_major):
    blocks = range(N_HEADS)
    if group_major:
        order = [N_KV_HEADS * (blk % N_KV_HEADS) + blk // N_KV_HEADS for blk in blocks]
    else:
        order = [Q_PER_KV * (blk % Q_PER_KV) + blk // Q_PER_KV for blk in blocks]
    return jnp.concatenate([t[:, h * HEAD_DIM:(h + 1) * HEAD_DIM] for h in order], axis=1)


def _norm_proj_kernel(x_ref, g_ref, w_ref, o_ref, *rest, glu, kv_t):
    h = _rms(x_ref[...], g_ref[...]).astype(BF16)
    y = _dot(h, w_ref[...])
    if glu:
        half = y.shape[1] // 2
        y = y[:, :half] * jax.nn.sigmoid(y[:, half:])
    if kv_t:
        (t_ref,) = rest
        o_ref[:, :D_MODEL] = _regroup_heads(y[:, :D_MODEL], group_major=True)
        o_ref[:, D_MODEL:] = y[:, D_MODEL:]
        t_ref[...] = y[:, D_MODEL:].T
    else:
        o_ref[...] = y


def _weight_spec(w, layer):
    return _const_spec(w.shape) if layer is None else _layer_spec(w.shape, layer)


def _norm_proj(x, g, w, layer=None, *, glu=False, kv_t=False):
    n = x.shape[0]
    n_out = w.shape[-1] // 2 if glu else w.shape[-1]
    out_shape = [jax.ShapeDtypeStruct((n, n_out), F32)]
    if kv_t:
        out_shape.append(jax.ShapeDtypeStruct((n_out - D_MODEL, n), F32))
    out = pl.pallas_call(
        functools.partial(_norm_proj_kernel, glu=glu, kv_t=kv_t),
        out_shape=tuple(out_shape),
        grid=(1,),
        in_specs=[_const_spec(x.shape), _const_spec(g.shape), _weight_spec(w, layer)],
        out_specs=tuple(_const_spec(o.shape) for o in out_shape),
        compiler_params=_params("arbitrary"),
        name="norm_proj_glu" if glu else "norm_proj",
    )(x, g, w)
    return out if kv_t else out[0]


def _proj_res_kernel(x_ref, a_ref, w_ref, *rest, conv_tail, heads_grouped):
    a = a_ref[...]
    if conv_tail:
        lg_ref, lb_ref, o_ref = rest
        a = _layer_norm(a, lg_ref[...], lb_ref[...])
        a = a * jax.nn.sigmoid(a)
    else:
        (o_ref,) = rest
    if heads_grouped:
        a = _regroup_heads(a, group_major=False)
    o_ref[...] = x_ref[...] + _dot(a.astype(BF16), w_ref[...])


def _proj_res(x, a, w, layer=None, ln=None, heads_grouped=False):
    args = [x, a, w] + (list(ln) if ln is not None else [])
    in_specs = [_const_spec(t.shape) for t in args]
    in_specs[2] = _weight_spec(w, layer)
    return pl.pallas_call(
        functools.partial(_proj_res_kernel, conv_tail=ln is not None, heads_grouped=heads_grouped),
        out_shape=jax.ShapeDtypeStruct(x.shape, F32),
        grid=(1,),
        in_specs=in_specs,
        out_specs=_const_spec(x.shape),
        compiler_params=_params("arbitrary"),
        name="proj_res_ln" if ln is not None else "proj_res",
    )(*args)


def _t5_bucket(dist):
    n = jnp.maximum(dist, 0)
    max_exact = N_BUCKETS // 2
    nf = jnp.maximum(n, 1).astype(F32)
    large = max_exact + (jnp.log(nf / max_exact) / math.log(MAX_DISTANCE / max_exact)
                         * (N_BUCKETS - max_exact)).astype(jnp.int32)
    large = jnp.minimum(large, N_BUCKETS - 1)
    return jnp.where(n < max_exact, n, large)


def _distance_bias(rel_bias):
    buckets = _t5_bucket(jnp.arange(WINDOW + 1, dtype=jnp.int32))
    onehot = (buckets[:, None] == jnp.arange(N_BUCKETS, dtype=jnp.int32)[None, :]).astype(F32)
    return jnp.dot(onehot, rel_bias.astype(F32), precision=lax.Precision.HIGHEST)


def _prompt_bias_tables(dist_bias, sinks):
    period = 3 * WINDOW
    line = jnp.concatenate([jnp.broadcast_to(dist_bias[WINDOW:], (WINDOW - 1, N_HEADS)),
                            dist_bias[::-1],
                            jnp.broadcast_to(dist_bias[:1], (WINDOW, N_HEADS))]).T
    skew = jnp.tile(line, (1, WINDOW))[:, :WINDOW * (period - 1)]
    per_head = skew.reshape(N_HEADS, WINDOW, period - 1)[:, :, WINDOW - 1:period - 1]
    per_head = per_head.reshape(N_KV_HEADS, 2, 2, WINDOW, 2 * WINDOW)
    bias = per_head.transpose(0, 2, 4, 1, 3).reshape(N_KV_HEADS, 4 * WINDOW, 2 * WINDOW)

    qi = jnp.arange(WINDOW, dtype=jnp.int32)[None, :]
    kj = jnp.arange(2 * WINDOW, dtype=jnp.int32)[:, None]
    dist = qi - kj + WINDOW
    allowed = (dist >= 0) & (dist <= WINDOW)
    first = allowed & (kj >= WINDOW)
    cap = jnp.stack([jnp.where(allowed, F32_MAX, NEG_INF), jnp.where(first, F32_MAX, NEG_INF)])
    cap = jnp.tile(cap.astype(F32), (1, 2, 2))

    sink_rows = sinks.astype(F32).reshape(N_KV_HEADS, 2, 2).transpose(0, 2, 1).reshape(2 * N_KV_HEADS, 2)
    sink_rows = jnp.repeat(sink_rows, WINDOW, axis=1)
    return bias, cap, sink_rows


def _attn_prompt_kernel(x_ref, xres_ref, g_ref, wqkv_ref, wo_ref, bias_ref, cap_ref, sink_ref,
                        o_ref, kc_ref, vc_ref,
                        q0_scr, q1_scr, klo0_scr, klo1_scr, khi0_scr, khi1_scr, vt0_scr, vt1_scr,
                        a0_scr, a1_scr, *, tq, tiles_per_seq):
    s = pl.program_id(0)
    sets = ((q0_scr, klo0_scr, khi0_scr, vt0_scr, a0_scr),
            (q1_scr, klo1_scr, khi1_scr, vt1_scr, a1_scr))

    @pl.when(s == 0)
    def _():
        for scr in sets[0] + sets[1]:
            scr[...] = jnp.zeros(scr.shape, BF16)

    low = lax.broadcasted_iota(jnp.int32, (tq, LANES), 1) < HEAD_DIM
    n_col = D_MODEL // KV_DIM

    def step(new, old):
        q_new, klo_new, khi_new, vt_new, att_done = new
        q_old, klo_old, khi_old, vt_old, att_out = old

        starts_sequence = s % tiles_per_seq == 0
        klo_new[:, 0:WINDOW, :] = jnp.where(starts_sequence, 0.0, klo_old[:, tq:tq + WINDOW, :])
        khi_new[:, 0:WINDOW, :] = jnp.where(starts_sequence, 0.0, khi_old[:, tq:tq + WINDOW, :])
        vt_new[:, :, 0:WINDOW] = jnp.where(starts_sequence, 0.0, vt_old[:, :, tq:tq + WINDOW])
        h = _rms(x_ref[...], g_ref[...]).astype(BF16)

        def project_q(c):
            cols = slice(c * KV_DIM, (c + 1) * KV_DIM)
            q_new[:, cols] = (_dot(h, wqkv_ref[:, cols]) * ATTN_SCALE).astype(BF16)

        def project_k():
            k = _dot(h, wqkv_ref[:, D_MODEL:D_MODEL + KV_DIM])
            kc_ref[...] = k[tq - WINDOW:, :]
            for c in range(KV_DIM // LANES):
                kc = k[:, c * LANES:(c + 1) * LANES]
                kr = pltpu.roll(kc, HEAD_DIM, axis=1)
                klo_new[2 * c, WINDOW:, :] = jnp.where(low, kc, 0.0).astype(BF16)
                khi_new[2 * c, WINDOW:, :] = jnp.where(low, 0.0, kr).astype(BF16)
                klo_new[2 * c + 1, WINDOW:, :] = jnp.where(low, kr, 0.0).astype(BF16)
                khi_new[2 * c + 1, WINDOW:, :] = jnp.where(low, 0.0, kc).astype(BF16)

        def project_v():
            v = _dot(h, wqkv_ref[:, D_MODEL + KV_DIM:])
            vc_ref[...] = v[tq - WINDOW:, :]
            for c in range(KV_DIM // LANES):
                vt = v[:, c * LANES:(c + 1) * LANES].T.astype(BF16)
                vt_new[2 * c, :, WINDOW:] = vt[:HEAD_DIM]
                vt_new[2 * c + 1, :, WINDOW:] = vt[HEAD_DIM:]

        def project_out(c):
            cols = slice(c * KV_DIM, (c + 1) * KV_DIM)
            o_ref[:, cols] = xres_ref[:, cols] + _dot(att_done[...], wo_ref[:, cols])

        is_first = jnp.where((s - 1) % tiles_per_seq == 0, 1, 0)

        def scores(jb, kh):
            rows = slice(jb * WINDOW, (jb + 1) * WINDOW)
            band = slice(jb * WINDOW, (jb + 2) * WINDOW)
            cap = cap_ref[is_first] if jb == 0 else cap_ref[0]
            qst = jnp.concatenate([q_old[rows, (2 * kh) * LANES:(2 * kh + 1) * LANES],
                                   q_old[rows, (2 * kh + 1) * LANES:(2 * kh + 2) * LANES]], axis=0)
            kst = jnp.concatenate([klo_old[kh, band, :], khi_old[kh, band, :]], axis=0)
            return jnp.minimum(_dot_nt(kst, qst) + bias_ref[kh], cap)

        def attend(jb, kh, sc):
            rows = slice(jb * WINDOW, (jb + 1) * WINDOW)
            vt = vt_old[kh, :, jb * WINDOW:(jb + 2) * WINDOW]
            halves = []
            for half in range(2):
                sh = sc[half * 2 * WINDOW:(half + 1) * 2 * WINDOW]
                sink = sink_ref[2 * kh + half:2 * kh + half + 1, :]
                m = jnp.maximum(jnp.max(sh, axis=0, keepdims=True), sink)
                p = jnp.exp(sh - m)
                denom = jnp.sum(p, axis=0, keepdims=True) + jnp.exp(sink - m)
                halves.append(_dot(vt, p.astype(BF16)) * (1.0 / denom))
            o = jnp.concatenate(halves, axis=0).T.astype(BF16)
            att_out[rows, (2 * kh) * LANES:(2 * kh + 1) * LANES] = o[:WINDOW]
            att_out[rows, (2 * kh + 1) * LANES:(2 * kh + 2) * LANES] = o[WINDOW:]

        pieces = [functools.partial(project_q, c) for c in range(n_col)] + [project_k, project_v]
        pieces += [functools.partial(project_out, c) for c in range(n_col)]
        steps = [(jb, kh) for jb in range(tq // WINDOW) for kh in range(N_KV_HEADS)]
        piece_at = {(n * len(steps)) // len(pieces): piece for n, piece in enumerate(pieces)}
        assert len(piece_at) == len(pieces)
        s_next = scores(*steps[0])
        for n, st in enumerate(steps):
            s_cur = s_next
            if n + 1 < len(steps):
                s_next = scores(*steps[n + 1])
            if n in piece_at:
                piece_at[n]()
            attend(*st, s_cur)

    @pl.when(s % 2 == 0)
    def _():
        step(sets[0], sets[1])

    @pl.when(s % 2 == 1)
    def _():
        step(sets[1], sets[0])


def _attn_prompt(x, g, w_qkv, w_o, layer, bias, cap, sink_rows, *, tq):
    b, seq, _ = x.shape
    tiles_per_seq = seq // tq
    n = b * tiles_per_seq
    x2 = x.reshape(b * seq, D_MODEL)
    lagged = pl.BlockSpec((tq, D_MODEL), lambda s: (jnp.maximum(s - 2, 0), 0))
    cache = pl.BlockSpec((None, WINDOW, KV_DIM),
                         lambda s: (jnp.minimum(s, n - 1) // tiles_per_seq, 0, 0))
    q_scr = pltpu.VMEM((tq, D_MODEL), BF16)
    k_scr = pltpu.VMEM((N_KV_HEADS, WINDOW + tq, LANES), BF16)
    vt_scr = pltpu.VMEM((N_KV_HEADS, HEAD_DIM, WINDOW + tq), BF16)
    out, kc, vc = pl.pallas_call(
        functools.partial(_attn_prompt_kernel, tq=tq, tiles_per_seq=tiles_per_seq),
        out_shape=(jax.ShapeDtypeStruct((b * seq, D_MODEL), F32),
                   jax.ShapeDtypeStruct((b, WINDOW, KV_DIM), F32),
                   jax.ShapeDtypeStruct((b, WINDOW, KV_DIM), F32)),
        grid=(n + 2,),
        in_specs=[pl.BlockSpec((tq, D_MODEL), lambda s: (jnp.minimum(s, n - 1), 0)), lagged,
                  _const_spec((1, D_MODEL)),
                  _layer_spec(w_qkv.shape, layer), _layer_spec(w_o.shape, layer),
                  _const_spec(bias.shape), _const_spec(cap.shape), _const_spec(sink_rows.shape)],
        out_specs=(lagged, cache, cache),
        scratch_shapes=[q_scr, q_scr, k_scr, k_scr, k_scr, k_scr, vt_scr, vt_scr, q_scr, q_scr],
        compiler_params=_params("arbitrary"),
        name="attn_prompt",
    )(x2, x2, g, w_qkv, w_o, bias, cap, sink_rows)
    return out.reshape(b, seq, D_MODEL), kc, vc


ROWS_PAD = 8 * Q_PER_KV


def _sample_bias_tables(dist_bias, sinks):
    def rows(per_head):
        t = per_head.reshape(N_KV_HEADS, Q_PER_KV, -1).transpose(1, 0, 2)
        return jnp.pad(t, ((0, 0), (0, 8 - N_KV_HEADS), (0, 0))).reshape(ROWS_PAD, -1)

    bias = rows(dist_bias[:0:-1].T)
    extra = rows(jnp.stack([dist_bias[0], sinks.astype(F32)], axis=1))
    return bias, jnp.pad(extra, ((0, 0), (0, LANES - 2)))


def _attn_sample_kernel(q_ref, kn_ref, vn_ref, kvt_ref, kc_ref, vc_ref, bias_ref, extra_ref,
                        *rest, tb, n_aliased, out_slot):
    o_ref, ko_ref, vo_ref = rest[n_aliased:]
    if out_slot is not None:
        for out_ref in (ko_ref, vo_ref):
            for slot in range(out_ref.shape[0]):
                if slot != out_slot:
                    out_ref[slot] = jnp.zeros(out_ref.shape[1:], F32)
        ko_ref, vo_ref = ko_ref.at[out_slot], vo_ref.at[out_slot]
    t = pl.program_id(0)
    sub = lax.broadcasted_iota(jnp.int32, (8, KV_DIM), 0)
    lane_head = lax.broadcasted_iota(jnp.int32, (8, KV_DIM), 1) // HEAD_DIM
    own = jnp.logical_and(sub < N_KV_HEADS, lane_head == sub)
    newest = lax.broadcasted_iota(jnp.int32, (KV_DIM, WINDOW), 1) == WINDOW - 1
    bias = bias_ref[...]
    bias_new = extra_ref[:, 0:1]
    sink = extra_ref[:, 1:2]

    def scores(bb):
        q_rows = []
        for gq in range(Q_PER_KV):
            q_g = q_ref[bb:bb + 1, gq * KV_DIM:(gq + 1) * KV_DIM] * ATTN_SCALE
            q_rows.append(jnp.where(own, jnp.broadcast_to(q_g, (8, KV_DIM)), 0.0))
        q_blk = jnp.concatenate(q_rows, axis=0)
        s_old = _dot(q_blk.astype(BF16), kc_ref[bb].astype(BF16)) + bias
        s_new = jnp.sum(q_blk * kn_ref[bb:bb + 1, :], axis=1, keepdims=True) + bias_new
        return s_old, s_new

    def attend(bb, s_old, s_new):
        m = jnp.maximum(jnp.max(s_old, axis=1, keepdims=True), jnp.maximum(s_new, sink))
        p_old = jnp.exp(s_old - m)
        p_new = jnp.exp(s_new - m)
        denom = jnp.sum(p_old, axis=1, keepdims=True) + p_new + jnp.exp(sink - m)
        o = _dot_nt(p_old.astype(BF16), vc_ref[bb].astype(BF16)) + p_new * vn_ref[bb:bb + 1, :]
        o = o / denom
        for gq in range(Q_PER_KV):
            o_g = jnp.sum(jnp.where(own, o[8 * gq:8 * gq + 8], 0.0), axis=0, keepdims=True)
            o_ref[bb:bb + 1, gq * KV_DIM:(gq + 1) * KV_DIM] = o_g

    def shift(bb):
        to_last = (WINDOW - 1) - (t * tb + bb)
        for cache_ref, row0, out_ref in ((kc_ref, 0, ko_ref), (vc_ref, KV_DIM, vo_ref)):
            moved = pltpu.roll(cache_ref[bb], WINDOW - 1, axis=1)
            col = pltpu.roll(kvt_ref[row0:row0 + KV_DIM, :], to_last, axis=1)
            out_ref[bb] = jnp.where(newest, col, moved)

    for bb in range(tb):
        shift(bb)
    s_next = scores(0)
    for bb in range(tb):
        s_cur = s_next
        if bb + 1 < tb:
            s_next = scores(bb + 1)
        attend(bb, *s_cur)


def _attn_sample_core(qkv, kv_t, k_caches, v_caches, bias, extra, layer, new_caches=None, *, tb):
    n = qkv.shape[0]
    cache = pl.BlockSpec((None, tb, KV_DIM, WINDOW), lambda t: (layer, t, 0, 0))
    kv_col = D_MODEL // KV_DIM
    args = [qkv, qkv, qkv, kv_t, k_caches, v_caches, bias, extra]
    in_specs = [pl.BlockSpec((tb, D_MODEL), lambda t: (t, 0)),
                pl.BlockSpec((tb, KV_DIM), lambda t: (t, kv_col)),
                pl.BlockSpec((tb, KV_DIM), lambda t: (t, kv_col + 1)),
                _const_spec(kv_t.shape), cache, cache,
                _const_spec(bias.shape), _const_spec(extra.shape)]
    if new_caches is None:
        aliases, out_slot = {}, layer
        cache_out = pl.BlockSpec((k_caches.shape[0], tb, KV_DIM, WINDOW), lambda t: (0, t, 0, 0))
    else:
        aliases, out_slot = {len(args): 1, len(args) + 1: 2}, None
        cache_out = cache
        args += list(new_caches)
        in_specs += [pl.BlockSpec(memory_space=pl.ANY)] * 2
    return pl.pallas_call(
        functools.partial(_attn_sample_kernel, tb=tb, n_aliased=len(aliases), out_slot=out_slot),
        out_shape=(jax.ShapeDtypeStruct((n, D_MODEL), F32),
                   jax.ShapeDtypeStruct(k_caches.shape, F32),
                   jax.ShapeDtypeStruct(v_caches.shape, F32)),
        grid=(n // tb,),
        in_specs=in_specs,
        out_specs=(pl.BlockSpec((tb, D_MODEL), lambda t: (t, 0)), cache_out, cache_out),
        input_output_aliases=aliases,
        compiler_params=_params("parallel"),
        name="attn_sample",
    )(*args)


def _sgu_kernel(x_ref, g_ref, win_ref, lg_ref, lb_ref, sp_ref, bsp_ref, wout_ref, *rest,
                tm, sample):
    x = x_ref[...]
    h = _rms(x, g_ref[...]).astype(BF16)
    z = _dot(h, win_ref[...])
    z = 0.5 * z * (1.0 + lax.erf(z * INV_SQRT2))
    u = z[:, :D_MODEL]
    v = _layer_norm(z[:, D_MODEL:], lg_ref[...], lb_ref[...])
    if sample:
        o_ref, v_ref = rest
        v_ref[...] = v
        gated = (u * (v * sp_ref[...] + bsp_ref[...])).astype(BF16)
    else:
        o_ref, gated_scr = rest
        vb = v.astype(BF16)
        for c in range(tm // CHUNK):
            rows = slice(c * CHUNK, (c + 1) * CHUNK)
            for gi in range(SGU_GROUPS):
                cols = slice(gi * LANES, (gi + 1) * LANES)
                mixed = _dot(sp_ref[gi], vb[rows, cols]) + bsp_ref[gi]
                gated_scr[rows, cols] = (u[rows, cols] * mixed).astype(BF16)
        gated = gated_scr[...]
    o_ref[...] = x + _dot(gated, wout_ref[...])


def _sgu(x, g, w_in, ln_g, ln_b, sp, bsp, w_out, *, tm, sample):
    n = x.shape[0]
    row = pl.BlockSpec((tm, D_MODEL), lambda i: (i, 0))
    args = [x, g, w_in, ln_g, ln_b, sp, bsp, w_out]
    in_specs = [row] + [_const_spec(t.shape) for t in args[1:]]
    if sample:
        out_shape = (jax.ShapeDtypeStruct((n, D_MODEL), F32),) * 2
        out_specs = (row, row)
        scratch = []
    else:
        out_shape = jax.ShapeDtypeStruct((n, D_MODEL), F32)
        out_specs = row
        scratch = [pltpu.VMEM((tm, D_MODEL), BF16)]
    return pl.pallas_call(
        functools.partial(_sgu_kernel, tm=tm, sample=sample),
        out_shape=out_shape,
        grid=(n // tm,),
        in_specs=in_specs,
        out_specs=out_specs,
        scratch_shapes=scratch,
        compiler_params=_params("parallel"),
        name="sgu_sample" if sample else "sgu_prompt",
    )(*args)


CONV_ROWS = 64
CONV_LANES = 256
PROJ_LANES = 256


def _conv_prompt_kernel(x_ref, xprev_ref, g_ref, win_ref, wdw_ref, bdw_ref, lg_ref, lb_ref,
                        wout_ref, o_ref, tail_ref, a0_scr, a1_scr, sh_scr, c_scr,
                        *, tm, tiles_per_seq):
    s = pl.program_id(0)

    @pl.when(s == 0)
    def _():
        a0_scr[...] = jnp.zeros(a0_scr.shape, F32)
        a1_scr[...] = jnp.zeros(a1_scr.shape, F32)

    first_tap = CONV_HALO - (CONV_WIDTH - 1)
    reps = CONV_ROWS // SUBLANES

    def step(a_new, a_old):
        for r in range(1, SUBLANES):
            sh_scr[r - 1] = a_old[r:r + tm + CONV_HALO - SUBLANES, :]

        h = _rms(x_ref[...], g_ref[...]).astype(BF16)
        starts_sequence = s % tiles_per_seq == 0
        a_new[0:CONV_HALO, :] = jnp.where(starts_sequence, 0.0, a_old[tm:tm + CONV_HALO, :])

        def project(jc):
            cols = slice(jc * PROJ_LANES, (jc + 1) * PROJ_LANES)
            gate = slice(D_MODEL + jc * PROJ_LANES, D_MODEL + (jc + 1) * PROJ_LANES)
            a = _dot(h, win_ref[:, cols]) * jax.nn.sigmoid(_dot(h, win_ref[:, gate]))
            a_new[CONV_HALO:, cols] = a
            tail_ref[:, cols] = a[tm - CONV_HALO:, :]

        def convolve(rb):
            for lc in range(D_MODEL // CONV_LANES):
                cols = slice(lc * CONV_LANES, (lc + 1) * CONV_LANES)
                acc = jnp.concatenate([bdw_ref[:, cols]] * reps, axis=0)
                for kk in range(CONV_WIDTH):
                    whole, r = divmod(first_tap + kk, SUBLANES)
                    start = rb * CONV_ROWS + whole * SUBLANES
                    src = a_old if r == 0 else sh_scr.at[r - 1]
                    w = jnp.concatenate([wdw_ref[kk, :, cols]] * reps, axis=0)
                    acc = acc + src[start:start + CONV_ROWS, cols] * w
                c_scr[rb * CONV_ROWS:(rb + 1) * CONV_ROWS, cols] = acc

        n_proj = D_MODEL // PROJ_LANES
        n_conv = tm // CONV_ROWS
        for jc in range(n_proj):
            project(jc)
            for rb in range(jc * n_conv // n_proj, (jc + 1) * n_conv // n_proj):
                convolve(rb)
        c = _layer_norm(c_scr[...], lg_ref[...], lb_ref[...])
        c = (c * jax.nn.sigmoid(c)).astype(BF16)
        o_ref[...] = xprev_ref[...] + _dot(c, wout_ref[...])

    @pl.when(s % 2 == 0)
    def _():
        step(a0_scr, a1_scr)

    @pl.when(s % 2 == 1)
    def _():
        step(a1_scr, a0_scr)


def _conv_prompt(x, g, w_in, w_dw8, b_dw8, ln_g, ln_b, w_out, *, tm):
    b, seq, _ = x.shape
    tiles_per_seq = seq // tm
    n = b * tiles_per_seq
    consts = [g, w_in, w_dw8, b_dw8, ln_g, ln_b, w_out]
    a_scr = pltpu.VMEM((CONV_HALO + tm, D_MODEL), F32)
    x2 = x.reshape(b * seq, D_MODEL)
    out, tail = pl.pallas_call(
        functools.partial(_conv_prompt_kernel, tm=tm, tiles_per_seq=tiles_per_seq),
        out_shape=(jax.ShapeDtypeStruct((b * seq, D_MODEL), F32),
                   jax.ShapeDtypeStruct((b, CONV_HALO, D_MODEL), F32)),
        grid=(n + 1,),
        in_specs=[pl.BlockSpec((tm, D_MODEL), lambda s: (jnp.minimum(s, n - 1), 0)),
                  pl.BlockSpec((tm, D_MODEL), lambda s: (jnp.maximum(s - 1, 0), 0))]
                 + [_const_spec(t.shape) for t in consts],
        out_specs=(pl.BlockSpec((tm, D_MODEL), lambda s: (jnp.maximum(s - 1, 0), 0)),
                   pl.BlockSpec((None, CONV_HALO, D_MODEL),
                                lambda s: (jnp.minimum(s, n - 1) // tiles_per_seq, 0, 0))),
        scratch_shapes=[a_scr, a_scr,
                        pltpu.VMEM((SUBLANES - 1, CONV_HALO + tm - SUBLANES, D_MODEL), F32),
                        pltpu.VMEM((tm, D_MODEL), F32)],
        compiler_params=_params("arbitrary"),
        name="conv_prompt",
    )(x2, x2, *consts)
    return out.reshape(b, seq, D_MODEL), tail


def _conv_sample_kernel(a_ref, st_ref, wdw_ref, bdw_ref, c_ref, so_ref):
    n_hist = CONV_WIDTH - 1
    a = a_ref[...]
    acc = a * wdw_ref[n_hist:CONV_WIDTH, :] + bdw_ref[...]
    for k in range(n_hist):
        acc = acc + st_ref[k] * wdw_ref[k:k + 1, :]
    c_ref[...] = acc
    so_ref[0:n_hist - 1] = st_ref[1:n_hist]
    so_ref[n_hist - 1] = a


def _conv_sample_core(a, states, w_dw, b_dw, layer, *, tb):
    n = a.shape[0]
    row = pl.BlockSpec((tb, D_MODEL), lambda t: (t, 0))
    st_in = pl.BlockSpec((None, CONV_WIDTH - 1, tb, D_MODEL), lambda t: (layer, 0, t, 0))
    st_out = pl.BlockSpec((CONV_WIDTH - 1, tb, D_MODEL), lambda t: (0, t, 0))
    return pl.pallas_call(
        _conv_sample_kernel,
        out_shape=(jax.ShapeDtypeStruct((n, D_MODEL), F32),
                   jax.ShapeDtypeStruct(states.shape[1:], F32)),
        grid=(n // tb,),
        in_specs=[row, st_in, _const_spec(w_dw.shape), _const_spec(b_dw.shape)],
        out_specs=(row, st_out),
        compiler_params=_params("parallel"),
        name="conv_sample",
    )(a, states, w_dw, b_dw)


PROMPT_TILE = 512
FFN_TILE = 1024
SAMPLE_ATTN_TILE = 16
SAMPLE_CONV_TILE = 32


def kernel(x_prompt, x_sample, cache_swa_k, cache_swa_v, state_conv, rel_bias, norm_mix, norm_ffn, norm_final, attn_w_qkv, attn_w_o, attn_sinks, sgu_w_in, sgu_ln_g, sgu_ln_b, sgu_w_spatial, sgu_b_spatial, sgu_w_out, conv_w_in, conv_w_dw, conv_b_dw, conv_ln_g, conv_ln_b, conv_w_out, ffn_w_up, ffn_w_down):
    batch, seq, _ = x_prompt.shape
    dec = x_sample.shape[0]
    depth = norm_mix.shape[0]
    mixer_of_layer = tuple(i % 3 for i in range(depth))
    slot_of_layer = tuple(mixer_of_layer[:i].count(mixer_of_layer[i]) for i in range(depth))

    def row(v):
        return v.reshape(1, -1).astype(F32)

    n_attn = attn_w_qkv.shape[0]

    def caches_t(c):
        return jnp.transpose(c, (0, 1, 3, 4, 2)).reshape(n_attn, dec, KV_DIM, WINDOW)

    def caches_from_t(c):
        c = c.reshape(n_attn, dec, N_KV_HEADS, HEAD_DIM, WINDOW)
        return jnp.transpose(c, (0, 1, 4, 2, 3))

    xp = x_prompt
    xs = x_sample.reshape(dec, D_MODEL)
    dist_bias = _distance_bias(rel_bias)
    k_caches_t, v_caches_t = caches_t(cache_swa_k), caches_t(cache_swa_v)
    states = jnp.transpose(state_conv.astype(F32), (0, 2, 1, 3))
    w_qkv_all = attn_w_qkv.astype(BF16)
    w_o_all = attn_w_o.astype(BF16)
    w_up_all = ffn_w_up.astype(BF16)
    w_down_all = ffn_w_down.astype(BF16)

    kp, vp, sgu_v_new, convp, convs = [], [], [], [], []
    new_caches = None
    for i in range(depth):
        m, j = mixer_of_layer[i], slot_of_layer[i]
        g_mix = row(norm_mix[i])
        if m == 0:
            bias_p, cap_p, sink_rows = _prompt_bias_tables(dist_bias, attn_sinks[j])
            xp, k1, v1 = _attn_prompt(xp, g_mix, w_qkv_all, w_o_all, j, bias_p, cap_p, sink_rows,
                                      tq=PROMPT_TILE)
            kp.append(k1.reshape(batch, WINDOW, N_KV_HEADS, HEAD_DIM))
            vp.append(v1.reshape(batch, WINDOW, N_KV_HEADS, HEAD_DIM))

            bias_s, extra_s = _sample_bias_tables(dist_bias, attn_sinks[j])
            qkv_s, kv_t = _norm_proj(xs, g_mix, w_qkv_all, j, kv_t=True)
            o_s, *new_caches = _attn_sample_core(qkv_s, kv_t, k_caches_t, v_caches_t, bias_s,
                                                 extra_s, j, new_caches, tb=SAMPLE_ATTN_TILE)
            xs = _proj_res(xs, o_s, w_o_all, j, heads_grouped=True)
        elif m == 1:
            w_in = sgu_w_in[j].astype(BF16)
            w_out = sgu_w_out[j].astype(BF16)
            ln_g, ln_b = row(sgu_ln_g[j]), row(sgu_ln_b[j])
            sp = jnp.tril(sgu_w_spatial[j]).astype(BF16)
            bsp = jnp.broadcast_to(sgu_b_spatial[j].astype(F32)[:, :, None],
                                   (SGU_GROUPS, CHUNK, LANES))
            xp = _sgu(xp.reshape(batch * seq, D_MODEL), g_mix, w_in, ln_g, ln_b, sp, bsp, w_out,
                      tm=PROMPT_TILE, sample=False).reshape(batch, seq, D_MODEL)
            sp0 = row(jnp.repeat(sgu_w_spatial[j][:, 0, 0], LANES))
            bsp0 = row(jnp.repeat(sgu_b_spatial[j][:, 0], LANES))
            xs, v_rows = _sgu(xs, g_mix, w_in, ln_g, ln_b, sp0, bsp0, w_out, tm=dec, sample=True)
            sgu_v_new.append(v_rows.reshape(dec, 1, D_MODEL))
        else:
            w_in = conv_w_in[j].astype(BF16)
            w_out = conv_w_out[j].astype(BF16)
            w_dw = conv_w_dw[j].astype(F32)
            b_dw = row(conv_b_dw[j])
            ln_g, ln_b = row(conv_ln_g[j]), row(conv_ln_b[j])
            w_dw8 = jnp.broadcast_to(w_dw[:, None, :], (CONV_WIDTH, SUBLANES, D_MODEL))
            b_dw8 = jnp.broadcast_to(b_dw, (SUBLANES, D_MODEL))
            xp, tail = _conv_prompt(xp, g_mix, w_in, w_dw8, b_dw8, ln_g, ln_b, w_out,
                                    tm=PROMPT_TILE)
            convp.append(tail[:, CONV_HALO - (CONV_WIDTH - 1):, :])
            a_s = _norm_proj(xs, g_mix, w_in, glu=True)
            c_s, st2 = _conv_sample_core(a_s, states, w_dw, b_dw, j, tb=SAMPLE_CONV_TILE)
            xs = _proj_res(xs, c_s, w_out, ln=(ln_g, ln_b))
            convs.append(jnp.transpose(st2, (1, 0, 2)))

        g_fin = row(norm_final) if i == depth - 1 else None
        xp, xs = _ffn(xp.reshape(batch * seq, D_MODEL), xs, row(norm_ffn[i]), w_up_all, w_down_all,
                      i, g_fin, tm=FFN_TILE)
        xp = xp.reshape(batch, seq, D_MODEL)

    y_prompt = xp
    y_sample = xs.reshape(dec, 1, D_MODEL)
    k_new, v_new = (caches_from_t(c) for c in new_caches)
    return (y_prompt, y_sample, jnp.stack(kp), jnp.stack(vp), k_new, v_new,
            jnp.stack(sgu_v_new), jnp.stack(convp), jnp.stack(convs))
```

```python
import functools
import math

import jax
import jax.numpy as jnp
from jax import lax
from jax.experimental import pallas as pl
from jax.experimental.pallas import tpu as pltpu

D_MODEL = 1024
HEAD_DIM = 64
N_HEADS = 16
N_KV_HEADS = 4
Q_PER_KV = 4
KV_DIM = N_KV_HEADS * HEAD_DIM
QKV_DIM = D_MODEL + 2 * KV_DIM
WINDOW = 128
ATTN_SCALE = HEAD_DIM ** -0.5
N_BUCKETS = 32
MAX_DISTANCE = 128
CHUNK = 128
SGU_GROUPS = 8
CONV_WIDTH = 31
CONV_HALO = 32
D_FF = 4 * D_MODEL
EPS = 1e-6
NEG_INF = -1e30
F32_MAX = float(jnp.finfo(jnp.float32).max)
INV_SQRT2 = 1.0 / math.sqrt(2.0)

LANES = 128
SUBLANES = 8
VMEM_LIMIT = 56 * 1024 * 1024

F32 = jnp.float32
BF16 = jnp.bfloat16


def _const_spec(shape):
    n = len(shape)
    return pl.BlockSpec(shape, lambda *_: (0,) * n, pipeline_mode=pl.Buffered(1))


def _params(*sem):
    return pltpu.CompilerParams(dimension_semantics=sem, vmem_limit_bytes=VMEM_LIMIT)


def _rms(x, g):
    return x * lax.rsqrt(jnp.mean(x * x, axis=-1, keepdims=True) + EPS) * g


def _layer_norm(x, g, b):
    mu = jnp.mean(x, axis=-1, keepdims=True)
    xc = x - mu
    var = jnp.mean(xc * xc, axis=-1, keepdims=True)
    return xc * lax.rsqrt(var + EPS) * g + b


def _dot(a, b):
    return jnp.dot(a, b, preferred_element_type=F32)


def _dot_nt(a, b):
    return lax.dot_general(a, b, (((1,), (1,)), ((), ())), preferred_element_type=F32)


FFN_CHUNK = 512
N_FFN_CHUNKS = D_FF // FFN_CHUNK


def _layer_spec(shape, layer):
    zeros = (0,) * (len(shape) - 1)
    return pl.BlockSpec((None,) + tuple(shape[1:]), lambda *_: (layer,) + zeros,
                        pipeline_mode=pl.Buffered(1))


def _ffn_kernel(xp_ref, xs_ref, g_ref, wu_ref, wd_ref, *rest, final, n_prompt_tiles):
    *rest, wu_scr, wd_scr = rest
    if final:
        gf_ref, op_ref, os_ref = rest
    else:
        op_ref, os_ref = rest

    def block(x_ref, o_ref):
        x = x_ref[...]
        h = _rms(x, g_ref[...]).astype(BF16)
        y = x
        for c in range(N_FFN_CHUNKS):
            u = _dot(h, wu_scr[c])
            u = jnp.square(jnp.maximum(u, 0.0)).astype(BF16)
            y = y + _dot(u, wd_scr[c])
        o_ref[...] = _rms(y, gf_ref[...]) if final else y

    i = pl.program_id(0)
    tile = i - N_FFN_CHUNKS

    @pl.when(i < N_FFN_CHUNKS)
    def _():
        wu_scr[i] = wu_ref[...].astype(BF16)
        wd_scr[i] = wd_ref[...].astype(BF16)

    @pl.when(jnp.logical_and(tile >= 0, tile < n_prompt_tiles))
    def _():
        block(xp_ref, op_ref)

    @pl.when(tile == n_prompt_tiles)
    def _():
        block(xs_ref, os_ref)


def _ffn(xp, xs, g, w_up, w_down, layer, g_final=None, *, tm):
    n = xp.shape[0]
    nt = n // tm
    last_chunk = N_FFN_CHUNKS - 1
    row = pl.BlockSpec((tm, D_MODEL), lambda i: (jnp.clip(i - N_FFN_CHUNKS, 0, nt - 1), 0))
    in_specs = [row, _const_spec(xs.shape), _const_spec((1, D_MODEL)),
                pl.BlockSpec((None, D_MODEL, FFN_CHUNK),
                             lambda i: (layer, 0, jnp.minimum(i, last_chunk))),
                pl.BlockSpec((None, FFN_CHUNK, D_MODEL),
                             lambda i: (layer, jnp.minimum(i, last_chunk), 0))]
    args = [xp, xs, g, w_up, w_down]
    if g_final is not None:
        in_specs.append(_const_spec((1, D_MODEL)))
        args.append(g_final)
    return pl.pallas_call(
        functools.partial(_ffn_kernel, final=g_final is not None, n_prompt_tiles=nt),
        out_shape=(jax.ShapeDtypeStruct((n, D_MODEL), F32),
                   jax.ShapeDtypeStruct(xs.shape, F32)),
        grid=(N_FFN_CHUNKS + nt + 1,),
        in_specs=in_specs,
        out_specs=(row, pl.BlockSpec(xs.shape, lambda i: (0, 0))),
        scratch_shapes=[pltpu.VMEM((N_FFN_CHUNKS, D_MODEL, FFN_CHUNK), BF16),
                        pltpu.VMEM((N_FFN_CHUNKS, FFN_CHUNK, D_MODEL), BF16)],
        compiler_params=_params("arbitrary"),
        name="ffn_final" if g_final is not None else "ffn",
    )(*args)


def _regroup_heads(t, group_major):
    blocks = range(N_HEADS)
    if group_major:
        order = [N_KV_HEADS * (blk % N_KV_HEADS) + blk // N_KV_HEADS for blk in blocks]
    else:
        order = [Q_PER_KV * (blk % Q_PER_KV) + blk // Q_PER_KV for blk in blocks]
    return jnp.concatenate([t[:, h * HEAD_DIM:(h + 1) * HEAD_DIM] for h in order], axis=1)


def _norm_proj_kernel(x_ref, g_ref, w_ref, o_ref, *rest, glu, kv_t):
    h = _rms(x_ref[...], g_ref[...]).astype(BF16)
    y = _dot(h, w_ref[...])
    if glu:
        half = y.shape[1] // 2
        y = y[:, :half] * jax.nn.sigmoid(y[:, half:])
    if kv_t:
        (t_ref,) = rest
        o_ref[:, :D_MODEL] = _regroup_heads(y[:, :D_MODEL], group_major=True)
        o_ref[:, D_MODEL:] = y[:, D_MODEL:]
        t_ref[...] = y[:, D_MODEL:].T
    else:
        o_ref[...] = y


def _weight_spec(w, layer):
    return _const_spec(w.shape) if layer is None else _layer_spec(w.shape, layer)


def _norm_proj(x, g, w, layer=None, *, glu=False, kv_t=False):
    n = x.shape[0]
    n_out = w.shape[-1] // 2 if glu else w.shape[-1]
    out_shape = [jax.ShapeDtypeStruct((n, n_out), F32)]
    if kv_t:
        out_shape.append(jax.ShapeDtypeStruct((n_out - D_MODEL, n), F32))
    out = pl.pallas_call(
        functools.partial(_norm_proj_kernel, glu=glu, kv_t=kv_t),
        out_shape=tuple(out_shape),
        grid=(1,),
        in_specs=[_const_spec(x.shape), _const_spec(g.shape), _weight_spec(w, layer)],
        out_specs=tuple(_const_spec(o.shape) for o in out_shape),
        compiler_params=_params("arbitrary"),
        name="norm_proj_glu" if glu else "norm_proj",
    )(x, g, w)
    return out if kv_t else out[0]


def _proj_res_kernel(x_ref, a_ref, w_ref, *rest, conv_tail, heads_grouped):
    a = a_ref[...]
    if conv_tail:
        lg_ref, lb_ref, o_ref = rest
        a = _layer_norm(a, lg_ref[...], lb_ref[...])
        a = a * jax.nn.sigmoid(a)
    else:
        (o_ref,) = rest
    if heads_grouped:
        a = _regroup_heads(a, group_major=False)
    o_ref[...] = x_ref[...] + _dot(a.astype(BF16), w_ref[...])


def _proj_res(x, a, w, layer=None, ln=None, heads_grouped=False):
    args = [x, a, w] + (list(ln) if ln is not None else [])
    in_specs = [_const_spec(t.shape) for t in args]
    in_specs[2] = _weight_spec(w, layer)
    return pl.pallas_call(
        functools.partial(_proj_res_kernel, conv_tail=ln is not None, heads_grouped=heads_grouped),
        out_shape=jax.ShapeDtypeStruct(x.shape, F32),
        grid=(1,),
        in_specs=in_specs,
        out_specs=_const_spec(x.shape),
        compiler_params=_params("arbitrary"),
        name="proj_res_ln" if ln is not None else "proj_res",
    )(*args)


def _t5_bucket(dist):
    n = jnp.maximum(dist, 0)
    max_exact = N_BUCKETS // 2
    nf = jnp.maximum(n, 1).astype(F32)
    large = max_exact + (jnp.log(nf / max_exact) / math.log(MAX_DISTANCE / max_exact)
                         * (N_BUCKETS - max_exact)).astype(jnp.int32)
    large = jnp.minimum(large, N_BUCKETS - 1)
    return jnp.where(n < max_exact, n, large)


def _distance_bias(rel_bias):
    buckets = _t5_bucket(jnp.arange(WINDOW + 1, dtype=jnp.int32))
    onehot = (buckets[:, None] == jnp.arange(N_BUCKETS, dtype=jnp.int32)[None, :]).astype(F32)
    return jnp.dot(onehot, rel_bias.astype(F32), precision=lax.Precision.HIGHEST)


def _prompt_bias_tables(dist_bias, sinks):
    period = 3 * WINDOW
    line = jnp.concatenate([jnp.broadcast_to(dist_bias[WINDOW:], (WINDOW - 1, N_HEADS)),
                            dist_bias[::-1],
                            jnp.broadcast_to(dist_bias[:1], (WINDOW, N_HEADS))]).T
    skew = jnp.tile(line, (1, WINDOW))[:, :WINDOW * (period - 1)]
    per_head = skew.reshape(N_HEADS, WINDOW, period - 1)[:, :, WINDOW - 1:period - 1]
    per_head = per_head.reshape(N_KV_HEADS, 2, 2, WINDOW, 2 * WINDOW)
    bias = per_head.transpose(0, 2, 4, 1, 3).reshape(N_KV_HEADS, 4 * WINDOW, 2 * WINDOW)

    qi = jnp.arange(WINDOW, dtype=jnp.int32)[None, :]
    kj = jnp.arange(2 * WINDOW, dtype=jnp.int32)[:, None]
    dist = qi - kj + WINDOW
    allowed = (dist >= 0) & (dist <= WINDOW)
    first = allowed & (kj >= WINDOW)
    cap = jnp.stack([jnp.where(allowed, F32_MAX, NEG_INF), jnp.where(first, F32_MAX, NEG_INF)])
    cap = jnp.tile(cap.astype(F32), (1, 2, 2))

    sink_rows = sinks.astype(F32).reshape(N_KV_HEADS, 2, 2).transpose(0, 2, 1).reshape(2 * N_KV_HEADS, 2)
    sink_rows = jnp.repeat(sink_rows, WINDOW, axis=1)
    return bias, cap, sink_rows


def _attn_prompt_kernel(x_ref, xres_ref, g_ref, wqkv_ref, wo_ref, bias_ref, cap_ref, sink_ref,
                        o_ref, kc_ref, vc_ref,
                        q0_scr, q1_scr, klo0_scr, klo1_scr, khi0_scr, khi1_scr, vt0_scr, vt1_scr,
                        a0_scr, a1_scr, *, tq, tiles_per_seq):
    s = pl.program_id(0)
    sets = ((q0_scr, klo0_scr, khi0_scr, vt0_scr, a0_scr),
            (q1_scr, klo1_scr, khi1_scr, vt1_scr, a1_scr))

    @pl.when(s == 0)
    def _():
        for scr in sets[0] + sets[1]:
            scr[...] = jnp.zeros(scr.shape, BF16)

    low = lax.broadcasted_iota(jnp.int32, (tq, LANES), 1) < HEAD_DIM
    n_col = D_MODEL // KV_DIM

    def step(new, old):
        q_new, klo_new, khi_new, vt_new, att_done = new
        q_old, klo_old, khi_old, vt_old, att_out = old

        starts_sequence = s % tiles_per_seq == 0
        klo_new[:, 0:WINDOW, :] = jnp.where(starts_sequence, 0.0, klo_old[:, tq:tq + WINDOW, :])
        khi_new[:, 0:WINDOW, :] = jnp.where(starts_sequence, 0.0, khi_old[:, tq:tq + WINDOW, :])
        vt_new[:, :, 0:WINDOW] = jnp.where(starts_sequence, 0.0, vt_old[:, :, tq:tq + WINDOW])
        h = _rms(x_ref[...], g_ref[...]).astype(BF16)

        def project_q(c):
            cols = slice(c * KV_DIM, (c + 1) * KV_DIM)
            q_new[:, cols] = (_dot(h, wqkv_ref[:, cols]) * ATTN_SCALE).astype(BF16)

        def project_k():
            k = _dot(h, wqkv_ref[:, D_MODEL:D_MODEL + KV_DIM])
            kc_ref[...] = k[tq - WINDOW:, :]
            for c in range(KV_DIM // LANES):
                kc = k[:, c * LANES:(c + 1) * LANES]
                kr = pltpu.roll(kc, HEAD_DIM, axis=1)
                klo_new[2 * c, WINDOW:, :] = jnp.where(low, kc, 0.0).astype(BF16)
                khi_new[2 * c, WINDOW:, :] = jnp.where(low, 0.0, kr).astype(BF16)
                klo_new[2 * c + 1, WINDOW:, :] = jnp.where(low, kr, 0.0).astype(BF16)
                khi_new[2 * c + 1, WINDOW:, :] = jnp.where(low, 0.0, kc).astype(BF16)

        def project_v():
            v = _dot(h, wqkv_ref[:, D_MODEL + KV_DIM:])
            vc_ref[...] = v[tq - WINDOW:, :]
            for c in range(KV_DIM // LANES):
                vt = v[:, c * LANES:(c + 1) * LANES].T.astype(BF16)
                vt_new[2 * c, :, WINDOW:] = vt[:HEAD_DIM]
                vt_new[2 * c + 1, :, WINDOW:] = vt[HEAD_DIM:]

        def project_out(c):
            cols = slice(c * KV_DIM, (c + 1) * KV_DIM)
            o_ref[:, cols] = xres_ref[:, cols] + _dot(att_done[...], wo_ref[:, cols])

        is_first = jnp.where((s - 1) % tiles_per_seq == 0, 1, 0)

        def scores(jb, kh):
            rows = slice(jb * WINDOW, (jb + 1) * WINDOW)
            band = slice(jb * WINDOW, (jb + 2) * WINDOW)
            cap = cap_ref[is_first] if jb == 0 else cap_ref[0]
            qst = jnp.concatenate([q_old[rows, (2 * kh) * LANES:(2 * kh + 1) * LANES],
                                   q_old[rows, (2 * kh + 1) * LANES:(2 * kh + 2) * LANES]], axis=0)
            kst = jnp.concatenate([klo_old[kh, band, :], khi_old[kh, band, :]], axis=0)
            return jnp.minimum(_dot_nt(kst, qst) + bias_ref[kh], cap)

        def attend(jb, kh, sc):
            rows = slice(jb * WINDOW, (jb + 1) * WINDOW)
            vt = vt_old[kh, :, jb * WINDOW:(jb + 2) * WINDOW]
            halves = []
            for half in range(2):
                sh = sc[half * 2 * WINDOW:(half + 1) * 2 * WINDOW]
                sink = sink_ref[2 * kh + half:2 * kh + half + 1, :]
                m = jnp.maximum(jnp.max(sh, axis=0, keepdims=True), sink)
                p = jnp.exp(sh - m)
                denom = jnp.sum(p, axis=0, keepdims=True) + jnp.exp(sink - m)
                halves.append(_dot(vt, p.astype(BF16)) * (1.0 / denom))
            o = jnp.concatenate(halves, axis=0).T.astype(BF16)
            att_out[rows, (2 * kh) * LANES:(2 * kh + 1) * LANES] = o[:WINDOW]
            att_out[rows, (2 * kh + 1) * LANES:(2 * kh + 2) * LANES] = o[WINDOW:]

        pieces = [functools.partial(project_q, c) for c in range(n_col)] + [project_k, project_v]
        pieces += [functools.partial(project_out, c) for c in range(n_col)]
        steps = [(jb, kh) for jb in range(tq // WINDOW) for kh in range(N_KV_HEADS)]
        piece_at = {(n * len(steps)) // len(pieces): piece for n, piece in enumerate(pieces)}
        assert len(piece_at) == len(pieces)
        s_next = scores(*steps[0])
        for n, st in enumerate(steps):
            s_cur = s_next
            if n + 1 < len(steps):
                s_next = scores(*steps[n + 1])
            if n in piece_at:
                piece_at[n]()
            attend(*st, s_cur)

    @pl.when(s % 2 == 0)
    def _():
        step(sets[0], sets[1])

    @pl.when(s % 2 == 1)
    def _():
        step(sets[1], sets[0])


def _attn_prompt(x, g, w_qkv, w_o, layer, bias, cap, sink_rows, *, tq):
    b, seq, _ = x.shape
    tiles_per_seq = seq // tq
    n = b * tiles_per_seq
    x2 = x.reshape(b * seq, D_MODEL)
    lagged = pl.BlockSpec((tq, D_MODEL), lambda s: (jnp.maximum(s - 2, 0), 0))
    cache = pl.BlockSpec((None, WINDOW, KV_DIM),
                         lambda s: (jnp.minimum(s, n - 1) // tiles_per_seq, 0, 0))
    q_scr = pltpu.VMEM((tq, D_MODEL), BF16)
    k_scr = pltpu.VMEM((N_KV_HEADS, WINDOW + tq, LANES), BF16)
    vt_scr = pltpu.VMEM((N_KV_HEADS, HEAD_DIM, WINDOW + tq), BF16)
    out, kc, vc = pl.pallas_call(
        functools.partial(_attn_prompt_kernel, tq=tq, tiles_per_seq=tiles_per_seq),
        out_shape=(jax.ShapeDtypeStruct((b * seq, D_MODEL), F32),
                   jax.ShapeDtypeStruct((b, WINDOW, KV_DIM), F32),
                   jax.ShapeDtypeStruct((b, WINDOW, KV_DIM), F32)),
        grid=(n + 2,),
        in_specs=[pl.BlockSpec((tq, D_MODEL), lambda s: (jnp.minimum(s, n - 1), 0)), lagged,
                  _const_spec((1, D_MODEL)),
                  _layer_spec(w_qkv.shape, layer), _layer_spec(w_o.shape, layer),
                  _const_spec(bias.shape), _const_spec(cap.shape), _const_spec(sink_rows.shape)],
        out_specs=(lagged, cache, cache),
        scratch_shapes=[q_scr, q_scr, k_scr, k_scr, k_scr, k_scr, vt_scr, vt_scr, q_scr, q_scr],
        compiler_params=_params("arbitrary"),
        name="attn_prompt",
    )(x2, x2, g, w_qkv, w_o, bias, cap, sink_rows)
    return out.reshape(b, seq, D_MODEL), kc, vc


ROWS_PAD = 8 * Q_PER_KV


def _sample_bias_tables(dist_bias, sinks):
    def rows(per_head):
        t = per_head.reshape(N_KV_HEADS, Q_PER_KV, -1).transpose(1, 0, 2)
        return jnp.pad(t, ((0, 0), (0, 8 - N_KV_HEADS), (0, 0))).reshape(ROWS_PAD, -1)

    bias = rows(dist_bias[:0:-1].T)
    extra = rows(jnp.stack([dist_bias[0], sinks.astype(F32)], axis=1))
    return bias, jnp.pad(extra, ((0, 0), (0, LANES - 2)))


def _attn_sample_kernel(q_ref, kn_ref, vn_ref, kvt_ref, kc_ref, vc_ref, bias_ref, extra_ref,
                        *rest, tb, n_aliased, out_slot):
    o_ref, ko_ref, vo_ref = rest[n_aliased:]
    if out_slot is not None:
        for out_ref in (ko_ref, vo_ref):
            for slot in range(out_ref.shape[0]):
                if slot != out_slot:
                    out_ref[slot] = jnp.zeros(out_ref.shape[1:], F32)
        ko_ref, vo_ref = ko_ref.at[out_slot], vo_ref.at[out_slot]
    t = pl.program_id(0)
    sub = lax.broadcasted_iota(jnp.int32, (8, KV_DIM), 0)
    lane_head = lax.broadcasted_iota(jnp.int32, (8, KV_DIM), 1) // HEAD_DIM
    own = jnp.logical_and(sub < N_KV_HEADS, lane_head == sub)
    newest = lax.broadcasted_iota(jnp.int32, (KV_DIM, WINDOW), 1) == WINDOW - 1
    bias = bias_ref[...]
    bias_new = extra_ref[:, 0:1]
    sink = extra_ref[:, 1:2]

    def scores(bb):
        q_rows = []
        for gq in range(Q_PER_KV):
            q_g = q_ref[bb:bb + 1, gq * KV_DIM:(gq + 1) * KV_DIM] * ATTN_SCALE
            q_rows.append(jnp.where(own, jnp.broadcast_to(q_g, (8, KV_DIM)), 0.0))
        q_blk = jnp.concatenate(q_rows, axis=0)
        s_old = _dot(q_blk.astype(BF16), kc_ref[bb].astype(BF16)) + bias
        s_new = jnp.sum(q_blk * kn_ref[bb:bb + 1, :], axis=1, keepdims=True) + bias_new
        return s_old, s_new

    def attend(bb, s_old, s_new):
        m = jnp.maximum(jnp.max(s_old, axis=1, keepdims=True), jnp.maximum(s_new, sink))
        p_old = jnp.exp(s_old - m)
        p_new = jnp.exp(s_new - m)
        denom = jnp.sum(p_old, axis=1, keepdims=True) + p_new + jnp.exp(sink - m)
        o = _dot_nt(p_old.astype(BF16), vc_ref[bb].astype(BF16)) + p_new * vn_ref[bb:bb + 1, :]
        o = o / denom
        for gq in range(Q_PER_KV):
            o_g = jnp.sum(jnp.where(own, o[8 * gq:8 * gq + 8], 0.0), axis=0, keepdims=True)
            o_ref[bb:bb + 1, gq * KV_DIM:(gq + 1) * KV_DIM] = o_g

    def shift(bb):
        to_last = (WINDOW - 1) - (t * tb + bb)
        for cache_ref, row0, out_ref in ((kc_ref, 0, ko_ref), (vc_ref, KV_DIM, vo_ref)):
            moved = pltpu.roll(cache_ref[bb], WINDOW - 1, axis=1)
            col = pltpu.roll(kvt_ref[row0:row0 + KV_DIM, :], to_last, axis=1)
            out_ref[bb] = jnp.where(newest, col, moved)

    for bb in range(tb):
        shift(bb)
    s_next = scores(0)
    for bb in range(tb):
        s_cur = s_next
        if bb + 1 < tb:
            s_next = scores(bb + 1)
        attend(bb, *s_cur)


def _attn_sample_core(qkv, kv_t, k_caches, v_caches, bias, extra, layer, new_caches=None, *, tb):
    n = qkv.shape[0]
    cache = pl.BlockSpec((None, tb, KV_DIM, WINDOW), lambda t: (layer, t, 0, 0))
    kv_col = D_MODEL // KV_DIM
    args = [qkv, qkv, qkv, kv_t, k_caches, v_caches, bias, extra]
    in_specs = [pl.BlockSpec((tb, D_MODEL), lambda t: (t, 0)),
                pl.BlockSpec((tb, KV_DIM), lambda t: (t, kv_col)),
                pl.BlockSpec((tb, KV_DIM), lambda t: (t, kv_col + 1)),
                _const_spec(kv_t.shape), cache, cache,
                _const_spec(bias.shape), _const_spec(extra.shape)]
    if new_caches is None:
        aliases, out_slot = {}, layer
        cache_out = pl.BlockSpec((k_caches.shape[0], tb, KV_DIM, WINDOW), lambda t: (0, t, 0, 0))
    else:
        aliases, out_slot = {len(args): 1, len(args) + 1: 2}, None
        cache_out = cache
        args += list(new_caches)
        in_specs += [pl.BlockSpec(memory_space=pl.ANY)] * 2
    return pl.pallas_call(
        functools.partial(_attn_sample_kernel, tb=tb, n_aliased=len(aliases), out_slot=out_slot),
        out_shape=(jax.ShapeDtypeStruct((n, D_MODEL), F32),
                   jax.ShapeDtypeStruct(k_caches.shape, F32),
                   jax.ShapeDtypeStruct(v_caches.shape, F32)),
        grid=(n // tb,),
        in_specs=in_specs,
        out_specs=(pl.BlockSpec((tb, D_MODEL), lambda t: (t, 0)), cache_out, cache_out),
        input_output_aliases=aliases,
        compiler_params=_params("parallel"),
        name="attn_sample",
    )(*args)


def _sgu_kernel(x_ref, g_ref, win_ref, lg_ref, lb_ref, sp_ref, bsp_ref, wout_ref, *rest,
                tm, sample):
    x = x_ref[...]
    h = _rms(x, g_ref[...]).astype(BF16)
    z = _dot(h, win_ref[...])
    z = 0.5 * z * (1.0 + lax.erf(z * INV_SQRT2))
    u = z[:, :D_MODEL]
    v = _layer_norm(z[:, D_MODEL:], lg_ref[...], lb_ref[...])
    if sample:
        o_ref, v_ref = rest
        v_ref[...] = v
        gated = (u * (v * sp_ref[...] + bsp_ref[...])).astype(BF16)
    else:
        o_ref, gated_scr = rest
        vb = v.astype(BF16)
        for c in range(tm // CHUNK):
            rows = slice(c * CHUNK, (c + 1) * CHUNK)
            for gi in range(SGU_GROUPS):
                cols = slice(gi * LANES, (gi + 1) * LANES)
                mixed = _dot(sp_ref[gi], vb[rows, cols]) + bsp_ref[gi]
                gated_scr[rows, cols] = (u[rows, cols] * mixed).astype(BF16)
        gated = gated_scr[...]
    o_ref[...] = x + _dot(gated, wout_ref[...])


def _sgu(x, g, w_in, ln_g, ln_b, sp, bsp, w_out, *, tm, sample):
    n = x.shape[0]
    row = pl.BlockSpec((tm, D_MODEL), lambda i: (i, 0))
    args = [x, g, w_in, ln_g, ln_b, sp, bsp, w_out]
    in_specs = [row] + [_const_spec(t.shape) for t in args[1:]]
    if sample:
        out_shape = (jax.ShapeDtypeStruct((n, D_MODEL), F32),) * 2
        out_specs = (row, row)
        scratch = []
    else:
        out_shape = jax.ShapeDtypeStruct((n, D_MODEL), F32)
        out_specs = row
        scratch = [pltpu.VMEM((tm, D_MODEL), BF16)]
    return pl.pallas_call(
        functools.partial(_sgu_kernel, tm=tm, sample=sample),
        out_shape=out_shape,
        grid=(n // tm,),
        in_specs=in_specs,
        out_specs=out_specs,
        scratch_shapes=scratch,
        compiler_params=_params("parallel"),
        name="sgu_sample" if sample else "sgu_prompt",
    )(*args)


CONV_ROWS = 64
CONV_LANES = 256
PROJ_LANES = 256


def _conv_prompt_kernel(x_ref, xprev_ref, g_ref, win_ref, wdw_ref, bdw_ref, lg_ref, lb_ref,
                        wout_ref, o_ref, tail_ref, a0_scr, a1_scr, sh_scr, c_scr,
                        *, tm, tiles_per_seq):
    s = pl.program_id(0)

    @pl.when(s == 0)
    def _():
        a0_scr[...] = jnp.zeros(a0_scr.shape, F32)
        a1_scr[...] = jnp.zeros(a1_scr.shape, F32)

    first_tap = CONV_HALO - (CONV_WIDTH - 1)
    reps = CONV_ROWS // SUBLANES

    def step(a_new, a_old):
        for r in range(1, SUBLANES):
            sh_scr[r - 1] = a_old[r:r + tm + CONV_HALO - SUBLANES, :]

        h = _rms(x_ref[...], g_ref[...]).astype(BF16)
        starts_sequence = s % tiles_per_seq == 0
        a_new[0:CONV_HALO, :] = jnp.where(starts_sequence, 0.0, a_old[tm:tm + CONV_HALO, :])

        def project(jc):
            cols = slice(jc * PROJ_LANES, (jc + 1) * PROJ_LANES)
            gate = slice(D_MODEL + jc * PROJ_LANES, D_MODEL + (jc + 1) * PROJ_LANES)
            a = _dot(h, win_ref[:, cols]) * jax.nn.sigmoid(_dot(h, win_ref[:, gate]))
            a_new[CONV_HALO:, cols] = a
            tail_ref[:, cols] = a[tm - CONV_HALO:, :]

        def convolve(rb):
            for lc in range(D_MODEL // CONV_LANES):
                cols = slice(lc * CONV_LANES, (lc + 1) * CONV_LANES)
                acc = jnp.concatenate([bdw_ref[:, cols]] * reps, axis=0)
                for kk in range(CONV_WIDTH):
                    whole, r = divmod(first_tap + kk, SUBLANES)
                    start = rb * CONV_ROWS + whole * SUBLANES
                    src = a_old if r == 0 else sh_scr.at[r - 1]
                    w = jnp.concatenate([wdw_ref[kk, :, cols]] * reps, axis=0)
                    acc = acc + src[start:start + CONV_ROWS, cols] * w
                c_scr[rb * CONV_ROWS:(rb + 1) * CONV_ROWS, cols] = acc

        n_proj = D_MODEL // PROJ_LANES
        n_conv = tm // CONV_ROWS
        for jc in range(n_proj):
            project(jc)
            for rb in range(jc * n_conv // n_proj, (jc + 1) * n_conv // n_proj):
                convolve(rb)
        c = _layer_norm(c_scr[...], lg_ref[...], lb_ref[...])
        c = (c * jax.nn.sigmoid(c)).astype(BF16)
        o_ref[...] = xprev_ref[...] + _dot(c, wout_ref[...])

    @pl.when(s % 2 == 0)
    def _():
        step(a0_scr, a1_scr)

    @pl.when(s % 2 == 1)
    def _():
        step(a1_scr, a0_scr)


def _conv_prompt(x, g, w_in, w_dw8, b_dw8, ln_g, ln_b, w_out, *, tm):
    b, seq, _ = x.shape
    tiles_per_seq = seq // tm
    n = b * tiles_per_seq
    consts = [g, w_in, w_dw8, b_dw8, ln_g, ln_b, w_out]
    a_scr = pltpu.VMEM((CONV_HALO + tm, D_MODEL), F32)
    x2 = x.reshape(b * seq, D_MODEL)
    out, tail = pl.pallas_call(
        functools.partial(_conv_prompt_kernel, tm=tm, tiles_per_seq=tiles_per_seq),
        out_shape=(jax.ShapeDtypeStruct((b * seq, D_MODEL), F32),
                   jax.ShapeDtypeStruct((b, CONV_HALO, D_MODEL), F32)),
        grid=(n + 1,),
        in_specs=[pl.BlockSpec((tm, D_MODEL), lambda s: (jnp.minimum(s, n - 1), 0)),
                  pl.BlockSpec((tm, D_MODEL), lambda s: (jnp.maximum(s - 1, 0), 0))]
                 + [_const_spec(t.shape) for t in consts],
        out_specs=(pl.BlockSpec((tm, D_MODEL), lambda s: (jnp.maximum(s - 1, 0), 0)),
                   pl.BlockSpec((None, CONV_HALO, D_MODEL),
                                lambda s: (jnp.minimum(s, n - 1) // tiles_per_seq, 0, 0))),
        scratch_shapes=[a_scr, a_scr,
                        pltpu.VMEM((SUBLANES - 1, CONV_HALO + tm - SUBLANES, D_MODEL), F32),
                        pltpu.VMEM((tm, D_MODEL), F32)],
        compiler_params=_params("arbitrary"),
        name="conv_prompt",
    )(x2, x2, *consts)
    return out.reshape(b, seq, D_MODEL), tail


def _conv_sample_kernel(a_ref, st_ref, wdw_ref, bdw_ref, c_ref, so_ref):
    n_hist = CONV_WIDTH - 1
    a = a_ref[...]
    acc = a * wdw_ref[n_hist:CONV_WIDTH, :] + bdw_ref[...]
    for k in range(n_hist):
        acc = acc + st_ref[k] * wdw_ref[k:k + 1, :]
    c_ref[...] = acc
    so_ref[0:n_hist - 1] = st_ref[1:n_hist]
    so_ref[n_hist - 1] = a


def _conv_sample_core(a, states, w_dw, b_dw, layer, *, tb):
    n = a.shape[0]
    row = pl.BlockSpec((tb, D_MODEL), lambda t: (t, 0))
    st_in = pl.BlockSpec((None, CONV_WIDTH - 1, tb, D_MODEL), lambda t: (layer, 0, t, 0))
    st_out = pl.BlockSpec((CONV_WIDTH - 1, tb, D_MODEL), lambda t: (0, t, 0))
    return pl.pallas_call(
        _conv_sample_kernel,
        out_shape=(jax.ShapeDtypeStruct((n, D_MODEL), F32),
                   jax.ShapeDtypeStruct(states.shape[1:], F32)),
        grid=(n // tb,),
        in_specs=[row, st_in, _const_spec(w_dw.shape), _const_spec(b_dw.shape)],
        out_specs=(row, st_out),
        compiler_params=_params("parallel"),
        name="conv_sample",
    )(a, states, w_dw, b_dw)


PROMPT_TILE = 512
FFN_TILE = 1024
SAMPLE_ATTN_TILE = 16
SAMPLE_CONV_TILE = 32


def kernel(x_prompt, x_sample, cache_swa_k, cache_swa_v, state_conv, rel_bias, norm_mix, norm_ffn, norm_final, attn_w_qkv, attn_w_o, attn_sinks, sgu_w_in, sgu_ln_g, sgu_ln_b, sgu_w_spatial, sgu_b_spatial, sgu_w_out, conv_w_in, conv_w_dw, conv_b_dw, conv_ln_g, conv_ln_b, conv_w_out, ffn_w_up, ffn_w_down):
    batch, seq, _ = x_prompt.shape
    dec = x_sample.shape[0]
    depth = norm_mix.shape[0]
    mixer_of_layer = tuple(i % 3 for i in range(depth))
    slot_of_layer = tuple(mixer_of_layer[:i].count(mixer_of_layer[i]) for i in range(depth))

    def row(v):
        return v.reshape(1, -1).astype(F32)

    n_attn = attn_w_qkv.shape[0]

    def caches_t(c):
        return jnp.transpose(c, (0, 1, 3, 4, 2)).reshape(n_attn, dec, KV_DIM, WINDOW)

    def caches_from_t(c):
        c = c.reshape(n_attn, dec, N_KV_HEADS, HEAD_DIM, WINDOW)
        return jnp.transpose(c, (0, 1, 4, 2, 3))

    xp = x_prompt
    xs = x_sample.reshape(dec, D_MODEL)
    dist_bias = _distance_bias(rel_bias)
    k_caches_t, v_caches_t = caches_t(cache_swa_k), caches_t(cache_swa_v)
    states = jnp.transpose(state_conv.astype(F32), (0, 2, 1, 3))
    w_qkv_all = attn_w_qkv.astype(BF16)
    w_o_all = attn_w_o.astype(BF16)

    kp, vp, sgu_v_new, convp, convs = [], [], [], [], []
    new_caches = None
    for i in range(depth):
        m, j = mixer_of_layer[i], slot_of_layer[i]
        g_mix = row(norm_mix[i])
        if m == 0:
            bias_p, cap_p, sink_rows = _prompt_bias_tables(dist_bias, attn_sinks[j])
            xp, k1, v1 = _attn_prompt(xp, g_mix, w_qkv_all, w_o_all, j, bias_p, cap_p, sink_rows,
                                      tq=PROMPT_TILE)
            kp.append(k1.reshape(batch, WINDOW, N_KV_HEADS, HEAD_DIM))
            vp.append(v1.reshape(batch, WINDOW, N_KV_HEADS, HEAD_DIM))

            bias_s, extra_s = _sample_bias_tables(dist_bias, attn_sinks[j])
            qkv_s, kv_t = _norm_proj(xs, g_mix, w_qkv_all, j, kv_t=True)
            o_s, *new_caches = _attn_sample_core(qkv_s, kv_t, k_caches_t, v_caches_t, bias_s,
                                                 extra_s, j, new_caches, tb=SAMPLE_ATTN_TILE)
            xs = _proj_res(xs, o_s, w_o_all, j, heads_grouped=True)
        elif m == 1:
            w_in = sgu_w_in[j].astype(BF16)
            w_out = sgu_w_out[j].astype(BF16)
            ln_g, ln_b = row(sgu_ln_g[j]), row(sgu_ln_b[j])
            sp = jnp.tril(sgu_w_spatial[j]).astype(BF16)
            bsp = jnp.broadcast_to(sgu_b_spatial[j].astype(F32)[:, :, None],
                                   (SGU_GROUPS, CHUNK, LANES))
            xp = _sgu(xp.reshape(batch * seq, D_MODEL), g_mix, w_in, ln_g, ln_b, sp, bsp, w_out,
                      tm=PROMPT_TILE, sample=False).reshape(batch, seq, D_MODEL)
            sp0 = row(jnp.repeat(sgu_w_spatial[j][:, 0, 0], LANES))
            bsp0 = row(jnp.repeat(sgu_b_spatial[j][:, 0], LANES))
            xs, v_rows = _sgu(xs, g_mix, w_in, ln_g, ln_b, sp0, bsp0, w_out, tm=dec, sample=True)
            sgu_v_new.append(v_rows.reshape(dec, 1, D_MODEL))
        else:
            w_in = conv_w_in[j].astype(BF16)
            w_out = conv_w_out[j].astype(BF16)
            w_dw = conv_w_dw[j].astype(F32)
            b_dw = row(conv_b_dw[j])
            ln_g, ln_b = row(conv_ln_g[j]), row(conv_ln_b[j])
            w_dw8 = jnp.broadcast_to(w_dw[:, None, :], (CONV_WIDTH, SUBLANES, D_MODEL))
            b_dw8 = jnp.broadcast_to(b_dw, (SUBLANES, D_MODEL))
            xp, tail = _conv_prompt(xp, g_mix, w_in, w_dw8, b_dw8, ln_g, ln_b, w_out,
                                    tm=PROMPT_TILE)
            convp.append(tail[:, CONV_HALO - (CONV_WIDTH - 1):, :])
            a_s = _norm_proj(xs, g_mix, w_in, glu=True)
            c_s, st2 = _conv_sample_core(a_s, states, w_dw, b_dw, j, tb=SAMPLE_CONV_TILE)
            xs = _proj_res(xs, c_s, w_out, ln=(ln_g, ln_b))
            convs.append(jnp.transpose(st2, (1, 0, 2)))

        g_fin = row(norm_final) if i == depth - 1 else None
        xp, xs = _ffn(xp.reshape(batch * seq, D_MODEL), xs, row(norm_ffn[i]), ffn_w_up, ffn_w_down,
                      i, g_fin, tm=FFN_TILE)
        xp = xp.reshape(batch, seq, D_MODEL)

    y_prompt = xp
    y_sample = xs.reshape(dec, 1, D_MODEL)
    k_new, v_new = (caches_from_t(c) for c in new_caches)
    return (y_prompt, y_sample, jnp.stack(kp), jnp.stack(vp), k_new, v_new,
            jnp.stack(sgu_v_new), jnp.stack(convp), jnp.stack(convs))
```

```python
import functools
import math

import jax
import jax.numpy as jnp
from jax import lax
from jax.experimental import pallas as pl
from jax.experimental.pallas import tpu as pltpu

D_MODEL = 1024
HEAD_DIM = 64
N_HEADS = 16
N_KV_HEADS = 4
Q_PER_KV = 4
KV_DIM = N_KV_HEADS * HEAD_DIM
QKV_DIM = D_MODEL + 2 * KV_DIM
WINDOW = 128
ATTN_SCALE = HEAD_DIM ** -0.5
N_BUCKETS = 32
MAX_DISTANCE = 128
CHUNK = 128
SGU_GROUPS = 8
SGU_PARTS = 2
CONV_WIDTH = 31
CONV_HALO = 32
D_FF = 4 * D_MODEL
EPS = 1e-6
NEG_INF = -1e30
F32_MAX = float(jnp.finfo(jnp.float32).max)
INV_SQRT2 = 1.0 / math.sqrt(2.0)

LANES = 128
SUBLANES = 8
VMEM_LIMIT = 56 * 1024 * 1024

F32 = jnp.float32
BF16 = jnp.bfloat16


def _const_spec(shape):
    n = len(shape)
    return pl.BlockSpec(shape, lambda *_: (0,) * n, pipeline_mode=pl.Buffered(1))


def _params(*sem):
    return pltpu.CompilerParams(dimension_semantics=sem, vmem_limit_bytes=VMEM_LIMIT)


def _rms(x, g):
    return x * lax.rsqrt(jnp.mean(x * x, axis=-1, keepdims=True) + EPS) * g


def _layer_norm(x, g, b):
    mu = jnp.mean(x, axis=-1, keepdims=True)
    xc = x - mu
    var = jnp.mean(xc * xc, axis=-1, keepdims=True)
    return xc * lax.rsqrt(var + EPS) * g + b


def _dot(a, b):
    return jnp.dot(a, b, preferred_element_type=F32)


def _dot_nt(a, b):
    return lax.dot_general(a, b, (((1,), (1,)), ((), ())), preferred_element_type=F32)


FFN_CHUNK = 512
N_FFN_CHUNKS = D_FF // FFN_CHUNK


def _layer_spec(shape, layer):
    zeros = (0,) * (len(shape) - 1)
    return pl.BlockSpec((None,) + tuple(shape[1:]), lambda *_: (layer,) + zeros,
                        pipeline_mode=pl.Buffered(1))


def _ffn_kernel(xp_ref, xs_ref, g_ref, wu_ref, wd_ref, *rest, final, n_prompt_tiles):
    *rest, wu_scr, wd_scr = rest
    if final:
        gf_ref, op_ref, os_ref = rest
    else:
        op_ref, os_ref = rest

    def block(x_ref, o_ref):
        x = x_ref[...]
        h = _rms(x, g_ref[...]).astype(BF16)
        y = x
        for c in range(N_FFN_CHUNKS):
            u = _dot(h, wu_scr[c])
            u = jnp.square(jnp.maximum(u, 0.0)).astype(BF16)
            y = y + _dot(u, wd_scr[c])
        o_ref[...] = _rms(y, gf_ref[...]) if final else y

    i = pl.program_id(0)
    tile = i - N_FFN_CHUNKS

    @pl.when(i < N_FFN_CHUNKS)
    def _():
        wu_scr[i] = wu_ref[...].astype(BF16)
        wd_scr[i] = wd_ref[...].astype(BF16)

    @pl.when(jnp.logical_and(tile >= 0, tile < n_prompt_tiles))
    def _():
        block(xp_ref, op_ref)

    @pl.when(tile == n_prompt_tiles)
    def _():
        block(xs_ref, os_ref)


def _ffn(xp, xs, g, w_up, w_down, layer, g_final=None, *, tm):
    n = xp.shape[0]
    nt = n // tm
    last_chunk = N_FFN_CHUNKS - 1
    row = pl.BlockSpec((tm, D_MODEL), lambda i: (jnp.clip(i - N_FFN_CHUNKS, 0, nt - 1), 0))
    in_specs = [row, _const_spec(xs.shape), _const_spec((1, D_MODEL)),
                pl.BlockSpec((None, D_MODEL, FFN_CHUNK),
                             lambda i: (layer, 0, jnp.minimum(i, last_chunk))),
                pl.BlockSpec((None, FFN_CHUNK, D_MODEL),
                             lambda i: (layer, jnp.minimum(i, last_chunk), 0))]
    args = [xp, xs, g, w_up, w_down]
    if g_final is not None:
        in_specs.append(_const_spec((1, D_MODEL)))
        args.append(g_final)
    return pl.pallas_call(
        functools.partial(_ffn_kernel, final=g_final is not None, n_prompt_tiles=nt),
        out_shape=(jax.ShapeDtypeStruct((n, D_MODEL), F32),
                   jax.ShapeDtypeStruct(xs.shape, F32)),
        grid=(N_FFN_CHUNKS + nt + 1,),
        in_specs=in_specs,
        out_specs=(row, pl.BlockSpec(xs.shape, lambda i: (0, 0))),
        scratch_shapes=[pltpu.VMEM((N_FFN_CHUNKS, D_MODEL, FFN_CHUNK), BF16),
                        pltpu.VMEM((N_FFN_CHUNKS, FFN_CHUNK, D_MODEL), BF16)],
        compiler_params=_params("arbitrary"),
        name="ffn_final" if g_final is not None else "ffn",
    )(*args)


def _regroup_heads(t, group_major):
    blocks = range(N_HEADS)
    if group_major:
        order = [N_KV_HEADS * (blk % N_KV_HEADS) + blk // N_KV_HEADS for blk in blocks]
    else:
        order = [Q_PER_KV * (blk % Q_PER_KV) + blk // Q_PER_KV for blk in blocks]
    return jnp.concatenate([t[:, h * HEAD_DIM:(h + 1) * HEAD_DIM] for h in order], axis=1)


def _norm_proj_kernel(x_ref, g_ref, w_ref, o_ref, *rest, glu, kv_t):
    h = _rms(x_ref[...], g_ref[...]).astype(BF16)
    y = _dot(h, w_ref[...])
    if glu:
        half = y.shape[1] // 2
        y = y[:, :half] * jax.nn.sigmoid(y[:, half:])
    if kv_t:
        (t_ref,) = rest
        o_ref[:, :D_MODEL] = _regroup_heads(y[:, :D_MODEL], group_major=True)
        o_ref[:, D_MODEL:] = y[:, D_MODEL:]
        t_ref[...] = y[:, D_MODEL:].T
    else:
        o_ref[...] = y


def _weight_spec(w, layer):
    return _const_spec(w.shape) if layer is None else _layer_spec(w.shape, layer)


def _norm_proj(x, g, w, layer=None, *, glu=False, kv_t=False):
    n = x.shape[0]
    n_out = w.shape[-1] // 2 if glu else w.shape[-1]
    out_shape = [jax.ShapeDtypeStruct((n, n_out), F32)]
    if kv_t:
        out_shape.append(jax.ShapeDtypeStruct((n_out - D_MODEL, n), F32))
    out = pl.pallas_call(
        functools.partial(_norm_proj_kernel, glu=glu, kv_t=kv_t),
        out_shape=tuple(out_shape),
        grid=(1,),
        in_specs=[_const_spec(x.shape), _const_spec(g.shape), _weight_spec(w, layer)],
        out_specs=tuple(_const_spec(o.shape) for o in out_shape),
        compiler_params=_params("arbitrary"),
        name="norm_proj_glu" if glu else "norm_proj",
    )(x, g, w)
    return out if kv_t else out[0]


def _proj_res_kernel(x_ref, a_ref, w_ref, *rest, conv_tail, heads_grouped):
    a = a_ref[...]
    if conv_tail:
        lg_ref, lb_ref, o_ref = rest
        a = _layer_norm(a, lg_ref[...], lb_ref[...])
        a = a * jax.nn.sigmoid(a)
    else:
        (o_ref,) = rest
    if heads_grouped:
        a = _regroup_heads(a, group_major=False)
    o_ref[...] = x_ref[...] + _dot(a.astype(BF16), w_ref[...])


def _proj_res(x, a, w, layer=None, ln=None, heads_grouped=False):
    args = [x, a, w] + (list(ln) if ln is not None else [])
    in_specs = [_const_spec(t.shape) for t in args]
    in_specs[2] = _weight_spec(w, layer)
    return pl.pallas_call(
        functools.partial(_proj_res_kernel, conv_tail=ln is not None, heads_grouped=heads_grouped),
        out_shape=jax.ShapeDtypeStruct(x.shape, F32),
        grid=(1,),
        in_specs=in_specs,
        out_specs=_const_spec(x.shape),
        compiler_params=_params("arbitrary"),
        name="proj_res_ln" if ln is not None else "proj_res",
    )(*args)


def _t5_bucket(dist):
    n = jnp.maximum(dist, 0)
    max_exact = N_BUCKETS // 2
    nf = jnp.maximum(n, 1).astype(F32)
    large = max_exact + (jnp.log(nf / max_exact) / math.log(MAX_DISTANCE / max_exact)
                         * (N_BUCKETS - max_exact)).astype(jnp.int32)
    large = jnp.minimum(large, N_BUCKETS - 1)
    return jnp.where(n < max_exact, n, large)


def _distance_bias(rel_bias):
    buckets = _t5_bucket(jnp.arange(WINDOW + 1, dtype=jnp.int32))
    onehot = (buckets[:, None] == jnp.arange(N_BUCKETS, dtype=jnp.int32)[None, :]).astype(F32)
    return jnp.dot(onehot, rel_bias.astype(F32), precision=lax.Precision.HIGHEST)


def _prompt_bias_tables(dist_bias, sinks):
    period = 3 * WINDOW
    line = jnp.concatenate([jnp.broadcast_to(dist_bias[WINDOW:], (WINDOW - 1, N_HEADS)),
                            dist_bias[::-1],
                            jnp.broadcast_to(dist_bias[:1], (WINDOW, N_HEADS))]).T
    skew = jnp.tile(line, (1, WINDOW))[:, :WINDOW * (period - 1)]
    per_head = skew.reshape(N_HEADS, WINDOW, period - 1)[:, :, WINDOW - 1:period - 1]
    per_head = per_head.reshape(N_KV_HEADS, 2, 2, WINDOW, 2 * WINDOW)
    bias = per_head.transpose(0, 2, 4, 1, 3).reshape(N_KV_HEADS, 4 * WINDOW, 2 * WINDOW)

    qi = jnp.arange(WINDOW, dtype=jnp.int32)[None, :]
    kj = jnp.arange(2 * WINDOW, dtype=jnp.int32)[:, None]
    dist = qi - kj + WINDOW
    allowed = (dist >= 0) & (dist <= WINDOW)
    first = allowed & (kj >= WINDOW)
    cap = jnp.stack([jnp.where(allowed, F32_MAX, NEG_INF), jnp.where(first, F32_MAX, NEG_INF)])
    cap = jnp.tile(cap.astype(F32), (1, 2, 2))

    sink_rows = sinks.astype(F32).reshape(N_KV_HEADS, 2, 2).transpose(0, 2, 1).reshape(2 * N_KV_HEADS, 2)
    sink_rows = jnp.repeat(sink_rows, WINDOW, axis=1)
    return bias, cap, sink_rows


def _attn_prompt_kernel(x_ref, xres_ref, g_ref, wqkv_ref, wo_ref, bias_ref, cap_ref, sink_ref,
                        o_ref, kc_ref, vc_ref,
                        q0_scr, q1_scr, klo0_scr, klo1_scr, khi0_scr, khi1_scr, vt0_scr, vt1_scr,
                        a0_scr, a1_scr, *, tq, tiles_per_seq):
    s = pl.program_id(0)
    sets = ((q0_scr, klo0_scr, khi0_scr, vt0_scr, a0_scr),
            (q1_scr, klo1_scr, khi1_scr, vt1_scr, a1_scr))

    @pl.when(s == 0)
    def _():
        for scr in sets[0] + sets[1]:
            scr[...] = jnp.zeros(scr.shape, BF16)

    low = lax.broadcasted_iota(jnp.int32, (tq, LANES), 1) < HEAD_DIM
    n_col = D_MODEL // KV_DIM

    def step(new, old):
        q_new, klo_new, khi_new, vt_new, att_done = new
        q_old, klo_old, khi_old, vt_old, att_out = old

        starts_sequence = s % tiles_per_seq == 0
        klo_new[:, 0:WINDOW, :] = jnp.where(starts_sequence, 0.0, klo_old[:, tq:tq + WINDOW, :])
        khi_new[:, 0:WINDOW, :] = jnp.where(starts_sequence, 0.0, khi_old[:, tq:tq + WINDOW, :])
        vt_new[:, :, 0:WINDOW] = jnp.where(starts_sequence, 0.0, vt_old[:, :, tq:tq + WINDOW])
        h = _rms(x_ref[...], g_ref[...]).astype(BF16)

        def project_q(c):
            cols = slice(c * KV_DIM, (c + 1) * KV_DIM)
            q_new[:, cols] = (_dot(h, wqkv_ref[:, cols]) * ATTN_SCALE).astype(BF16)

        def project_k():
            k = _dot(h, wqkv_ref[:, D_MODEL:D_MODEL + KV_DIM])
            kc_ref[...] = k[tq - WINDOW:, :]
            for c in range(KV_DIM // LANES):
                kc = k[:, c * LANES:(c + 1) * LANES]
                kr = pltpu.roll(kc, HEAD_DIM, axis=1)
                klo_new[2 * c, WINDOW:, :] = jnp.where(low, kc, 0.0).astype(BF16)
                khi_new[2 * c, WINDOW:, :] = jnp.where(low, 0.0, kr).astype(BF16)
                klo_new[2 * c + 1, WINDOW:, :] = jnp.where(low, kr, 0.0).astype(BF16)
                khi_new[2 * c + 1, WINDOW:, :] = jnp.where(low, 0.0, kc).astype(BF16)

        def project_v():
            v = _dot(h, wqkv_ref[:, D_MODEL + KV_DIM:])
            vc_ref[...] = v[tq - WINDOW:, :]
            for c in range(KV_DIM // LANES):
                vt = v[:, c * LANES:(c + 1) * LANES].T.astype(BF16)
                vt_new[2 * c, :, WINDOW:] = vt[:HEAD_DIM]
                vt_new[2 * c + 1, :, WINDOW:] = vt[HEAD_DIM:]

        def project_out(c):
            cols = slice(c * KV_DIM, (c + 1) * KV_DIM)
            o_ref[:, cols] = xres_ref[:, cols] + _dot(att_done[...], wo_ref[:, cols])

        is_first = jnp.where((s - 1) % tiles_per_seq == 0, 1, 0)

        def scores(jb, kh):
            rows = slice(jb * WINDOW, (jb + 1) * WINDOW)
            band = slice(jb * WINDOW, (jb + 2) * WINDOW)
            cap = cap_ref[is_first] if jb == 0 else cap_ref[0]
            qst = jnp.concatenate([q_old[rows, (2 * kh) * LANES:(2 * kh + 1) * LANES],
                                   q_old[rows, (2 * kh + 1) * LANES:(2 * kh + 2) * LANES]], axis=0)
            kst = jnp.concatenate([klo_old[kh, band, :], khi_old[kh, band, :]], axis=0)
            return jnp.minimum(_dot_nt(kst, qst) + bias_ref[kh], cap)

        def attend(jb, kh, sc):
            rows = slice(jb * WINDOW, (jb + 1) * WINDOW)
            vt = vt_old[kh, :, jb * WINDOW:(jb + 2) * WINDOW]
            halves = []
            for half in range(2):
                sh = sc[half * 2 * WINDOW:(half + 1) * 2 * WINDOW]
                sink = sink_ref[2 * kh + half:2 * kh + half + 1, :]
                m = jnp.maximum(jnp.max(sh, axis=0, keepdims=True), sink)
                p = jnp.exp(sh - m)
                denom = jnp.sum(p, axis=0, keepdims=True) + jnp.exp(sink - m)
                halves.append(_dot(vt, p.astype(BF16)) * (1.0 / denom))
            o = jnp.concatenate(halves, axis=0).T.astype(BF16)
            att_out[rows, (2 * kh) * LANES:(2 * kh + 1) * LANES] = o[:WINDOW]
            att_out[rows, (2 * kh + 1) * LANES:(2 * kh + 2) * LANES] = o[WINDOW:]

        pieces = [functools.partial(project_q, c) for c in range(n_col)] + [project_k, project_v]
        pieces += [functools.partial(project_out, c) for c in range(n_col)]
        steps = [(jb, kh) for jb in range(tq // WINDOW) for kh in range(N_KV_HEADS)]
        piece_at = {(n * len(steps)) // len(pieces): piece for n, piece in enumerate(pieces)}
        assert len(piece_at) == len(pieces)
        s_next = scores(*steps[0])
        for n, st in enumerate(steps):
            s_cur = s_next
            if n + 1 < len(steps):
                s_next = scores(*steps[n + 1])
            if n in piece_at:
                piece_at[n]()
            attend(*st, s_cur)

    @pl.when(s % 2 == 0)
    def _():
        step(sets[0], sets[1])

    @pl.when(s % 2 == 1)
    def _():
        step(sets[1], sets[0])


def _attn_prompt(x, g, w_qkv, w_o, layer, bias, cap, sink_rows, *, tq):
    b, seq, _ = x.shape
    tiles_per_seq = seq // tq
    n = b * tiles_per_seq
    x2 = x.reshape(b * seq, D_MODEL)
    lagged = pl.BlockSpec((tq, D_MODEL), lambda s: (jnp.maximum(s - 2, 0), 0))
    cache = pl.BlockSpec((None, WINDOW, KV_DIM),
                         lambda s: (jnp.minimum(s, n - 1) // tiles_per_seq, 0, 0))
    q_scr = pltpu.VMEM((tq, D_MODEL), BF16)
    k_scr = pltpu.VMEM((N_KV_HEADS, WINDOW + tq, LANES), BF16)
    vt_scr = pltpu.VMEM((N_KV_HEADS, HEAD_DIM, WINDOW + tq), BF16)
    out, kc, vc = pl.pallas_call(
        functools.partial(_attn_prompt_kernel, tq=tq, tiles_per_seq=tiles_per_seq),
        out_shape=(jax.ShapeDtypeStruct((b * seq, D_MODEL), F32),
                   jax.ShapeDtypeStruct((b, WINDOW, KV_DIM), F32),
                   jax.ShapeDtypeStruct((b, WINDOW, KV_DIM), F32)),
        grid=(n + 2,),
        in_specs=[pl.BlockSpec((tq, D_MODEL), lambda s: (jnp.minimum(s, n - 1), 0)), lagged,
                  _const_spec((1, D_MODEL)),
                  _layer_spec(w_qkv.shape, layer), _layer_spec(w_o.shape, layer),
                  _const_spec(bias.shape), _const_spec(cap.shape), _const_spec(sink_rows.shape)],
        out_specs=(lagged, cache, cache),
        scratch_shapes=[q_scr, q_scr, k_scr, k_scr, k_scr, k_scr, vt_scr, vt_scr, q_scr, q_scr],
        compiler_params=_params("arbitrary"),
        name="attn_prompt",
    )(x2, x2, g, w_qkv, w_o, bias, cap, sink_rows)
    return out.reshape(b, seq, D_MODEL), kc, vc


ROWS_PAD = 8 * Q_PER_KV


def _sample_bias_tables(dist_bias, sinks):
    def rows(per_head):
        t = per_head.reshape(N_KV_HEADS, Q_PER_KV, -1).transpose(1, 0, 2)
        return jnp.pad(t, ((0, 0), (0, 8 - N_KV_HEADS), (0, 0))).reshape(ROWS_PAD, -1)

    bias = rows(dist_bias[:0:-1].T)
    extra = rows(jnp.stack([dist_bias[0], sinks.astype(F32)], axis=1))
    return bias, jnp.pad(extra, ((0, 0), (0, LANES - 2)))


def _attn_sample_kernel(q_ref, kn_ref, vn_ref, kvt_ref, kc_ref, vc_ref, bias_ref, extra_ref,
                        *rest, tb, n_aliased, out_slot):
    o_ref, ko_ref, vo_ref = rest[n_aliased:]
    if out_slot is not None:
        for out_ref in (ko_ref, vo_ref):
            for slot in range(out_ref.shape[0]):
                if slot != out_slot:
                    out_ref[slot] = jnp.zeros(out_ref.shape[1:], F32)
        ko_ref, vo_ref = ko_ref.at[out_slot], vo_ref.at[out_slot]
    t = pl.program_id(0)
    sub = lax.broadcasted_iota(jnp.int32, (8, KV_DIM), 0)
    lane_head = lax.broadcasted_iota(jnp.int32, (8, KV_DIM), 1) // HEAD_DIM
    own = jnp.logical_and(sub < N_KV_HEADS, lane_head == sub)
    newest = lax.broadcasted_iota(jnp.int32, (KV_DIM, WINDOW), 1) == WINDOW - 1
    bias = bias_ref[...]
    bias_new = extra_ref[:, 0:1]
    sink = extra_ref[:, 1:2]

    def scores(bb):
        q_rows = []
        for gq in range(Q_PER_KV):
            q_g = q_ref[bb:bb + 1, gq * KV_DIM:(gq + 1) * KV_DIM] * ATTN_SCALE
            q_rows.append(jnp.where(own, jnp.broadcast_to(q_g, (8, KV_DIM)), 0.0))
        q_blk = jnp.concatenate(q_rows, axis=0)
        s_old = _dot(q_blk.astype(BF16), kc_ref[bb].astype(BF16)) + bias
        s_new = jnp.sum(q_blk * kn_ref[bb:bb + 1, :], axis=1, keepdims=True) + bias_new
        return s_old, s_new

    def attend(bb, s_old, s_new):
        m = jnp.maximum(jnp.max(s_old, axis=1, keepdims=True), jnp.maximum(s_new, sink))
        p_old = jnp.exp(s_old - m)
        p_new = jnp.exp(s_new - m)
        denom = jnp.sum(p_old, axis=1, keepdims=True) + p_new + jnp.exp(sink - m)
        o = _dot_nt(p_old.astype(BF16), vc_ref[bb].astype(BF16)) + p_new * vn_ref[bb:bb + 1, :]
        o = o / denom
        for gq in range(Q_PER_KV):
            o_g = jnp.sum(jnp.where(own, o[8 * gq:8 * gq + 8], 0.0), axis=0, keepdims=True)
            o_ref[bb:bb + 1, gq * KV_DIM:(gq + 1) * KV_DIM] = o_g

    def shift(bb):
        to_last = (WINDOW - 1) - (t * tb + bb)
        for cache_ref, row0, out_ref in ((kc_ref, 0, ko_ref), (vc_ref, KV_DIM, vo_ref)):
            moved = pltpu.roll(cache_ref[bb], WINDOW - 1, axis=1)
            col = pltpu.roll(kvt_ref[row0:row0 + KV_DIM, :], to_last, axis=1)
            out_ref[bb] = jnp.where(newest, col, moved)

    for bb in range(tb):
        shift(bb)
    s_next = scores(0)
    for bb in range(tb):
        s_cur = s_next
        if bb + 1 < tb:
            s_next = scores(bb + 1)
        attend(bb, *s_cur)


def _attn_sample_core(qkv, kv_t, k_caches, v_caches, bias, extra, layer, new_caches=None, *, tb):
    n = qkv.shape[0]
    cache = pl.BlockSpec((None, tb, KV_DIM, WINDOW), lambda t: (layer, t, 0, 0))
    kv_col = D_MODEL // KV_DIM
    args = [qkv, qkv, qkv, kv_t, k_caches, v_caches, bias, extra]
    in_specs = [pl.BlockSpec((tb, D_MODEL), lambda t: (t, 0)),
                pl.BlockSpec((tb, KV_DIM), lambda t: (t, kv_col)),
                pl.BlockSpec((tb, KV_DIM), lambda t: (t, kv_col + 1)),
                _const_spec(kv_t.shape), cache, cache,
                _const_spec(bias.shape), _const_spec(extra.shape)]
    if new_caches is None:
        aliases, out_slot = {}, layer
        cache_out = pl.BlockSpec((k_caches.shape[0], tb, KV_DIM, WINDOW), lambda t: (0, t, 0, 0))
    else:
        aliases, out_slot = {len(args): 1, len(args) + 1: 2}, None
        cache_out = cache
        args += list(new_caches)
        in_specs += [pl.BlockSpec(memory_space=pl.ANY)] * 2
    return pl.pallas_call(
        functools.partial(_attn_sample_kernel, tb=tb, n_aliased=len(aliases), out_slot=out_slot),
        out_shape=(jax.ShapeDtypeStruct((n, D_MODEL), F32),
                   jax.ShapeDtypeStruct(k_caches.shape, F32),
                   jax.ShapeDtypeStruct(v_caches.shape, F32)),
        grid=(n // tb,),
        in_specs=in_specs,
        out_specs=(pl.BlockSpec((tb, D_MODEL), lambda t: (t, 0)), cache_out, cache_out),
        input_output_aliases=aliases,
        compiler_params=_params("parallel"),
        name="attn_sample",
    )(*args)


def _sgu_kernel(x_ref, g_ref, win_ref, lg_ref, lb_ref, sp_ref, bsp_ref, wout_ref, *rest,
                tm, sample):
    def gelu_and_norm(z):
        z = 0.5 * z * (1.0 + lax.erf(z * INV_SQRT2))
        return z[:, :D_MODEL], _layer_norm(z[:, D_MODEL:], lg_ref[...], lb_ref[...])

    if sample:
        o_ref, v_ref = rest
        x = x_ref[...]
        u, v = gelu_and_norm(_dot(_rms(x, g_ref[...]).astype(BF16), win_ref[...]))
        v_ref[...] = v
        gated = (u * (v * sp_ref[...] + bsp_ref[...])).astype(BF16)
        o_ref[...] = x + _dot(gated, wout_ref[...])
        return

    o_ref, gated_scr = rest
    tp = tm // SGU_PARTS
    parts = [slice(r * tp, (r + 1) * tp) for r in range(SGU_PARTS)]
    zs = [_dot(_rms(x_ref[rows, :], g_ref[...]).astype(BF16), win_ref[...]) for rows in parts]
    for rows, z in zip(parts, zs):
        u, v = gelu_and_norm(z)
        vb = v.astype(BF16)
        for c in range(tp // CHUNK):
            chunk = slice(c * CHUNK, (c + 1) * CHUNK)
            dst = slice(rows.start + c * CHUNK, rows.start + (c + 1) * CHUNK)
            for gi in range(SGU_GROUPS):
                cols = slice(gi * LANES, (gi + 1) * LANES)
                mixed = _dot(sp_ref[gi], vb[chunk, cols]) + bsp_ref[gi]
                gated_scr[dst, cols] = (u[chunk, cols] * mixed).astype(BF16)
        o_ref[rows, :] = x_ref[rows, :] + _dot(gated_scr[rows, :], wout_ref[...])


def _sgu(x, g, w_in, ln_g, ln_b, sp, bsp, w_out, *, tm, sample):
    n = x.shape[0]
    row = pl.BlockSpec((tm, D_MODEL), lambda i: (i, 0))
    args = [x, g, w_in, ln_g, ln_b, sp, bsp, w_out]
    in_specs = [row] + [_const_spec(t.shape) for t in args[1:]]
    if sample:
        out_shape = (jax.ShapeDtypeStruct((n, D_MODEL), F32),) * 2
        out_specs = (row, row)
        scratch = []
    else:
        out_shape = jax.ShapeDtypeStruct((n, D_MODEL), F32)
        out_specs = row
        scratch = [pltpu.VMEM((tm, D_MODEL), BF16)]
    return pl.pallas_call(
        functools.partial(_sgu_kernel, tm=tm, sample=sample),
        out_shape=out_shape,
        grid=(n // tm,),
        in_specs=in_specs,
        out_specs=out_specs,
        scratch_shapes=scratch,
        compiler_params=_params("parallel"),
        name="sgu_sample" if sample else "sgu_prompt",
    )(*args)


CONV_ROWS = 64
CONV_LANES = 256
PROJ_LANES = 256


def _conv_prompt_kernel(x_ref, xprev_ref, g_ref, win_ref, wdw_ref, bdw_ref, lg_ref, lb_ref,
                        wout_ref, o_ref, tail_ref, a0_scr, a1_scr, sh_scr, c_scr,
                        *, tm, tiles_per_seq):
    s = pl.program_id(0)

    @pl.when(s == 0)
    def _():
        a0_scr[...] = jnp.zeros(a0_scr.shape, F32)
        a1_scr[...] = jnp.zeros(a1_scr.shape, F32)

    first_tap = CONV_HALO - (CONV_WIDTH - 1)
    reps = CONV_ROWS // SUBLANES

    def step(a_new, a_old):
        for r in range(1, SUBLANES):
            sh_scr[r - 1] = a_old[r:r + tm + CONV_HALO - SUBLANES, :]

        h = _rms(x_ref[...], g_ref[...]).astype(BF16)
        starts_sequence = s % tiles_per_seq == 0
        a_new[0:CONV_HALO, :] = jnp.where(starts_sequence, 0.0, a_old[tm:tm + CONV_HALO, :])

        def project(jc):
            cols = slice(jc * PROJ_LANES, (jc + 1) * PROJ_LANES)
            gate = slice(D_MODEL + jc * PROJ_LANES, D_MODEL + (jc + 1) * PROJ_LANES)
            a = _dot(h, win_ref[:, cols]) * jax.nn.sigmoid(_dot(h, win_ref[:, gate]))
            a_new[CONV_HALO:, cols] = a
            tail_ref[:, cols] = a[tm - CONV_HALO:, :]

        def convolve(rb):
            for lc in range(D_MODEL // CONV_LANES):
                cols = slice(lc * CONV_LANES, (lc + 1) * CONV_LANES)
                acc = jnp.concatenate([bdw_ref[:, cols]] * reps, axis=0)
                for kk in range(CONV_WIDTH):
                    whole, r = divmod(first_tap + kk, SUBLANES)
                    start = rb * CONV_ROWS + whole * SUBLANES
                    src = a_old if r == 0 else sh_scr.at[r - 1]
                    w = jnp.concatenate([wdw_ref[kk, :, cols]] * reps, axis=0)
                    acc = acc + src[start:start + CONV_ROWS, cols] * w
                c_scr[rb * CONV_ROWS:(rb + 1) * CONV_ROWS, cols] = acc

        n_proj = D_MODEL // PROJ_LANES
        n_conv = tm // CONV_ROWS
        for jc in range(n_proj):
            project(jc)
            for rb in range(jc * n_conv // n_proj, (jc + 1) * n_conv // n_proj):
                convolve(rb)
        c = _layer_norm(c_scr[...], lg_ref[...], lb_ref[...])
        c = (c * jax.nn.sigmoid(c)).astype(BF16)
        o_ref[...] = xprev_ref[...] + _dot(c, wout_ref[...])

    @pl.when(s % 2 == 0)
    def _():
        step(a0_scr, a1_scr)

    @pl.when(s % 2 == 1)
    def _():
        step(a1_scr, a0_scr)


def _conv_prompt(x, g, w_in, w_dw8, b_dw8, ln_g, ln_b, w_out, *, tm):
    b, seq, _ = x.shape
    tiles_per_seq = seq // tm
    n = b * tiles_per_seq
    consts = [g, w_in, w_dw8, b_dw8, ln_g, ln_b, w_out]
    a_scr = pltpu.VMEM((CONV_HALO + tm, D_MODEL), F32)
    x2 = x.reshape(b * seq, D_MODEL)
    out, tail = pl.pallas_call(
        functools.partial(_conv_prompt_kernel, tm=tm, tiles_per_seq=tiles_per_seq),
        out_shape=(jax.ShapeDtypeStruct((b * seq, D_MODEL), F32),
                   jax.ShapeDtypeStruct((b, CONV_HALO, D_MODEL), F32)),
        grid=(n + 1,),
        in_specs=[pl.BlockSpec((tm, D_MODEL), lambda s: (jnp.minimum(s, n - 1), 0)),
                  pl.BlockSpec((tm, D_MODEL), lambda s: (jnp.maximum(s - 1, 0), 0))]
                 + [_const_spec(t.shape) for t in consts],
        out_specs=(pl.BlockSpec((tm, D_MODEL), lambda s: (jnp.maximum(s - 1, 0), 0)),
                   pl.BlockSpec((None, CONV_HALO, D_MODEL),
                                lambda s: (jnp.minimum(s, n - 1) // tiles_per_seq, 0, 0))),
        scratch_shapes=[a_scr, a_scr,
                        pltpu.VMEM((SUBLANES - 1, CONV_HALO + tm - SUBLANES, D_MODEL), F32),
                        pltpu.VMEM((tm, D_MODEL), F32)],
        compiler_params=_params("arbitrary"),
        name="conv_prompt",
    )(x2, x2, *consts)
    return out.reshape(b, seq, D_MODEL), tail


def _conv_sample_kernel(a_ref, st_ref, wdw_ref, bdw_ref, c_ref, so_ref):
    n_hist = CONV_WIDTH - 1
    a = a_ref[...]
    acc = a * wdw_ref[n_hist:CONV_WIDTH, :] + bdw_ref[...]
    for k in range(n_hist):
        acc = acc + st_ref[k] * wdw_ref[k:k + 1, :]
    c_ref[...] = acc
    so_ref[0:n_hist - 1] = st_ref[1:n_hist]
    so_ref[n_hist - 1] = a


def _conv_sample_core(a, states, w_dw, b_dw, layer, *, tb):
    n = a.shape[0]
    row = pl.BlockSpec((tb, D_MODEL), lambda t: (t, 0))
    st_in = pl.BlockSpec((None, CONV_WIDTH - 1, tb, D_MODEL), lambda t: (layer, 0, t, 0))
    st_out = pl.BlockSpec((CONV_WIDTH - 1, tb, D_MODEL), lambda t: (0, t, 0))
    return pl.pallas_call(
        _conv_sample_kernel,
        out_shape=(jax.ShapeDtypeStruct((n, D_MODEL), F32),
                   jax.ShapeDtypeStruct(states.shape[1:], F32)),
        grid=(n // tb,),
        in_specs=[row, st_in, _const_spec(w_dw.shape), _const_spec(b_dw.shape)],
        out_specs=(row, st_out),
        compiler_params=_params("parallel"),
        name="conv_sample",
    )(a, states, w_dw, b_dw)


PROMPT_TILE = 512
FFN_TILE = 1024
SAMPLE_ATTN_TILE = 16
SAMPLE_CONV_TILE = 32


def kernel(x_prompt, x_sample, cache_swa_k, cache_swa_v, state_conv, rel_bias, norm_mix, norm_ffn, norm_final, attn_w_qkv, attn_w_o, attn_sinks, sgu_w_in, sgu_ln_g, sgu_ln_b, sgu_w_spatial, sgu_b_spatial, sgu_w_out, conv_w_in, conv_w_dw, conv_b_dw, conv_ln_g, conv_ln_b, conv_w_out, ffn_w_up, ffn_w_down):
    batch, seq, _ = x_prompt.shape
    dec = x_sample.shape[0]
    depth = norm_mix.shape[0]
    mixer_of_layer = tuple(i % 3 for i in range(depth))
    slot_of_layer = tuple(mixer_of_layer[:i].count(mixer_of_layer[i]) for i in range(depth))

    def row(v):
        return v.reshape(1, -1).astype(F32)

    n_attn = attn_w_qkv.shape[0]

    def caches_t(c):
        return jnp.transpose(c, (0, 1, 3, 4, 2)).reshape(n_attn, dec, KV_DIM, WINDOW)

    def caches_from_t(c):
        c = c.reshape(n_attn, dec, N_KV_HEADS, HEAD_DIM, WINDOW)
        return jnp.transpose(c, (0, 1, 4, 2, 3))

    xp = x_prompt
    xs = x_sample.reshape(dec, D_MODEL)
    dist_bias = _distance_bias(rel_bias)
    k_caches_t, v_caches_t = caches_t(cache_swa_k), caches_t(cache_swa_v)
    states = jnp.transpose(state_conv.astype(F32), (0, 2, 1, 3))
    w_qkv_all = attn_w_qkv.astype(BF16)
    w_o_all = attn_w_o.astype(BF16)

    kp, vp, sgu_v_new, convp, convs = [], [], [], [], []
    new_caches = None
    for i in range(depth):
        m, j = mixer_of_layer[i], slot_of_layer[i]
        g_mix = row(norm_mix[i])
        if m == 0:
            bias_p, cap_p, sink_rows = _prompt_bias_tables(dist_bias, attn_sinks[j])
            xp, k1, v1 = _attn_prompt(xp, g_mix, w_qkv_all, w_o_all, j, bias_p, cap_p, sink_rows,
                                      tq=PROMPT_TILE)
            kp.append(k1.reshape(batch, WINDOW, N_KV_HEADS, HEAD_DIM))
            vp.append(v1.reshape(batch, WINDOW, N_KV_HEADS, HEAD_DIM))

            bias_s, extra_s = _sample_bias_tables(dist_bias, attn_sinks[j])
            qkv_s, kv_t = _norm_proj(xs, g_mix, w_qkv_all, j, kv_t=True)
            o_s, *new_caches = _attn_sample_core(qkv_s, kv_t, k_caches_t, v_caches_t, bias_s,
                                                 extra_s, j, new_caches, tb=SAMPLE_ATTN_TILE)
            xs = _proj_res(xs, o_s, w_o_all, j, heads_grouped=True)
        elif m == 1:
            w_in = sgu_w_in[j].astype(BF16)
            w_out = sgu_w_out[j].astype(BF16)
            ln_g, ln_b = row(sgu_ln_g[j]), row(sgu_ln_b[j])
            sp = jnp.tril(sgu_w_spatial[j]).astype(BF16)
            bsp = jnp.broadcast_to(sgu_b_spatial[j].astype(F32)[:, :, None],
                                   (SGU_GROUPS, CHUNK, LANES))
            xp = _sgu(xp.reshape(batch * seq, D_MODEL), g_mix, w_in, ln_g, ln_b, sp, bsp, w_out,
                      tm=PROMPT_TILE, sample=False).reshape(batch, seq, D_MODEL)
            sp0 = row(jnp.repeat(sgu_w_spatial[j][:, 0, 0], LANES))
            bsp0 = row(jnp.repeat(sgu_b_spatial[j][:, 0], LANES))
            xs, v_rows = _sgu(xs, g_mix, w_in, ln_g, ln_b, sp0, bsp0, w_out, tm=dec, sample=True)
            sgu_v_new.append(v_rows.reshape(dec, 1, D_MODEL))
        else:
            w_in = conv_w_in[j].astype(BF16)
            w_out = conv_w_out[j].astype(BF16)
            w_dw = conv_w_dw[j].astype(F32)
            b_dw = row(conv_b_dw[j])
            ln_g, ln_b = row(conv_ln_g[j]), row(conv_ln_b[j])
            w_dw8 = jnp.broadcast_to(w_dw[:, None, :], (CONV_WIDTH, SUBLANES, D_MODEL))
            b_dw8 = jnp.broadcast_to(b_dw, (SUBLANES, D_MODEL))
            xp, tail = _conv_prompt(xp, g_mix, w_in, w_dw8, b_dw8, ln_g, ln_b, w_out,
                                    tm=PROMPT_TILE)
            convp.append(tail[:, CONV_HALO - (CONV_WIDTH - 1):, :])
            a_s = _norm_proj(xs, g_mix, w_in, glu=True)
            c_s, st2 = _conv_sample_core(a_s, states, w_dw, b_dw, j, tb=SAMPLE_CONV_TILE)
            xs = _proj_res(xs, c_s, w_out, ln=(ln_g, ln_b))
            convs.append(jnp.transpose(st2, (1, 0, 2)))

        g_fin = row(norm_final) if i == depth - 1 else None
        xp, xs = _ffn(xp.reshape(batch * seq, D_MODEL), xs, row(norm_ffn[i]), ffn_w_up, ffn_w_down,
                      i, g_fin, tm=FFN_TILE)
        xp = xp.reshape(batch, seq, D_MODEL)

    y_prompt = xp
    y_sample = xs.reshape(dec, 1, D_MODEL)
    k_new, v_new = (caches_from_t(c) for c in new_caches)
    return (y_prompt, y_sample, jnp.stack(kp), jnp.stack(vp), k_new, v_new,
            jnp.stack(sgu_v_new), jnp.stack(convp), jnp.stack(convs))
```

```python
import functools
import math

import jax
import jax.numpy as jnp
from jax import lax
from jax.experimental import pallas as pl
from jax.experimental.pallas import tpu as pltpu

D_MODEL = 1024
HEAD_DIM = 64
N_HEADS = 16
N_KV_HEADS = 4
Q_PER_KV = 4
KV_DIM = N_KV_HEADS * HEAD_DIM
QKV_DIM = D_MODEL + 2 * KV_DIM
WINDOW = 128
ATTN_SCALE = HEAD_DIM ** -0.5
N_BUCKETS = 32
MAX_DISTANCE = 128
CHUNK = 128
SGU_GROUPS = 8
SGU_PARTS = 2
CONV_WIDTH = 31
CONV_HALO = 32
D_FF = 4 * D_MODEL
EPS = 1e-6
NEG_INF = -1e30
F32_MAX = float(jnp.finfo(jnp.float32).max)
INV_SQRT2 = 1.0 / math.sqrt(2.0)

LANES = 128
SUBLANES = 8
VMEM_LIMIT = 56 * 1024 * 1024

F32 = jnp.float32
BF16 = jnp.bfloat16


def _const_spec(shape):
    n = len(shape)
    return pl.BlockSpec(shape, lambda *_: (0,) * n, pipeline_mode=pl.Buffered(1))


def _params(*sem):
    return pltpu.CompilerParams(dimension_semantics=sem, vmem_limit_bytes=VMEM_LIMIT)


def _rms(x, g):
    return x * lax.rsqrt(jnp.mean(x * x, axis=-1, keepdims=True) + EPS) * g


def _layer_norm(x, g, b):
    mu = jnp.mean(x, axis=-1, keepdims=True)
    xc = x - mu
    var = jnp.mean(xc * xc, axis=-1, keepdims=True)
    return xc * lax.rsqrt(var + EPS) * g + b


def _dot(a, b):
    return jnp.dot(a, b, preferred_element_type=F32)


def _dot_nt(a, b):
    return lax.dot_general(a, b, (((1,), (1,)), ((), ())), preferred_element_type=F32)


FFN_CHUNK = 512
N_FFN_CHUNKS = D_FF // FFN_CHUNK


def _layer_spec(shape, layer):
    zeros = (0,) * (len(shape) - 1)
    return pl.BlockSpec((None,) + tuple(shape[1:]), lambda *_: (layer,) + zeros,
                        pipeline_mode=pl.Buffered(1))


def _ffn_kernel(xp_ref, xs_ref, g_ref, wu_ref, wd_ref, *rest, final, n_prompt_tiles):
    *rest, wu_scr, wd_scr, h_scr = rest
    if final:
        gf_ref, op_ref, os_ref = rest
    else:
        op_ref, os_ref = rest

    def chunk(h, c):
        u = _dot(h, wu_scr[c])
        return _dot(jnp.square(jnp.maximum(u, 0.0)).astype(BF16), wd_scr[c])

    def finish(y, o_ref):
        o_ref[...] = _rms(y, gf_ref[...]) if final else y

    def block(x_ref, o_ref):
        x = x_ref[...]
        h = _rms(x, g_ref[...]).astype(BF16)
        y = x
        for c in range(N_FFN_CHUNKS):
            y = y + chunk(h, c)
        finish(y, o_ref)

    i = pl.program_id(0)
    tile = i - (N_FFN_CHUNKS - 1)

    @pl.when(i < N_FFN_CHUNKS)
    def _():
        wu_scr[i] = wu_ref[...].astype(BF16)
        wd_scr[i] = wd_ref[...].astype(BF16)

        @pl.when(i == 0)
        def _():
            x = xp_ref[...]
            h_scr[...] = _rms(x, g_ref[...]).astype(BF16)
            op_ref[...] = x

        op_ref[...] += chunk(h_scr[...], i)

        if final:
            @pl.when(i == N_FFN_CHUNKS - 1)
            def _():
                finish(op_ref[...], op_ref)

    @pl.when(jnp.logical_and(tile >= 1, tile < n_prompt_tiles))
    def _():
        block(xp_ref, op_ref)

    @pl.when(tile == n_prompt_tiles)
    def _():
        block(xs_ref, os_ref)


def _ffn(xp, xs, g, w_up, w_down, layer, g_final=None, *, tm):
    n = xp.shape[0]
    nt = n // tm
    last_chunk = N_FFN_CHUNKS - 1
    row = pl.BlockSpec((tm, D_MODEL), lambda i: (jnp.clip(i - last_chunk, 0, nt - 1), 0))
    in_specs = [row, _const_spec(xs.shape), _const_spec((1, D_MODEL)),
                pl.BlockSpec((None, D_MODEL, FFN_CHUNK),
                             lambda i: (layer, 0, jnp.minimum(i, last_chunk))),
                pl.BlockSpec((None, FFN_CHUNK, D_MODEL),
                             lambda i: (layer, jnp.minimum(i, last_chunk), 0))]
    args = [xp, xs, g, w_up, w_down]
    if g_final is not None:
        in_specs.append(_const_spec((1, D_MODEL)))
        args.append(g_final)
    return pl.pallas_call(
        functools.partial(_ffn_kernel, final=g_final is not None, n_prompt_tiles=nt),
        out_shape=(jax.ShapeDtypeStruct((n, D_MODEL), F32),
                   jax.ShapeDtypeStruct(xs.shape, F32)),
        grid=(last_chunk + nt + 1,),
        in_specs=in_specs,
        out_specs=(row, pl.BlockSpec(xs.shape, lambda i: (0, 0))),
        scratch_shapes=[pltpu.VMEM((N_FFN_CHUNKS, D_MODEL, FFN_CHUNK), BF16),
                        pltpu.VMEM((N_FFN_CHUNKS, FFN_CHUNK, D_MODEL), BF16),
                        pltpu.VMEM((tm, D_MODEL), BF16)],
        compiler_params=_params("arbitrary"),
        name="ffn_final" if g_final is not None else "ffn",
    )(*args)


def _regroup_heads(t, group_major):
    blocks = range(N_HEADS)
    if group_major:
        order = [N_KV_HEADS * (blk % N_KV_HEADS) + blk // N_KV_HEADS for blk in blocks]
    else:
        order = [Q_PER_KV * (blk % Q_PER_KV) + blk // Q_PER_KV for blk in blocks]
    return jnp.concatenate([t[:, h * HEAD_DIM:(h + 1) * HEAD_DIM] for h in order], axis=1)


def _norm_proj_kernel(x_ref, g_ref, w_ref, o_ref, *rest, glu, kv_t):
    h = _rms(x_ref[...], g_ref[...]).astype(BF16)
    y = _dot(h, w_ref[...])
    if glu:
        half = y.shape[1] // 2
        y = y[:, :half] * jax.nn.sigmoid(y[:, half:])
    if kv_t:
        (t_ref,) = rest
        o_ref[:, :D_MODEL] = _regroup_heads(y[:, :D_MODEL], group_major=True)
        o_ref[:, D_MODEL:] = y[:, D_MODEL:]
        t_ref[...] = y[:, D_MODEL:].T
    else:
        o_ref[...] = y


def _weight_spec(w, layer):
    return _const_spec(w.shape) if layer is None else _layer_spec(w.shape, layer)


def _norm_proj(x, g, w, layer=None, *, glu=False, kv_t=False):
    n = x.shape[0]
    n_out = w.shape[-1] // 2 if glu else w.shape[-1]
    out_shape = [jax.ShapeDtypeStruct((n, n_out), F32)]
    if kv_t:
        out_shape.append(jax.ShapeDtypeStruct((n_out - D_MODEL, n), F32))
    out = pl.pallas_call(
        functools.partial(_norm_proj_kernel, glu=glu, kv_t=kv_t),
        out_shape=tuple(out_shape),
        grid=(1,),
        in_specs=[_const_spec(x.shape), _const_spec(g.shape), _weight_spec(w, layer)],
        out_specs=tuple(_const_spec(o.shape) for o in out_shape),
        compiler_params=_params("arbitrary"),
        name="norm_proj_glu" if glu else "norm_proj",
    )(x, g, w)
    return out if kv_t else out[0]


def _proj_res_kernel(x_ref, a_ref, w_ref, *rest, conv_tail, heads_grouped):
    a = a_ref[...]
    if conv_tail:
        lg_ref, lb_ref, o_ref = rest
        a = _layer_norm(a, lg_ref[...], lb_ref[...])
        a = a * jax.nn.sigmoid(a)
    else:
        (o_ref,) = rest
    if heads_grouped:
        a = _regroup_heads(a, group_major=False)
    o_ref[...] = x_ref[...] + _dot(a.astype(BF16), w_ref[...])


def _proj_res(x, a, w, layer=None, ln=None, heads_grouped=False):
    args = [x, a, w] + (list(ln) if ln is not None else [])
    in_specs = [_const_spec(t.shape) for t in args]
    in_specs[2] = _weight_spec(w, layer)
    return pl.pallas_call(
        functools.partial(_proj_res_kernel, conv_tail=ln is not None, heads_grouped=heads_grouped),
        out_shape=jax.ShapeDtypeStruct(x.shape, F32),
        grid=(1,),
        in_specs=in_specs,
        out_specs=_const_spec(x.shape),
        compiler_params=_params("arbitrary"),
        name="proj_res_ln" if ln is not None else "proj_res",
    )(*args)


def _t5_bucket(dist):
    n = jnp.maximum(dist, 0)
    max_exact = N_BUCKETS // 2
    nf = jnp.maximum(n, 1).astype(F32)
    large = max_exact + (jnp.log(nf / max_exact) / math.log(MAX_DISTANCE / max_exact)
                         * (N_BUCKETS - max_exact)).astype(jnp.int32)
    large = jnp.minimum(large, N_BUCKETS - 1)
    return jnp.where(n < max_exact, n, large)


def _distance_bias(rel_bias):
    buckets = _t5_bucket(jnp.arange(WINDOW + 1, dtype=jnp.int32))
    onehot = (buckets[:, None] == jnp.arange(N_BUCKETS, dtype=jnp.int32)[None, :]).astype(F32)
    return jnp.dot(onehot, rel_bias.astype(F32), precision=lax.Precision.HIGHEST)


def _prompt_bias_tables(dist_bias, sinks):
    period = 3 * WINDOW
    line = jnp.concatenate([jnp.broadcast_to(dist_bias[WINDOW:], (WINDOW - 1, N_HEADS)),
                            dist_bias[::-1],
                            jnp.broadcast_to(dist_bias[:1], (WINDOW, N_HEADS))]).T
    skew = jnp.tile(line, (1, WINDOW))[:, :WINDOW * (period - 1)]
    per_head = skew.reshape(N_HEADS, WINDOW, period - 1)[:, :, WINDOW - 1:period - 1]
    per_head = per_head.reshape(N_KV_HEADS, 2, 2, WINDOW, 2 * WINDOW)
    bias = per_head.transpose(0, 2, 4, 1, 3).reshape(N_KV_HEADS, 4 * WINDOW, 2 * WINDOW)

    qi = jnp.arange(WINDOW, dtype=jnp.int32)[None, :]
    kj = jnp.arange(2 * WINDOW, dtype=jnp.int32)[:, None]
    dist = qi - kj + WINDOW
    allowed = (dist >= 0) & (dist <= WINDOW)
    first = allowed & (kj >= WINDOW)
    cap = jnp.stack([jnp.where(allowed, F32_MAX, NEG_INF), jnp.where(first, F32_MAX, NEG_INF)])
    cap = jnp.tile(cap.astype(F32), (1, 2, 2))

    sink_rows = sinks.astype(F32).reshape(N_KV_HEADS, 2, 2).transpose(0, 2, 1).reshape(2 * N_KV_HEADS, 2)
    sink_rows = jnp.repeat(sink_rows, WINDOW, axis=1)
    return bias, cap, sink_rows


def _attn_prompt_kernel(x_ref, xres_ref, g_ref, wqkv_ref, wo_ref, bias_ref, cap_ref, sink_ref,
                        o_ref, kc_ref, vc_ref,
                        q0_scr, q1_scr, klo0_scr, klo1_scr, khi0_scr, khi1_scr, vt0_scr, vt1_scr,
                        a0_scr, a1_scr, *, tq, tiles_per_seq):
    s = pl.program_id(0)
    sets = ((q0_scr, klo0_scr, khi0_scr, vt0_scr, a0_scr),
            (q1_scr, klo1_scr, khi1_scr, vt1_scr, a1_scr))

    @pl.when(s == 0)
    def _():
        for scr in sets[0] + sets[1]:
            scr[...] = jnp.zeros(scr.shape, BF16)

    low = lax.broadcasted_iota(jnp.int32, (tq, LANES), 1) < HEAD_DIM
    n_col = D_MODEL // KV_DIM

    def step(new, old):
        q_new, klo_new, khi_new, vt_new, att_done = new
        q_old, klo_old, khi_old, vt_old, att_out = old

        starts_sequence = s % tiles_per_seq == 0
        klo_new[:, 0:WINDOW, :] = jnp.where(starts_sequence, 0.0, klo_old[:, tq:tq + WINDOW, :])
        khi_new[:, 0:WINDOW, :] = jnp.where(starts_sequence, 0.0, khi_old[:, tq:tq + WINDOW, :])
        vt_new[:, :, 0:WINDOW] = jnp.where(starts_sequence, 0.0, vt_old[:, :, tq:tq + WINDOW])
        h = _rms(x_ref[...], g_ref[...]).astype(BF16)

        def project_q(c):
            cols = slice(c * KV_DIM, (c + 1) * KV_DIM)
            q_new[:, cols] = (_dot(h, wqkv_ref[:, cols]) * ATTN_SCALE).astype(BF16)

        def project_k():
            k = _dot(h, wqkv_ref[:, D_MODEL:D_MODEL + KV_DIM])
            kc_ref[...] = k[tq - WINDOW:, :]
            for c in range(KV_DIM // LANES):
                kc = k[:, c * LANES:(c + 1) * LANES]
                kr = pltpu.roll(kc, HEAD_DIM, axis=1)
                klo_new[2 * c, WINDOW:, :] = jnp.where(low, kc, 0.0).astype(BF16)
                khi_new[2 * c, WINDOW:, :] = jnp.where(low, 0.0, kr).astype(BF16)
                klo_new[2 * c + 1, WINDOW:, :] = jnp.where(low, kr, 0.0).astype(BF16)
                khi_new[2 * c + 1, WINDOW:, :] = jnp.where(low, 0.0, kc).astype(BF16)

        def project_v():
            v = _dot(h, wqkv_ref[:, D_MODEL + KV_DIM:])
            vc_ref[...] = v[tq - WINDOW:, :]
            for c in range(KV_DIM // LANES):
                vt = v[:, c * LANES:(c + 1) * LANES].T.astype(BF16)
                vt_new[2 * c, :, WINDOW:] = vt[:HEAD_DIM]
                vt_new[2 * c + 1, :, WINDOW:] = vt[HEAD_DIM:]

        def project_out(c):
            cols = slice(c * KV_DIM, (c + 1) * KV_DIM)
            o_ref[:, cols] = xres_ref[:, cols] + _dot(att_done[...], wo_ref[:, cols])

        is_first = jnp.where((s - 1) % tiles_per_seq == 0, 1, 0)

        def scores(jb, kh):
            rows = slice(jb * WINDOW, (jb + 1) * WINDOW)
            band = slice(jb * WINDOW, (jb + 2) * WINDOW)
            cap = cap_ref[is_first] if jb == 0 else cap_ref[0]
            qst = jnp.concatenate([q_old[rows, (2 * kh) * LANES:(2 * kh + 1) * LANES],
                                   q_old[rows, (2 * kh + 1) * LANES:(2 * kh + 2) * LANES]], axis=0)
            kst = jnp.concatenate([klo_old[kh, band, :], khi_old[kh, band, :]], axis=0)
            return jnp.minimum(_dot_nt(kst, qst) + bias_ref[kh], cap)

        def attend(jb, kh, sc):
            rows = slice(jb * WINDOW, (jb + 1) * WINDOW)
            vt = vt_old[kh, :, jb * WINDOW:(jb + 2) * WINDOW]
            halves = []
            for half in range(2):
                sh = sc[half * 2 * WINDOW:(half + 1) * 2 * WINDOW]
                sink = sink_ref[2 * kh + half:2 * kh + half + 1, :]
                m = jnp.maximum(jnp.max(sh, axis=0, keepdims=True), sink)
                p = jnp.exp(sh - m)
                denom = jnp.sum(p, axis=0, keepdims=True) + jnp.exp(sink - m)
                halves.append(_dot(vt, p.astype(BF16)) * (1.0 / denom))
            o = jnp.concatenate(halves, axis=0).T.astype(BF16)
            att_out[rows, (2 * kh) * LANES:(2 * kh + 1) * LANES] = o[:WINDOW]
            att_out[rows, (2 * kh + 1) * LANES:(2 * kh + 2) * LANES] = o[WINDOW:]

        pieces = [functools.partial(project_q, c) for c in range(n_col)] + [project_k, project_v]
        pieces += [functools.partial(project_out, c) for c in range(n_col)]
        steps = [(jb, kh) for jb in range(tq // WINDOW) for kh in range(N_KV_HEADS)]
        piece_at = {(n * len(steps)) // len(pieces): piece for n, piece in enumerate(pieces)}
        assert len(piece_at) == len(pieces)
        s_next = scores(*steps[0])
        for n, st in enumerate(steps):
            s_cur = s_next
            if n + 1 < len(steps):
                s_next = scores(*steps[n + 1])
            if n in piece_at:
                piece_at[n]()
            attend(*st, s_cur)

    @pl.when(s % 2 == 0)
    def _():
        step(sets[0], sets[1])

    @pl.when(s % 2 == 1)
    def _():
        step(sets[1], sets[0])


def _attn_prompt(x, g, w_qkv, w_o, layer, bias, cap, sink_rows, *, tq):
    b, seq, _ = x.shape
    tiles_per_seq = seq // tq
    n = b * tiles_per_seq
    x2 = x.reshape(b * seq, D_MODEL)
    lagged = pl.BlockSpec((tq, D_MODEL), lambda s: (jnp.maximum(s - 2, 0), 0))
    cache = pl.BlockSpec((None, WINDOW, KV_DIM),
                         lambda s: (jnp.minimum(s, n - 1) // tiles_per_seq, 0, 0))
    q_scr = pltpu.VMEM((tq, D_MODEL), BF16)
    k_scr = pltpu.VMEM((N_KV_HEADS, WINDOW + tq, LANES), BF16)
    vt_scr = pltpu.VMEM((N_KV_HEADS, HEAD_DIM, WINDOW + tq), BF16)
    out, kc, vc = pl.pallas_call(
        functools.partial(_attn_prompt_kernel, tq=tq, tiles_per_seq=tiles_per_seq),
        out_shape=(jax.ShapeDtypeStruct((b * seq, D_MODEL), F32),
                   jax.ShapeDtypeStruct((b, WINDOW, KV_DIM), F32),
                   jax.ShapeDtypeStruct((b, WINDOW, KV_DIM), F32)),
        grid=(n + 2,),
        in_specs=[pl.BlockSpec((tq, D_MODEL), lambda s: (jnp.minimum(s, n - 1), 0)), lagged,
                  _const_spec((1, D_MODEL)),
                  _layer_spec(w_qkv.shape, layer), _layer_spec(w_o.shape, layer),
                  _const_spec(bias.shape), _const_spec(cap.shape), _const_spec(sink_rows.shape)],
        out_specs=(lagged, cache, cache),
        scratch_shapes=[q_scr, q_scr, k_scr, k_scr, k_scr, k_scr, vt_scr, vt_scr, q_scr, q_scr],
        compiler_params=_params("arbitrary"),
        name="attn_prompt",
    )(x2, x2, g, w_qkv, w_o, bias, cap, sink_rows)
    return out.reshape(b, seq, D_MODEL), kc, vc


ROWS_PAD = 8 * Q_PER_KV


def _sample_bias_tables(dist_bias, sinks):
    def rows(per_head):
        t = per_head.reshape(N_KV_HEADS, Q_PER_KV, -1).transpose(1, 0, 2)
        return jnp.pad(t, ((0, 0), (0, 8 - N_KV_HEADS), (0, 0))).reshape(ROWS_PAD, -1)

    bias = rows(dist_bias[:0:-1].T)
    extra = rows(jnp.stack([dist_bias[0], sinks.astype(F32)], axis=1))
    return bias, jnp.pad(extra, ((0, 0), (0, LANES - 2)))


def _attn_sample_kernel(q_ref, kn_ref, vn_ref, kvt_ref, kc_ref, vc_ref, bias_ref, extra_ref,
                        *rest, tb, n_aliased, out_slot):
    o_ref, ko_ref, vo_ref = rest[n_aliased:]
    if out_slot is not None:
        for out_ref in (ko_ref, vo_ref):
            for slot in range(out_ref.shape[0]):
                if slot != out_slot:
                    out_ref[slot] = jnp.zeros(out_ref.shape[1:], F32)
        ko_ref, vo_ref = ko_ref.at[out_slot], vo_ref.at[out_slot]
    t = pl.program_id(0)
    sub = lax.broadcasted_iota(jnp.int32, (8, KV_DIM), 0)
    lane_head = lax.broadcasted_iota(jnp.int32, (8, KV_DIM), 1) // HEAD_DIM
    own = jnp.logical_and(sub < N_KV_HEADS, lane_head == sub)
    newest = lax.broadcasted_iota(jnp.int32, (KV_DIM, WINDOW), 1) == WINDOW - 1
    bias = bias_ref[...]
    bias_new = extra_ref[:, 0:1]
    sink = extra_ref[:, 1:2]

    def scores(bb):
        q_rows = []
        for gq in range(Q_PER_KV):
            q_g = q_ref[bb:bb + 1, gq * KV_DIM:(gq + 1) * KV_DIM] * ATTN_SCALE
            q_rows.append(jnp.where(own, jnp.broadcast_to(q_g, (8, KV_DIM)), 0.0))
        q_blk = jnp.concatenate(q_rows, axis=0)
        s_old = _dot(q_blk.astype(BF16), kc_ref[bb].astype(BF16)) + bias
        s_new = jnp.sum(q_blk * kn_ref[bb:bb + 1, :], axis=1, keepdims=True) + bias_new
        return s_old, s_new

    def attend(bb, s_old, s_new):
        m = jnp.maximum(jnp.max(s_old, axis=1, keepdims=True), jnp.maximum(s_new, sink))
        p_old = jnp.exp(s_old - m)
        p_new = jnp.exp(s_new - m)
        denom = jnp.sum(p_old, axis=1, keepdims=True) + p_new + jnp.exp(sink - m)
        o = _dot_nt(p_old.astype(BF16), vc_ref[bb].astype(BF16)) + p_new * vn_ref[bb:bb + 1, :]
        o = o / denom
        for gq in range(Q_PER_KV):
            o_g = jnp.sum(jnp.where(own, o[8 * gq:8 * gq + 8], 0.0), axis=0, keepdims=True)
            o_ref[bb:bb + 1, gq * KV_DIM:(gq + 1) * KV_DIM] = o_g

    def shift(bb):
        to_last = (WINDOW - 1) - (t * tb + bb)
        for cache_ref, row0, out_ref in ((kc_ref, 0, ko_ref), (vc_ref, KV_DIM, vo_ref)):
            moved = pltpu.roll(cache_ref[bb], WINDOW - 1, axis=1)
            col = pltpu.roll(kvt_ref[row0:row0 + KV_DIM, :], to_last, axis=1)
            out_ref[bb] = jnp.where(newest, col, moved)

    for bb in range(tb):
        shift(bb)
    s_next = scores(0)
    for bb in range(tb):
        s_cur = s_next
        if bb + 1 < tb:
            s_next = scores(bb + 1)
        attend(bb, *s_cur)


def _attn_sample_core(qkv, kv_t, k_caches, v_caches, bias, extra, layer, new_caches=None, *, tb):
    n = qkv.shape[0]
    cache = pl.BlockSpec((None, tb, KV_DIM, WINDOW), lambda t: (layer, t, 0, 0))
    kv_col = D_MODEL // KV_DIM
    args = [qkv, qkv, qkv, kv_t, k_caches, v_caches, bias, extra]
    in_specs = [pl.BlockSpec((tb, D_MODEL), lambda t: (t, 0)),
                pl.BlockSpec((tb, KV_DIM), lambda t: (t, kv_col)),
                pl.BlockSpec((tb, KV_DIM), lambda t: (t, kv_col + 1)),
                _const_spec(kv_t.shape), cache, cache,
                _const_spec(bias.shape), _const_spec(extra.shape)]
    if new_caches is None:
        aliases, out_slot = {}, layer
        cache_out = pl.BlockSpec((k_caches.shape[0], tb, KV_DIM, WINDOW), lambda t: (0, t, 0, 0))
    else:
        aliases, out_slot = {len(args): 1, len(args) + 1: 2}, None
        cache_out = cache
        args += list(new_caches)
        in_specs += [pl.BlockSpec(memory_space=pl.ANY)] * 2
    return pl.pallas_call(
        functools.partial(_attn_sample_kernel, tb=tb, n_aliased=len(aliases), out_slot=out_slot),
        out_shape=(jax.ShapeDtypeStruct((n, D_MODEL), F32),
                   jax.ShapeDtypeStruct(k_caches.shape, F32),
                   jax.ShapeDtypeStruct(v_caches.shape, F32)),
        grid=(n // tb,),
        in_specs=in_specs,
        out_specs=(pl.BlockSpec((tb, D_MODEL), lambda t: (t, 0)), cache_out, cache_out),
        input_output_aliases=aliases,
        compiler_params=_params("parallel"),
        name="attn_sample",
    )(*args)


def _sgu_kernel(x_ref, g_ref, win_ref, lg_ref, lb_ref, sp_ref, bsp_ref, wout_ref, *rest,
                tm, sample):
    def gelu_and_norm(z):
        z = 0.5 * z * (1.0 + lax.erf(z * INV_SQRT2))
        return z[:, :D_MODEL], _layer_norm(z[:, D_MODEL:], lg_ref[...], lb_ref[...])

    if sample:
        o_ref, v_ref = rest
        x = x_ref[...]
        u, v = gelu_and_norm(_dot(_rms(x, g_ref[...]).astype(BF16), win_ref[...]))
        v_ref[...] = v
        gated = (u * (v * sp_ref[...] + bsp_ref[...])).astype(BF16)
        o_ref[...] = x + _dot(gated, wout_ref[...])
        return

    o_ref, gated_scr = rest
    tp = tm // SGU_PARTS
    parts = [slice(r * tp, (r + 1) * tp) for r in range(SGU_PARTS)]
    zs = [_dot(_rms(x_ref[rows, :], g_ref[...]).astype(BF16), win_ref[...]) for rows in parts]
    for rows, z in zip(parts, zs):
        u, v = gelu_and_norm(z)
        vb = v.astype(BF16)
        for c in range(tp // CHUNK):
            chunk = slice(c * CHUNK, (c + 1) * CHUNK)
            dst = slice(rows.start + c * CHUNK, rows.start + (c + 1) * CHUNK)
            for gi in range(SGU_GROUPS):
                cols = slice(gi * LANES, (gi + 1) * LANES)
                mixed = _dot(sp_ref[gi], vb[chunk, cols]) + bsp_ref[gi]
                gated_scr[dst, cols] = (u[chunk, cols] * mixed).astype(BF16)
        o_ref[rows, :] = x_ref[rows, :] + _dot(gated_scr[rows, :], wout_ref[...])


def _sgu(x, g, w_in, ln_g, ln_b, sp, bsp, w_out, *, tm, sample):
    n = x.shape[0]
    row = pl.BlockSpec((tm, D_MODEL), lambda i: (i, 0))
    args = [x, g, w_in, ln_g, ln_b, sp, bsp, w_out]
    in_specs = [row] + [_const_spec(t.shape) for t in args[1:]]
    if sample:
        out_shape = (jax.ShapeDtypeStruct((n, D_MODEL), F32),) * 2
        out_specs = (row, row)
        scratch = []
    else:
        out_shape = jax.ShapeDtypeStruct((n, D_MODEL), F32)
        out_specs = row
        scratch = [pltpu.VMEM((tm, D_MODEL), BF16)]
    return pl.pallas_call(
        functools.partial(_sgu_kernel, tm=tm, sample=sample),
        out_shape=out_shape,
        grid=(n // tm,),
        in_specs=in_specs,
        out_specs=out_specs,
        scratch_shapes=scratch,
        compiler_params=_params("parallel"),
        name="sgu_sample" if sample else "sgu_prompt",
    )(*args)


CONV_ROWS = 64
CONV_LANES = 256
PROJ_LANES = 256


def _conv_prompt_kernel(x_ref, xprev_ref, g_ref, win_ref, wdw_ref, bdw_ref, lg_ref, lb_ref,
                        wout_ref, o_ref, tail_ref, a0_scr, a1_scr, sh_scr, c_scr,
                        *, tm, tiles_per_seq):
    s = pl.program_id(0)

    @pl.when(s == 0)
    def _():
        a0_scr[...] = jnp.zeros(a0_scr.shape, F32)
        a1_scr[...] = jnp.zeros(a1_scr.shape, F32)

    first_tap = CONV_HALO - (CONV_WIDTH - 1)
    reps = CONV_ROWS // SUBLANES

    def step(a_new, a_old):
        for r in range(1, SUBLANES):
            sh_scr[r - 1] = a_old[r:r + tm + CONV_HALO - SUBLANES, :]

        h = _rms(x_ref[...], g_ref[...]).astype(BF16)
        starts_sequence = s % tiles_per_seq == 0
        a_new[0:CONV_HALO, :] = jnp.where(starts_sequence, 0.0, a_old[tm:tm + CONV_HALO, :])

        def project(jc):
            cols = slice(jc * PROJ_LANES, (jc + 1) * PROJ_LANES)
            gate = slice(D_MODEL + jc * PROJ_LANES, D_MODEL + (jc + 1) * PROJ_LANES)
            a = _dot(h, win_ref[:, cols]) * jax.nn.sigmoid(_dot(h, win_ref[:, gate]))
            a_new[CONV_HALO:, cols] = a
            tail_ref[:, cols] = a[tm - CONV_HALO:, :]

        def convolve(rb):
            for lc in range(D_MODEL // CONV_LANES):
                cols = slice(lc * CONV_LANES, (lc + 1) * CONV_LANES)
                acc = jnp.concatenate([bdw_ref[:, cols]] * reps, axis=0)
                for kk in range(CONV_WIDTH):
                    whole, r = divmod(first_tap + kk, SUBLANES)
                    start = rb * CONV_ROWS + whole * SUBLANES
                    src = a_old if r == 0 else sh_scr.at[r - 1]
                    w = jnp.concatenate([wdw_ref[kk, :, cols]] * reps, axis=0)
                    acc = acc + src[start:start + CONV_ROWS, cols] * w
                c_scr[rb * CONV_ROWS:(rb + 1) * CONV_ROWS, cols] = acc

        n_proj = D_MODEL // PROJ_LANES
        n_conv = tm // CONV_ROWS
        for jc in range(n_proj):
            project(jc)
            for rb in range(jc * n_conv // n_proj, (jc + 1) * n_conv // n_proj):
                convolve(rb)
        c = _layer_norm(c_scr[...], lg_ref[...], lb_ref[...])
        c = (c * jax.nn.sigmoid(c)).astype(BF16)
        o_ref[...] = xprev_ref[...] + _dot(c, wout_ref[...])

    @pl.when(s % 2 == 0)
    def _():
        step(a0_scr, a1_scr)

    @pl.when(s % 2 == 1)
    def _():
        step(a1_scr, a0_scr)


def _conv_prompt(x, g, w_in, w_dw8, b_dw8, ln_g, ln_b, w_out, *, tm):
    b, seq, _ = x.shape
    tiles_per_seq = seq // tm
    n = b * tiles_per_seq
    consts = [g, w_in, w_dw8, b_dw8, ln_g, ln_b, w_out]
    a_scr = pltpu.VMEM((CONV_HALO + tm, D_MODEL), F32)
    x2 = x.reshape(b * seq, D_MODEL)
    out, tail = pl.pallas_call(
        functools.partial(_conv_prompt_kernel, tm=tm, tiles_per_seq=tiles_per_seq),
        out_shape=(jax.ShapeDtypeStruct((b * seq, D_MODEL), F32),
                   jax.ShapeDtypeStruct((b, CONV_HALO, D_MODEL), F32)),
        grid=(n + 1,),
        in_specs=[pl.BlockSpec((tm, D_MODEL), lambda s: (jnp.minimum(s, n - 1), 0)),
                  pl.BlockSpec((tm, D_MODEL), lambda s: (jnp.maximum(s - 1, 0), 0))]
                 + [_const_spec(t.shape) for t in consts],
        out_specs=(pl.BlockSpec((tm, D_MODEL), lambda s: (jnp.maximum(s - 1, 0), 0)),
                   pl.BlockSpec((None, CONV_HALO, D_MODEL),
                                lambda s: (jnp.minimum(s, n - 1) // tiles_per_seq, 0, 0))),
        scratch_shapes=[a_scr, a_scr,
                        pltpu.VMEM((SUBLANES - 1, CONV_HALO + tm - SUBLANES, D_MODEL), F32),
                        pltpu.VMEM((tm, D_MODEL), F32)],
        compiler_params=_params("arbitrary"),
        name="conv_prompt",
    )(x2, x2, *consts)
    return out.reshape(b, seq, D_MODEL), tail


def _conv_sample_kernel(a_ref, st_ref, wdw_ref, bdw_ref, c_ref, so_ref):
    n_hist = CONV_WIDTH - 1
    a = a_ref[...]
    acc = a * wdw_ref[n_hist:CONV_WIDTH, :] + bdw_ref[...]
    for k in range(n_hist):
        acc = acc + st_ref[k] * wdw_ref[k:k + 1, :]
    c_ref[...] = acc
    so_ref[0:n_hist - 1] = st_ref[1:n_hist]
    so_ref[n_hist - 1] = a


def _conv_sample_core(a, states, w_dw, b_dw, layer, *, tb):
    n = a.shape[0]
    row = pl.BlockSpec((tb, D_MODEL), lambda t: (t, 0))
    st_in = pl.BlockSpec((None, CONV_WIDTH - 1, tb, D_MODEL), lambda t: (layer, 0, t, 0))
    st_out = pl.BlockSpec((CONV_WIDTH - 1, tb, D_MODEL), lambda t: (0, t, 0))
    return pl.pallas_call(
        _conv_sample_kernel,
        out_shape=(jax.ShapeDtypeStruct((n, D_MODEL), F32),
                   jax.ShapeDtypeStruct(states.shape[1:], F32)),
        grid=(n // tb,),
        in_specs=[row, st_in, _const_spec(w_dw.shape), _const_spec(b_dw.shape)],
        out_specs=(row, st_out),
        compiler_params=_params("parallel"),
        name="conv_sample",
    )(a, states, w_dw, b_dw)


PROMPT_TILE = 512
FFN_TILE = 1024
SAMPLE_ATTN_TILE = 16
SAMPLE_CONV_TILE = 32


def kernel(x_prompt, x_sample, cache_swa_k, cache_swa_v, state_conv, rel_bias, norm_mix, norm_ffn, norm_final, attn_w_qkv, attn_w_o, attn_sinks, sgu_w_in, sgu_ln_g, sgu_ln_b, sgu_w_spatial, sgu_b_spatial, sgu_w_out, conv_w_in, conv_w_dw, conv_b_dw, conv_ln_g, conv_ln_b, conv_w_out, ffn_w_up, ffn_w_down):
    batch, seq, _ = x_prompt.shape
    dec = x_sample.shape[0]
    depth = norm_mix.shape[0]
    mixer_of_layer = tuple(i % 3 for i in range(depth))
    slot_of_layer = tuple(mixer_of_layer[:i].count(mixer_of_layer[i]) for i in range(depth))

    def row(v):
        return v.reshape(1, -1).astype(F32)

    n_attn = attn_w_qkv.shape[0]

    def caches_t(c):
        return jnp.transpose(c, (0, 1, 3, 4, 2)).reshape(n_attn, dec, KV_DIM, WINDOW)

    def caches_from_t(c):
        c = c.reshape(n_attn, dec, N_KV_HEADS, HEAD_DIM, WINDOW)
        return jnp.transpose(c, (0, 1, 4, 2, 3))

    xp = x_prompt
    xs = x_sample.reshape(dec, D_MODEL)
    dist_bias = _distance_bias(rel_bias)
    k_caches_t, v_caches_t = caches_t(cache_swa_k), caches_t(cache_swa_v)
    states = jnp.transpose(state_conv.astype(F32), (0, 2, 1, 3))
    w_qkv_all = attn_w_qkv.astype(BF16)
    w_o_all = attn_w_o.astype(BF16)

    kp, vp, sgu_v_new, convp, convs = [], [], [], [], []
    new_caches = None
    for i in range(depth):
        m, j = mixer_of_layer[i], slot_of_layer[i]
        g_mix = row(norm_mix[i])
        if m == 0:
            bias_p, cap_p, sink_rows = _prompt_bias_tables(dist_bias, attn_sinks[j])
            xp, k1, v1 = _attn_prompt(xp, g_mix, w_qkv_all, w_o_all, j, bias_p, cap_p, sink_rows,
                                      tq=PROMPT_TILE)
            kp.append(k1.reshape(batch, WINDOW, N_KV_HEADS, HEAD_DIM))
            vp.append(v1.reshape(batch, WINDOW, N_KV_HEADS, HEAD_DIM))

            bias_s, extra_s = _sample_bias_tables(dist_bias, attn_sinks[j])
            qkv_s, kv_t = _norm_proj(xs, g_mix, w_qkv_all, j, kv_t=True)
            o_s, *new_caches = _attn_sample_core(qkv_s, kv_t, k_caches_t, v_caches_t, bias_s,
                                                 extra_s, j, new_caches, tb=SAMPLE_ATTN_TILE)
            xs = _proj_res(xs, o_s, w_o_all, j, heads_grouped=True)
        elif m == 1:
            w_in = sgu_w_in[j].astype(BF16)
            w_out = sgu_w_out[j].astype(BF16)
            ln_g, ln_b = row(sgu_ln_g[j]), row(sgu_ln_b[j])
            sp = jnp.tril(sgu_w_spatial[j]).astype(BF16)
            bsp = jnp.broadcast_to(sgu_b_spatial[j].astype(F32)[:, :, None],
                                   (SGU_GROUPS, CHUNK, LANES))
            xp = _sgu(xp.reshape(batch * seq, D_MODEL), g_mix, w_in, ln_g, ln_b, sp, bsp, w_out,
                      tm=PROMPT_TILE, sample=False).reshape(batch, seq, D_MODEL)
            sp0 = row(jnp.repeat(sgu_w_spatial[j][:, 0, 0], LANES))
            bsp0 = row(jnp.repeat(sgu_b_spatial[j][:, 0], LANES))
            xs, v_rows = _sgu(xs, g_mix, w_in, ln_g, ln_b, sp0, bsp0, w_out, tm=dec, sample=True)
            sgu_v_new.append(v_rows.reshape(dec, 1, D_MODEL))
        else:
            w_in = conv_w_in[j].astype(BF16)
            w_out = conv_w_out[j].astype(BF16)
            w_dw = conv_w_dw[j].astype(F32)
            b_dw = row(conv_b_dw[j])
            ln_g, ln_b = row(conv_ln_g[j]), row(conv_ln_b[j])
            w_dw8 = jnp.broadcast_to(w_dw[:, None, :], (CONV_WIDTH, SUBLANES, D_MODEL))
            b_dw8 = jnp.broadcast_to(b_dw, (SUBLANES, D_MODEL))
            xp, tail = _conv_prompt(xp, g_mix, w_in, w_dw8, b_dw8, ln_g, ln_b, w_out,
                                    tm=PROMPT_TILE)
            convp.append(tail[:, CONV_HALO - (CONV_WIDTH - 1):, :])
            a_s = _norm_proj(xs, g_mix, w_in, glu=True)
            c_s, st2 = _conv_sample_core(a_s, states, w_dw, b_dw, j, tb=SAMPLE_CONV_TILE)
            xs = _proj_res(xs, c_s, w_out, ln=(ln_g, ln_b))
            convs.append(jnp.transpose(st2, (1, 0, 2)))

        g_fin = row(norm_final) if i == depth - 1 else None
        xp, xs = _ffn(xp.reshape(batch * seq, D_MODEL), xs, row(norm_ffn[i]), ffn_w_up, ffn_w_down,
                      i, g_fin, tm=FFN_TILE)
        xp = xp.reshape(batch, seq, D_MODEL)

    y_prompt = xp
    y_sample = xs.reshape(dec, 1, D_MODEL)
    k_new, v_new = (caches_from_t(c) for c in new_caches)
    return (y_prompt, y_sample, jnp.stack(kp), jnp.stack(vp), k_new, v_new,
            jnp.stack(sgu_v_new), jnp.stack(convp), jnp.stack(convs))
```

```python
import functools
import math

import jax
import jax.numpy as jnp
from jax import lax
from jax.experimental import pallas as pl
from jax.experimental.pallas import tpu as pltpu

D_MODEL = 1024
HEAD_DIM = 64
N_HEADS = 16
N_KV_HEADS = 4
Q_PER_KV = 4
KV_DIM = N_KV_HEADS * HEAD_DIM
WINDOW = 128
ATTN_SCALE = HEAD_DIM ** -0.5
N_BUCKETS = 32
MAX_DISTANCE = 128
CHUNK = 128
SGU_GROUPS = 8
SGU_PARTS = 2
CONV_WIDTH = 31
CONV_HALO = 32
D_FF = 4 * D_MODEL
EPS = 1e-6
NEG_INF = -1e30
F32_MAX = float(jnp.finfo(jnp.float32).max)
INV_SQRT2 = 1.0 / math.sqrt(2.0)

LANES = 128
SUBLANES = 8
VMEM_LIMIT = 56 * 1024 * 1024

F32 = jnp.float32
BF16 = jnp.bfloat16


def _const_spec(shape):
    n = len(shape)
    return pl.BlockSpec(shape, lambda *_: (0,) * n, pipeline_mode=pl.Buffered(1))


def _params(*sem):
    return pltpu.CompilerParams(dimension_semantics=sem, vmem_limit_bytes=VMEM_LIMIT)


def _rms(x, g):
    return x * lax.rsqrt(jnp.mean(x * x, axis=-1, keepdims=True) + EPS) * g


def _layer_norm(x, g, b):
    mu = jnp.mean(x, axis=-1, keepdims=True)
    xc = x - mu
    var = jnp.mean(xc * xc, axis=-1, keepdims=True)
    return xc * lax.rsqrt(var + EPS) * g + b


def _dot(a, b):
    return jnp.dot(a, b, preferred_element_type=F32)


def _dot_nt(a, b):
    return lax.dot_general(a, b, (((1,), (1,)), ((), ())), preferred_element_type=F32)


FFN_CHUNK = 512
N_FFN_CHUNKS = D_FF // FFN_CHUNK


def _layer_spec(shape, layer):
    zeros = (0,) * (len(shape) - 1)
    return pl.BlockSpec((None,) + tuple(shape[1:]), lambda *_: (layer,) + zeros,
                        pipeline_mode=pl.Buffered(1))


def _ffn_kernel(xp_ref, xs_ref, g_ref, wu_ref, wd_ref, *rest, final, n_prompt_tiles):
    *rest, wu_scr, wd_scr = rest
    if final:
        gf_ref, op_ref, os_ref = rest
    else:
        op_ref, os_ref = rest

    def block(x_ref, o_ref):
        x = x_ref[...]
        h = _rms(x, g_ref[...]).astype(BF16)
        y = x
        for c in range(N_FFN_CHUNKS):
            u = _dot(h, wu_scr[c])
            u = jnp.square(jnp.maximum(u, 0.0)).astype(BF16)
            y = y + _dot(u, wd_scr[c])
        o_ref[...] = _rms(y, gf_ref[...]) if final else y

    i = pl.program_id(0)
    tile = i - N_FFN_CHUNKS

    @pl.when(i < N_FFN_CHUNKS)
    def _():
        wu_scr[i] = wu_ref[...].astype(BF16)
        wd_scr[i] = wd_ref[...].astype(BF16)

    @pl.when(jnp.logical_and(tile >= 0, tile < n_prompt_tiles))
    def _():
        block(xp_ref, op_ref)

    @pl.when(tile == n_prompt_tiles)
    def _():
        block(xs_ref, os_ref)


def _ffn(xp, xs, g, w_up, w_down, layer, g_final=None, *, tm):
    n = xp.shape[0]
    nt = n // tm
    last_chunk = N_FFN_CHUNKS - 1
    row = pl.BlockSpec((tm, D_MODEL), lambda i: (jnp.clip(i - N_FFN_CHUNKS, 0, nt - 1), 0))
    in_specs = [row, _const_spec(xs.shape), _const_spec((1, D_MODEL)),
                pl.BlockSpec((None, D_MODEL, FFN_CHUNK),
                             lambda i: (layer, 0, jnp.minimum(i, last_chunk))),
                pl.BlockSpec((None, FFN_CHUNK, D_MODEL),
                             lambda i: (layer, jnp.minimum(i, last_chunk), 0))]
    args = [xp, xs, g, w_up, w_down]
    if g_final is not None:
        in_specs.append(_const_spec((1, D_MODEL)))
        args.append(g_final)
    return pl.pallas_call(
        functools.partial(_ffn_kernel, final=g_final is not None, n_prompt_tiles=nt),
        out_shape=(jax.ShapeDtypeStruct((n, D_MODEL), F32),
                   jax.ShapeDtypeStruct(xs.shape, F32)),
        grid=(N_FFN_CHUNKS + nt + 1,),
        in_specs=in_specs,
        out_specs=(row, pl.BlockSpec(xs.shape, lambda i: (0, 0))),
        scratch_shapes=[pltpu.VMEM((N_FFN_CHUNKS, D_MODEL, FFN_CHUNK), BF16),
                        pltpu.VMEM((N_FFN_CHUNKS, FFN_CHUNK, D_MODEL), BF16)],
        compiler_params=_params("arbitrary"),
        name="ffn_final" if g_final is not None else "ffn",
    )(*args)


def _regroup_heads(t, group_major):
    blocks = range(N_HEADS)
    if group_major:
        order = [N_KV_HEADS * (blk % N_KV_HEADS) + blk // N_KV_HEADS for blk in blocks]
    else:
        order = [Q_PER_KV * (blk % Q_PER_KV) + blk // Q_PER_KV for blk in blocks]
    return jnp.concatenate([t[:, h * HEAD_DIM:(h + 1) * HEAD_DIM] for h in order], axis=1)


def _norm_proj_kernel(x_ref, g_ref, w_ref, o_ref, *rest, glu, kv_t):
    h = _rms(x_ref[...], g_ref[...]).astype(BF16)
    y = _dot(h, w_ref[...])
    if glu:
        half = y.shape[1] // 2
        y = y[:, :half] * jax.nn.sigmoid(y[:, half:])
    if kv_t:
        (t_ref,) = rest
        o_ref[:, :D_MODEL] = _regroup_heads(y[:, :D_MODEL], group_major=True)
        o_ref[:, D_MODEL:] = y[:, D_MODEL:]
        t_ref[...] = y[:, D_MODEL:].T
    else:
        o_ref[...] = y


def _weight_spec(w, layer):
    return _const_spec(w.shape) if layer is None else _layer_spec(w.shape, layer)


def _norm_proj(x, g, w, layer=None, *, glu=False, kv_t=False):
    n = x.shape[0]
    n_out = w.shape[-1] // 2 if glu else w.shape[-1]
    out_shape = [jax.ShapeDtypeStruct((n, n_out), F32)]
    if kv_t:
        out_shape.append(jax.ShapeDtypeStruct((n_out - D_MODEL, n), F32))
    out = pl.pallas_call(
        functools.partial(_norm_proj_kernel, glu=glu, kv_t=kv_t),
        out_shape=tuple(out_shape),
        grid=(1,),
        in_specs=[_const_spec(x.shape), _const_spec(g.shape), _weight_spec(w, layer)],
        out_specs=tuple(_const_spec(o.shape) for o in out_shape),
        compiler_params=_params("arbitrary"),
        name="norm_proj_glu" if glu else "norm_proj",
    )(x, g, w)
    return out if kv_t else out[0]


def _proj_res_kernel(x_ref, a_ref, w_ref, *rest, conv_tail, heads_grouped):
    a = a_ref[...]
    if conv_tail:
        lg_ref, lb_ref, o_ref = rest
        a = _layer_norm(a, lg_ref[...], lb_ref[...])
        a = a * jax.nn.sigmoid(a)
    else:
        (o_ref,) = rest
    if heads_grouped:
        a = _regroup_heads(a, group_major=False)
    o_ref[...] = x_ref[...] + _dot(a.astype(BF16), w_ref[...])


def _proj_res(x, a, w, layer=None, ln=None, heads_grouped=False):
    args = [x, a, w] + (list(ln) if ln is not None else [])
    in_specs = [_const_spec(t.shape) for t in args]
    in_specs[2] = _weight_spec(w, layer)
    return pl.pallas_call(
        functools.partial(_proj_res_kernel, conv_tail=ln is not None, heads_grouped=heads_grouped),
        out_shape=jax.ShapeDtypeStruct(x.shape, F32),
        grid=(1,),
        in_specs=in_specs,
        out_specs=_const_spec(x.shape),
        compiler_params=_params("arbitrary"),
        name="proj_res_ln" if ln is not None else "proj_res",
    )(*args)


def _t5_bucket(dist):
    n = jnp.maximum(dist, 0)
    max_exact = N_BUCKETS // 2
    nf = jnp.maximum(n, 1).astype(F32)
    large = max_exact + (jnp.log(nf / max_exact) / math.log(MAX_DISTANCE / max_exact)
                         * (N_BUCKETS - max_exact)).astype(jnp.int32)
    large = jnp.minimum(large, N_BUCKETS - 1)
    return jnp.where(n < max_exact, n, large)


def _distance_bias(rel_bias):
    buckets = _t5_bucket(jnp.arange(WINDOW + 1, dtype=jnp.int32))
    onehot = (buckets[:, None] == jnp.arange(N_BUCKETS, dtype=jnp.int32)[None, :]).astype(F32)
    return jnp.dot(onehot, rel_bias.astype(F32), precision=lax.Precision.HIGHEST)


def _prompt_bias_tables(dist_bias, sinks):
    period = 3 * WINDOW
    line = jnp.concatenate([jnp.broadcast_to(dist_bias[WINDOW:], (WINDOW - 1, N_HEADS)),
                            dist_bias[::-1],
                            jnp.broadcast_to(dist_bias[:1], (WINDOW, N_HEADS))]).T
    skew = jnp.tile(line, (1, WINDOW))[:, :WINDOW * (period - 1)]
    per_head = skew.reshape(N_HEADS, WINDOW, period - 1)[:, :, WINDOW - 1:period - 1]
    per_head = per_head.reshape(N_KV_HEADS, 2, 2, WINDOW, 2 * WINDOW)
    bias = per_head.transpose(0, 2, 4, 1, 3).reshape(N_KV_HEADS, 4 * WINDOW, 2 * WINDOW)

    qi = jnp.arange(WINDOW, dtype=jnp.int32)[None, :]
    kj = jnp.arange(2 * WINDOW, dtype=jnp.int32)[:, None]
    dist = qi - kj + WINDOW
    allowed = (dist >= 0) & (dist <= WINDOW)
    first = allowed & (kj >= WINDOW)
    cap = jnp.stack([jnp.where(allowed, F32_MAX, NEG_INF), jnp.where(first, F32_MAX, NEG_INF)])
    cap = jnp.tile(cap.astype(F32), (1, 2, 2))

    sink_rows = sinks.astype(F32).reshape(N_KV_HEADS, 2, 2).transpose(0, 2, 1).reshape(2 * N_KV_HEADS, 2)
    sink_rows = jnp.repeat(sink_rows, WINDOW, axis=1)
    return bias, cap, sink_rows


def _attn_prompt_kernel(x_ref, xres_ref, g_ref, wqkv_ref, wo_ref, bias_ref, cap_ref, sink_ref,
                        o_ref, kc_ref, vc_ref,
                        q0_scr, q1_scr, klo0_scr, klo1_scr, khi0_scr, khi1_scr, vt0_scr, vt1_scr,
                        a0_scr, a1_scr, *, tq, tiles_per_seq):
    s = pl.program_id(0)
    sets = ((q0_scr, klo0_scr, khi0_scr, vt0_scr, a0_scr),
            (q1_scr, klo1_scr, khi1_scr, vt1_scr, a1_scr))

    @pl.when(s == 0)
    def _():
        for scr in sets[0] + sets[1]:
            scr[...] = jnp.zeros(scr.shape, BF16)

    low = lax.broadcasted_iota(jnp.int32, (tq, LANES), 1) < HEAD_DIM
    n_col = D_MODEL // KV_DIM

    def step(new, old):
        q_new, klo_new, khi_new, vt_new, att_done = new
        q_old, klo_old, khi_old, vt_old, att_out = old

        starts_sequence = s % tiles_per_seq == 0
        klo_new[:, 0:WINDOW, :] = jnp.where(starts_sequence, 0.0, klo_old[:, tq:tq + WINDOW, :])
        khi_new[:, 0:WINDOW, :] = jnp.where(starts_sequence, 0.0, khi_old[:, tq:tq + WINDOW, :])
        vt_new[:, :, 0:WINDOW] = jnp.where(starts_sequence, 0.0, vt_old[:, :, tq:tq + WINDOW])
        h = _rms(x_ref[...], g_ref[...]).astype(BF16)

        def project_q(c):
            cols = slice(c * KV_DIM, (c + 1) * KV_DIM)
            q_new[:, cols] = (_dot(h, wqkv_ref[:, cols]) * ATTN_SCALE).astype(BF16)

        def project_k():
            k = _dot(h, wqkv_ref[:, D_MODEL:D_MODEL + KV_DIM])
            kc_ref[...] = k[tq - WINDOW:, :]
            for c in range(KV_DIM // LANES):
                kc = k[:, c * LANES:(c + 1) * LANES]
                kr = pltpu.roll(kc, HEAD_DIM, axis=1)
                klo_new[2 * c, WINDOW:, :] = jnp.where(low, kc, 0.0).astype(BF16)
                khi_new[2 * c, WINDOW:, :] = jnp.where(low, 0.0, kr).astype(BF16)
                klo_new[2 * c + 1, WINDOW:, :] = jnp.where(low, kr, 0.0).astype(BF16)
                khi_new[2 * c + 1, WINDOW:, :] = jnp.where(low, 0.0, kc).astype(BF16)

        def project_v():
            v = _dot(h, wqkv_ref[:, D_MODEL + KV_DIM:])
            vc_ref[...] = v[tq - WINDOW:, :]
            for c in range(KV_DIM // LANES):
                vt = v[:, c * LANES:(c + 1) * LANES].T.astype(BF16)
                vt_new[2 * c, :, WINDOW:] = vt[:HEAD_DIM]
                vt_new[2 * c + 1, :, WINDOW:] = vt[HEAD_DIM:]

        def project_out(c):
            cols = slice(c * KV_DIM, (c + 1) * KV_DIM)
            o_ref[:, cols] = xres_ref[:, cols] + _dot(att_done[...], wo_ref[:, cols])

        is_first = jnp.where((s - 1) % tiles_per_seq == 0, 1, 0)

        def scores(jb, kh):
            rows = slice(jb * WINDOW, (jb + 1) * WINDOW)
            band = slice(jb * WINDOW, (jb + 2) * WINDOW)
            cap = cap_ref[is_first] if jb == 0 else cap_ref[0]
            qst = jnp.concatenate([q_old[rows, (2 * kh) * LANES:(2 * kh + 1) * LANES],
                                   q_old[rows, (2 * kh + 1) * LANES:(2 * kh + 2) * LANES]], axis=0)
            kst = jnp.concatenate([klo_old[kh, band, :], khi_old[kh, band, :]], axis=0)
            return jnp.minimum(_dot_nt(kst, qst) + bias_ref[kh], cap)

        def attend(jb, kh, sc):
            rows = slice(jb * WINDOW, (jb + 1) * WINDOW)
            vt = vt_old[kh, :, jb * WINDOW:(jb + 2) * WINDOW]
            halves = []
            for half in range(2):
                sh = sc[half * 2 * WINDOW:(half + 1) * 2 * WINDOW]
                sink = sink_ref[2 * kh + half:2 * kh + half + 1, :]
                m = jnp.maximum(jnp.max(sh, axis=0, keepdims=True), sink)
                p = jnp.exp(sh - m)
                denom = jnp.sum(p, axis=0, keepdims=True) + jnp.exp(sink - m)
                halves.append(_dot(vt, p.astype(BF16)) * (1.0 / denom))
            o = jnp.concatenate(halves, axis=0).T.astype(BF16)
            att_out[rows, (2 * kh) * LANES:(2 * kh + 1) * LANES] = o[:WINDOW]
            att_out[rows, (2 * kh + 1) * LANES:(2 * kh + 2) * LANES] = o[WINDOW:]

        pieces = [functools.partial(project_q, c) for c in range(n_col)] + [project_k, project_v]
        pieces += [functools.partial(project_out, c) for c in range(n_col)]
        steps = [(jb, kh) for jb in range(tq // WINDOW) for kh in range(N_KV_HEADS)]
        piece_at = {(n * len(steps)) // len(pieces): piece for n, piece in enumerate(pieces)}
        assert len(piece_at) == len(pieces)
        s_next = scores(*steps[0])
        for n, st in enumerate(steps):
            s_cur = s_next
            if n + 1 < len(steps):
                s_next = scores(*steps[n + 1])
            if n in piece_at:
                piece_at[n]()
            attend(*st, s_cur)

    @pl.when(s % 2 == 0)
    def _():
        step(sets[0], sets[1])

    @pl.when(s % 2 == 1)
    def _():
        step(sets[1], sets[0])


def _attn_prompt(x, g, w_qkv, w_o, layer, bias, cap, sink_rows, *, tq):
    b, seq, _ = x.shape
    tiles_per_seq = seq // tq
    n = b * tiles_per_seq
    x2 = x.reshape(b * seq, D_MODEL)
    lagged = pl.BlockSpec((tq, D_MODEL), lambda s: (jnp.maximum(s - 2, 0), 0))
    cache = pl.BlockSpec((None, WINDOW, KV_DIM),
                         lambda s: (jnp.minimum(s, n - 1) // tiles_per_seq, 0, 0))
    q_scr = pltpu.VMEM((tq, D_MODEL), BF16)
    k_scr = pltpu.VMEM((N_KV_HEADS, WINDOW + tq, LANES), BF16)
    vt_scr = pltpu.VMEM((N_KV_HEADS, HEAD_DIM, WINDOW + tq), BF16)
    out, kc, vc = pl.pallas_call(
        functools.partial(_attn_prompt_kernel, tq=tq, tiles_per_seq=tiles_per_seq),
        out_shape=(jax.ShapeDtypeStruct((b * seq, D_MODEL), F32),
                   jax.ShapeDtypeStruct((b, WINDOW, KV_DIM), F32),
                   jax.ShapeDtypeStruct((b, WINDOW, KV_DIM), F32)),
        grid=(n + 2,),
        in_specs=[pl.BlockSpec((tq, D_MODEL), lambda s: (jnp.minimum(s, n - 1), 0)), lagged,
                  _const_spec((1, D_MODEL)),
                  _layer_spec(w_qkv.shape, layer), _layer_spec(w_o.shape, layer),
                  _const_spec(bias.shape), _const_spec(cap.shape), _const_spec(sink_rows.shape)],
        out_specs=(lagged, cache, cache),
        scratch_shapes=[q_scr, q_scr, k_scr, k_scr, k_scr, k_scr, vt_scr, vt_scr, q_scr, q_scr],
        compiler_params=_params("arbitrary"),
        name="attn_prompt",
    )(x2, x2, g, w_qkv, w_o, bias, cap, sink_rows)
    return out.reshape(b, seq, D_MODEL), kc, vc


ROWS_PAD = SUBLANES * Q_PER_KV


def _sample_bias_tables(dist_bias, sinks):
    def rows(per_head):
        t = per_head.reshape(N_KV_HEADS, Q_PER_KV, -1).transpose(1, 0, 2)
        return jnp.pad(t, ((0, 0), (0, SUBLANES - N_KV_HEADS), (0, 0))).reshape(ROWS_PAD, -1)

    bias = rows(dist_bias[:0:-1].T)
    extra = rows(jnp.stack([dist_bias[0], sinks.astype(F32)], axis=1))
    return bias, jnp.pad(extra, ((0, 0), (0, LANES - 2)))


def _attn_sample_kernel(q_ref, kn_ref, vn_ref, kvt_ref, kc_ref, vc_ref, bias_ref, extra_ref,
                        *rest, tb, n_aliased, out_slot):
    o_ref, ko_ref, vo_ref = rest[n_aliased:]
    if out_slot is not None:
        for out_ref in (ko_ref, vo_ref):
            for slot in range(out_ref.shape[0]):
                if slot != out_slot:
                    out_ref[slot] = jnp.zeros(out_ref.shape[1:], F32)
        ko_ref, vo_ref = ko_ref.at[out_slot], vo_ref.at[out_slot]
    t = pl.program_id(0)
    sub = lax.broadcasted_iota(jnp.int32, (SUBLANES, KV_DIM), 0)
    lane_head = lax.broadcasted_iota(jnp.int32, (SUBLANES, KV_DIM), 1) // HEAD_DIM
    own = jnp.logical_and(sub < N_KV_HEADS, lane_head == sub)
    newest = lax.broadcasted_iota(jnp.int32, (KV_DIM, WINDOW), 1) == WINDOW - 1
    bias = bias_ref[...]
    bias_new = extra_ref[:, 0:1]
    sink = extra_ref[:, 1:2]

    def scores(bb):
        q_rows = []
        for gq in range(Q_PER_KV):
            q_g = q_ref[bb:bb + 1, gq * KV_DIM:(gq + 1) * KV_DIM] * ATTN_SCALE
            q_rows.append(jnp.where(own, jnp.broadcast_to(q_g, (SUBLANES, KV_DIM)), 0.0))
        q_blk = jnp.concatenate(q_rows, axis=0)
        s_old = _dot(q_blk.astype(BF16), kc_ref[bb].astype(BF16)) + bias
        s_new = jnp.sum(q_blk * kn_ref[bb:bb + 1, :], axis=1, keepdims=True) + bias_new
        return s_old, s_new

    def attend(bb, s_old, s_new):
        m = jnp.maximum(jnp.max(s_old, axis=1, keepdims=True), jnp.maximum(s_new, sink))
        p_old = jnp.exp(s_old - m)
        p_new = jnp.exp(s_new - m)
        denom = jnp.sum(p_old, axis=1, keepdims=True) + p_new + jnp.exp(sink - m)
        o = _dot_nt(p_old.astype(BF16), vc_ref[bb].astype(BF16)) + p_new * vn_ref[bb:bb + 1, :]
        o = o / denom
        for gq in range(Q_PER_KV):
            o_rows = o[SUBLANES * gq:SUBLANES * (gq + 1)]
            o_g = jnp.sum(jnp.where(own, o_rows, 0.0), axis=0, keepdims=True)
            o_ref[bb:bb + 1, gq * KV_DIM:(gq + 1) * KV_DIM] = o_g

    def shift(bb):
        to_last = (WINDOW - 1) - (t * tb + bb)
        for cache_ref, row0, out_ref in ((kc_ref, 0, ko_ref), (vc_ref, KV_DIM, vo_ref)):
            moved = pltpu.roll(cache_ref[bb], WINDOW - 1, axis=1)
            col = pltpu.roll(kvt_ref[row0:row0 + KV_DIM, :], to_last, axis=1)
            out_ref[bb] = jnp.where(newest, col, moved)

    for bb in range(tb):
        shift(bb)
    s_next = scores(0)
    for bb in range(tb):
        s_cur = s_next
        if bb + 1 < tb:
            s_next = scores(bb + 1)
        attend(bb, *s_cur)


def _attn_sample_core(qkv, kv_t, k_caches, v_caches, bias, extra, layer, new_caches=None, *, tb):
    n = qkv.shape[0]
    cache = pl.BlockSpec((None, tb, KV_DIM, WINDOW), lambda t: (layer, t, 0, 0))
    kv_col = D_MODEL // KV_DIM
    args = [qkv, qkv, qkv, kv_t, k_caches, v_caches, bias, extra]
    in_specs = [pl.BlockSpec((tb, D_MODEL), lambda t: (t, 0)),
                pl.BlockSpec((tb, KV_DIM), lambda t: (t, kv_col)),
                pl.BlockSpec((tb, KV_DIM), lambda t: (t, kv_col + 1)),
                _const_spec(kv_t.shape), cache, cache,
                _const_spec(bias.shape), _const_spec(extra.shape)]
    if new_caches is None:
        aliases, out_slot = {}, layer
        cache_out = pl.BlockSpec((k_caches.shape[0], tb, KV_DIM, WINDOW), lambda t: (0, t, 0, 0))
    else:
        aliases, out_slot = {len(args): 1, len(args) + 1: 2}, None
        cache_out = cache
        args += list(new_caches)
        in_specs += [pl.BlockSpec(memory_space=pl.ANY)] * 2
    return pl.pallas_call(
        functools.partial(_attn_sample_kernel, tb=tb, n_aliased=len(aliases), out_slot=out_slot),
        out_shape=(jax.ShapeDtypeStruct((n, D_MODEL), F32),
                   jax.ShapeDtypeStruct(k_caches.shape, F32),
                   jax.ShapeDtypeStruct(v_caches.shape, F32)),
        grid=(n // tb,),
        in_specs=in_specs,
        out_specs=(pl.BlockSpec((tb, D_MODEL), lambda t: (t, 0)), cache_out, cache_out),
        input_output_aliases=aliases,
        compiler_params=_params("parallel"),
        name="attn_sample",
    )(*args)


def _sgu_kernel(x_ref, g_ref, win_ref, lg_ref, lb_ref, sp_ref, bsp_ref, wout_ref, *rest,
                tm, sample):
    def gelu_and_norm(z):
        z = 0.5 * z * (1.0 + lax.erf(z * INV_SQRT2))
        return z[:, :D_MODEL], _layer_norm(z[:, D_MODEL:], lg_ref[...], lb_ref[...])

    if sample:
        o_ref, v_ref = rest
        x = x_ref[...]
        u, v = gelu_and_norm(_dot(_rms(x, g_ref[...]).astype(BF16), win_ref[...]))
        v_ref[...] = v
        gated = (u * (v * sp_ref[...] + bsp_ref[...])).astype(BF16)
        o_ref[...] = x + _dot(gated, wout_ref[...])
        return

    o_ref, gated_scr = rest
    tp = tm // SGU_PARTS
    parts = [slice(r * tp, (r + 1) * tp) for r in range(SGU_PARTS)]
    zs = [_dot(_rms(x_ref[rows, :], g_ref[...]).astype(BF16), win_ref[...]) for rows in parts]
    for rows, z in zip(parts, zs):
        u, v = gelu_and_norm(z)
        vb = v.astype(BF16)
        for c in range(tp // CHUNK):
            chunk = slice(c * CHUNK, (c + 1) * CHUNK)
            dst = slice(rows.start + c * CHUNK, rows.start + (c + 1) * CHUNK)
            for gi in range(SGU_GROUPS):
                cols = slice(gi * LANES, (gi + 1) * LANES)
                mixed = _dot(sp_ref[gi], vb[chunk, cols]) + bsp_ref[gi]
                gated_scr[dst, cols] = (u[chunk, cols] * mixed).astype(BF16)
        o_ref[rows, :] = x_ref[rows, :] + _dot(gated_scr[rows, :], wout_ref[...])


def _sgu(x, g, w_in, ln_g, ln_b, sp, bsp, w_out, *, tm, sample):
    n = x.shape[0]
    row = pl.BlockSpec((tm, D_MODEL), lambda i: (i, 0))
    args = [x, g, w_in, ln_g, ln_b, sp, bsp, w_out]
    in_specs = [row] + [_const_spec(t.shape) for t in args[1:]]
    if sample:
        out_shape = (jax.ShapeDtypeStruct((n, D_MODEL), F32),) * 2
        out_specs = (row, row)
        scratch = []
    else:
        out_shape = jax.ShapeDtypeStruct((n, D_MODEL), F32)
        out_specs = row
        scratch = [pltpu.VMEM((tm, D_MODEL), BF16)]
    return pl.pallas_call(
        functools.partial(_sgu_kernel, tm=tm, sample=sample),
        out_shape=out_shape,
        grid=(n // tm,),
        in_specs=in_specs,
        out_specs=out_specs,
        scratch_shapes=scratch,
        compiler_params=_params("parallel"),
        name="sgu_sample" if sample else "sgu_prompt",
    )(*args)


CONV_ROWS = 64
CONV_LANES = 256
PROJ_LANES = 256


def _conv_prompt_kernel(x_ref, xprev_ref, g_ref, win_ref, wdw_ref, bdw_ref, lg_ref, lb_ref,
                        wout_ref, o_ref, tail_ref, a0_scr, a1_scr, sh_scr, c_scr,
                        *, tm, tiles_per_seq):
    s = pl.program_id(0)

    @pl.when(s == 0)
    def _():
        a0_scr[...] = jnp.zeros(a0_scr.shape, F32)
        a1_scr[...] = jnp.zeros(a1_scr.shape, F32)

    first_tap = CONV_HALO - (CONV_WIDTH - 1)
    reps = CONV_ROWS // SUBLANES

    def step(a_new, a_old):
        for r in range(1, SUBLANES):
            sh_scr[r - 1] = a_old[r:r + tm + CONV_HALO - SUBLANES, :]

        h = _rms(x_ref[...], g_ref[...]).astype(BF16)
        starts_sequence = s % tiles_per_seq == 0
        a_new[0:CONV_HALO, :] = jnp.where(starts_sequence, 0.0, a_old[tm:tm + CONV_HALO, :])

        def project(jc):
            cols = slice(jc * PROJ_LANES, (jc + 1) * PROJ_LANES)
            gate = slice(D_MODEL + jc * PROJ_LANES, D_MODEL + (jc + 1) * PROJ_LANES)
            a = _dot(h, win_ref[:, cols]) * jax.nn.sigmoid(_dot(h, win_ref[:, gate]))
            a_new[CONV_HALO:, cols] = a
            tail_ref[:, cols] = a[tm - CONV_HALO:, :]

        def convolve(rb):
            for lc in range(D_MODEL // CONV_LANES):
                cols = slice(lc * CONV_LANES, (lc + 1) * CONV_LANES)
                acc = jnp.concatenate([bdw_ref[:, cols]] * reps, axis=0)
                for kk in range(CONV_WIDTH):
                    whole, r = divmod(first_tap + kk, SUBLANES)
                    start = rb * CONV_ROWS + whole * SUBLANES
                    src = a_old if r == 0 else sh_scr.at[r - 1]
                    w = jnp.concatenate([wdw_ref[kk, :, cols]] * reps, axis=0)
                    acc = acc + src[start:start + CONV_ROWS, cols] * w
                c_scr[rb * CONV_ROWS:(rb + 1) * CONV_ROWS, cols] = acc

        n_proj = D_MODEL // PROJ_LANES
        n_conv = tm // CONV_ROWS
        for jc in range(n_proj):
            project(jc)
            for rb in range(jc * n_conv // n_proj, (jc + 1) * n_conv // n_proj):
                convolve(rb)
        c = _layer_norm(c_scr[...], lg_ref[...], lb_ref[...])
        c = (c * jax.nn.sigmoid(c)).astype(BF16)
        o_ref[...] = xprev_ref[...] + _dot(c, wout_ref[...])

    @pl.when(s % 2 == 0)
    def _():
        step(a0_scr, a1_scr)

    @pl.when(s % 2 == 1)
    def _():
        step(a1_scr, a0_scr)


def _conv_prompt(x, g, w_in, w_dw8, b_dw8, ln_g, ln_b, w_out, *, tm):
    b, seq, _ = x.shape
    tiles_per_seq = seq // tm
    n = b * tiles_per_seq
    consts = [g, w_in, w_dw8, b_dw8, ln_g, ln_b, w_out]
    a_scr = pltpu.VMEM((CONV_HALO + tm, D_MODEL), F32)
    x2 = x.reshape(b * seq, D_MODEL)
    out, tail = pl.pallas_call(
        functools.partial(_conv_prompt_kernel, tm=tm, tiles_per_seq=tiles_per_seq),
        out_shape=(jax.ShapeDtypeStruct((b * seq, D_MODEL), F32),
                   jax.ShapeDtypeStruct((b, CONV_HALO, D_MODEL), F32)),
        grid=(n + 1,),
        in_specs=[pl.BlockSpec((tm, D_MODEL), lambda s: (jnp.minimum(s, n - 1), 0)),
                  pl.BlockSpec((tm, D_MODEL), lambda s: (jnp.maximum(s - 1, 0), 0))]
                 + [_const_spec(t.shape) for t in consts],
        out_specs=(pl.BlockSpec((tm, D_MODEL), lambda s: (jnp.maximum(s - 1, 0), 0)),
                   pl.BlockSpec((None, CONV_HALO, D_MODEL),
                                lambda s: (jnp.minimum(s, n - 1) // tiles_per_seq, 0, 0))),
        scratch_shapes=[a_scr, a_scr,
                        pltpu.VMEM((SUBLANES - 1, CONV_HALO + tm - SUBLANES, D_MODEL), F32),
                        pltpu.VMEM((tm, D_MODEL), F32)],
        compiler_params=_params("arbitrary"),
        name="conv_prompt",
    )(x2, x2, *consts)
    return out.reshape(b, seq, D_MODEL), tail


def _conv_sample_kernel(a_ref, st_ref, wdw_ref, bdw_ref, c_ref, so_ref):
    n_hist = CONV_WIDTH - 1
    a = a_ref[...]
    acc = a * wdw_ref[n_hist:CONV_WIDTH, :] + bdw_ref[...]
    for k in range(n_hist):
        acc = acc + st_ref[k] * wdw_ref[k:k + 1, :]
    c_ref[...] = acc
    so_ref[0:n_hist - 1] = st_ref[1:n_hist]
    so_ref[n_hist - 1] = a


def _conv_sample_core(a, states, w_dw, b_dw, layer, *, tb):
    n = a.shape[0]
    row = pl.BlockSpec((tb, D_MODEL), lambda t: (t, 0))
    st_in = pl.BlockSpec((None, CONV_WIDTH - 1, tb, D_MODEL), lambda t: (layer, 0, t, 0))
    st_out = pl.BlockSpec((CONV_WIDTH - 1, tb, D_MODEL), lambda t: (0, t, 0))
    return pl.pallas_call(
        _conv_sample_kernel,
        out_shape=(jax.ShapeDtypeStruct((n, D_MODEL), F32),
                   jax.ShapeDtypeStruct(states.shape[1:], F32)),
        grid=(n // tb,),
        in_specs=[row, st_in, _const_spec(w_dw.shape), _const_spec(b_dw.shape)],
        out_specs=(row, st_out),
        compiler_params=_params("parallel"),
        name="conv_sample",
    )(a, states, w_dw, b_dw)


PROMPT_TILE = 512
SGU_TILE = 1024
FFN_TILE = 1024
SAMPLE_ATTN_TILE = 16
SAMPLE_CONV_TILE = 32


def kernel(x_prompt, x_sample, cache_swa_k, cache_swa_v, state_conv, rel_bias, norm_mix, norm_ffn, norm_final, attn_w_qkv, attn_w_o, attn_sinks, sgu_w_in, sgu_ln_g, sgu_ln_b, sgu_w_spatial, sgu_b_spatial, sgu_w_out, conv_w_in, conv_w_dw, conv_b_dw, conv_ln_g, conv_ln_b, conv_w_out, ffn_w_up, ffn_w_down):
    batch, seq, _ = x_prompt.shape
    dec = x_sample.shape[0]
    depth = norm_mix.shape[0]
    assert x_prompt.shape[2] == D_MODEL and x_sample.shape[1:] == (1, D_MODEL)
    assert seq % PROMPT_TILE == 0 and (batch * seq) % FFN_TILE == 0 and seq % SGU_TILE == 0
    assert dec % SAMPLE_ATTN_TILE == 0 and dec % SAMPLE_CONV_TILE == 0 and dec == WINDOW
    assert cache_swa_k.shape[2:] == (WINDOW, N_KV_HEADS, HEAD_DIM)
    assert state_conv.shape[2:] == (CONV_WIDTH - 1, D_MODEL)
    mixer_of_layer = tuple(i % 3 for i in range(depth))
    slot_of_layer = tuple(mixer_of_layer[:i].count(mixer_of_layer[i]) for i in range(depth))

    def row(v):
        return v.reshape(1, -1).astype(F32)

    n_attn = attn_w_qkv.shape[0]

    def caches_t(c):
        return jnp.transpose(c, (0, 1, 3, 4, 2)).reshape(n_attn, dec, KV_DIM, WINDOW)

    def caches_from_t(c):
        c = c.reshape(n_attn, dec, N_KV_HEADS, HEAD_DIM, WINDOW)
        return jnp.transpose(c, (0, 1, 4, 2, 3))

    xp = x_prompt
    xs = x_sample.reshape(dec, D_MODEL)
    dist_bias = _distance_bias(rel_bias)
    k_caches_t, v_caches_t = caches_t(cache_swa_k), caches_t(cache_swa_v)
    states = jnp.transpose(state_conv.astype(F32), (0, 2, 1, 3))
    w_qkv_all = attn_w_qkv.astype(BF16)
    w_o_all = attn_w_o.astype(BF16)

    kp, vp, sgu_v_new, convp, convs = [], [], [], [], []
    new_caches = None
    for i in range(depth):
        m, j = mixer_of_layer[i], slot_of_layer[i]
        g_mix = row(norm_mix[i])
        if m == 0:
            bias_p, cap_p, sink_rows = _prompt_bias_tables(dist_bias, attn_sinks[j])
            xp, k1, v1 = _attn_prompt(xp, g_mix, w_qkv_all, w_o_all, j, bias_p, cap_p, sink_rows,
                                      tq=PROMPT_TILE)
            kp.append(k1.reshape(batch, WINDOW, N_KV_HEADS, HEAD_DIM))
            vp.append(v1.reshape(batch, WINDOW, N_KV_HEADS, HEAD_DIM))

            bias_s, extra_s = _sample_bias_tables(dist_bias, attn_sinks[j])
            qkv_s, kv_t = _norm_proj(xs, g_mix, w_qkv_all, j, kv_t=True)
            o_s, *new_caches = _attn_sample_core(qkv_s, kv_t, k_caches_t, v_caches_t, bias_s,
                                                 extra_s, j, new_caches, tb=SAMPLE_ATTN_TILE)
            xs = _proj_res(xs, o_s, w_o_all, j, heads_grouped=True)
        elif m == 1:
            w_in = sgu_w_in[j].astype(BF16)
            w_out = sgu_w_out[j].astype(BF16)
            ln_g, ln_b = row(sgu_ln_g[j]), row(sgu_ln_b[j])
            sp = jnp.tril(sgu_w_spatial[j]).astype(BF16)
            bsp = jnp.broadcast_to(sgu_b_spatial[j].astype(F32)[:, :, None],
                                   (SGU_GROUPS, CHUNK, LANES))
            xp = _sgu(xp.reshape(batch * seq, D_MODEL), g_mix, w_in, ln_g, ln_b, sp, bsp, w_out,
                      tm=SGU_TILE, sample=False).reshape(batch, seq, D_MODEL)
            sp0 = row(jnp.repeat(sgu_w_spatial[j][:, 0, 0], LANES))
            bsp0 = row(jnp.repeat(sgu_b_spatial[j][:, 0], LANES))
            xs, v_rows = _sgu(xs, g_mix, w_in, ln_g, ln_b, sp0, bsp0, w_out, tm=dec, sample=True)
            sgu_v_new.append(v_rows.reshape(dec, 1, D_MODEL))
        else:
            w_in = conv_w_in[j].astype(BF16)
            w_out = conv_w_out[j].astype(BF16)
            w_dw = conv_w_dw[j].astype(F32)
            b_dw = row(conv_b_dw[j])
            ln_g, ln_b = row(conv_ln_g[j]), row(conv_ln_b[j])
            w_dw8 = jnp.broadcast_to(w_dw[:, None, :], (CONV_WIDTH, SUBLANES, D_MODEL))
            b_dw8 = jnp.broadcast_to(b_dw, (SUBLANES, D_MODEL))
            xp, tail = _conv_prompt(xp, g_mix, w_in, w_dw8, b_dw8, ln_g, ln_b, w_out,
                                    tm=PROMPT_TILE)
            convp.append(tail[:, CONV_HALO - (CONV_WIDTH - 1):, :])
            a_s = _norm_proj(xs, g_mix, w_in, glu=True)
            c_s, st2 = _conv_sample_core(a_s, states, w_dw, b_dw, j, tb=SAMPLE_CONV_TILE)
            xs = _proj_res(xs, c_s, w_out, ln=(ln_g, ln_b))
            convs.append(jnp.transpose(st2, (1, 0, 2)))

        g_fin = row(norm_final) if i == depth - 1 else None
        xp, xs = _ffn(xp.reshape(batch * seq, D_MODEL), xs, row(norm_ffn[i]), ffn_w_up, ffn_w_down,
                      i, g_fin, tm=FFN_TILE)
        xp = xp.reshape(batch, seq, D_MODEL)

    y_prompt = xp
    y_sample = xs.reshape(dec, 1, D_MODEL)
    k_new, v_new = (caches_from_t(c) for c in new_caches)
    return (y_prompt, y_sample, jnp.stack(kp), jnp.stack(vp), k_new, v_new,
            jnp.stack(sgu_v_new), jnp.stack(convp), jnp.stack(convs))
```

```python
import functools
import math

import jax
import jax.numpy as jnp
from jax import lax
from jax.experimental import pallas as pl
from jax.experimental.pallas import tpu as pltpu

D_MODEL = 1024
HEAD_DIM = 64
N_HEADS = 16
N_KV_HEADS = 4
Q_PER_KV = 4
KV_DIM = N_KV_HEADS * HEAD_DIM
WINDOW = 128
ATTN_SCALE = HEAD_DIM ** -0.5
N_BUCKETS = 32
MAX_DISTANCE = 128
CHUNK = 128
SGU_GROUPS = 8
SGU_PARTS = 2
CONV_WIDTH = 31
CONV_HALO = 32
D_FF = 4 * D_MODEL
EPS = 1e-6
NEG_INF = -1e30
F32_MAX = float(jnp.finfo(jnp.float32).max)
INV_SQRT2 = 1.0 / math.sqrt(2.0)

LANES = 128
SUBLANES = 8
VMEM_LIMIT = 56 * 1024 * 1024

F32 = jnp.float32
BF16 = jnp.bfloat16


def _const_spec(shape):
    n = len(shape)
    return pl.BlockSpec(shape, lambda *_: (0,) * n, pipeline_mode=pl.Buffered(1))


def _params(*sem):
    return pltpu.CompilerParams(dimension_semantics=sem, vmem_limit_bytes=VMEM_LIMIT)


def _rms(x, g):
    return x * lax.rsqrt(jnp.mean(x * x, axis=-1, keepdims=True) + EPS) * g


def _layer_norm(x, g, b):
    mu = jnp.mean(x, axis=-1, keepdims=True)
    xc = x - mu
    var = jnp.mean(xc * xc, axis=-1, keepdims=True)
    return xc * lax.rsqrt(var + EPS) * g + b


def _dot(a, b):
    return jnp.dot(a, b, preferred_element_type=F32)


def _dot_nt(a, b):
    return lax.dot_general(a, b, (((1,), (1,)), ((), ())), preferred_element_type=F32)


FFN_CHUNK = 512
N_FFN_CHUNKS = D_FF // FFN_CHUNK


def _layer_spec(shape, layer):
    zeros = (0,) * (len(shape) - 1)
    return pl.BlockSpec((None,) + tuple(shape[1:]), lambda *_: (layer,) + zeros,
                        pipeline_mode=pl.Buffered(1))


def _ffn_kernel(xp_ref, xs_ref, g_ref, wu_ref, wd_ref, *rest, final, n_prompt_tiles):
    *rest, wu_scr, wd_scr = rest
    if final:
        gf_ref, op_ref, os_ref = rest
    else:
        op_ref, os_ref = rest

    def block(x_ref, o_ref):
        x = x_ref[...]
        h = _rms(x, g_ref[...]).astype(BF16)
        y = x
        for c in range(N_FFN_CHUNKS):
            u = _dot(h, wu_scr[c])
            u = jnp.square(jnp.maximum(u, 0.0)).astype(BF16)
            y = y + _dot(u, wd_scr[c])
        o_ref[...] = _rms(y, gf_ref[...]) if final else y

    i = pl.program_id(0)
    tile = i - N_FFN_CHUNKS

    @pl.when(i < N_FFN_CHUNKS)
    def _():
        wu_scr[i] = wu_ref[...].astype(BF16)
        wd_scr[i] = wd_ref[...].astype(BF16)

    @pl.when(jnp.logical_and(tile >= 0, tile < n_prompt_tiles))
    def _():
        block(xp_ref, op_ref)

    @pl.when(tile == n_prompt_tiles)
    def _():
        block(xs_ref, os_ref)


def _ffn(xp, xs, g, w_up, w_down, layer, g_final=None, *, tm):
    n = xp.shape[0]
    nt = n // tm
    last_chunk = N_FFN_CHUNKS - 1
    row = pl.BlockSpec((tm, D_MODEL), lambda i: (jnp.clip(i - N_FFN_CHUNKS, 0, nt - 1), 0))
    in_specs = [row, _const_spec(xs.shape), _const_spec((1, D_MODEL)),
                pl.BlockSpec((None, D_MODEL, FFN_CHUNK),
                             lambda i: (layer, 0, jnp.minimum(i, last_chunk))),
                pl.BlockSpec((None, FFN_CHUNK, D_MODEL),
                             lambda i: (layer, jnp.minimum(i, last_chunk), 0))]
    args = [xp, xs, g, w_up, w_down]
    if g_final is not None:
        in_specs.append(_const_spec((1, D_MODEL)))
        args.append(g_final)
    return pl.pallas_call(
        functools.partial(_ffn_kernel, final=g_final is not None, n_prompt_tiles=nt),
        out_shape=(jax.ShapeDtypeStruct((n, D_MODEL), F32),
                   jax.ShapeDtypeStruct(xs.shape, F32)),
        grid=(N_FFN_CHUNKS + nt + 1,),
        in_specs=in_specs,
        out_specs=(row, pl.BlockSpec(xs.shape, lambda i: (0, 0))),
        scratch_shapes=[pltpu.VMEM((N_FFN_CHUNKS, D_MODEL, FFN_CHUNK), BF16),
                        pltpu.VMEM((N_FFN_CHUNKS, FFN_CHUNK, D_MODEL), BF16)],
        compiler_params=_params("arbitrary"),
        name="ffn_final" if g_final is not None else "ffn",
    )(*args)


def _regroup_heads(t, group_major):
    blocks = range(N_HEADS)
    if group_major:
        order = [N_KV_HEADS * (blk % N_KV_HEADS) + blk // N_KV_HEADS for blk in blocks]
    else:
        order = [Q_PER_KV * (blk % Q_PER_KV) + blk // Q_PER_KV for blk in blocks]
    return jnp.concatenate([t[:, h * HEAD_DIM:(h + 1) * HEAD_DIM] for h in order], axis=1)


def _norm_proj_kernel(x_ref, g_ref, w_ref, o_ref, *rest, glu, kv_t):
    h = _rms(x_ref[...], g_ref[...]).astype(BF16)
    y = _dot(h, w_ref[...])
    if glu:
        half = y.shape[1] // 2
        y = y[:, :half] * jax.nn.sigmoid(y[:, half:])
    if kv_t:
        (t_ref,) = rest
        o_ref[:, :D_MODEL] = _regroup_heads(y[:, :D_MODEL], group_major=True)
        o_ref[:, D_MODEL:] = y[:, D_MODEL:]
        t_ref[...] = y[:, D_MODEL:].T
    else:
        o_ref[...] = y


def _weight_spec(w, layer):
    return _const_spec(w.shape) if layer is None else _layer_spec(w.shape, layer)


def _norm_proj(x, g, w, layer=None, *, glu=False, kv_t=False):
    n = x.shape[0]
    n_out = w.shape[-1] // 2 if glu else w.shape[-1]
    out_shape = [jax.ShapeDtypeStruct((n, n_out), F32)]
    if kv_t:
        out_shape.append(jax.ShapeDtypeStruct((n_out - D_MODEL, n), F32))
    out = pl.pallas_call(
        functools.partial(_norm_proj_kernel, glu=glu, kv_t=kv_t),
        out_shape=tuple(out_shape),
        grid=(1,),
        in_specs=[_const_spec(x.shape), _const_spec(g.shape), _weight_spec(w, layer)],
        out_specs=tuple(_const_spec(o.shape) for o in out_shape),
        compiler_params=_params("arbitrary"),
        name="norm_proj_glu" if glu else "norm_proj",
    )(x, g, w)
    return out if kv_t else out[0]


def _proj_res_kernel(x_ref, a_ref, w_ref, *rest, conv_tail, heads_grouped):
    a = a_ref[...]
    if conv_tail:
        lg_ref, lb_ref, o_ref = rest
        a = _layer_norm(a, lg_ref[...], lb_ref[...])
        a = a * jax.nn.sigmoid(a)
    else:
        (o_ref,) = rest
    if heads_grouped:
        a = _regroup_heads(a, group_major=False)
    o_ref[...] = x_ref[...] + _dot(a.astype(BF16), w_ref[...])


def _proj_res(x, a, w, layer=None, ln=None, heads_grouped=False):
    args = [x, a, w] + (list(ln) if ln is not None else [])
    in_specs = [_const_spec(t.shape) for t in args]
    in_specs[2] = _weight_spec(w, layer)
    return pl.pallas_call(
        functools.partial(_proj_res_kernel, conv_tail=ln is not None, heads_grouped=heads_grouped),
        out_shape=jax.ShapeDtypeStruct(x.shape, F32),
        grid=(1,),
        in_specs=in_specs,
        out_specs=_const_spec(x.shape),
        compiler_params=_params("arbitrary"),
        name="proj_res_ln" if ln is not None else "proj_res",
    )(*args)


def _t5_bucket(dist):
    n = jnp.maximum(dist, 0)
    max_exact = N_BUCKETS // 2
    nf = jnp.maximum(n, 1).astype(F32)
    large = max_exact + (jnp.log(nf / max_exact) / math.log(MAX_DISTANCE / max_exact)
                         * (N_BUCKETS - max_exact)).astype(jnp.int32)
    large = jnp.minimum(large, N_BUCKETS - 1)
    return jnp.where(n < max_exact, n, large)


def _distance_bias(rel_bias):
    buckets = _t5_bucket(jnp.arange(WINDOW + 1, dtype=jnp.int32))
    onehot = (buckets[:, None] == jnp.arange(N_BUCKETS, dtype=jnp.int32)[None, :]).astype(F32)
    return jnp.dot(onehot, rel_bias.astype(F32), precision=lax.Precision.HIGHEST)


def _prompt_bias_tables(dist_bias, sinks):
    period = 3 * WINDOW
    line = jnp.concatenate([jnp.broadcast_to(dist_bias[WINDOW:], (WINDOW - 1, N_HEADS)),
                            dist_bias[::-1],
                            jnp.broadcast_to(dist_bias[:1], (WINDOW, N_HEADS))]).T
    skew = jnp.tile(line, (1, WINDOW))[:, :WINDOW * (period - 1)]
    per_head = skew.reshape(N_HEADS, WINDOW, period - 1)[:, :, WINDOW - 1:period - 1]
    per_head = per_head.reshape(N_KV_HEADS, 2, 2, WINDOW, 2 * WINDOW)
    bias = per_head.transpose(0, 2, 4, 1, 3).reshape(N_KV_HEADS, 4 * WINDOW, 2 * WINDOW)

    qi = jnp.arange(WINDOW, dtype=jnp.int32)[None, :]
    kj = jnp.arange(2 * WINDOW, dtype=jnp.int32)[:, None]
    dist = qi - kj + WINDOW
    allowed = (dist >= 0) & (dist <= WINDOW)
    first = allowed & (kj >= WINDOW)
    cap = jnp.stack([jnp.where(allowed, F32_MAX, NEG_INF), jnp.where(first, F32_MAX, NEG_INF)])
    cap = jnp.tile(cap.astype(F32), (1, 2, 2))

    sink_rows = sinks.astype(F32).reshape(N_KV_HEADS, 2, 2).transpose(0, 2, 1).reshape(2 * N_KV_HEADS, 2)
    sink_rows = jnp.repeat(sink_rows, WINDOW, axis=1)
    return bias, cap, sink_rows


def _attn_prompt_kernel(x_ref, xres_ref, g_ref, wqkv_ref, wo_ref, bias_ref, cap_ref, sink_ref,
                        o_ref, kc_ref, vc_ref,
                        q0_scr, q1_scr, klo0_scr, klo1_scr, khi0_scr, khi1_scr, vt0_scr, vt1_scr,
                        a0_scr, a1_scr, *, tq, tiles_per_seq):
    s = pl.program_id(0)
    sets = ((q0_scr, klo0_scr, khi0_scr, vt0_scr, a0_scr),
            (q1_scr, klo1_scr, khi1_scr, vt1_scr, a1_scr))

    @pl.when(s == 0)
    def _():
        for scr in sets[0] + sets[1]:
            scr[...] = jnp.zeros(scr.shape, BF16)

    low = lax.broadcasted_iota(jnp.int32, (tq, LANES), 1) < HEAD_DIM
    n_col = D_MODEL // KV_DIM

    def step(new, old):
        q_new, klo_new, khi_new, vt_new, att_done = new
        q_old, klo_old, khi_old, vt_old, att_out = old

        starts_sequence = s % tiles_per_seq == 0
        klo_new[:, 0:WINDOW, :] = jnp.where(starts_sequence, 0.0, klo_old[:, tq:tq + WINDOW, :])
        khi_new[:, 0:WINDOW, :] = jnp.where(starts_sequence, 0.0, khi_old[:, tq:tq + WINDOW, :])
        vt_new[:, :, 0:WINDOW] = jnp.where(starts_sequence, 0.0, vt_old[:, :, tq:tq + WINDOW])
        h = _rms(x_ref[...], g_ref[...]).astype(BF16)

        def project_q(c):
            cols = slice(c * KV_DIM, (c + 1) * KV_DIM)
            q_new[cols, :] = (_dot(h, wqkv_ref[:, cols]) * ATTN_SCALE).T.astype(BF16)

        def project_k():
            k = _dot(h, wqkv_ref[:, D_MODEL:D_MODEL + KV_DIM])
            kc_ref[...] = k[tq - WINDOW:, :]
            for c in range(KV_DIM // LANES):
                kc = k[:, c * LANES:(c + 1) * LANES]
                kr = pltpu.roll(kc, HEAD_DIM, axis=1)
                klo_new[2 * c, WINDOW:, :] = jnp.where(low, kc, 0.0).astype(BF16)
                khi_new[2 * c, WINDOW:, :] = jnp.where(low, 0.0, kr).astype(BF16)
                klo_new[2 * c + 1, WINDOW:, :] = jnp.where(low, kr, 0.0).astype(BF16)
                khi_new[2 * c + 1, WINDOW:, :] = jnp.where(low, 0.0, kc).astype(BF16)

        def project_v():
            v = _dot(h, wqkv_ref[:, D_MODEL + KV_DIM:])
            vc_ref[...] = v[tq - WINDOW:, :]
            for c in range(KV_DIM // LANES):
                vt = v[:, c * LANES:(c + 1) * LANES].T.astype(BF16)
                vt_new[2 * c, :, WINDOW:] = vt[:HEAD_DIM]
                vt_new[2 * c + 1, :, WINDOW:] = vt[HEAD_DIM:]

        def project_out(c):
            cols = slice(c * KV_DIM, (c + 1) * KV_DIM)
            o_ref[:, cols] = xres_ref[:, cols] + _dot(att_done[...], wo_ref[:, cols])

        is_first = jnp.where((s - 1) % tiles_per_seq == 0, 1, 0)

        def scores(jb, kh):
            rows = slice(jb * WINDOW, (jb + 1) * WINDOW)
            band = slice(jb * WINDOW, (jb + 2) * WINDOW)
            cap = cap_ref[is_first] if jb == 0 else cap_ref[0]
            qst_t = jnp.concatenate([q_old[(2 * kh) * LANES:(2 * kh + 1) * LANES, rows],
                                     q_old[(2 * kh + 1) * LANES:(2 * kh + 2) * LANES, rows]], axis=1)
            kst = jnp.concatenate([klo_old[kh, band, :], khi_old[kh, band, :]], axis=0)
            return jnp.minimum(_dot(kst, qst_t) + bias_ref[kh], cap)

        def attend(jb, kh, sc):
            rows = slice(jb * WINDOW, (jb + 1) * WINDOW)
            vt = vt_old[kh, :, jb * WINDOW:(jb + 2) * WINDOW]
            halves = []
            for half in range(2):
                sh = sc[half * 2 * WINDOW:(half + 1) * 2 * WINDOW]
                sink = sink_ref[2 * kh + half:2 * kh + half + 1, :]
                m = jnp.maximum(jnp.max(sh, axis=0, keepdims=True), sink)
                p = jnp.exp(sh - m)
                denom = jnp.sum(p, axis=0, keepdims=True) + jnp.exp(sink - m)
                halves.append(_dot(vt, p.astype(BF16)) * (1.0 / denom))
            o = jnp.concatenate(halves, axis=0).T.astype(BF16)
            att_out[rows, (2 * kh) * LANES:(2 * kh + 1) * LANES] = o[:WINDOW]
            att_out[rows, (2 * kh + 1) * LANES:(2 * kh + 2) * LANES] = o[WINDOW:]

        pieces = [functools.partial(project_q, c) for c in range(n_col)] + [project_k, project_v]
        pieces += [functools.partial(project_out, c) for c in range(n_col)]
        steps = [(jb, kh) for jb in range(tq // WINDOW) for kh in range(N_KV_HEADS)]
        piece_at = {(n * len(steps)) // len(pieces): piece for n, piece in enumerate(pieces)}
        assert len(piece_at) == len(pieces)
        s_next = scores(*steps[0])
        for n, st in enumerate(steps):
            s_cur = s_next
            if n + 1 < len(steps):
                s_next = scores(*steps[n + 1])
            if n in piece_at:
                piece_at[n]()
            attend(*st, s_cur)

    @pl.when(s % 2 == 0)
    def _():
        step(sets[0], sets[1])

    @pl.when(s % 2 == 1)
    def _():
        step(sets[1], sets[0])


def _attn_prompt(x, g, w_qkv, w_o, layer, bias, cap, sink_rows, *, tq):
    b, seq, _ = x.shape
    tiles_per_seq = seq // tq
    n = b * tiles_per_seq
    x2 = x.reshape(b * seq, D_MODEL)
    lagged = pl.BlockSpec((tq, D_MODEL), lambda s: (jnp.maximum(s - 2, 0), 0))
    cache = pl.BlockSpec((None, WINDOW, KV_DIM),
                         lambda s: (jnp.minimum(s, n - 1) // tiles_per_seq, 0, 0))
    q_scr = pltpu.VMEM((D_MODEL, tq), BF16)
    a_scr = pltpu.VMEM((tq, D_MODEL), BF16)
    k_scr = pltpu.VMEM((N_KV_HEADS, WINDOW + tq, LANES), BF16)
    vt_scr = pltpu.VMEM((N_KV_HEADS, HEAD_DIM, WINDOW + tq), BF16)
    out, kc, vc = pl.pallas_call(
        functools.partial(_attn_prompt_kernel, tq=tq, tiles_per_seq=tiles_per_seq),
        out_shape=(jax.ShapeDtypeStruct((b * seq, D_MODEL), F32),
                   jax.ShapeDtypeStruct((b, WINDOW, KV_DIM), F32),
                   jax.ShapeDtypeStruct((b, WINDOW, KV_DIM), F32)),
        grid=(n + 2,),
        in_specs=[pl.BlockSpec((tq, D_MODEL), lambda s: (jnp.minimum(s, n - 1), 0)), lagged,
                  _const_spec((1, D_MODEL)),
                  _layer_spec(w_qkv.shape, layer), _layer_spec(w_o.shape, layer),
                  _const_spec(bias.shape), _const_spec(cap.shape), _const_spec(sink_rows.shape)],
        out_specs=(lagged, cache, cache),
        scratch_shapes=[q_scr, q_scr, k_scr, k_scr, k_scr, k_scr, vt_scr, vt_scr, a_scr, a_scr],
        compiler_params=_params("arbitrary"),
        name="attn_prompt",
    )(x2, x2, g, w_qkv, w_o, bias, cap, sink_rows)
    return out.reshape(b, seq, D_MODEL), kc, vc


ROWS_PAD = SUBLANES * Q_PER_KV


def _sample_bias_tables(dist_bias, sinks):
    def rows(per_head):
        t = per_head.reshape(N_KV_HEADS, Q_PER_KV, -1).transpose(1, 0, 2)
        return jnp.pad(t, ((0, 0), (0, SUBLANES - N_KV_HEADS), (0, 0))).reshape(ROWS_PAD, -1)

    bias = rows(dist_bias[:0:-1].T)
    extra = rows(jnp.stack([dist_bias[0], sinks.astype(F32)], axis=1))
    return bias, jnp.pad(extra, ((0, 0), (0, LANES - 2)))


def _attn_sample_kernel(q_ref, kn_ref, vn_ref, kvt_ref, kc_ref, vc_ref, bias_ref, extra_ref,
                        *rest, tb, n_aliased, out_slot):
    o_ref, ko_ref, vo_ref = rest[n_aliased:]
    if out_slot is not None:
        for out_ref in (ko_ref, vo_ref):
            for slot in range(out_ref.shape[0]):
                if slot != out_slot:
                    out_ref[slot] = jnp.zeros(out_ref.shape[1:], F32)
        ko_ref, vo_ref = ko_ref.at[out_slot], vo_ref.at[out_slot]
    t = pl.program_id(0)
    sub = lax.broadcasted_iota(jnp.int32, (SUBLANES, KV_DIM), 0)
    lane_head = lax.broadcasted_iota(jnp.int32, (SUBLANES, KV_DIM), 1) // HEAD_DIM
    own = jnp.logical_and(sub < N_KV_HEADS, lane_head == sub)
    newest = lax.broadcasted_iota(jnp.int32, (KV_DIM, WINDOW), 1) == WINDOW - 1
    bias = bias_ref[...]
    bias_new = extra_ref[:, 0:1]
    sink = extra_ref[:, 1:2]

    def scores(bb):
        q_rows = []
        for gq in range(Q_PER_KV):
            q_g = q_ref[bb:bb + 1, gq * KV_DIM:(gq + 1) * KV_DIM] * ATTN_SCALE
            q_rows.append(jnp.where(own, jnp.broadcast_to(q_g, (SUBLANES, KV_DIM)), 0.0))
        q_blk = jnp.concatenate(q_rows, axis=0)
        s_old = _dot(q_blk.astype(BF16), kc_ref[bb].astype(BF16)) + bias
        s_new = jnp.sum(q_blk * kn_ref[bb:bb + 1, :], axis=1, keepdims=True) + bias_new
        return s_old, s_new

    def attend(bb, s_old, s_new):
        m = jnp.maximum(jnp.max(s_old, axis=1, keepdims=True), jnp.maximum(s_new, sink))
        p_old = jnp.exp(s_old - m)
        p_new = jnp.exp(s_new - m)
        denom = jnp.sum(p_old, axis=1, keepdims=True) + p_new + jnp.exp(sink - m)
        o = _dot_nt(p_old.astype(BF16), vc_ref[bb].astype(BF16)) + p_new * vn_ref[bb:bb + 1, :]
        o = o / denom
        for gq in range(Q_PER_KV):
            o_rows = o[SUBLANES * gq:SUBLANES * (gq + 1)]
            o_g = jnp.sum(jnp.where(own, o_rows, 0.0), axis=0, keepdims=True)
            o_ref[bb:bb + 1, gq * KV_DIM:(gq + 1) * KV_DIM] = o_g

    def shift(bb):
        to_last = (WINDOW - 1) - (t * tb + bb)
        for cache_ref, row0, out_ref in ((kc_ref, 0, ko_ref), (vc_ref, KV_DIM, vo_ref)):
            moved = pltpu.roll(cache_ref[bb], WINDOW - 1, axis=1)
            col = pltpu.roll(kvt_ref[row0:row0 + KV_DIM, :], to_last, axis=1)
            out_ref[bb] = jnp.where(newest, col, moved)

    for bb in range(tb):
        shift(bb)
    s_next = scores(0)
    for bb in range(tb):
        s_cur = s_next
        if bb + 1 < tb:
            s_next = scores(bb + 1)
        attend(bb, *s_cur)


def _attn_sample_core(qkv, kv_t, k_caches, v_caches, bias, extra, layer, new_caches=None, *, tb):
    n = qkv.shape[0]
    cache = pl.BlockSpec((None, tb, KV_DIM, WINDOW), lambda t: (layer, t, 0, 0))
    kv_col = D_MODEL // KV_DIM
    args = [qkv, qkv, qkv, kv_t, k_caches, v_caches, bias, extra]
    in_specs = [pl.BlockSpec((tb, D_MODEL), lambda t: (t, 0)),
                pl.BlockSpec((tb, KV_DIM), lambda t: (t, kv_col)),
                pl.BlockSpec((tb, KV_DIM), lambda t: (t, kv_col + 1)),
                _const_spec(kv_t.shape), cache, cache,
                _const_spec(bias.shape), _const_spec(extra.shape)]
    if new_caches is None:
        aliases, out_slot = {}, layer
        cache_out = pl.BlockSpec((k_caches.shape[0], tb, KV_DIM, WINDOW), lambda t: (0, t, 0, 0))
    else:
        aliases, out_slot = {len(args): 1, len(args) + 1: 2}, None
        cache_out = cache
        args += list(new_caches)
        in_specs += [pl.BlockSpec(memory_space=pl.ANY)] * 2
    return pl.pallas_call(
        functools.partial(_attn_sample_kernel, tb=tb, n_aliased=len(aliases), out_slot=out_slot),
        out_shape=(jax.ShapeDtypeStruct((n, D_MODEL), F32),
                   jax.ShapeDtypeStruct(k_caches.shape, F32),
                   jax.ShapeDtypeStruct(v_caches.shape, F32)),
        grid=(n // tb,),
        in_specs=in_specs,
        out_specs=(pl.BlockSpec((tb, D_MODEL), lambda t: (t, 0)), cache_out, cache_out),
        input_output_aliases=aliases,
        compiler_params=_params("parallel"),
        name="attn_sample",
    )(*args)


def _sgu_kernel(x_ref, g_ref, win_ref, lg_ref, lb_ref, sp_ref, bsp_ref, wout_ref, *rest,
                tm, sample):
    def gelu_and_norm(z):
        z = 0.5 * z * (1.0 + lax.erf(z * INV_SQRT2))
        return z[:, :D_MODEL], _layer_norm(z[:, D_MODEL:], lg_ref[...], lb_ref[...])

    if sample:
        o_ref, v_ref = rest
        x = x_ref[...]
        u, v = gelu_and_norm(_dot(_rms(x, g_ref[...]).astype(BF16), win_ref[...]))
        v_ref[...] = v
        gated = (u * (v * sp_ref[...] + bsp_ref[...])).astype(BF16)
        o_ref[...] = x + _dot(gated, wout_ref[...])
        return

    o_ref, gated_scr = rest
    tp = tm // SGU_PARTS
    parts = [slice(r * tp, (r + 1) * tp) for r in range(SGU_PARTS)]
    zs = [_dot(_rms(x_ref[rows, :], g_ref[...]).astype(BF16), win_ref[...]) for rows in parts]
    for rows, z in zip(parts, zs):
        u, v = gelu_and_norm(z)
        vb = v.astype(BF16)
        for c in range(tp // CHUNK):
            chunk = slice(c * CHUNK, (c + 1) * CHUNK)
            dst = slice(rows.start + c * CHUNK, rows.start + (c + 1) * CHUNK)
            for gi in range(SGU_GROUPS):
                cols = slice(gi * LANES, (gi + 1) * LANES)
                mixed = _dot(sp_ref[gi], vb[chunk, cols]) + bsp_ref[gi]
                gated_scr[dst, cols] = (u[chunk, cols] * mixed).astype(BF16)
        o_ref[rows, :] = x_ref[rows, :] + _dot(gated_scr[rows, :], wout_ref[...])


def _sgu(x, g, w_in, ln_g, ln_b, sp, bsp, w_out, *, tm, sample):
    n = x.shape[0]
    row = pl.BlockSpec((tm, D_MODEL), lambda i: (i, 0))
    args = [x, g, w_in, ln_g, ln_b, sp, bsp, w_out]
    in_specs = [row] + [_const_spec(t.shape) for t in args[1:]]
    if sample:
        out_shape = (jax.ShapeDtypeStruct((n, D_MODEL), F32),) * 2
        out_specs = (row, row)
        scratch = []
    else:
        out_shape = jax.ShapeDtypeStruct((n, D_MODEL), F32)
        out_specs = row
        scratch = [pltpu.VMEM((tm, D_MODEL), BF16)]
    return pl.pallas_call(
        functools.partial(_sgu_kernel, tm=tm, sample=sample),
        out_shape=out_shape,
        grid=(n // tm,),
        in_specs=in_specs,
        out_specs=out_specs,
        scratch_shapes=scratch,
        compiler_params=_params("parallel"),
        name="sgu_sample" if sample else "sgu_prompt",
    )(*args)


CONV_ROWS = 64
CONV_LANES = 256
PROJ_LANES = 256


def _conv_prompt_kernel(x_ref, xprev_ref, g_ref, win_ref, wdw_ref, bdw_ref, lg_ref, lb_ref,
                        wout_ref, o_ref, tail_ref, a0_scr, a1_scr, sh_scr, c_scr,
                        *, tm, tiles_per_seq):
    s = pl.program_id(0)

    @pl.when(s == 0)
    def _():
        a0_scr[...] = jnp.zeros(a0_scr.shape, F32)
        a1_scr[...] = jnp.zeros(a1_scr.shape, F32)

    first_tap = CONV_HALO - (CONV_WIDTH - 1)
    reps = CONV_ROWS // SUBLANES

    def step(a_new, a_old):
        for r in range(1, SUBLANES):
            sh_scr[r - 1] = a_old[r:r + tm + CONV_HALO - SUBLANES, :]

        h = _rms(x_ref[...], g_ref[...]).astype(BF16)
        starts_sequence = s % tiles_per_seq == 0
        a_new[0:CONV_HALO, :] = jnp.where(starts_sequence, 0.0, a_old[tm:tm + CONV_HALO, :])

        def project(jc):
            cols = slice(jc * PROJ_LANES, (jc + 1) * PROJ_LANES)
            gate = slice(D_MODEL + jc * PROJ_LANES, D_MODEL + (jc + 1) * PROJ_LANES)
            a = _dot(h, win_ref[:, cols]) * jax.nn.sigmoid(_dot(h, win_ref[:, gate]))
            a_new[CONV_HALO:, cols] = a
            tail_ref[:, cols] = a[tm - CONV_HALO:, :]

        def convolve(rb):
            for lc in range(D_MODEL // CONV_LANES):
                cols = slice(lc * CONV_LANES, (lc + 1) * CONV_LANES)
                acc = jnp.concatenate([bdw_ref[:, cols]] * reps, axis=0)
                for kk in range(CONV_WIDTH):
                    whole, r = divmod(first_tap + kk, SUBLANES)
                    start = rb * CONV_ROWS + whole * SUBLANES
                    src = a_old if r == 0 else sh_scr.at[r - 1]
                    w = jnp.concatenate([wdw_ref[kk, :, cols]] * reps, axis=0)
                    acc = acc + src[start:start + CONV_ROWS, cols] * w
                c_scr[rb * CONV_ROWS:(rb + 1) * CONV_ROWS, cols] = acc

        n_proj = D_MODEL // PROJ_LANES
        n_conv = tm // CONV_ROWS
        for jc in range(n_proj):
            project(jc)
            for rb in range(jc * n_conv // n_proj, (jc + 1) * n_conv // n_proj):
                convolve(rb)
        c = _layer_norm(c_scr[...], lg_ref[...], lb_ref[...])
        c = (c * jax.nn.sigmoid(c)).astype(BF16)
        o_ref[...] = xprev_ref[...] + _dot(c, wout_ref[...])

    @pl.when(s % 2 == 0)
    def _():
        step(a0_scr, a1_scr)

    @pl.when(s % 2 == 1)
    def _():
        step(a1_scr, a0_scr)


def _conv_prompt(x, g, w_in, w_dw8, b_dw8, ln_g, ln_b, w_out, *, tm):
    b, seq, _ = x.shape
    tiles_per_seq = seq // tm
    n = b * tiles_per_seq
    consts = [g, w_in, w_dw8, b_dw8, ln_g, ln_b, w_out]
    a_scr = pltpu.VMEM((CONV_HALO + tm, D_MODEL), F32)
    x2 = x.reshape(b * seq, D_MODEL)
    out, tail = pl.pallas_call(
        functools.partial(_conv_prompt_kernel, tm=tm, tiles_per_seq=tiles_per_seq),
        out_shape=(jax.ShapeDtypeStruct((b * seq, D_MODEL), F32),
                   jax.ShapeDtypeStruct((b, CONV_HALO, D_MODEL), F32)),
        grid=(n + 1,),
        in_specs=[pl.BlockSpec((tm, D_MODEL), lambda s: (jnp.minimum(s, n - 1), 0)),
                  pl.BlockSpec((tm, D_MODEL), lambda s: (jnp.maximum(s - 1, 0), 0))]
                 + [_const_spec(t.shape) for t in consts],
        out_specs=(pl.BlockSpec((tm, D_MODEL), lambda s: (jnp.maximum(s - 1, 0), 0)),
                   pl.BlockSpec((None, CONV_HALO, D_MODEL),
                                lambda s: (jnp.minimum(s, n - 1) // tiles_per_seq, 0, 0))),
        scratch_shapes=[a_scr, a_scr,
                        pltpu.VMEM((SUBLANES - 1, CONV_HALO + tm - SUBLANES, D_MODEL), F32),
                        pltpu.VMEM((tm, D_MODEL), F32)],
        compiler_params=_params("arbitrary"),
        name="conv_prompt",
    )(x2, x2, *consts)
    return out.reshape(b, seq, D_MODEL), tail


def _conv_sample_kernel(a_ref, st_ref, wdw_ref, bdw_ref, c_ref, so_ref):
    n_hist = CONV_WIDTH - 1
    a = a_ref[...]
    acc = a * wdw_ref[n_hist:CONV_WIDTH, :] + bdw_ref[...]
    for k in range(n_hist):
        acc = acc + st_ref[k] * wdw_ref[k:k + 1, :]
    c_ref[...] = acc
    so_ref[0:n_hist - 1] = st_ref[1:n_hist]
    so_ref[n_hist - 1] = a


def _conv_sample_core(a, states, w_dw, b_dw, layer, *, tb):
    n = a.shape[0]
    row = pl.BlockSpec((tb, D_MODEL), lambda t: (t, 0))
    st_in = pl.BlockSpec((None, CONV_WIDTH - 1, tb, D_MODEL), lambda t: (layer, 0, t, 0))
    st_out = pl.BlockSpec((CONV_WIDTH - 1, tb, D_MODEL), lambda t: (0, t, 0))
    return pl.pallas_call(
        _conv_sample_kernel,
        out_shape=(jax.ShapeDtypeStruct((n, D_MODEL), F32),
                   jax.ShapeDtypeStruct(states.shape[1:], F32)),
        grid=(n // tb,),
        in_specs=[row, st_in, _const_spec(w_dw.shape), _const_spec(b_dw.shape)],
        out_specs=(row, st_out),
        compiler_params=_params("parallel"),
        name="conv_sample",
    )(a, states, w_dw, b_dw)


PROMPT_TILE = 512
SGU_TILE = 1024
FFN_TILE = 1024
SAMPLE_ATTN_TILE = 16
SAMPLE_CONV_TILE = 32


def kernel(x_prompt, x_sample, cache_swa_k, cache_swa_v, state_conv, rel_bias, norm_mix, norm_ffn, norm_final, attn_w_qkv, attn_w_o, attn_sinks, sgu_w_in, sgu_ln_g, sgu_ln_b, sgu_w_spatial, sgu_b_spatial, sgu_w_out, conv_w_in, conv_w_dw, conv_b_dw, conv_ln_g, conv_ln_b, conv_w_out, ffn_w_up, ffn_w_down):
    batch, seq, _ = x_prompt.shape
    dec = x_sample.shape[0]
    depth = norm_mix.shape[0]
    assert x_prompt.shape[2] == D_MODEL and x_sample.shape[1:] == (1, D_MODEL)
    assert seq % PROMPT_TILE == 0 and (batch * seq) % FFN_TILE == 0 and seq % SGU_TILE == 0
    assert dec % SAMPLE_ATTN_TILE == 0 and dec % SAMPLE_CONV_TILE == 0 and dec == WINDOW
    assert cache_swa_k.shape[2:] == (WINDOW, N_KV_HEADS, HEAD_DIM)
    assert state_conv.shape[2:] == (CONV_WIDTH - 1, D_MODEL)
    mixer_of_layer = tuple(i % 3 for i in range(depth))
    slot_of_layer = tuple(mixer_of_layer[:i].count(mixer_of_layer[i]) for i in range(depth))

    def row(v):
        return v.reshape(1, -1).astype(F32)

    n_attn = attn_w_qkv.shape[0]

    def caches_t(c):
        return jnp.transpose(c, (0, 1, 3, 4, 2)).reshape(n_attn, dec, KV_DIM, WINDOW)

    def caches_from_t(c):
        c = c.reshape(n_attn, dec, N_KV_HEADS, HEAD_DIM, WINDOW)
        return jnp.transpose(c, (0, 1, 4, 2, 3))

    xp = x_prompt
    xs = x_sample.reshape(dec, D_MODEL)
    dist_bias = _distance_bias(rel_bias)
    k_caches_t, v_caches_t = caches_t(cache_swa_k), caches_t(cache_swa_v)
    states = jnp.transpose(state_conv.astype(F32), (0, 2, 1, 3))
    w_qkv_all = attn_w_qkv.astype(BF16)
    w_o_all = attn_w_o.astype(BF16)

    kp, vp, sgu_v_new, convp, convs = [], [], [], [], []
    new_caches = None
    for i in range(depth):
        m, j = mixer_of_layer[i], slot_of_layer[i]
        g_mix = row(norm_mix[i])
        if m == 0:
            bias_p, cap_p, sink_rows = _prompt_bias_tables(dist_bias, attn_sinks[j])
            xp, k1, v1 = _attn_prompt(xp, g_mix, w_qkv_all, w_o_all, j, bias_p, cap_p, sink_rows,
                                      tq=PROMPT_TILE)
            kp.append(k1.reshape(batch, WINDOW, N_KV_HEADS, HEAD_DIM))
            vp.append(v1.reshape(batch, WINDOW, N_KV_HEADS, HEAD_DIM))

            bias_s, extra_s = _sample_bias_tables(dist_bias, attn_sinks[j])
            qkv_s, kv_t = _norm_proj(xs, g_mix, w_qkv_all, j, kv_t=True)
            o_s, *new_caches = _attn_sample_core(qkv_s, kv_t, k_caches_t, v_caches_t, bias_s,
                                                 extra_s, j, new_caches, tb=SAMPLE_ATTN_TILE)
            xs = _proj_res(xs, o_s, w_o_all, j, heads_grouped=True)
        elif m == 1:
            w_in = sgu_w_in[j].astype(BF16)
            w_out = sgu_w_out[j].astype(BF16)
            ln_g, ln_b = row(sgu_ln_g[j]), row(sgu_ln_b[j])
            sp = jnp.tril(sgu_w_spatial[j]).astype(BF16)
            bsp = jnp.broadcast_to(sgu_b_spatial[j].astype(F32)[:, :, None],
                                   (SGU_GROUPS, CHUNK, LANES))
            xp = _sgu(xp.reshape(batch * seq, D_MODEL), g_mix, w_in, ln_g, ln_b, sp, bsp, w_out,
                      tm=SGU_TILE, sample=False).reshape(batch, seq, D_MODEL)
            sp0 = row(jnp.repeat(sgu_w_spatial[j][:, 0, 0], LANES))
            bsp0 = row(jnp.repeat(sgu_b_spatial[j][:, 0], LANES))
            xs, v_rows = _sgu(xs, g_mix, w_in, ln_g, ln_b, sp0, bsp0, w_out, tm=dec, sample=True)
            sgu_v_new.append(v_rows.reshape(dec, 1, D_MODEL))
        else:
            w_in = conv_w_in[j].astype(BF16)
            w_out = conv_w_out[j].astype(BF16)
            w_dw = conv_w_dw[j].astype(F32)
            b_dw = row(conv_b_dw[j])
            ln_g, ln_b = row(conv_ln_g[j]), row(conv_ln_b[j])
            w_dw8 = jnp.broadcast_to(w_dw[:, None, :], (CONV_WIDTH, SUBLANES, D_MODEL))
            b_dw8 = jnp.broadcast_to(b_dw, (SUBLANES, D_MODEL))
            xp, tail = _conv_prompt(xp, g_mix, w_in, w_dw8, b_dw8, ln_g, ln_b, w_out,
                                    tm=PROMPT_TILE)
            convp.append(tail[:, CONV_HALO - (CONV_WIDTH - 1):, :])
            a_s = _norm_proj(xs, g_mix, w_in, glu=True)
            c_s, st2 = _conv_sample_core(a_s, states, w_dw, b_dw, j, tb=SAMPLE_CONV_TILE)
            xs = _proj_res(xs, c_s, w_out, ln=(ln_g, ln_b))
            convs.append(jnp.transpose(st2, (1, 0, 2)))

        g_fin = row(norm_final) if i == depth - 1 else None
        xp, xs = _ffn(xp.reshape(batch * seq, D_MODEL), xs, row(norm_ffn[i]), ffn_w_up, ffn_w_down,
                      i, g_fin, tm=FFN_TILE)
        xp = xp.reshape(batch, seq, D_MODEL)

    y_prompt = xp
    y_sample = xs.reshape(dec, 1, D_MODEL)
    k_new, v_new = (caches_from_t(c) for c in new_caches)
    return (y_prompt, y_sample, jnp.stack(kp), jnp.stack(vp), k_new, v_new,
            jnp.stack(sgu_v_new), jnp.stack(convp), jnp.stack(convs))
```

```python
import functools
import math

import jax
import jax.numpy as jnp
from jax import lax
from jax.experimental import pallas as pl
from jax.experimental.pallas import tpu as pltpu

D_MODEL = 1024
HEAD_DIM = 64
N_HEADS = 16
N_KV_HEADS = 4
Q_PER_KV = 4
KV_DIM = N_KV_HEADS * HEAD_DIM
WINDOW = 128
ATTN_SCALE = HEAD_DIM ** -0.5
N_BUCKETS = 32
MAX_DISTANCE = 128
CHUNK = 128
SGU_GROUPS = 8
SGU_PARTS = 2
CONV_WIDTH = 31
CONV_HALO = 32
D_FF = 4 * D_MODEL
EPS = 1e-6
NEG_INF = -1e30
F32_MAX = float(jnp.finfo(jnp.float32).max)
INV_SQRT2 = 1.0 / math.sqrt(2.0)

LANES = 128
SUBLANES = 8
VMEM_LIMIT = 56 * 1024 * 1024

F32 = jnp.float32
BF16 = jnp.bfloat16


def _const_spec(shape):
    n = len(shape)
    return pl.BlockSpec(shape, lambda *_: (0,) * n, pipeline_mode=pl.Buffered(1))


def _params(*sem):
    return pltpu.CompilerParams(dimension_semantics=sem, vmem_limit_bytes=VMEM_LIMIT)


def _rms(x, g):
    return x * lax.rsqrt(jnp.mean(x * x, axis=-1, keepdims=True) + EPS) * g


def _layer_norm(x, g, b):
    mu = jnp.mean(x, axis=-1, keepdims=True)
    xc = x - mu
    var = jnp.mean(xc * xc, axis=-1, keepdims=True)
    return xc * lax.rsqrt(var + EPS) * g + b


def _dot(a, b):
    return jnp.dot(a, b, preferred_element_type=F32)


def _dot_nt(a, b):
    return lax.dot_general(a, b, (((1,), (1,)), ((), ())), preferred_element_type=F32)


FFN_CHUNK = 512
N_FFN_CHUNKS = D_FF // FFN_CHUNK


def _layer_spec(shape, layer):
    zeros = (0,) * (len(shape) - 1)
    return pl.BlockSpec((None,) + tuple(shape[1:]), lambda *_: (layer,) + zeros,
                        pipeline_mode=pl.Buffered(1))


def _ffn_kernel(xp_ref, xs_ref, g_ref, wu_ref, wd_ref, *rest, final, n_prompt_tiles):
    *rest, wu_scr, wd_scr = rest
    if final:
        gf_ref, op_ref, os_ref = rest
    else:
        op_ref, os_ref = rest

    def block(x_ref, o_ref):
        x = x_ref[...]
        h = _rms(x, g_ref[...]).astype(BF16)
        y = x
        for c in range(N_FFN_CHUNKS):
            u = _dot(h, wu_scr[c])
            u = jnp.square(jnp.maximum(u, 0.0)).astype(BF16)
            y = y + _dot(u, wd_scr[c])
        o_ref[...] = _rms(y, gf_ref[...]) if final else y

    i = pl.program_id(0)
    tile = i - N_FFN_CHUNKS

    @pl.when(i < N_FFN_CHUNKS)
    def _():
        wu_scr[i] = wu_ref[...].astype(BF16)
        wd_scr[i] = wd_ref[...].astype(BF16)

    @pl.when(jnp.logical_and(tile >= 0, tile < n_prompt_tiles))
    def _():
        block(xp_ref, op_ref)

    @pl.when(tile == n_prompt_tiles)
    def _():
        block(xs_ref, os_ref)


def _ffn(xp, xs, g, w_up, w_down, layer, g_final=None, *, tm):
    n = xp.shape[0]
    nt = n // tm
    last_chunk = N_FFN_CHUNKS - 1
    row = pl.BlockSpec((tm, D_MODEL), lambda i: (jnp.clip(i - N_FFN_CHUNKS, 0, nt - 1), 0))
    in_specs = [row, _const_spec(xs.shape), _const_spec((1, D_MODEL)),
                pl.BlockSpec((None, D_MODEL, FFN_CHUNK),
                             lambda i: (layer, 0, jnp.minimum(i, last_chunk))),
                pl.BlockSpec((None, FFN_CHUNK, D_MODEL),
                             lambda i: (layer, jnp.minimum(i, last_chunk), 0))]
    args = [xp, xs, g, w_up, w_down]
    if g_final is not None:
        in_specs.append(_const_spec((1, D_MODEL)))
        args.append(g_final)
    return pl.pallas_call(
        functools.partial(_ffn_kernel, final=g_final is not None, n_prompt_tiles=nt),
        out_shape=(jax.ShapeDtypeStruct((n, D_MODEL), F32),
                   jax.ShapeDtypeStruct(xs.shape, F32)),
        grid=(N_FFN_CHUNKS + nt + 1,),
        in_specs=in_specs,
        out_specs=(row, pl.BlockSpec(xs.shape, lambda i: (0, 0))),
        scratch_shapes=[pltpu.VMEM((N_FFN_CHUNKS, D_MODEL, FFN_CHUNK), BF16),
                        pltpu.VMEM((N_FFN_CHUNKS, FFN_CHUNK, D_MODEL), BF16)],
        compiler_params=_params("arbitrary"),
        name="ffn_final" if g_final is not None else "ffn",
    )(*args)


def _regroup_heads(t, group_major):
    blocks = range(N_HEADS)
    if group_major:
        order = [N_KV_HEADS * (blk % N_KV_HEADS) + blk // N_KV_HEADS for blk in blocks]
    else:
        order = [Q_PER_KV * (blk % Q_PER_KV) + blk // Q_PER_KV for blk in blocks]
    return jnp.concatenate([t[:, h * HEAD_DIM:(h + 1) * HEAD_DIM] for h in order], axis=1)


def _norm_proj_kernel(x_ref, g_ref, w_ref, o_ref, *rest, glu, kv_t):
    h = _rms(x_ref[...], g_ref[...]).astype(BF16)
    y = _dot(h, w_ref[...])
    if glu:
        half = y.shape[1] // 2
        y = y[:, :half] * jax.nn.sigmoid(y[:, half:])
    if kv_t:
        (t_ref,) = rest
        o_ref[:, :D_MODEL] = _regroup_heads(y[:, :D_MODEL], group_major=True)
        o_ref[:, D_MODEL:] = y[:, D_MODEL:]
        t_ref[...] = y[:, D_MODEL:].T
    else:
        o_ref[...] = y


def _weight_spec(w, layer):
    return _const_spec(w.shape) if layer is None else _layer_spec(w.shape, layer)


def _norm_proj(x, g, w, layer=None, *, glu=False, kv_t=False):
    n = x.shape[0]
    n_out = w.shape[-1] // 2 if glu else w.shape[-1]
    out_shape = [jax.ShapeDtypeStruct((n, n_out), F32)]
    if kv_t:
        out_shape.append(jax.ShapeDtypeStruct((n_out - D_MODEL, n), F32))
    out = pl.pallas_call(
        functools.partial(_norm_proj_kernel, glu=glu, kv_t=kv_t),
        out_shape=tuple(out_shape),
        grid=(1,),
        in_specs=[_const_spec(x.shape), _const_spec(g.shape), _weight_spec(w, layer)],
        out_specs=tuple(_const_spec(o.shape) for o in out_shape),
        compiler_params=_params("arbitrary"),
        name="norm_proj_glu" if glu else "norm_proj",
    )(x, g, w)
    return out if kv_t else out[0]


def _proj_res_kernel(x_ref, a_ref, w_ref, *rest, conv_tail, heads_grouped):
    a = a_ref[...]
    if conv_tail:
        lg_ref, lb_ref, o_ref = rest
        a = _layer_norm(a, lg_ref[...], lb_ref[...])
        a = a * jax.nn.sigmoid(a)
    else:
        (o_ref,) = rest
    if heads_grouped:
        a = _regroup_heads(a, group_major=False)
    o_ref[...] = x_ref[...] + _dot(a.astype(BF16), w_ref[...])


def _proj_res(x, a, w, layer=None, ln=None, heads_grouped=False):
    args = [x, a, w] + (list(ln) if ln is not None else [])
    in_specs = [_const_spec(t.shape) for t in args]
    in_specs[2] = _weight_spec(w, layer)
    return pl.pallas_call(
        functools.partial(_proj_res_kernel, conv_tail=ln is not None, heads_grouped=heads_grouped),
        out_shape=jax.ShapeDtypeStruct(x.shape, F32),
        grid=(1,),
        in_specs=in_specs,
        out_specs=_const_spec(x.shape),
        compiler_params=_params("arbitrary"),
        name="proj_res_ln" if ln is not None else "proj_res",
    )(*args)


def _t5_bucket(dist):
    n = jnp.maximum(dist, 0)
    max_exact = N_BUCKETS // 2
    nf = jnp.maximum(n, 1).astype(F32)
    large = max_exact + (jnp.log(nf / max_exact) / math.log(MAX_DISTANCE / max_exact)
                         * (N_BUCKETS - max_exact)).astype(jnp.int32)
    large = jnp.minimum(large, N_BUCKETS - 1)
    return jnp.where(n < max_exact, n, large)


def _distance_bias(rel_bias):
    buckets = _t5_bucket(jnp.arange(WINDOW + 1, dtype=jnp.int32))
    onehot = (buckets[:, None] == jnp.arange(N_BUCKETS, dtype=jnp.int32)[None, :]).astype(F32)
    return jnp.dot(onehot, rel_bias.astype(F32), precision=lax.Precision.HIGHEST)


def _prompt_bias_tables(dist_bias, sinks):
    period = 3 * WINDOW
    line = jnp.concatenate([jnp.broadcast_to(dist_bias[WINDOW:], (WINDOW - 1, N_HEADS)),
                            dist_bias[::-1],
                            jnp.broadcast_to(dist_bias[:1], (WINDOW, N_HEADS))]).T
    skew = jnp.tile(line, (1, WINDOW))[:, :WINDOW * (period - 1)]
    per_head = skew.reshape(N_HEADS, WINDOW, period - 1)[:, :, WINDOW - 1:period - 1]
    per_head = per_head.reshape(N_KV_HEADS, 2, 2, WINDOW, 2 * WINDOW)
    bias = per_head.transpose(0, 2, 4, 1, 3).reshape(N_KV_HEADS, 4 * WINDOW, 2 * WINDOW)

    qi = jnp.arange(WINDOW, dtype=jnp.int32)[None, :]
    kj = jnp.arange(2 * WINDOW, dtype=jnp.int32)[:, None]
    dist = qi - kj + WINDOW
    allowed = (dist >= 0) & (dist <= WINDOW)
    first = allowed & (kj >= WINDOW)
    cap = jnp.stack([jnp.where(allowed, F32_MAX, NEG_INF), jnp.where(first, F32_MAX, NEG_INF)])
    cap = jnp.tile(cap.astype(F32), (1, 2, 2))

    sink_rows = sinks.astype(F32).reshape(N_KV_HEADS, 2, 2).transpose(0, 2, 1).reshape(2 * N_KV_HEADS, 2)
    sink_rows = jnp.repeat(sink_rows, WINDOW, axis=1)
    return bias, cap, sink_rows


def _attn_prompt_kernel(x_ref, xres_ref, g_ref, wqkv_ref, wo_ref, bias_ref, cap_ref, sink_ref,
                        o_ref, kc_ref, vc_ref,
                        q0_scr, q1_scr, klo0_scr, klo1_scr, khi0_scr, khi1_scr, vt0_scr, vt1_scr,
                        a0_scr, a1_scr, *, tq, tiles_per_seq):
    s = pl.program_id(0)
    sets = ((q0_scr, klo0_scr, khi0_scr, vt0_scr, a0_scr),
            (q1_scr, klo1_scr, khi1_scr, vt1_scr, a1_scr))

    @pl.when(s == 0)
    def _():
        for scr in sets[0] + sets[1]:
            scr[...] = jnp.zeros(scr.shape, BF16)

    low = lax.broadcasted_iota(jnp.int32, (tq, LANES), 1) < HEAD_DIM
    n_col = D_MODEL // KV_DIM

    def step(new, old):
        q_new, klo_new, khi_new, vt_new, att_done = new
        q_old, klo_old, khi_old, vt_old, att_out = old

        starts_sequence = s % tiles_per_seq == 0
        klo_new[:, 0:WINDOW, :] = jnp.where(starts_sequence, 0.0, klo_old[:, tq:tq + WINDOW, :])
        khi_new[:, 0:WINDOW, :] = jnp.where(starts_sequence, 0.0, khi_old[:, tq:tq + WINDOW, :])
        vt_new[:, :, 0:WINDOW] = jnp.where(starts_sequence, 0.0, vt_old[:, :, tq:tq + WINDOW])
        h = _rms(x_ref[...], g_ref[...]).astype(BF16)

        def project_q(c):
            cols = slice(c * KV_DIM, (c + 1) * KV_DIM)
            q_new[cols, :] = (_dot(h, wqkv_ref[:, cols]) * ATTN_SCALE).T.astype(BF16)

        def project_k():
            k = _dot(h, wqkv_ref[:, D_MODEL:D_MODEL + KV_DIM])
            kc_ref[...] = k[tq - WINDOW:, :]
            for c in range(KV_DIM // LANES):
                kc = k[:, c * LANES:(c + 1) * LANES]
                kr = pltpu.roll(kc, HEAD_DIM, axis=1)
                klo_new[2 * c, WINDOW:, :] = jnp.where(low, kc, 0.0).astype(BF16)
                khi_new[2 * c, WINDOW:, :] = jnp.where(low, 0.0, kr).astype(BF16)
                klo_new[2 * c + 1, WINDOW:, :] = jnp.where(low, kr, 0.0).astype(BF16)
                khi_new[2 * c + 1, WINDOW:, :] = jnp.where(low, 0.0, kc).astype(BF16)

        def project_v():
            v = _dot(h, wqkv_ref[:, D_MODEL + KV_DIM:])
            vc_ref[...] = v[tq - WINDOW:, :]
            for c in range(KV_DIM // LANES):
                vt = v[:, c * LANES:(c + 1) * LANES].T.astype(BF16)
                vt_new[2 * c, :, WINDOW:] = vt[:HEAD_DIM]
                vt_new[2 * c + 1, :, WINDOW:] = vt[HEAD_DIM:]

        def project_out(c):
            cols = slice(c * KV_DIM, (c + 1) * KV_DIM)
            o_ref[:, cols] = xres_ref[:, cols] + _dot(att_done[...], wo_ref[:, cols])

        is_first = jnp.where((s - 1) % tiles_per_seq == 0, 1, 0)

        def scores(jb, kh):
            rows = slice(jb * WINDOW, (jb + 1) * WINDOW)
            band = slice(jb * WINDOW, (jb + 2) * WINDOW)
            cap = cap_ref[is_first] if jb == 0 else cap_ref[0]
            qst_t = jnp.concatenate([q_old[(2 * kh) * LANES:(2 * kh + 1) * LANES, rows],
                                     q_old[(2 * kh + 1) * LANES:(2 * kh + 2) * LANES, rows]], axis=1)
            kst = jnp.concatenate([klo_old[kh, band, :], khi_old[kh, band, :]], axis=0)
            return jnp.minimum(_dot(kst, qst_t) + bias_ref[kh], cap)

        def attend(jb, kh, sc):
            rows = slice(jb * WINDOW, (jb + 1) * WINDOW)
            vt = vt_old[kh, :, jb * WINDOW:(jb + 2) * WINDOW]
            halves = []
            for half in range(2):
                sh = sc[half * 2 * WINDOW:(half + 1) * 2 * WINDOW]
                sink = sink_ref[2 * kh + half:2 * kh + half + 1, :]
                m = jnp.maximum(jnp.max(sh, axis=0, keepdims=True), sink)
                p = jnp.exp(sh - m)
                denom = jnp.sum(p, axis=0, keepdims=True) + jnp.exp(sink - m)
                halves.append(_dot(vt, p.astype(BF16)) * (1.0 / denom))
            o = jnp.concatenate(halves, axis=0).T.astype(BF16)
            att_out[rows, (2 * kh) * LANES:(2 * kh + 1) * LANES] = o[:WINDOW]
            att_out[rows, (2 * kh + 1) * LANES:(2 * kh + 2) * LANES] = o[WINDOW:]

        pieces = [functools.partial(project_q, c) for c in range(n_col)] + [project_k, project_v]
        pieces += [functools.partial(project_out, c) for c in range(n_col)]
        steps = [(jb, kh) for jb in range(tq // WINDOW) for kh in range(N_KV_HEADS)]
        piece_at = {(n * len(steps)) // len(pieces): piece for n, piece in enumerate(pieces)}
        assert len(piece_at) == len(pieces)
        s_next = scores(*steps[0])
        for n, st in enumerate(steps):
            s_cur = s_next
            if n + 1 < len(steps):
                s_next = scores(*steps[n + 1])
            if n in piece_at:
                piece_at[n]()
            attend(*st, s_cur)

    @pl.when(s % 2 == 0)
    def _():
        step(sets[0], sets[1])

    @pl.when(s % 2 == 1)
    def _():
        step(sets[1], sets[0])


def _attn_prompt(x, g, w_qkv, w_o, layer, bias, cap, sink_rows, *, tq):
    b, seq, _ = x.shape
    tiles_per_seq = seq // tq
    n = b * tiles_per_seq
    x2 = x.reshape(b * seq, D_MODEL)
    lagged = pl.BlockSpec((tq, D_MODEL), lambda s: (jnp.maximum(s - 2, 0), 0))
    cache = pl.BlockSpec((None, WINDOW, KV_DIM),
                         lambda s: (jnp.minimum(s, n - 1) // tiles_per_seq, 0, 0))
    q_scr = pltpu.VMEM((D_MODEL, tq), BF16)
    a_scr = pltpu.VMEM((tq, D_MODEL), BF16)
    k_scr = pltpu.VMEM((N_KV_HEADS, WINDOW + tq, LANES), BF16)
    vt_scr = pltpu.VMEM((N_KV_HEADS, HEAD_DIM, WINDOW + tq), BF16)
    out, kc, vc = pl.pallas_call(
        functools.partial(_attn_prompt_kernel, tq=tq, tiles_per_seq=tiles_per_seq),
        out_shape=(jax.ShapeDtypeStruct((b * seq, D_MODEL), F32),
                   jax.ShapeDtypeStruct((b, WINDOW, KV_DIM), F32),
                   jax.ShapeDtypeStruct((b, WINDOW, KV_DIM), F32)),
        grid=(n + 2,),
        in_specs=[pl.BlockSpec((tq, D_MODEL), lambda s: (jnp.minimum(s, n - 1), 0)), lagged,
                  _const_spec((1, D_MODEL)),
                  _layer_spec(w_qkv.shape, layer), _layer_spec(w_o.shape, layer),
                  _const_spec(bias.shape), _const_spec(cap.shape), _const_spec(sink_rows.shape)],
        out_specs=(lagged, cache, cache),
        scratch_shapes=[q_scr, q_scr, k_scr, k_scr, k_scr, k_scr, vt_scr, vt_scr, a_scr, a_scr],
        compiler_params=_params("arbitrary"),
        name="attn_prompt",
    )(x2, x2, g, w_qkv, w_o, bias, cap, sink_rows)
    return out.reshape(b, seq, D_MODEL), kc, vc


ROWS_PAD = SUBLANES * Q_PER_KV
SAMPLE_GROUP = 16


def _sample_bias_tables(dist_bias, sinks):
    def rows(per_head):
        t = per_head.reshape(N_KV_HEADS, Q_PER_KV, -1).transpose(1, 0, 2)
        return jnp.pad(t, ((0, 0), (0, SUBLANES - N_KV_HEADS), (0, 0))).reshape(ROWS_PAD, -1)

    bias = rows(dist_bias[:0:-1].T)
    extra = rows(jnp.stack([dist_bias[0], sinks.astype(F32)], axis=1))
    return bias, jnp.pad(extra, ((0, 0), (0, LANES - 2)))


def _attn_sample_kernel(q_ref, kn_ref, vn_ref, kvt_ref, kc_ref, vc_ref, bias_ref, extra_ref,
                        *rest, tb, n_aliased, out_slot):
    o_ref, ko_ref, vo_ref = rest[n_aliased:]
    if out_slot is not None:
        for out_ref in (ko_ref, vo_ref):
            for slot in range(out_ref.shape[0]):
                if slot != out_slot:
                    out_ref[slot] = jnp.zeros(out_ref.shape[1:], F32)
        ko_ref, vo_ref = ko_ref.at[out_slot], vo_ref.at[out_slot]
    t = pl.program_id(0)
    sub = lax.broadcasted_iota(jnp.int32, (SUBLANES, KV_DIM), 0)
    lane_head = lax.broadcasted_iota(jnp.int32, (SUBLANES, KV_DIM), 1) // HEAD_DIM
    own = jnp.logical_and(sub < N_KV_HEADS, lane_head == sub)
    newest = lax.broadcasted_iota(jnp.int32, (KV_DIM, WINDOW), 1) == WINDOW - 1
    bias = bias_ref[...]
    bias_new = extra_ref[:, 0:1]
    sink = extra_ref[:, 1:2]

    def scores(group):
        s_old, s_new = [], []
        for bb in group:
            q_rows = []
            for gq in range(Q_PER_KV):
                q_g = q_ref[bb:bb + 1, gq * KV_DIM:(gq + 1) * KV_DIM] * ATTN_SCALE
                q_rows.append(jnp.where(own, jnp.broadcast_to(q_g, (SUBLANES, KV_DIM)), 0.0))
            q_blk = jnp.concatenate(q_rows, axis=0)
            s_old.append(_dot(q_blk.astype(BF16), kc_ref[bb].astype(BF16)) + bias)
            s_new.append(jnp.sum(q_blk * kn_ref[bb:bb + 1, :], axis=1, keepdims=True) + bias_new)
        return jnp.concatenate(s_old, axis=0), jnp.concatenate(s_new, axis=0)

    def attend(group, s_old, s_new):
        sinks = jnp.concatenate([sink] * len(group), axis=0)
        m = jnp.maximum(jnp.max(s_old, axis=1, keepdims=True), jnp.maximum(s_new, sinks))
        p_old = jnp.exp(s_old - m)
        p_new = jnp.exp(s_new - m)
        denom = jnp.sum(p_old, axis=1, keepdims=True) + p_new + jnp.exp(sinks - m)
        p_old = p_old.astype(BF16)
        for gi, bb in enumerate(group):
            rows = slice(gi * ROWS_PAD, (gi + 1) * ROWS_PAD)
            o = _dot_nt(p_old[rows], vc_ref[bb].astype(BF16)) + p_new[rows] * vn_ref[bb:bb + 1, :]
            o = o / denom[rows]
            for gq in range(Q_PER_KV):
                o_rows = o[SUBLANES * gq:SUBLANES * (gq + 1)]
                o_g = jnp.sum(jnp.where(own, o_rows, 0.0), axis=0, keepdims=True)
                o_ref[bb:bb + 1, gq * KV_DIM:(gq + 1) * KV_DIM] = o_g

    def shift(bb):
        to_last = (WINDOW - 1) - (t * tb + bb)
        for cache_ref, row0, out_ref in ((kc_ref, 0, ko_ref), (vc_ref, KV_DIM, vo_ref)):
            moved = pltpu.roll(cache_ref[bb], WINDOW - 1, axis=1)
            col = pltpu.roll(kvt_ref[row0:row0 + KV_DIM, :], to_last, axis=1)
            out_ref[bb] = jnp.where(newest, col, moved)

    for bb in range(tb):
        shift(bb)
    groups = [tuple(range(g, g + SAMPLE_GROUP)) for g in range(0, tb, SAMPLE_GROUP)]
    s_next = scores(groups[0])
    for n, group in enumerate(groups):
        s_cur = s_next
        if n + 1 < len(groups):
            s_next = scores(groups[n + 1])
        attend(group, *s_cur)


def _attn_sample_core(qkv, kv_t, k_caches, v_caches, bias, extra, layer, new_caches=None, *, tb):
    n = qkv.shape[0]
    cache = pl.BlockSpec((None, tb, KV_DIM, WINDOW), lambda t: (layer, t, 0, 0))
    kv_col = D_MODEL // KV_DIM
    args = [qkv, qkv, qkv, kv_t, k_caches, v_caches, bias, extra]
    in_specs = [pl.BlockSpec((tb, D_MODEL), lambda t: (t, 0)),
                pl.BlockSpec((tb, KV_DIM), lambda t: (t, kv_col)),
                pl.BlockSpec((tb, KV_DIM), lambda t: (t, kv_col + 1)),
                _const_spec(kv_t.shape), cache, cache,
                _const_spec(bias.shape), _const_spec(extra.shape)]
    if new_caches is None:
        aliases, out_slot = {}, layer
        cache_out = pl.BlockSpec((k_caches.shape[0], tb, KV_DIM, WINDOW), lambda t: (0, t, 0, 0))
    else:
        aliases, out_slot = {len(args): 1, len(args) + 1: 2}, None
        cache_out = cache
        args += list(new_caches)
        in_specs += [pl.BlockSpec(memory_space=pl.ANY)] * 2
    return pl.pallas_call(
        functools.partial(_attn_sample_kernel, tb=tb, n_aliased=len(aliases), out_slot=out_slot),
        out_shape=(jax.ShapeDtypeStruct((n, D_MODEL), F32),
                   jax.ShapeDtypeStruct(k_caches.shape, F32),
                   jax.ShapeDtypeStruct(v_caches.shape, F32)),
        grid=(n // tb,),
        in_specs=in_specs,
        out_specs=(pl.BlockSpec((tb, D_MODEL), lambda t: (t, 0)), cache_out, cache_out),
        input_output_aliases=aliases,
        compiler_params=_params("parallel"),
        name="attn_sample",
    )(*args)


def _sgu_kernel(x_ref, g_ref, win_ref, lg_ref, lb_ref, sp_ref, bsp_ref, wout_ref, *rest,
                tm, sample):
    def gelu_and_norm(z):
        z = 0.5 * z * (1.0 + lax.erf(z * INV_SQRT2))
        return z[:, :D_MODEL], _layer_norm(z[:, D_MODEL:], lg_ref[...], lb_ref[...])

    if sample:
        o_ref, v_ref = rest
        x = x_ref[...]
        u, v = gelu_and_norm(_dot(_rms(x, g_ref[...]).astype(BF16), win_ref[...]))
        v_ref[...] = v
        gated = (u * (v * sp_ref[...] + bsp_ref[...])).astype(BF16)
        o_ref[...] = x + _dot(gated, wout_ref[...])
        return

    o_ref, gated_scr = rest
    tp = tm // SGU_PARTS
    parts = [slice(r * tp, (r + 1) * tp) for r in range(SGU_PARTS)]
    zs = [_dot(_rms(x_ref[rows, :], g_ref[...]).astype(BF16), win_ref[...]) for rows in parts]
    for rows, z in zip(parts, zs):
        u, v = gelu_and_norm(z)
        vb = v.astype(BF16)
        for c in range(tp // CHUNK):
            chunk = slice(c * CHUNK, (c + 1) * CHUNK)
            dst = slice(rows.start + c * CHUNK, rows.start + (c + 1) * CHUNK)
            for gi in range(SGU_GROUPS):
                cols = slice(gi * LANES, (gi + 1) * LANES)
                mixed = _dot(sp_ref[gi], vb[chunk, cols]) + bsp_ref[gi]
                gated_scr[dst, cols] = (u[chunk, cols] * mixed).astype(BF16)
        o_ref[rows, :] = x_ref[rows, :] + _dot(gated_scr[rows, :], wout_ref[...])


def _sgu(x, g, w_in, ln_g, ln_b, sp, bsp, w_out, *, tm, sample):
    n = x.shape[0]
    row = pl.BlockSpec((tm, D_MODEL), lambda i: (i, 0))
    args = [x, g, w_in, ln_g, ln_b, sp, bsp, w_out]
    in_specs = [row] + [_const_spec(t.shape) for t in args[1:]]
    if sample:
        out_shape = (jax.ShapeDtypeStruct((n, D_MODEL), F32),) * 2
        out_specs = (row, row)
        scratch = []
    else:
        out_shape = jax.ShapeDtypeStruct((n, D_MODEL), F32)
        out_specs = row
        scratch = [pltpu.VMEM((tm, D_MODEL), BF16)]
    return pl.pallas_call(
        functools.partial(_sgu_kernel, tm=tm, sample=sample),
        out_shape=out_shape,
        grid=(n // tm,),
        in_specs=in_specs,
        out_specs=out_specs,
        scratch_shapes=scratch,
        compiler_params=_params("parallel"),
        name="sgu_sample" if sample else "sgu_prompt",
    )(*args)


CONV_ROWS = 64
CONV_LANES = 256
PROJ_LANES = 256


def _conv_prompt_kernel(x_ref, xprev_ref, g_ref, win_ref, wdw_ref, bdw_ref, lg_ref, lb_ref,
                        wout_ref, o_ref, tail_ref, a0_scr, a1_scr, sh_scr, c_scr,
                        *, tm, tiles_per_seq):
    s = pl.program_id(0)

    @pl.when(s == 0)
    def _():
        a0_scr[...] = jnp.zeros(a0_scr.shape, F32)
        a1_scr[...] = jnp.zeros(a1_scr.shape, F32)

    first_tap = CONV_HALO - (CONV_WIDTH - 1)
    reps = CONV_ROWS // SUBLANES

    def step(a_new, a_old):
        for r in range(1, SUBLANES):
            sh_scr[r - 1] = a_old[r:r + tm + CONV_HALO - SUBLANES, :]

        h = _rms(x_ref[...], g_ref[...]).astype(BF16)
        starts_sequence = s % tiles_per_seq == 0
        a_new[0:CONV_HALO, :] = jnp.where(starts_sequence, 0.0, a_old[tm:tm + CONV_HALO, :])

        def project(jc):
            cols = slice(jc * PROJ_LANES, (jc + 1) * PROJ_LANES)
            gate = slice(D_MODEL + jc * PROJ_LANES, D_MODEL + (jc + 1) * PROJ_LANES)
            a = _dot(h, win_ref[:, cols]) * jax.nn.sigmoid(_dot(h, win_ref[:, gate]))
            a_new[CONV_HALO:, cols] = a
            tail_ref[:, cols] = a[tm - CONV_HALO:, :]

        def convolve(rb):
            for lc in range(D_MODEL // CONV_LANES):
                cols = slice(lc * CONV_LANES, (lc + 1) * CONV_LANES)
                acc = jnp.concatenate([bdw_ref[:, cols]] * reps, axis=0)
                for kk in range(CONV_WIDTH):
                    whole, r = divmod(first_tap + kk, SUBLANES)
                    start = rb * CONV_ROWS + whole * SUBLANES
                    src = a_old if r == 0 else sh_scr.at[r - 1]
                    w = jnp.concatenate([wdw_ref[kk, :, cols]] * reps, axis=0)
                    acc = acc + src[start:start + CONV_ROWS, cols] * w
                c_scr[rb * CONV_ROWS:(rb + 1) * CONV_ROWS, cols] = acc

        n_proj = D_MODEL // PROJ_LANES
        n_conv = tm // CONV_ROWS
        for jc in range(n_proj):
            project(jc)
            for rb in range(jc * n_conv // n_proj, (jc + 1) * n_conv // n_proj):
                convolve(rb)
        c = _layer_norm(c_scr[...], lg_ref[...], lb_ref[...])
        c = (c * jax.nn.sigmoid(c)).astype(BF16)
        o_ref[...] = xprev_ref[...] + _dot(c, wout_ref[...])

    @pl.when(s % 2 == 0)
    def _():
        step(a0_scr, a1_scr)

    @pl.when(s % 2 == 1)
    def _():
        step(a1_scr, a0_scr)


def _conv_prompt(x, g, w_in, w_dw8, b_dw8, ln_g, ln_b, w_out, *, tm):
    b, seq, _ = x.shape
    tiles_per_seq = seq // tm
    n = b * tiles_per_seq
    consts = [g, w_in, w_dw8, b_dw8, ln_g, ln_b, w_out]
    a_scr = pltpu.VMEM((CONV_HALO + tm, D_MODEL), F32)
    x2 = x.reshape(b * seq, D_MODEL)
    out, tail = pl.pallas_call(
        functools.partial(_conv_prompt_kernel, tm=tm, tiles_per_seq=tiles_per_seq),
        out_shape=(jax.ShapeDtypeStruct((b * seq, D_MODEL), F32),
                   jax.ShapeDtypeStruct((b, CONV_HALO, D_MODEL), F32)),
        grid=(n + 1,),
        in_specs=[pl.BlockSpec((tm, D_MODEL), lambda s: (jnp.minimum(s, n - 1), 0)),
                  pl.BlockSpec((tm, D_MODEL), lambda s: (jnp.maximum(s - 1, 0), 0))]
                 + [_const_spec(t.shape) for t in consts],
        out_specs=(pl.BlockSpec((tm, D_MODEL), lambda s: (jnp.maximum(s - 1, 0), 0)),
                   pl.BlockSpec((None, CONV_HALO, D_MODEL),
                                lambda s: (jnp.minimum(s, n - 1) // tiles_per_seq, 0, 0))),
        scratch_shapes=[a_scr, a_scr,
                        pltpu.VMEM((SUBLANES - 1, CONV_HALO + tm - SUBLANES, D_MODEL), F32),
                        pltpu.VMEM((tm, D_MODEL), F32)],
        compiler_params=_params("arbitrary"),
        name="conv_prompt",
    )(x2, x2, *consts)
    return out.reshape(b, seq, D_MODEL), tail


def _conv_sample_kernel(a_ref, st_ref, wdw_ref, bdw_ref, c_ref, so_ref):
    n_hist = CONV_WIDTH - 1
    a = a_ref[...]
    acc = a * wdw_ref[n_hist:CONV_WIDTH, :] + bdw_ref[...]
    for k in range(n_hist):
        acc = acc + st_ref[k] * wdw_ref[k:k + 1, :]
    c_ref[...] = acc
    so_ref[0:n_hist - 1] = st_ref[1:n_hist]
    so_ref[n_hist - 1] = a


def _conv_sample_core(a, states, w_dw, b_dw, layer, *, tb):
    n = a.shape[0]
    row = pl.BlockSpec((tb, D_MODEL), lambda t: (t, 0))
    st_in = pl.BlockSpec((None, CONV_WIDTH - 1, tb, D_MODEL), lambda t: (layer, 0, t, 0))
    st_out = pl.BlockSpec((CONV_WIDTH - 1, tb, D_MODEL), lambda t: (0, t, 0))
    return pl.pallas_call(
        _conv_sample_kernel,
        out_shape=(jax.ShapeDtypeStruct((n, D_MODEL), F32),
                   jax.ShapeDtypeStruct(states.shape[1:], F32)),
        grid=(n // tb,),
        in_specs=[row, st_in, _const_spec(w_dw.shape), _const_spec(b_dw.shape)],
        out_specs=(row, st_out),
        compiler_params=_params("parallel"),
        name="conv_sample",
    )(a, states, w_dw, b_dw)


PROMPT_TILE = 512
SGU_TILE = 1024
FFN_TILE = 1024
SAMPLE_ATTN_TILE = 16
SAMPLE_CONV_TILE = 32


def kernel(x_prompt, x_sample, cache_swa_k, cache_swa_v, state_conv, rel_bias, norm_mix, norm_ffn, norm_final, attn_w_qkv, attn_w_o, attn_sinks, sgu_w_in, sgu_ln_g, sgu_ln_b, sgu_w_spatial, sgu_b_spatial, sgu_w_out, conv_w_in, conv_w_dw, conv_b_dw, conv_ln_g, conv_ln_b, conv_w_out, ffn_w_up, ffn_w_down):
    batch, seq, _ = x_prompt.shape
    dec = x_sample.shape[0]
    depth = norm_mix.shape[0]
    assert x_prompt.shape[2] == D_MODEL and x_sample.shape[1:] == (1, D_MODEL)
    assert seq % PROMPT_TILE == 0 and (batch * seq) % FFN_TILE == 0 and seq % SGU_TILE == 0
    assert dec % SAMPLE_ATTN_TILE == 0 and dec % SAMPLE_CONV_TILE == 0 and dec == WINDOW
    assert cache_swa_k.shape[2:] == (WINDOW, N_KV_HEADS, HEAD_DIM)
    assert state_conv.shape[2:] == (CONV_WIDTH - 1, D_MODEL)
    mixer_of_layer = tuple(i % 3 for i in range(depth))
    slot_of_layer = tuple(mixer_of_layer[:i].count(mixer_of_layer[i]) for i in range(depth))

    def row(v):
        return v.reshape(1, -1).astype(F32)

    n_attn = attn_w_qkv.shape[0]

    def caches_t(c):
        return jnp.transpose(c, (0, 1, 3, 4, 2)).reshape(n_attn, dec, KV_DIM, WINDOW)

    def caches_from_t(c):
        c = c.reshape(n_attn, dec, N_KV_HEADS, HEAD_DIM, WINDOW)
        return jnp.transpose(c, (0, 1, 4, 2, 3))

    xp = x_prompt
    xs = x_sample.reshape(dec, D_MODEL)
    dist_bias = _distance_bias(rel_bias)
    k_caches_t, v_caches_t = caches_t(cache_swa_k), caches_t(cache_swa_v)
    states = jnp.transpose(state_conv.astype(F32), (0, 2, 1, 3))
    w_qkv_all = attn_w_qkv.astype(BF16)
    w_o_all = attn_w_o.astype(BF16)

    kp, vp, sgu_v_new, convp, convs = [], [], [], [], []
    new_caches = None
    for i in range(depth):
        m, j = mixer_of_layer[i], slot_of_layer[i]
        g_mix = row(norm_mix[i])
        if m == 0:
            bias_p, cap_p, sink_rows = _prompt_bias_tables(dist_bias, attn_sinks[j])
            xp, k1, v1 = _attn_prompt(xp, g_mix, w_qkv_all, w_o_all, j, bias_p, cap_p, sink_rows,
                                      tq=PROMPT_TILE)
            kp.append(k1.reshape(batch, WINDOW, N_KV_HEADS, HEAD_DIM))
            vp.append(v1.reshape(batch, WINDOW, N_KV_HEADS, HEAD_DIM))

            bias_s, extra_s = _sample_bias_tables(dist_bias, attn_sinks[j])
            qkv_s, kv_t = _norm_proj(xs, g_mix, w_qkv_all, j, kv_t=True)
            o_s, *new_caches = _attn_sample_core(qkv_s, kv_t, k_caches_t, v_caches_t, bias_s,
                                                 extra_s, j, new_caches, tb=SAMPLE_ATTN_TILE)
            xs = _proj_res(xs, o_s, w_o_all, j, heads_grouped=True)
        elif m == 1:
            w_in = sgu_w_in[j].astype(BF16)
            w_out = sgu_w_out[j].astype(BF16)
            ln_g, ln_b = row(sgu_ln_g[j]), row(sgu_ln_b[j])
            sp = jnp.tril(sgu_w_spatial[j]).astype(BF16)
            bsp = jnp.broadcast_to(sgu_b_spatial[j].astype(F32)[:, :, None],
                                   (SGU_GROUPS, CHUNK, LANES))
            xp = _sgu(xp.reshape(batch * seq, D_MODEL), g_mix, w_in, ln_g, ln_b, sp, bsp, w_out,
                      tm=SGU_TILE, sample=False).reshape(batch, seq, D_MODEL)
            sp0 = row(jnp.repeat(sgu_w_spatial[j][:, 0, 0], LANES))
            bsp0 = row(jnp.repeat(sgu_b_spatial[j][:, 0], LANES))
            xs, v_rows = _sgu(xs, g_mix, w_in, ln_g, ln_b, sp0, bsp0, w_out, tm=dec, sample=True)
            sgu_v_new.append(v_rows.reshape(dec, 1, D_MODEL))
        else:
            w_in = conv_w_in[j].astype(BF16)
            w_out = conv_w_out[j].astype(BF16)
            w_dw = conv_w_dw[j].astype(F32)
            b_dw = row(conv_b_dw[j])
            ln_g, ln_b = row(conv_ln_g[j]), row(conv_ln_b[j])
            w_dw8 = jnp.broadcast_to(w_dw[:, None, :], (CONV_WIDTH, SUBLANES, D_MODEL))
            b_dw8 = jnp.broadcast_to(b_dw, (SUBLANES, D_MODEL))
            xp, tail = _conv_prompt(xp, g_mix, w_in, w_dw8, b_dw8, ln_g, ln_b, w_out,
                                    tm=PROMPT_TILE)
            convp.append(tail[:, CONV_HALO - (CONV_WIDTH - 1):, :])
            a_s = _norm_proj(xs, g_mix, w_in, glu=True)
            c_s, st2 = _conv_sample_core(a_s, states, w_dw, b_dw, j, tb=SAMPLE_CONV_TILE)
            xs = _proj_res(xs, c_s, w_out, ln=(ln_g, ln_b))
            convs.append(jnp.transpose(st2, (1, 0, 2)))

        g_fin = row(norm_final) if i == depth - 1 else None
        xp, xs = _ffn(xp.reshape(batch * seq, D_MODEL), xs, row(norm_ffn[i]), ffn_w_up, ffn_w_down,
                      i, g_fin, tm=FFN_TILE)
        xp = xp.reshape(batch, seq, D_MODEL)

    y_prompt = xp
    y_sample = xs.reshape(dec, 1, D_MODEL)
    k_new, v_new = (caches_from_t(c) for c in new_caches)
    return (y_prompt, y_sample, jnp.stack(kp), jnp.stack(vp), k_new, v_new,
            jnp.stack(sgu_v_new), jnp.stack(convp), jnp.stack(convs))
```

```python
import functools
import math

import jax
import jax.numpy as jnp
from jax import lax
from jax.experimental import pallas as pl
from jax.experimental.pallas import tpu as pltpu

D_MODEL = 1024
HEAD_DIM = 64
N_HEADS = 16
N_KV_HEADS = 4
Q_PER_KV = 4
KV_DIM = N_KV_HEADS * HEAD_DIM
WINDOW = 128
ATTN_SCALE = HEAD_DIM ** -0.5
N_BUCKETS = 32
MAX_DISTANCE = 128
CHUNK = 128
SGU_GROUPS = 8
SGU_PARTS = 2
CONV_WIDTH = 31
CONV_HALO = 32
D_FF = 4 * D_MODEL
EPS = 1e-6
NEG_INF = -1e30
F32_MAX = float(jnp.finfo(jnp.float32).max)
INV_SQRT2 = 1.0 / math.sqrt(2.0)

LANES = 128
SUBLANES = 8
VMEM_LIMIT = 56 * 1024 * 1024

F32 = jnp.float32
BF16 = jnp.bfloat16


def _const_spec(shape):
    n = len(shape)
    return pl.BlockSpec(shape, lambda *_: (0,) * n, pipeline_mode=pl.Buffered(1))


def _params(*sem):
    return pltpu.CompilerParams(dimension_semantics=sem, vmem_limit_bytes=VMEM_LIMIT)


def _rms(x, g):
    return x * lax.rsqrt(jnp.mean(x * x, axis=-1, keepdims=True) + EPS) * g


def _layer_norm(x, g, b):
    mu = jnp.mean(x, axis=-1, keepdims=True)
    xc = x - mu
    var = jnp.mean(xc * xc, axis=-1, keepdims=True)
    return xc * lax.rsqrt(var + EPS) * g + b


def _dot(a, b):
    return jnp.dot(a, b, preferred_element_type=F32)


def _dot_nt(a, b):
    return lax.dot_general(a, b, (((1,), (1,)), ((), ())), preferred_element_type=F32)


FFN_CHUNK = 512
N_FFN_CHUNKS = D_FF // FFN_CHUNK


def _layer_spec(shape, layer):
    zeros = (0,) * (len(shape) - 1)
    return pl.BlockSpec((None,) + tuple(shape[1:]), lambda *_: (layer,) + zeros,
                        pipeline_mode=pl.Buffered(1))


def _ffn_kernel(xp_ref, xs_ref, g_ref, wu_ref, wd_ref, *rest, final, n_prompt_tiles):
    *rest, wu_scr, wd_scr = rest
    if final:
        gf_ref, op_ref, os_ref = rest
    else:
        op_ref, os_ref = rest

    def block(x_ref, o_ref):
        x = x_ref[...]
        h = _rms(x, g_ref[...]).astype(BF16)
        y = x
        for c in range(N_FFN_CHUNKS):
            u = _dot(h, wu_scr[c])
            u = jnp.square(jnp.maximum(u, 0.0)).astype(BF16)
            y = y + _dot(u, wd_scr[c])
        o_ref[...] = _rms(y, gf_ref[...]) if final else y

    i = pl.program_id(0)
    tile = i - N_FFN_CHUNKS

    @pl.when(i < N_FFN_CHUNKS)
    def _():
        wu_scr[i] = wu_ref[...].astype(BF16)
        wd_scr[i] = wd_ref[...].astype(BF16)

    @pl.when(jnp.logical_and(tile >= 0, tile < n_prompt_tiles))
    def _():
        block(xp_ref, op_ref)

    @pl.when(tile == n_prompt_tiles)
    def _():
        block(xs_ref, os_ref)


def _ffn(xp, xs, g, w_up, w_down, layer, g_final=None, *, tm):
    n = xp.shape[0]
    nt = n // tm
    last_chunk = N_FFN_CHUNKS - 1
    row = pl.BlockSpec((tm, D_MODEL), lambda i: (jnp.clip(i - N_FFN_CHUNKS, 0, nt - 1), 0))
    in_specs = [row, _const_spec(xs.shape), _const_spec((1, D_MODEL)),
                pl.BlockSpec((None, D_MODEL, FFN_CHUNK),
                             lambda i: (layer, 0, jnp.minimum(i, last_chunk))),
                pl.BlockSpec((None, FFN_CHUNK, D_MODEL),
                             lambda i: (layer, jnp.minimum(i, last_chunk), 0))]
    args = [xp, xs, g, w_up, w_down]
    if g_final is not None:
        in_specs.append(_const_spec((1, D_MODEL)))
        args.append(g_final)
    return pl.pallas_call(
        functools.partial(_ffn_kernel, final=g_final is not None, n_prompt_tiles=nt),
        out_shape=(jax.ShapeDtypeStruct((n, D_MODEL), F32),
                   jax.ShapeDtypeStruct(xs.shape, F32)),
        grid=(N_FFN_CHUNKS + nt + 1,),
        in_specs=in_specs,
        out_specs=(row, pl.BlockSpec(xs.shape, lambda i: (0, 0))),
        scratch_shapes=[pltpu.VMEM((N_FFN_CHUNKS, D_MODEL, FFN_CHUNK), BF16),
                        pltpu.VMEM((N_FFN_CHUNKS, FFN_CHUNK, D_MODEL), BF16)],
        compiler_params=_params("arbitrary"),
        name="ffn_final" if g_final is not None else "ffn",
    )(*args)


def _regroup_heads(t, group_major):
    blocks = range(N_HEADS)
    if group_major:
        order = [N_KV_HEADS * (blk % N_KV_HEADS) + blk // N_KV_HEADS for blk in blocks]
    else:
        order = [Q_PER_KV * (blk % Q_PER_KV) + blk // Q_PER_KV for blk in blocks]
    return jnp.concatenate([t[:, h * HEAD_DIM:(h + 1) * HEAD_DIM] for h in order], axis=1)


def _norm_proj_kernel(x_ref, g_ref, w_ref, o_ref, *rest, glu, kv_t):
    h = _rms(x_ref[...], g_ref[...]).astype(BF16)
    y = _dot(h, w_ref[...])
    if glu:
        half = y.shape[1] // 2
        y = y[:, :half] * jax.nn.sigmoid(y[:, half:])
    if kv_t:
        (t_ref,) = rest
        o_ref[:, :D_MODEL] = _regroup_heads(y[:, :D_MODEL], group_major=True)
        o_ref[:, D_MODEL:] = y[:, D_MODEL:]
        t_ref[...] = y[:, D_MODEL:].T
    else:
        o_ref[...] = y


def _weight_spec(w, layer):
    return _const_spec(w.shape) if layer is None else _layer_spec(w.shape, layer)


def _norm_proj(x, g, w, layer=None, *, glu=False, kv_t=False):
    n = x.shape[0]
    n_out = w.shape[-1] // 2 if glu else w.shape[-1]
    out_shape = [jax.ShapeDtypeStruct((n, n_out), F32)]
    if kv_t:
        out_shape.append(jax.ShapeDtypeStruct((n_out - D_MODEL, n), F32))
    out = pl.pallas_call(
        functools.partial(_norm_proj_kernel, glu=glu, kv_t=kv_t),
        out_shape=tuple(out_shape),
        grid=(1,),
        in_specs=[_const_spec(x.shape), _const_spec(g.shape), _weight_spec(w, layer)],
        out_specs=tuple(_const_spec(o.shape) for o in out_shape),
        compiler_params=_params("arbitrary"),
        name="norm_proj_glu" if glu else "norm_proj",
    )(x, g, w)
    return out if kv_t else out[0]


def _proj_res_kernel(x_ref, a_ref, w_ref, *rest, conv_tail, heads_grouped):
    a = a_ref[...]
    if conv_tail:
        lg_ref, lb_ref, o_ref = rest
        a = _layer_norm(a, lg_ref[...], lb_ref[...])
        a = a * jax.nn.sigmoid(a)
    else:
        (o_ref,) = rest
    if heads_grouped:
        a = _regroup_heads(a, group_major=False)
    o_ref[...] = x_ref[...] + _dot(a.astype(BF16), w_ref[...])


def _proj_res(x, a, w, layer=None, ln=None, heads_grouped=False):
    args = [x, a, w] + (list(ln) if ln is not None else [])
    in_specs = [_const_spec(t.shape) for t in args]
    in_specs[2] = _weight_spec(w, layer)
    return pl.pallas_call(
        functools.partial(_proj_res_kernel, conv_tail=ln is not None, heads_grouped=heads_grouped),
        out_shape=jax.ShapeDtypeStruct(x.shape, F32),
        grid=(1,),
        in_specs=in_specs,
        out_specs=_const_spec(x.shape),
        compiler_params=_params("arbitrary"),
        name="proj_res_ln" if ln is not None else "proj_res",
    )(*args)


def _t5_bucket(dist):
    n = jnp.maximum(dist, 0)
    max_exact = N_BUCKETS // 2
    nf = jnp.maximum(n, 1).astype(F32)
    large = max_exact + (jnp.log(nf / max_exact) / math.log(MAX_DISTANCE / max_exact)
                         * (N_BUCKETS - max_exact)).astype(jnp.int32)
    large = jnp.minimum(large, N_BUCKETS - 1)
    return jnp.where(n < max_exact, n, large)


def _distance_bias(rel_bias):
    buckets = _t5_bucket(jnp.arange(WINDOW + 1, dtype=jnp.int32))
    onehot = (buckets[:, None] == jnp.arange(N_BUCKETS, dtype=jnp.int32)[None, :]).astype(F32)
    return jnp.dot(onehot, rel_bias.astype(F32), precision=lax.Precision.HIGHEST)


def _prompt_bias_tables(dist_bias, sinks):
    period = 3 * WINDOW
    line = jnp.concatenate([jnp.broadcast_to(dist_bias[WINDOW:], (WINDOW - 1, N_HEADS)),
                            dist_bias[::-1],
                            jnp.broadcast_to(dist_bias[:1], (WINDOW, N_HEADS))]).T
    skew = jnp.tile(line, (1, WINDOW))[:, :WINDOW * (period - 1)]
    per_head = skew.reshape(N_HEADS, WINDOW, period - 1)[:, :, WINDOW - 1:period - 1]
    per_head = per_head.reshape(N_KV_HEADS, 2, 2, WINDOW, 2 * WINDOW)
    bias = per_head.transpose(0, 2, 4, 1, 3).reshape(N_KV_HEADS, 4 * WINDOW, 2 * WINDOW)

    qi = jnp.arange(WINDOW, dtype=jnp.int32)[None, :]
    kj = jnp.arange(2 * WINDOW, dtype=jnp.int32)[:, None]
    dist = qi - kj + WINDOW
    allowed = (dist >= 0) & (dist <= WINDOW)
    first = allowed & (kj >= WINDOW)
    cap = jnp.stack([jnp.where(allowed, F32_MAX, NEG_INF), jnp.where(first, F32_MAX, NEG_INF)])
    cap = jnp.tile(cap.astype(F32), (1, 2, 2))

    sink_rows = sinks.astype(F32).reshape(N_KV_HEADS, 2, 2).transpose(0, 2, 1).reshape(2 * N_KV_HEADS, 2)
    sink_rows = jnp.repeat(sink_rows, WINDOW, axis=1)
    return bias, cap, sink_rows


def _attn_prompt_kernel(x_ref, xres_ref, g_ref, wqkv_ref, wo_ref, bias_ref, cap_ref, sink_ref,
                        o_ref, kc_ref, vc_ref,
                        q0_scr, q1_scr, klo0_scr, klo1_scr, khi0_scr, khi1_scr, vt0_scr, vt1_scr,
                        a0_scr, a1_scr, *, tq, tiles_per_seq):
    s = pl.program_id(0)
    sets = ((q0_scr, klo0_scr, khi0_scr, vt0_scr, a0_scr),
            (q1_scr, klo1_scr, khi1_scr, vt1_scr, a1_scr))

    @pl.when(s == 0)
    def _():
        for scr in sets[0] + sets[1]:
            scr[...] = jnp.zeros(scr.shape, BF16)

    low = lax.broadcasted_iota(jnp.int32, (tq, LANES), 1) < HEAD_DIM
    n_col = D_MODEL // KV_DIM

    def step(new, old):
        q_new, klo_new, khi_new, vt_new, att_done = new
        q_old, klo_old, khi_old, vt_old, att_out = old

        starts_sequence = s % tiles_per_seq == 0
        klo_new[:, 0:WINDOW, :] = jnp.where(starts_sequence, 0.0, klo_old[:, tq:tq + WINDOW, :])
        khi_new[:, 0:WINDOW, :] = jnp.where(starts_sequence, 0.0, khi_old[:, tq:tq + WINDOW, :])
        vt_new[:, :, 0:WINDOW] = jnp.where(starts_sequence, 0.0, vt_old[:, :, tq:tq + WINDOW])
        h = _rms(x_ref[...], g_ref[...]).astype(BF16)

        def project_q(c):
            cols = slice(c * KV_DIM, (c + 1) * KV_DIM)
            q_new[cols, :] = (_dot(h, wqkv_ref[:, cols]) * ATTN_SCALE).T.astype(BF16)

        def project_k():
            k = _dot(h, wqkv_ref[:, D_MODEL:D_MODEL + KV_DIM])
            kc_ref[...] = k[tq - WINDOW:, :]
            for c in range(KV_DIM // LANES):
                kc = k[:, c * LANES:(c + 1) * LANES]
                kr = pltpu.roll(kc, HEAD_DIM, axis=1)
                klo_new[2 * c, WINDOW:, :] = jnp.where(low, kc, 0.0).astype(BF16)
                khi_new[2 * c, WINDOW:, :] = jnp.where(low, 0.0, kr).astype(BF16)
                klo_new[2 * c + 1, WINDOW:, :] = jnp.where(low, kr, 0.0).astype(BF16)
                khi_new[2 * c + 1, WINDOW:, :] = jnp.where(low, 0.0, kc).astype(BF16)

        def project_v():
            v = _dot(h, wqkv_ref[:, D_MODEL + KV_DIM:])
            vc_ref[...] = v[tq - WINDOW:, :]
            for c in range(KV_DIM // LANES):
                vt = v[:, c * LANES:(c + 1) * LANES].T.astype(BF16)
                vt_new[2 * c, :, WINDOW:] = vt[:HEAD_DIM]
                vt_new[2 * c + 1, :, WINDOW:] = vt[HEAD_DIM:]

        def project_out(c):
            cols = slice(c * KV_DIM, (c + 1) * KV_DIM)
            o_ref[:, cols] = xres_ref[:, cols] + _dot(att_done[...], wo_ref[:, cols])

        is_first = jnp.where((s - 1) % tiles_per_seq == 0, 1, 0)

        def scores(jb, kh):
            rows = slice(jb * WINDOW, (jb + 1) * WINDOW)
            band = slice(jb * WINDOW, (jb + 2) * WINDOW)
            cap = cap_ref[is_first] if jb == 0 else cap_ref[0]
            qst_t = jnp.concatenate([q_old[(2 * kh) * LANES:(2 * kh + 1) * LANES, rows],
                                     q_old[(2 * kh + 1) * LANES:(2 * kh + 2) * LANES, rows]], axis=1)
            kst = jnp.concatenate([klo_old[kh, band, :], khi_old[kh, band, :]], axis=0)
            return jnp.minimum(_dot(kst, qst_t) + bias_ref[kh], cap)

        def attend(jb, kh, sc):
            rows = slice(jb * WINDOW, (jb + 1) * WINDOW)
            vt = vt_old[kh, :, jb * WINDOW:(jb + 2) * WINDOW]
            halves = []
            for half in range(2):
                sh = sc[half * 2 * WINDOW:(half + 1) * 2 * WINDOW]
                sink = sink_ref[2 * kh + half:2 * kh + half + 1, :]
                m = jnp.maximum(jnp.max(sh, axis=0, keepdims=True), sink)
                p = jnp.exp(sh - m)
                denom = jnp.sum(p, axis=0, keepdims=True) + jnp.exp(sink - m)
                halves.append(_dot(vt, p.astype(BF16)) * (1.0 / denom))
            o = jnp.concatenate(halves, axis=0).T.astype(BF16)
            att_out[rows, (2 * kh) * LANES:(2 * kh + 1) * LANES] = o[:WINDOW]
            att_out[rows, (2 * kh + 1) * LANES:(2 * kh + 2) * LANES] = o[WINDOW:]

        pieces = [functools.partial(project_q, c) for c in range(n_col)] + [project_k, project_v]
        pieces += [functools.partial(project_out, c) for c in range(n_col)]
        steps = [(jb, kh) for jb in range(tq // WINDOW) for kh in range(N_KV_HEADS)]
        piece_at = {(n * len(steps)) // len(pieces): piece for n, piece in enumerate(pieces)}
        assert len(piece_at) == len(pieces)
        s_next = scores(*steps[0])
        for n, st in enumerate(steps):
            s_cur = s_next
            if n + 1 < len(steps):
                s_next = scores(*steps[n + 1])
            if n in piece_at:
                piece_at[n]()
            attend(*st, s_cur)

    @pl.when(s % 2 == 0)
    def _():
        step(sets[0], sets[1])

    @pl.when(s % 2 == 1)
    def _():
        step(sets[1], sets[0])


def _attn_prompt(x, g, w_qkv, w_o, layer, bias, cap, sink_rows, *, tq):
    b, seq, _ = x.shape
    tiles_per_seq = seq // tq
    n = b * tiles_per_seq
    x2 = x.reshape(b * seq, D_MODEL)
    lagged = pl.BlockSpec((tq, D_MODEL), lambda s: (jnp.maximum(s - 2, 0), 0))
    cache = pl.BlockSpec((None, WINDOW, KV_DIM),
                         lambda s: (jnp.minimum(s, n - 1) // tiles_per_seq, 0, 0))
    q_scr = pltpu.VMEM((D_MODEL, tq), BF16)
    a_scr = pltpu.VMEM((tq, D_MODEL), BF16)
    k_scr = pltpu.VMEM((N_KV_HEADS, WINDOW + tq, LANES), BF16)
    vt_scr = pltpu.VMEM((N_KV_HEADS, HEAD_DIM, WINDOW + tq), BF16)
    out, kc, vc = pl.pallas_call(
        functools.partial(_attn_prompt_kernel, tq=tq, tiles_per_seq=tiles_per_seq),
        out_shape=(jax.ShapeDtypeStruct((b * seq, D_MODEL), F32),
                   jax.ShapeDtypeStruct((b, WINDOW, KV_DIM), F32),
                   jax.ShapeDtypeStruct((b, WINDOW, KV_DIM), F32)),
        grid=(n + 2,),
        in_specs=[pl.BlockSpec((tq, D_MODEL), lambda s: (jnp.minimum(s, n - 1), 0)), lagged,
                  _const_spec((1, D_MODEL)),
                  _layer_spec(w_qkv.shape, layer), _layer_spec(w_o.shape, layer),
                  _const_spec(bias.shape), _const_spec(cap.shape), _const_spec(sink_rows.shape)],
        out_specs=(lagged, cache, cache),
        scratch_shapes=[q_scr, q_scr, k_scr, k_scr, k_scr, k_scr, vt_scr, vt_scr, a_scr, a_scr],
        compiler_params=_params("arbitrary"),
        name="attn_prompt",
    )(x2, x2, g, w_qkv, w_o, bias, cap, sink_rows)
    return out.reshape(b, seq, D_MODEL), kc, vc


ROWS_PAD = SUBLANES * Q_PER_KV
SAMPLE_GROUP = 16


def _sample_bias_tables(dist_bias, sinks):
    def rows(per_head):
        t = per_head.reshape(N_KV_HEADS, Q_PER_KV, -1).transpose(1, 0, 2)
        return jnp.pad(t, ((0, 0), (0, SUBLANES - N_KV_HEADS), (0, 0))).reshape(ROWS_PAD, -1)

    bias = rows(dist_bias[:0:-1].T)
    extra = rows(jnp.stack([dist_bias[0], sinks.astype(F32)], axis=1))
    return bias, jnp.pad(extra, ((0, 0), (0, LANES - 2)))


def _attn_sample_kernel(q_ref, kn_ref, vn_ref, kvt_ref, kc_ref, vc_ref, bias_ref, extra_ref,
                        *rest, tb, n_aliased, out_slot):
    o_ref, ko_ref, vo_ref = rest[n_aliased:]
    if out_slot is not None:
        for out_ref in (ko_ref, vo_ref):
            for slot in range(out_ref.shape[0]):
                if slot != out_slot:
                    out_ref[slot] = jnp.zeros(out_ref.shape[1:], F32)
        ko_ref, vo_ref = ko_ref.at[out_slot], vo_ref.at[out_slot]
    t = pl.program_id(0)
    sub = lax.broadcasted_iota(jnp.int32, (SUBLANES, KV_DIM), 0)
    lane_head = lax.broadcasted_iota(jnp.int32, (SUBLANES, KV_DIM), 1) // HEAD_DIM
    own = jnp.logical_and(sub < N_KV_HEADS, lane_head == sub)
    newest = lax.broadcasted_iota(jnp.int32, (KV_DIM, WINDOW), 1) == WINDOW - 1
    bias = bias_ref[...]
    bias_new = extra_ref[:, 0:1]
    sink = extra_ref[:, 1:2]

    def scores(group):
        s_old, s_new = [], []
        for bb in group:
            q_rows = []
            for gq in range(Q_PER_KV):
                q_g = q_ref[bb:bb + 1, gq * KV_DIM:(gq + 1) * KV_DIM] * ATTN_SCALE
                q_rows.append(jnp.where(own, jnp.broadcast_to(q_g, (SUBLANES, KV_DIM)), 0.0))
            q_blk = jnp.concatenate(q_rows, axis=0)
            s_old.append(_dot(q_blk.astype(BF16), kc_ref[bb].astype(BF16)) + bias)
            s_new.append(jnp.sum(q_blk * kn_ref[bb:bb + 1, :], axis=1, keepdims=True) + bias_new)
        return jnp.concatenate(s_old, axis=0), jnp.concatenate(s_new, axis=0)

    def attend(group, s_old, s_new):
        sinks = jnp.concatenate([sink] * len(group), axis=0)
        m = jnp.maximum(jnp.max(s_old, axis=1, keepdims=True), jnp.maximum(s_new, sinks))
        p_old = jnp.exp(s_old - m)
        p_new = jnp.exp(s_new - m)
        denom = jnp.sum(p_old, axis=1, keepdims=True) + p_new + jnp.exp(sinks - m)
        p_old = p_old.astype(BF16)
        for gi, bb in enumerate(group):
            rows = slice(gi * ROWS_PAD, (gi + 1) * ROWS_PAD)
            o = _dot_nt(p_old[rows], vc_ref[bb].astype(BF16)) + p_new[rows] * vn_ref[bb:bb + 1, :]
            o = o / denom[rows]
            for gq in range(Q_PER_KV):
                o_rows = o[SUBLANES * gq:SUBLANES * (gq + 1)]
                o_g = jnp.sum(jnp.where(own, o_rows, 0.0), axis=0, keepdims=True)
                o_ref[bb:bb + 1, gq * KV_DIM:(gq + 1) * KV_DIM] = o_g

    def shift(bb):
        to_last = (WINDOW - 1) - (t * tb + bb)
        for cache_ref, row0, out_ref in ((kc_ref, 0, ko_ref), (vc_ref, KV_DIM, vo_ref)):
            moved = pltpu.roll(cache_ref[bb], WINDOW - 1, axis=1)
            col = pltpu.roll(kvt_ref[row0:row0 + KV_DIM, :], to_last, axis=1)
            out_ref[bb] = jnp.where(newest, col, moved)

    for bb in range(tb):
        shift(bb)
    groups = [tuple(range(g, g + SAMPLE_GROUP)) for g in range(0, tb, SAMPLE_GROUP)]
    s_next = scores(groups[0])
    for n, group in enumerate(groups):
        s_cur = s_next
        if n + 1 < len(groups):
            s_next = scores(groups[n + 1])
        attend(group, *s_cur)


def _attn_sample_core(qkv, kv_t, k_caches, v_caches, bias, extra, layer, new_caches=None, *, tb):
    n = qkv.shape[0]
    cache = pl.BlockSpec((None, tb, KV_DIM, WINDOW), lambda t: (layer, t, 0, 0))
    kv_col = D_MODEL // KV_DIM
    args = [qkv, qkv, qkv, kv_t, k_caches, v_caches, bias, extra]
    in_specs = [pl.BlockSpec((tb, D_MODEL), lambda t: (t, 0)),
                pl.BlockSpec((tb, KV_DIM), lambda t: (t, kv_col)),
                pl.BlockSpec((tb, KV_DIM), lambda t: (t, kv_col + 1)),
                _const_spec(kv_t.shape), cache, cache,
                _const_spec(bias.shape), _const_spec(extra.shape)]
    if new_caches is None:
        aliases, out_slot = {}, layer
        cache_out = pl.BlockSpec((k_caches.shape[0], tb, KV_DIM, WINDOW), lambda t: (0, t, 0, 0))
    else:
        aliases, out_slot = {len(args): 1, len(args) + 1: 2}, None
        cache_out = cache
        args += list(new_caches)
        in_specs += [pl.BlockSpec(memory_space=pl.ANY)] * 2
    return pl.pallas_call(
        functools.partial(_attn_sample_kernel, tb=tb, n_aliased=len(aliases), out_slot=out_slot),
        out_shape=(jax.ShapeDtypeStruct((n, D_MODEL), F32),
                   jax.ShapeDtypeStruct(k_caches.shape, F32),
                   jax.ShapeDtypeStruct(v_caches.shape, F32)),
        grid=(n // tb,),
        in_specs=in_specs,
        out_specs=(pl.BlockSpec((tb, D_MODEL), lambda t: (t, 0)), cache_out, cache_out),
        input_output_aliases=aliases,
        compiler_params=_params("parallel"),
        name="attn_sample",
    )(*args)


def _sgu_kernel(x_ref, g_ref, win_ref, lg_ref, lb_ref, sp_ref, bsp_ref, wout_ref, *rest,
                tm, sample):
    def gelu_and_norm(z):
        z = 0.5 * z * (1.0 + lax.erf(z * INV_SQRT2))
        return z[:, :D_MODEL], _layer_norm(z[:, D_MODEL:], lg_ref[...], lb_ref[...])

    if sample:
        o_ref, v_ref = rest
        x = x_ref[...]
        u, v = gelu_and_norm(_dot(_rms(x, g_ref[...]).astype(BF16), win_ref[...]))
        v_ref[...] = v
        gated = (u * (v * sp_ref[...] + bsp_ref[...])).astype(BF16)
        o_ref[...] = x + _dot(gated, wout_ref[...])
        return

    o_ref, gated_scr = rest
    tp = tm // SGU_PARTS
    parts = [slice(r * tp, (r + 1) * tp) for r in range(SGU_PARTS)]
    zs = [_dot(_rms(x_ref[rows, :], g_ref[...]).astype(BF16), win_ref[...]) for rows in parts]
    for rows, z in zip(parts, zs):
        u, v = gelu_and_norm(z)
        vb = v.astype(BF16)
        for c in range(0, tp // CHUNK, 2):
            pair = [slice((c + j) * CHUNK, (c + j + 1) * CHUNK) for j in range(2)]
            for gi in range(SGU_GROUPS):
                cols = slice(gi * LANES, (gi + 1) * LANES)
                both = jnp.concatenate([vb[chunk, cols] for chunk in pair], axis=1)
                mixed = _dot(sp_ref[gi], both)
                for j, chunk in enumerate(pair):
                    dst = slice(rows.start + chunk.start, rows.start + chunk.stop)
                    m_j = mixed[:, j * LANES:(j + 1) * LANES] + bsp_ref[gi]
                    gated_scr[dst, cols] = (u[chunk, cols] * m_j).astype(BF16)
        o_ref[rows, :] = x_ref[rows, :] + _dot(gated_scr[rows, :], wout_ref[...])


def _sgu(x, g, w_in, ln_g, ln_b, sp, bsp, w_out, *, tm, sample):
    n = x.shape[0]
    row = pl.BlockSpec((tm, D_MODEL), lambda i: (i, 0))
    args = [x, g, w_in, ln_g, ln_b, sp, bsp, w_out]
    in_specs = [row] + [_const_spec(t.shape) for t in args[1:]]
    if sample:
        out_shape = (jax.ShapeDtypeStruct((n, D_MODEL), F32),) * 2
        out_specs = (row, row)
        scratch = []
    else:
        out_shape = jax.ShapeDtypeStruct((n, D_MODEL), F32)
        out_specs = row
        scratch = [pltpu.VMEM((tm, D_MODEL), BF16)]
    return pl.pallas_call(
        functools.partial(_sgu_kernel, tm=tm, sample=sample),
        out_shape=out_shape,
        grid=(n // tm,),
        in_specs=in_specs,
        out_specs=out_specs,
        scratch_shapes=scratch,
        compiler_params=_params("parallel"),
        name="sgu_sample" if sample else "sgu_prompt",
    )(*args)


CONV_ROWS = 64
CONV_LANES = 256
PROJ_LANES = 256


def _conv_prompt_kernel(x_ref, xprev_ref, g_ref, win_ref, wdw_ref, bdw_ref, lg_ref, lb_ref,
                        wout_ref, o_ref, tail_ref, a0_scr, a1_scr, sh_scr, c_scr,
                        *, tm, tiles_per_seq):
    s = pl.program_id(0)

    @pl.when(s == 0)
    def _():
        a0_scr[...] = jnp.zeros(a0_scr.shape, F32)
        a1_scr[...] = jnp.zeros(a1_scr.shape, F32)

    first_tap = CONV_HALO - (CONV_WIDTH - 1)
    reps = CONV_ROWS // SUBLANES

    def step(a_new, a_old):
        for r in range(1, SUBLANES):
            sh_scr[r - 1] = a_old[r:r + tm + CONV_HALO - SUBLANES, :]

        h = _rms(x_ref[...], g_ref[...]).astype(BF16)
        starts_sequence = s % tiles_per_seq == 0
        a_new[0:CONV_HALO, :] = jnp.where(starts_sequence, 0.0, a_old[tm:tm + CONV_HALO, :])

        def project(jc):
            cols = slice(jc * PROJ_LANES, (jc + 1) * PROJ_LANES)
            gate = slice(D_MODEL + jc * PROJ_LANES, D_MODEL + (jc + 1) * PROJ_LANES)
            a = _dot(h, win_ref[:, cols]) * jax.nn.sigmoid(_dot(h, win_ref[:, gate]))
            a_new[CONV_HALO:, cols] = a
            tail_ref[:, cols] = a[tm - CONV_HALO:, :]

        def convolve(rb):
            for lc in range(D_MODEL // CONV_LANES):
                cols = slice(lc * CONV_LANES, (lc + 1) * CONV_LANES)
                acc = jnp.concatenate([bdw_ref[:, cols]] * reps, axis=0)
                for kk in range(CONV_WIDTH):
                    whole, r = divmod(first_tap + kk, SUBLANES)
                    start = rb * CONV_ROWS + whole * SUBLANES
                    src = a_old if r == 0 else sh_scr.at[r - 1]
                    w = jnp.concatenate([wdw_ref[kk, :, cols]] * reps, axis=0)
                    acc = acc + src[start:start + CONV_ROWS, cols] * w
                c_scr[rb * CONV_ROWS:(rb + 1) * CONV_ROWS, cols] = acc

        n_proj = D_MODEL // PROJ_LANES
        n_conv = tm // CONV_ROWS
        for jc in range(n_proj):
            project(jc)
            for rb in range(jc * n_conv // n_proj, (jc + 1) * n_conv // n_proj):
                convolve(rb)
        c = _layer_norm(c_scr[...], lg_ref[...], lb_ref[...])
        c = (c * jax.nn.sigmoid(c)).astype(BF16)
        o_ref[...] = xprev_ref[...] + _dot(c, wout_ref[...])

    @pl.when(s % 2 == 0)
    def _():
        step(a0_scr, a1_scr)

    @pl.when(s % 2 == 1)
    def _():
        step(a1_scr, a0_scr)


def _conv_prompt(x, g, w_in, w_dw8, b_dw8, ln_g, ln_b, w_out, *, tm):
    b, seq, _ = x.shape
    tiles_per_seq = seq // tm
    n = b * tiles_per_seq
    consts = [g, w_in, w_dw8, b_dw8, ln_g, ln_b, w_out]
    a_scr = pltpu.VMEM((CONV_HALO + tm, D_MODEL), F32)
    x2 = x.reshape(b * seq, D_MODEL)
    out, tail = pl.pallas_call(
        functools.partial(_conv_prompt_kernel, tm=tm, tiles_per_seq=tiles_per_seq),
        out_shape=(jax.ShapeDtypeStruct((b * seq, D_MODEL), F32),
                   jax.ShapeDtypeStruct((b, CONV_HALO, D_MODEL), F32)),
        grid=(n + 1,),
        in_specs=[pl.BlockSpec((tm, D_MODEL), lambda s: (jnp.minimum(s, n - 1), 0)),
                  pl.BlockSpec((tm, D_MODEL), lambda s: (jnp.maximum(s - 1, 0), 0))]
                 + [_const_spec(t.shape) for t in consts],
        out_specs=(pl.BlockSpec((tm, D_MODEL), lambda s: (jnp.maximum(s - 1, 0), 0)),
                   pl.BlockSpec((None, CONV_HALO, D_MODEL),
                                lambda s: (jnp.minimum(s, n - 1) // tiles_per_seq, 0, 0))),
        scratch_shapes=[a_scr, a_scr,
                        pltpu.VMEM((SUBLANES - 1, CONV_HALO + tm - SUBLANES, D_MODEL), F32),
                        pltpu.VMEM((tm, D_MODEL), F32)],
        compiler_params=_params("arbitrary"),
        name="conv_prompt",
    )(x2, x2, *consts)
    return out.reshape(b, seq, D_MODEL), tail


def _conv_sample_kernel(a_ref, st_ref, wdw_ref, bdw_ref, c_ref, so_ref):
    n_hist = CONV_WIDTH - 1
    a = a_ref[...]
    acc = a * wdw_ref[n_hist:CONV_WIDTH, :] + bdw_ref[...]
    for k in range(n_hist):
        acc = acc + st_ref[k] * wdw_ref[k:k + 1, :]
    c_ref[...] = acc
    so_ref[0:n_hist - 1] = st_ref[1:n_hist]
    so_ref[n_hist - 1] = a


def _conv_sample_core(a, states, w_dw, b_dw, layer, *, tb):
    n = a.shape[0]
    row = pl.BlockSpec((tb, D_MODEL), lambda t: (t, 0))
    st_in = pl.BlockSpec((None, CONV_WIDTH - 1, tb, D_MODEL), lambda t: (layer, 0, t, 0))
    st_out = pl.BlockSpec((CONV_WIDTH - 1, tb, D_MODEL), lambda t: (0, t, 0))
    return pl.pallas_call(
        _conv_sample_kernel,
        out_shape=(jax.ShapeDtypeStruct((n, D_MODEL), F32),
                   jax.ShapeDtypeStruct(states.shape[1:], F32)),
        grid=(n // tb,),
        in_specs=[row, st_in, _const_spec(w_dw.shape), _const_spec(b_dw.shape)],
        out_specs=(row, st_out),
        compiler_params=_params("parallel"),
        name="conv_sample",
    )(a, states, w_dw, b_dw)


PROMPT_TILE = 512
SGU_TILE = 1024
FFN_TILE = 1024
SAMPLE_ATTN_TILE = 16
SAMPLE_CONV_TILE = 32


def kernel(x_prompt, x_sample, cache_swa_k, cache_swa_v, state_conv, rel_bias, norm_mix, norm_ffn, norm_final, attn_w_qkv, attn_w_o, attn_sinks, sgu_w_in, sgu_ln_g, sgu_ln_b, sgu_w_spatial, sgu_b_spatial, sgu_w_out, conv_w_in, conv_w_dw, conv_b_dw, conv_ln_g, conv_ln_b, conv_w_out, ffn_w_up, ffn_w_down):
    batch, seq, _ = x_prompt.shape
    dec = x_sample.shape[0]
    depth = norm_mix.shape[0]
    assert x_prompt.shape[2] == D_MODEL and x_sample.shape[1:] == (1, D_MODEL)
    assert seq % PROMPT_TILE == 0 and (batch * seq) % FFN_TILE == 0 and seq % SGU_TILE == 0
    assert dec % SAMPLE_ATTN_TILE == 0 and dec % SAMPLE_CONV_TILE == 0 and dec == WINDOW
    assert cache_swa_k.shape[2:] == (WINDOW, N_KV_HEADS, HEAD_DIM)
    assert state_conv.shape[2:] == (CONV_WIDTH - 1, D_MODEL)
    mixer_of_layer = tuple(i % 3 for i in range(depth))
    slot_of_layer = tuple(mixer_of_layer[:i].count(mixer_of_layer[i]) for i in range(depth))

    def row(v):
        return v.reshape(1, -1).astype(F32)

    n_attn = attn_w_qkv.shape[0]

    def caches_t(c):
        return jnp.transpose(c, (0, 1, 3, 4, 2)).reshape(n_attn, dec, KV_DIM, WINDOW)

    def caches_from_t(c):
        c = c.reshape(n_attn, dec, N_KV_HEADS, HEAD_DIM, WINDOW)
        return jnp.transpose(c, (0, 1, 4, 2, 3))

    xp = x_prompt
    xs = x_sample.reshape(dec, D_MODEL)
    dist_bias = _distance_bias(rel_bias)
    k_caches_t, v_caches_t = caches_t(cache_swa_k), caches_t(cache_swa_v)
    states = jnp.transpose(state_conv.astype(F32), (0, 2, 1, 3))
    w_qkv_all = attn_w_qkv.astype(BF16)
    w_o_all = attn_w_o.astype(BF16)

    kp, vp, sgu_v_new, convp, convs = [], [], [], [], []
    new_caches = None
    for i in range(depth):
        m, j = mixer_of_layer[i], slot_of_layer[i]
        g_mix = row(norm_mix[i])
        if m == 0:
            bias_p, cap_p, sink_rows = _prompt_bias_tables(dist_bias, attn_sinks[j])
            xp, k1, v1 = _attn_prompt(xp, g_mix, w_qkv_all, w_o_all, j, bias_p, cap_p, sink_rows,
                                      tq=PROMPT_TILE)
            kp.append(k1.reshape(batch, WINDOW, N_KV_HEADS, HEAD_DIM))
            vp.append(v1.reshape(batch, WINDOW, N_KV_HEADS, HEAD_DIM))

            bias_s, extra_s = _sample_bias_tables(dist_bias, attn_sinks[j])
            qkv_s, kv_t = _norm_proj(xs, g_mix, w_qkv_all, j, kv_t=True)
            o_s, *new_caches = _attn_sample_core(qkv_s, kv_t, k_caches_t, v_caches_t, bias_s,
                                                 extra_s, j, new_caches, tb=SAMPLE_ATTN_TILE)
            xs = _proj_res(xs, o_s, w_o_all, j, heads_grouped=True)
        elif m == 1:
            w_in = sgu_w_in[j].astype(BF16)
            w_out = sgu_w_out[j].astype(BF16)
            ln_g, ln_b = row(sgu_ln_g[j]), row(sgu_ln_b[j])
            sp = jnp.tril(sgu_w_spatial[j]).astype(BF16)
            bsp = jnp.broadcast_to(sgu_b_spatial[j].astype(F32)[:, :, None],
                                   (SGU_GROUPS, CHUNK, LANES))
            xp = _sgu(xp.reshape(batch * seq, D_MODEL), g_mix, w_in, ln_g, ln_b, sp, bsp, w_out,
                      tm=SGU_TILE, sample=False).reshape(batch, seq, D_MODEL)
            sp0 = row(jnp.repeat(sgu_w_spatial[j][:, 0, 0], LANES))
            bsp0 = row(jnp.repeat(sgu_b_spatial[j][:, 0], LANES))
            xs, v_rows = _sgu(xs, g_mix, w_in, ln_g, ln_b, sp0, bsp0, w_out, tm=dec, sample=True)
            sgu_v_new.append(v_rows.reshape(dec, 1, D_MODEL))
        else:
            w_in = conv_w_in[j].astype(BF16)
            w_out = conv_w_out[j].astype(BF16)
            w_dw = conv_w_dw[j].astype(F32)
            b_dw = row(conv_b_dw[j])
            ln_g, ln_b = row(conv_ln_g[j]), row(conv_ln_b[j])
            w_dw8 = jnp.broadcast_to(w_dw[:, None, :], (CONV_WIDTH, SUBLANES, D_MODEL))
            b_dw8 = jnp.broadcast_to(b_dw, (SUBLANES, D_MODEL))
            xp, tail = _conv_prompt(xp, g_mix, w_in, w_dw8, b_dw8, ln_g, ln_b, w_out,
                                    tm=PROMPT_TILE)
            convp.append(tail[:, CONV_HALO - (CONV_WIDTH - 1):, :])
            a_s = _norm_proj(xs, g_mix, w_in, glu=True)
            c_s, st2 = _conv_sample_core(a_s, states, w_dw, b_dw, j, tb=SAMPLE_CONV_TILE)
            xs = _proj_res(xs, c_s, w_out, ln=(ln_g, ln_b))
            convs.append(jnp.transpose(st2, (1, 0, 2)))

        g_fin = row(norm_final) if i == depth - 1 else None
        xp, xs = _ffn(xp.reshape(batch * seq, D_MODEL), xs, row(norm_ffn[i]), ffn_w_up, ffn_w_down,
                      i, g_fin, tm=FFN_TILE)
        xp = xp.reshape(batch, seq, D_MODEL)

    y_prompt = xp
    y_sample = xs.reshape(dec, 1, D_MODEL)
    k_new, v_new = (caches_from_t(c) for c in new_caches)
    return (y_prompt, y_sample, jnp.stack(kp), jnp.stack(vp), k_new, v_new,
            jnp.stack(sgu_v_new), jnp.stack(convp), jnp.stack(convs))
```

```python
import functools
import math

import jax
import jax.numpy as jnp
from jax import lax
from jax.experimental import pallas as pl
from jax.experimental.pallas import tpu as pltpu

D_MODEL = 1024
HEAD_DIM = 64
N_HEADS = 16
N_KV_HEADS = 4
Q_PER_KV = 4
KV_DIM = N_KV_HEADS * HEAD_DIM
WINDOW = 128
ATTN_SCALE = HEAD_DIM ** -0.5
N_BUCKETS = 32
MAX_DISTANCE = 128
CHUNK = 128
SGU_GROUPS = 8
CONV_WIDTH = 31
CONV_HALO = 32
D_FF = 4 * D_MODEL
EPS = 1e-6
NEG_INF = -1e30
F32_MAX = float(jnp.finfo(jnp.float32).max)
INV_SQRT2 = 1.0 / math.sqrt(2.0)

LANES = 128
SUBLANES = 8
VMEM_LIMIT = 56 * 1024 * 1024

F32 = jnp.float32
BF16 = jnp.bfloat16


def _const_spec(shape):
    n = len(shape)
    return pl.BlockSpec(shape, lambda *_: (0,) * n, pipeline_mode=pl.Buffered(1))


def _params(*sem):
    return pltpu.CompilerParams(dimension_semantics=sem, vmem_limit_bytes=VMEM_LIMIT)


def _rms(x, g):
    return x * lax.rsqrt(jnp.mean(x * x, axis=-1, keepdims=True) + EPS) * g


def _layer_norm(x, g, b):
    mu = jnp.mean(x, axis=-1, keepdims=True)
    xc = x - mu
    var = jnp.mean(xc * xc, axis=-1, keepdims=True)
    return xc * lax.rsqrt(var + EPS) * g + b


def _dot(a, b):
    return jnp.dot(a, b, preferred_element_type=F32)


def _dot_nt(a, b):
    return lax.dot_general(a, b, (((1,), (1,)), ((), ())), preferred_element_type=F32)


FFN_CHUNK = 512
N_FFN_CHUNKS = D_FF // FFN_CHUNK


def _layer_spec(shape, layer):
    zeros = (0,) * (len(shape) - 1)
    return pl.BlockSpec((None,) + tuple(shape[1:]), lambda *_: (layer,) + zeros,
                        pipeline_mode=pl.Buffered(1))


def _ffn_kernel(xp_ref, xs_ref, g_ref, wu_ref, wd_ref, *rest, final, n_prompt_tiles):
    *rest, wu_scr, wd_scr = rest
    if final:
        gf_ref, op_ref, os_ref = rest
    else:
        op_ref, os_ref = rest

    def block(x_ref, o_ref):
        x = x_ref[...]
        h = _rms(x, g_ref[...]).astype(BF16)
        y = x
        for c in range(N_FFN_CHUNKS):
            u = _dot(h, wu_scr[c])
            u = jnp.square(jnp.maximum(u, 0.0)).astype(BF16)
            y = y + _dot(u, wd_scr[c])
        o_ref[...] = _rms(y, gf_ref[...]) if final else y

    i = pl.program_id(0)
    tile = i - N_FFN_CHUNKS

    @pl.when(i < N_FFN_CHUNKS)
    def _():
        wu_scr[i] = wu_ref[...].astype(BF16)
        wd_scr[i] = wd_ref[...].astype(BF16)

    @pl.when(jnp.logical_and(tile >= 0, tile < n_prompt_tiles))
    def _():
        block(xp_ref, op_ref)

    @pl.when(tile == n_prompt_tiles)
    def _():
        block(xs_ref, os_ref)


def _ffn(xp, xs, g, w_up, w_down, layer, g_final=None, *, tm):
    n = xp.shape[0]
    nt = n // tm
    last_chunk = N_FFN_CHUNKS - 1
    row = pl.BlockSpec((tm, D_MODEL), lambda i: (jnp.clip(i - N_FFN_CHUNKS, 0, nt - 1), 0))
    in_specs = [row, _const_spec(xs.shape), _const_spec((1, D_MODEL)),
                pl.BlockSpec((None, D_MODEL, FFN_CHUNK),
                             lambda i: (layer, 0, jnp.minimum(i, last_chunk))),
                pl.BlockSpec((None, FFN_CHUNK, D_MODEL),
                             lambda i: (layer, jnp.minimum(i, last_chunk), 0))]
    args = [xp, xs, g, w_up, w_down]
    if g_final is not None:
        in_specs.append(_const_spec((1, D_MODEL)))
        args.append(g_final)
    return pl.pallas_call(
        functools.partial(_ffn_kernel, final=g_final is not None, n_prompt_tiles=nt),
        out_shape=(jax.ShapeDtypeStruct((n, D_MODEL), F32),
                   jax.ShapeDtypeStruct(xs.shape, F32)),
        grid=(N_FFN_CHUNKS + nt + 1,),
        in_specs=in_specs,
        out_specs=(row, pl.BlockSpec(xs.shape, lambda i: (0, 0))),
        scratch_shapes=[pltpu.VMEM((N_FFN_CHUNKS, D_MODEL, FFN_CHUNK), BF16),
                        pltpu.VMEM((N_FFN_CHUNKS, FFN_CHUNK, D_MODEL), BF16)],
        compiler_params=_params("arbitrary"),
        name="ffn_final" if g_final is not None else "ffn",
    )(*args)


def _regroup_heads(t, group_major):
    blocks = range(N_HEADS)
    if group_major:
        order = [N_KV_HEADS * (blk % N_KV_HEADS) + blk // N_KV_HEADS for blk in blocks]
    else:
        order = [Q_PER_KV * (blk % Q_PER_KV) + blk // Q_PER_KV for blk in blocks]
    return jnp.concatenate([t[:, h * HEAD_DIM:(h + 1) * HEAD_DIM] for h in order], axis=1)


def _norm_proj_kernel(x_ref, g_ref, w_ref, o_ref, *rest, glu, kv_t):
    h = _rms(x_ref[...], g_ref[...]).astype(BF16)
    y = _dot(h, w_ref[...])
    if glu:
        half = y.shape[1] // 2
        y = y[:, :half] * jax.nn.sigmoid(y[:, half:])
    if kv_t:
        (t_ref,) = rest
        o_ref[:, :D_MODEL] = _regroup_heads(y[:, :D_MODEL], group_major=True)
        o_ref[:, D_MODEL:] = y[:, D_MODEL:]
        t_ref[...] = y[:, D_MODEL:].T
    else:
        o_ref[...] = y


def _weight_spec(w, layer):
    return _const_spec(w.shape) if layer is None else _layer_spec(w.shape, layer)


def _norm_proj(x, g, w, layer=None, *, glu=False, kv_t=False):
    n = x.shape[0]
    n_out = w.shape[-1] // 2 if glu else w.shape[-1]
    out_shape = [jax.ShapeDtypeStruct((n, n_out), F32)]
    if kv_t:
        out_shape.append(jax.ShapeDtypeStruct((n_out - D_MODEL, n), F32))
    out = pl.pallas_call(
        functools.partial(_norm_proj_kernel, glu=glu, kv_t=kv_t),
        out_shape=tuple(out_shape),
        grid=(1,),
        in_specs=[_const_spec(x.shape), _const_spec(g.shape), _weight_spec(w, layer)],
        out_specs=tuple(_const_spec(o.shape) for o in out_shape),
        compiler_params=_params("arbitrary"),
        name="norm_proj_glu" if glu else "norm_proj",
    )(x, g, w)
    return out if kv_t else out[0]


def _proj_res_kernel(x_ref, a_ref, w_ref, *rest, conv_tail, heads_grouped):
    a = a_ref[...]
    if conv_tail:
        lg_ref, lb_ref, o_ref = rest
        a = _layer_norm(a, lg_ref[...], lb_ref[...])
        a = a * jax.nn.sigmoid(a)
    else:
        (o_ref,) = rest
    if heads_grouped:
        a = _regroup_heads(a, group_major=False)
    o_ref[...] = x_ref[...] + _dot(a.astype(BF16), w_ref[...])


def _proj_res(x, a, w, layer=None, ln=None, heads_grouped=False):
    args = [x, a, w] + (list(ln) if ln is not None else [])
    in_specs = [_const_spec(t.shape) for t in args]
    in_specs[2] = _weight_spec(w, layer)
    return pl.pallas_call(
        functools.partial(_proj_res_kernel, conv_tail=ln is not None, heads_grouped=heads_grouped),
        out_shape=jax.ShapeDtypeStruct(x.shape, F32),
        grid=(1,),
        in_specs=in_specs,
        out_specs=_const_spec(x.shape),
        compiler_params=_params("arbitrary"),
        name="proj_res_ln" if ln is not None else "proj_res",
    )(*args)


def _t5_bucket(dist):
    n = jnp.maximum(dist, 0)
    max_exact = N_BUCKETS // 2
    nf = jnp.maximum(n, 1).astype(F32)
    large = max_exact + (jnp.log(nf / max_exact) / math.log(MAX_DISTANCE / max_exact)
                         * (N_BUCKETS - max_exact)).astype(jnp.int32)
    large = jnp.minimum(large, N_BUCKETS - 1)
    return jnp.where(n < max_exact, n, large)


def _distance_bias(rel_bias):
    buckets = _t5_bucket(jnp.arange(WINDOW + 1, dtype=jnp.int32))
    onehot = (buckets[:, None] == jnp.arange(N_BUCKETS, dtype=jnp.int32)[None, :]).astype(F32)
    return jnp.dot(onehot, rel_bias.astype(F32), precision=lax.Precision.HIGHEST)


def _prompt_bias_tables(dist_bias, sinks):
    period = 3 * WINDOW
    line = jnp.concatenate([jnp.broadcast_to(dist_bias[WINDOW:], (WINDOW - 1, N_HEADS)),
                            dist_bias[::-1],
                            jnp.broadcast_to(dist_bias[:1], (WINDOW, N_HEADS))]).T
    skew = jnp.tile(line, (1, WINDOW))[:, :WINDOW * (period - 1)]
    per_head = skew.reshape(N_HEADS, WINDOW, period - 1)[:, :, WINDOW - 1:period - 1]
    per_head = per_head.reshape(N_KV_HEADS, 2, 2, WINDOW, 2 * WINDOW)
    bias = per_head.transpose(0, 2, 4, 1, 3).reshape(N_KV_HEADS, 4 * WINDOW, 2 * WINDOW)

    qi = jnp.arange(WINDOW, dtype=jnp.int32)[None, :]
    kj = jnp.arange(2 * WINDOW, dtype=jnp.int32)[:, None]
    dist = qi - kj + WINDOW
    allowed = (dist >= 0) & (dist <= WINDOW)
    first = allowed & (kj >= WINDOW)
    cap = jnp.stack([jnp.where(allowed, F32_MAX, NEG_INF), jnp.where(first, F32_MAX, NEG_INF)])
    cap = jnp.tile(cap.astype(F32), (1, 2, 2))

    sink_rows = sinks.astype(F32).reshape(N_KV_HEADS, 2, 2).transpose(0, 2, 1).reshape(2 * N_KV_HEADS, 2)
    sink_rows = jnp.repeat(sink_rows, WINDOW, axis=1)
    return bias, cap, sink_rows


def _attn_prompt_kernel(x_ref, xres_ref, g_ref, wqkv_ref, wo_ref, bias_ref, cap_ref, sink_ref,
                        o_ref, kc_ref, vc_ref,
                        q0_scr, q1_scr, klo0_scr, klo1_scr, khi0_scr, khi1_scr, vt0_scr, vt1_scr,
                        a0_scr, a1_scr, *, tq, tiles_per_seq):
    s = pl.program_id(0)
    sets = ((q0_scr, klo0_scr, khi0_scr, vt0_scr, a0_scr),
            (q1_scr, klo1_scr, khi1_scr, vt1_scr, a1_scr))

    @pl.when(s == 0)
    def _():
        for scr in sets[0] + sets[1]:
            scr[...] = jnp.zeros(scr.shape, BF16)

    low = lax.broadcasted_iota(jnp.int32, (tq, LANES), 1) < HEAD_DIM
    n_col = D_MODEL // KV_DIM

    def step(new, old):
        q_new, klo_new, khi_new, vt_new, att_done = new
        q_old, klo_old, khi_old, vt_old, att_out = old

        starts_sequence = s % tiles_per_seq == 0
        klo_new[:, 0:WINDOW, :] = jnp.where(starts_sequence, 0.0, klo_old[:, tq:tq + WINDOW, :])
        khi_new[:, 0:WINDOW, :] = jnp.where(starts_sequence, 0.0, khi_old[:, tq:tq + WINDOW, :])
        vt_new[:, :, 0:WINDOW] = jnp.where(starts_sequence, 0.0, vt_old[:, :, tq:tq + WINDOW])
        h = _rms(x_ref[...], g_ref[...]).astype(BF16)

        def project_q(c):
            cols = slice(c * KV_DIM, (c + 1) * KV_DIM)
            q_new[cols, :] = (_dot(h, wqkv_ref[:, cols]) * ATTN_SCALE).T.astype(BF16)

        def project_k():
            k = _dot(h, wqkv_ref[:, D_MODEL:D_MODEL + KV_DIM])
            kc_ref[...] = k[tq - WINDOW:, :]
            for c in range(KV_DIM // LANES):
                kc = k[:, c * LANES:(c + 1) * LANES]
                kr = pltpu.roll(kc, HEAD_DIM, axis=1)
                klo_new[2 * c, WINDOW:, :] = jnp.where(low, kc, 0.0).astype(BF16)
                khi_new[2 * c, WINDOW:, :] = jnp.where(low, 0.0, kr).astype(BF16)
                klo_new[2 * c + 1, WINDOW:, :] = jnp.where(low, kr, 0.0).astype(BF16)
                khi_new[2 * c + 1, WINDOW:, :] = jnp.where(low, 0.0, kc).astype(BF16)

        def project_v():
            v = _dot(h, wqkv_ref[:, D_MODEL + KV_DIM:])
            vc_ref[...] = v[tq - WINDOW:, :]
            for c in range(KV_DIM // LANES):
                vt = v[:, c * LANES:(c + 1) * LANES].T.astype(BF16)
                vt_new[2 * c, :, WINDOW:] = vt[:HEAD_DIM]
                vt_new[2 * c + 1, :, WINDOW:] = vt[HEAD_DIM:]

        def project_out(c):
            cols = slice(c * KV_DIM, (c + 1) * KV_DIM)
            o_ref[:, cols] = xres_ref[:, cols] + _dot(att_done[...], wo_ref[:, cols])

        is_first = jnp.where((s - 1) % tiles_per_seq == 0, 1, 0)

        def scores(jb, kh):
            rows = slice(jb * WINDOW, (jb + 1) * WINDOW)
            band = slice(jb * WINDOW, (jb + 2) * WINDOW)
            cap = cap_ref[is_first] if jb == 0 else cap_ref[0]
            qst_t = jnp.concatenate([q_old[(2 * kh) * LANES:(2 * kh + 1) * LANES, rows],
                                     q_old[(2 * kh + 1) * LANES:(2 * kh + 2) * LANES, rows]], axis=1)
            kst = jnp.concatenate([klo_old[kh, band, :], khi_old[kh, band, :]], axis=0)
            return jnp.minimum(_dot(kst, qst_t) + bias_ref[kh], cap)

        def attend(jb, kh, sc):
            rows = slice(jb * WINDOW, (jb + 1) * WINDOW)
            vt = vt_old[kh, :, jb * WINDOW:(jb + 2) * WINDOW]
            halves = []
            for half in range(2):
                sh = sc[half * 2 * WINDOW:(half + 1) * 2 * WINDOW]
                sink = sink_ref[2 * kh + half:2 * kh + half + 1, :]
                m = jnp.maximum(jnp.max(sh, axis=0, keepdims=True), sink)
                p = jnp.exp(sh - m)
                denom = jnp.sum(p, axis=0, keepdims=True) + jnp.exp(sink - m)
                halves.append(_dot(vt, p.astype(BF16)) * (1.0 / denom))
            o = jnp.concatenate(halves, axis=0).T.astype(BF16)
            att_out[rows, (2 * kh) * LANES:(2 * kh + 1) * LANES] = o[:WINDOW]
            att_out[rows, (2 * kh + 1) * LANES:(2 * kh + 2) * LANES] = o[WINDOW:]

        pieces = [functools.partial(project_q, c) for c in range(n_col)] + [project_k, project_v]
        pieces += [functools.partial(project_out, c) for c in range(n_col)]
        steps = [(jb, kh) for jb in range(tq // WINDOW) for kh in range(N_KV_HEADS)]
        piece_at = {(n * len(steps)) // len(pieces): piece for n, piece in enumerate(pieces)}
        assert len(piece_at) == len(pieces)
        s_next = scores(*steps[0])
        for n, st in enumerate(steps):
            s_cur = s_next
            if n + 1 < len(steps):
                s_next = scores(*steps[n + 1])
            if n in piece_at:
                piece_at[n]()
            attend(*st, s_cur)

    @pl.when(s % 2 == 0)
    def _():
        step(sets[0], sets[1])

    @pl.when(s % 2 == 1)
    def _():
        step(sets[1], sets[0])


def _attn_prompt(x, g, w_qkv, w_o, layer, bias, cap, sink_rows, *, tq):
    b, seq, _ = x.shape
    tiles_per_seq = seq // tq
    n = b * tiles_per_seq
    x2 = x.reshape(b * seq, D_MODEL)
    lagged = pl.BlockSpec((tq, D_MODEL), lambda s: (jnp.maximum(s - 2, 0), 0))
    cache = pl.BlockSpec((None, WINDOW, KV_DIM),
                         lambda s: (jnp.minimum(s, n - 1) // tiles_per_seq, 0, 0))
    q_scr = pltpu.VMEM((D_MODEL, tq), BF16)
    a_scr = pltpu.VMEM((tq, D_MODEL), BF16)
    k_scr = pltpu.VMEM((N_KV_HEADS, WINDOW + tq, LANES), BF16)
    vt_scr = pltpu.VMEM((N_KV_HEADS, HEAD_DIM, WINDOW + tq), BF16)
    out, kc, vc = pl.pallas_call(
        functools.partial(_attn_prompt_kernel, tq=tq, tiles_per_seq=tiles_per_seq),
        out_shape=(jax.ShapeDtypeStruct((b * seq, D_MODEL), F32),
                   jax.ShapeDtypeStruct((b, WINDOW, KV_DIM), F32),
                   jax.ShapeDtypeStruct((b, WINDOW, KV_DIM), F32)),
        grid=(n + 2,),
        in_specs=[pl.BlockSpec((tq, D_MODEL), lambda s: (jnp.minimum(s, n - 1), 0)), lagged,
                  _const_spec((1, D_MODEL)),
                  _layer_spec(w_qkv.shape, layer), _layer_spec(w_o.shape, layer),
                  _const_spec(bias.shape), _const_spec(cap.shape), _const_spec(sink_rows.shape)],
        out_specs=(lagged, cache, cache),
        scratch_shapes=[q_scr, q_scr, k_scr, k_scr, k_scr, k_scr, vt_scr, vt_scr, a_scr, a_scr],
        compiler_params=_params("arbitrary"),
        name="attn_prompt",
    )(x2, x2, g, w_qkv, w_o, bias, cap, sink_rows)
    return out.reshape(b, seq, D_MODEL), kc, vc


ROWS_PAD = SUBLANES * Q_PER_KV
SAMPLE_GROUP = 16


def _sample_bias_tables(dist_bias, sinks):
    def rows(per_head):
        t = per_head.reshape(N_KV_HEADS, Q_PER_KV, -1).transpose(1, 0, 2)
        return jnp.pad(t, ((0, 0), (0, SUBLANES - N_KV_HEADS), (0, 0))).reshape(ROWS_PAD, -1)

    bias = rows(dist_bias[:0:-1].T)
    extra = rows(jnp.stack([dist_bias[0], sinks.astype(F32)], axis=1))
    return bias, jnp.pad(extra, ((0, 0), (0, LANES - 2)))


def _attn_sample_kernel(q_ref, kn_ref, vn_ref, kvt_ref, kc_ref, vc_ref, bias_ref, extra_ref,
                        *rest, tb, n_aliased, out_slot):
    o_ref, ko_ref, vo_ref = rest[n_aliased:]
    if out_slot is not None:
        for out_ref in (ko_ref, vo_ref):
            for slot in range(out_ref.shape[0]):
                if slot != out_slot:
                    out_ref[slot] = jnp.zeros(out_ref.shape[1:], F32)
        ko_ref, vo_ref = ko_ref.at[out_slot], vo_ref.at[out_slot]
    t = pl.program_id(0)
    sub = lax.broadcasted_iota(jnp.int32, (SUBLANES, KV_DIM), 0)
    lane_head = lax.broadcasted_iota(jnp.int32, (SUBLANES, KV_DIM), 1) // HEAD_DIM
    own = jnp.logical_and(sub < N_KV_HEADS, lane_head == sub)
    newest = lax.broadcasted_iota(jnp.int32, (KV_DIM, WINDOW), 1) == WINDOW - 1
    bias = bias_ref[...]
    bias_new = extra_ref[:, 0:1]
    sink = extra_ref[:, 1:2]

    def scores(group):
        s_old, s_new = [], []
        for bb in group:
            q_rows = []
            for gq in range(Q_PER_KV):
                q_g = q_ref[bb:bb + 1, gq * KV_DIM:(gq + 1) * KV_DIM] * ATTN_SCALE
                q_rows.append(jnp.where(own, jnp.broadcast_to(q_g, (SUBLANES, KV_DIM)), 0.0))
            q_blk = jnp.concatenate(q_rows, axis=0)
            s_old.append(_dot(q_blk.astype(BF16), kc_ref[bb].astype(BF16)) + bias)
            s_new.append(jnp.sum(q_blk * kn_ref[bb:bb + 1, :], axis=1, keepdims=True) + bias_new)
        return jnp.concatenate(s_old, axis=0), jnp.concatenate(s_new, axis=0)

    def attend(group, s_old, s_new):
        sinks = jnp.concatenate([sink] * len(group), axis=0)
        m = jnp.maximum(jnp.max(s_old, axis=1, keepdims=True), jnp.maximum(s_new, sinks))
        p_old = jnp.exp(s_old - m)
        p_new = jnp.exp(s_new - m)
        denom = jnp.sum(p_old, axis=1, keepdims=True) + p_new + jnp.exp(sinks - m)
        p_old = p_old.astype(BF16)
        for gi, bb in enumerate(group):
            rows = slice(gi * ROWS_PAD, (gi + 1) * ROWS_PAD)
            o = _dot_nt(p_old[rows], vc_ref[bb].astype(BF16)) + p_new[rows] * vn_ref[bb:bb + 1, :]
            o = o / denom[rows]
            for gq in range(Q_PER_KV):
                o_rows = o[SUBLANES * gq:SUBLANES * (gq + 1)]
                o_g = jnp.sum(jnp.where(own, o_rows, 0.0), axis=0, keepdims=True)
                o_ref[bb:bb + 1, gq * KV_DIM:(gq + 1) * KV_DIM] = o_g

    def shift(bb):
        to_last = (WINDOW - 1) - (t * tb + bb)
        for cache_ref, row0, out_ref in ((kc_ref, 0, ko_ref), (vc_ref, KV_DIM, vo_ref)):
            moved = pltpu.roll(cache_ref[bb], WINDOW - 1, axis=1)
            col = pltpu.roll(kvt_ref[row0:row0 + KV_DIM, :], to_last, axis=1)
            out_ref[bb] = jnp.where(newest, col, moved)

    for bb in range(tb):
        shift(bb)
    groups = [tuple(range(g, g + SAMPLE_GROUP)) for g in range(0, tb, SAMPLE_GROUP)]
    s_next = scores(groups[0])
    for n, group in enumerate(groups):
        s_cur = s_next
        if n + 1 < len(groups):
            s_next = scores(groups[n + 1])
        attend(group, *s_cur)


def _attn_sample_core(qkv, kv_t, k_caches, v_caches, bias, extra, layer, new_caches=None, *, tb):
    n = qkv.shape[0]
    cache = pl.BlockSpec((None, tb, KV_DIM, WINDOW), lambda t: (layer, t, 0, 0))
    kv_col = D_MODEL // KV_DIM
    args = [qkv, qkv, qkv, kv_t, k_caches, v_caches, bias, extra]
    in_specs = [pl.BlockSpec((tb, D_MODEL), lambda t: (t, 0)),
                pl.BlockSpec((tb, KV_DIM), lambda t: (t, kv_col)),
                pl.BlockSpec((tb, KV_DIM), lambda t: (t, kv_col + 1)),
                _const_spec(kv_t.shape), cache, cache,
                _const_spec(bias.shape), _const_spec(extra.shape)]
    if new_caches is None:
        aliases, out_slot = {}, layer
        cache_out = pl.BlockSpec((k_caches.shape[0], tb, KV_DIM, WINDOW), lambda t: (0, t, 0, 0))
    else:
        aliases, out_slot = {len(args): 1, len(args) + 1: 2}, None
        cache_out = cache
        args += list(new_caches)
        in_specs += [pl.BlockSpec(memory_space=pl.ANY)] * 2
    return pl.pallas_call(
        functools.partial(_attn_sample_kernel, tb=tb, n_aliased=len(aliases), out_slot=out_slot),
        out_shape=(jax.ShapeDtypeStruct((n, D_MODEL), F32),
                   jax.ShapeDtypeStruct(k_caches.shape, F32),
                   jax.ShapeDtypeStruct(v_caches.shape, F32)),
        grid=(n // tb,),
        in_specs=in_specs,
        out_specs=(pl.BlockSpec((tb, D_MODEL), lambda t: (t, 0)), cache_out, cache_out),
        input_output_aliases=aliases,
        compiler_params=_params("parallel"),
        name="attn_sample",
    )(*args)


def _sgu_kernel(x_ref, g_ref, win_ref, lg_ref, lb_ref, sp_ref, bsp_ref, wout_ref, *rest,
                tm, sample):
    def gelu_and_norm(z):
        z = 0.5 * z * (1.0 + lax.erf(z * INV_SQRT2))
        return z[:, :D_MODEL], _layer_norm(z[:, D_MODEL:], lg_ref[...], lb_ref[...])

    if sample:
        o_ref, v_ref = rest
        x = x_ref[...]
        u, v = gelu_and_norm(_dot(_rms(x, g_ref[...]).astype(BF16), win_ref[...]))
        v_ref[...] = v
        gated = (u * (v * sp_ref[...] + bsp_ref[...])).astype(BF16)
        o_ref[...] = x + _dot(gated, wout_ref[...])
        return

    o_ref, gated_scr = rest
    x = x_ref[...]
    h = _rms(x, g_ref[...]).astype(BF16)

    def gelu(z):
        return 0.5 * z * (1.0 + lax.erf(z * INV_SQRT2))

    z_v = _dot(h, win_ref[:, D_MODEL:])
    z_u = _dot(h, win_ref[:, :D_MODEL])
    vb = _layer_norm(gelu(z_v), lg_ref[...], lb_ref[...]).astype(BF16)
    u = gelu(z_u)
    for c in range(0, tm // CHUNK, 2):
        pair = [slice((c + j) * CHUNK, (c + j + 1) * CHUNK) for j in range(2)]
        for gi in range(SGU_GROUPS):
            cols = slice(gi * LANES, (gi + 1) * LANES)
            both = jnp.concatenate([vb[chunk, cols] for chunk in pair], axis=1)
            mixed = _dot(sp_ref[gi], both)
            for j, chunk in enumerate(pair):
                m_j = mixed[:, j * LANES:(j + 1) * LANES] + bsp_ref[gi]
                gated_scr[chunk, cols] = (u[chunk, cols] * m_j).astype(BF16)
    o_ref[...] = x + _dot(gated_scr[...], wout_ref[...])


def _sgu(x, g, w_in, ln_g, ln_b, sp, bsp, w_out, *, tm, sample):
    n = x.shape[0]
    row = pl.BlockSpec((tm, D_MODEL), lambda i: (i, 0))
    args = [x, g, w_in, ln_g, ln_b, sp, bsp, w_out]
    in_specs = [row] + [_const_spec(t.shape) for t in args[1:]]
    if sample:
        out_shape = (jax.ShapeDtypeStruct((n, D_MODEL), F32),) * 2
        out_specs = (row, row)
        scratch = []
    else:
        out_shape = jax.ShapeDtypeStruct((n, D_MODEL), F32)
        out_specs = row
        scratch = [pltpu.VMEM((tm, D_MODEL), BF16)]
    return pl.pallas_call(
        functools.partial(_sgu_kernel, tm=tm, sample=sample),
        out_shape=out_shape,
        grid=(n // tm,),
        in_specs=in_specs,
        out_specs=out_specs,
        scratch_shapes=scratch,
        compiler_params=_params("parallel"),
        name="sgu_sample" if sample else "sgu_prompt",
    )(*args)


CONV_ROWS = 64
CONV_LANES = 256
PROJ_LANES = 256


def _conv_prompt_kernel(x_ref, xprev_ref, g_ref, win_ref, wdw_ref, bdw_ref, lg_ref, lb_ref,
                        wout_ref, o_ref, tail_ref, a0_scr, a1_scr, sh_scr, c_scr,
                        *, tm, tiles_per_seq):
    s = pl.program_id(0)

    @pl.when(s == 0)
    def _():
        a0_scr[...] = jnp.zeros(a0_scr.shape, F32)
        a1_scr[...] = jnp.zeros(a1_scr.shape, F32)

    first_tap = CONV_HALO - (CONV_WIDTH - 1)
    reps = CONV_ROWS // SUBLANES

    def step(a_new, a_old):
        for r in range(1, SUBLANES):
            sh_scr[r - 1] = a_old[r:r + tm + CONV_HALO - SUBLANES, :]

        h = _rms(x_ref[...], g_ref[...]).astype(BF16)
        starts_sequence = s % tiles_per_seq == 0
        a_new[0:CONV_HALO, :] = jnp.where(starts_sequence, 0.0, a_old[tm:tm + CONV_HALO, :])

        def project(jc):
            cols = slice(jc * PROJ_LANES, (jc + 1) * PROJ_LANES)
            gate = slice(D_MODEL + jc * PROJ_LANES, D_MODEL + (jc + 1) * PROJ_LANES)
            a = _dot(h, win_ref[:, cols]) * jax.nn.sigmoid(_dot(h, win_ref[:, gate]))
            a_new[CONV_HALO:, cols] = a
            tail_ref[:, cols] = a[tm - CONV_HALO:, :]

        def convolve(rb):
            for lc in range(D_MODEL // CONV_LANES):
                cols = slice(lc * CONV_LANES, (lc + 1) * CONV_LANES)
                acc = jnp.concatenate([bdw_ref[:, cols]] * reps, axis=0)
                for kk in range(CONV_WIDTH):
                    whole, r = divmod(first_tap + kk, SUBLANES)
                    start = rb * CONV_ROWS + whole * SUBLANES
                    src = a_old if r == 0 else sh_scr.at[r - 1]
                    w = jnp.concatenate([wdw_ref[kk, :, cols]] * reps, axis=0)
                    acc = acc + src[start:start + CONV_ROWS, cols] * w
                c_scr[rb * CONV_ROWS:(rb + 1) * CONV_ROWS, cols] = acc

        n_proj = D_MODEL // PROJ_LANES
        n_conv = tm // CONV_ROWS
        for jc in range(n_proj):
            project(jc)
            for rb in range(jc * n_conv // n_proj, (jc + 1) * n_conv // n_proj):
                convolve(rb)
        c = _layer_norm(c_scr[...], lg_ref[...], lb_ref[...])
        c = (c * jax.nn.sigmoid(c)).astype(BF16)
        o_ref[...] = xprev_ref[...] + _dot(c, wout_ref[...])

    @pl.when(s % 2 == 0)
    def _():
        step(a0_scr, a1_scr)

    @pl.when(s % 2 == 1)
    def _():
        step(a1_scr, a0_scr)


def _conv_prompt(x, g, w_in, w_dw8, b_dw8, ln_g, ln_b, w_out, *, tm):
    b, seq, _ = x.shape
    tiles_per_seq = seq // tm
    n = b * tiles_per_seq
    consts = [g, w_in, w_dw8, b_dw8, ln_g, ln_b, w_out]
    a_scr = pltpu.VMEM((CONV_HALO + tm, D_MODEL), F32)
    x2 = x.reshape(b * seq, D_MODEL)
    out, tail = pl.pallas_call(
        functools.partial(_conv_prompt_kernel, tm=tm, tiles_per_seq=tiles_per_seq),
        out_shape=(jax.ShapeDtypeStruct((b * seq, D_MODEL), F32),
                   jax.ShapeDtypeStruct((b, CONV_HALO, D_MODEL), F32)),
        grid=(n + 1,),
        in_specs=[pl.BlockSpec((tm, D_MODEL), lambda s: (jnp.minimum(s, n - 1), 0)),
                  pl.BlockSpec((tm, D_MODEL), lambda s: (jnp.maximum(s - 1, 0), 0))]
                 + [_const_spec(t.shape) for t in consts],
        out_specs=(pl.BlockSpec((tm, D_MODEL), lambda s: (jnp.maximum(s - 1, 0), 0)),
                   pl.BlockSpec((None, CONV_HALO, D_MODEL),
                                lambda s: (jnp.minimum(s, n - 1) // tiles_per_seq, 0, 0))),
        scratch_shapes=[a_scr, a_scr,
                        pltpu.VMEM((SUBLANES - 1, CONV_HALO + tm - SUBLANES, D_MODEL), F32),
                        pltpu.VMEM((tm, D_MODEL), F32)],
        compiler_params=_params("arbitrary"),
        name="conv_prompt",
    )(x2, x2, *consts)
    return out.reshape(b, seq, D_MODEL), tail


def _conv_sample_kernel(a_ref, st_ref, wdw_ref, bdw_ref, c_ref, so_ref):
    n_hist = CONV_WIDTH - 1
    a = a_ref[...]
    acc = a * wdw_ref[n_hist:CONV_WIDTH, :] + bdw_ref[...]
    for k in range(n_hist):
        acc = acc + st_ref[k] * wdw_ref[k:k + 1, :]
    c_ref[...] = acc
    so_ref[0:n_hist - 1] = st_ref[1:n_hist]
    so_ref[n_hist - 1] = a


def _conv_sample_core(a, states, w_dw, b_dw, layer, *, tb):
    n = a.shape[0]
    row = pl.BlockSpec((tb, D_MODEL), lambda t: (t, 0))
    st_in = pl.BlockSpec((None, CONV_WIDTH - 1, tb, D_MODEL), lambda t: (layer, 0, t, 0))
    st_out = pl.BlockSpec((CONV_WIDTH - 1, tb, D_MODEL), lambda t: (0, t, 0))
    return pl.pallas_call(
        _conv_sample_kernel,
        out_shape=(jax.ShapeDtypeStruct((n, D_MODEL), F32),
                   jax.ShapeDtypeStruct(states.shape[1:], F32)),
        grid=(n // tb,),
        in_specs=[row, st_in, _const_spec(w_dw.shape), _const_spec(b_dw.shape)],
        out_specs=(row, st_out),
        compiler_params=_params("parallel"),
        name="conv_sample",
    )(a, states, w_dw, b_dw)


PROMPT_TILE = 512
SGU_TILE = 1024
FFN_TILE = 1024
SAMPLE_ATTN_TILE = 16
SAMPLE_CONV_TILE = 32


def kernel(x_prompt, x_sample, cache_swa_k, cache_swa_v, state_conv, rel_bias, norm_mix, norm_ffn, norm_final, attn_w_qkv, attn_w_o, attn_sinks, sgu_w_in, sgu_ln_g, sgu_ln_b, sgu_w_spatial, sgu_b_spatial, sgu_w_out, conv_w_in, conv_w_dw, conv_b_dw, conv_ln_g, conv_ln_b, conv_w_out, ffn_w_up, ffn_w_down):
    batch, seq, _ = x_prompt.shape
    dec = x_sample.shape[0]
    depth = norm_mix.shape[0]
    assert x_prompt.shape[2] == D_MODEL and x_sample.shape[1:] == (1, D_MODEL)
    assert seq % PROMPT_TILE == 0 and (batch * seq) % FFN_TILE == 0 and seq % SGU_TILE == 0
    assert dec % SAMPLE_ATTN_TILE == 0 and dec % SAMPLE_CONV_TILE == 0 and dec == WINDOW
    assert cache_swa_k.shape[2:] == (WINDOW, N_KV_HEADS, HEAD_DIM)
    assert state_conv.shape[2:] == (CONV_WIDTH - 1, D_MODEL)
    mixer_of_layer = tuple(i % 3 for i in range(depth))
    slot_of_layer = tuple(mixer_of_layer[:i].count(mixer_of_layer[i]) for i in range(depth))

    def row(v):
        return v.reshape(1, -1).astype(F32)

    n_attn = attn_w_qkv.shape[0]

    def caches_t(c):
        return jnp.transpose(c, (0, 1, 3, 4, 2)).reshape(n_attn, dec, KV_DIM, WINDOW)

    def caches_from_t(c):
        c = c.reshape(n_attn, dec, N_KV_HEADS, HEAD_DIM, WINDOW)
        return jnp.transpose(c, (0, 1, 4, 2, 3))

    xp = x_prompt
    xs = x_sample.reshape(dec, D_MODEL)
    dist_bias = _distance_bias(rel_bias)
    k_caches_t, v_caches_t = caches_t(cache_swa_k), caches_t(cache_swa_v)
    states = jnp.transpose(state_conv.astype(F32), (0, 2, 1, 3))
    w_qkv_all = attn_w_qkv.astype(BF16)
    w_o_all = attn_w_o.astype(BF16)

    kp, vp, sgu_v_new, convp, convs = [], [], [], [], []
    new_caches = None
    for i in range(depth):
        m, j = mixer_of_layer[i], slot_of_layer[i]
        g_mix = row(norm_mix[i])
        if m == 0:
            bias_p, cap_p, sink_rows = _prompt_bias_tables(dist_bias, attn_sinks[j])
            xp, k1, v1 = _attn_prompt(xp, g_mix, w_qkv_all, w_o_all, j, bias_p, cap_p, sink_rows,
                                      tq=PROMPT_TILE)
            kp.append(k1.reshape(batch, WINDOW, N_KV_HEADS, HEAD_DIM))
            vp.append(v1.reshape(batch, WINDOW, N_KV_HEADS, HEAD_DIM))

            bias_s, extra_s = _sample_bias_tables(dist_bias, attn_sinks[j])
            qkv_s, kv_t = _norm_proj(xs, g_mix, w_qkv_all, j, kv_t=True)
            o_s, *new_caches = _attn_sample_core(qkv_s, kv_t, k_caches_t, v_caches_t, bias_s,
                                                 extra_s, j, new_caches, tb=SAMPLE_ATTN_TILE)
            xs = _proj_res(xs, o_s, w_o_all, j, heads_grouped=True)
        elif m == 1:
            w_in = sgu_w_in[j].astype(BF16)
            w_out = sgu_w_out[j].astype(BF16)
            ln_g, ln_b = row(sgu_ln_g[j]), row(sgu_ln_b[j])
            sp = jnp.tril(sgu_w_spatial[j]).astype(BF16)
            bsp = jnp.broadcast_to(sgu_b_spatial[j].astype(F32)[:, :, None],
                                   (SGU_GROUPS, CHUNK, LANES))
            xp = _sgu(xp.reshape(batch * seq, D_MODEL), g_mix, w_in, ln_g, ln_b, sp, bsp, w_out,
                      tm=SGU_TILE, sample=False).reshape(batch, seq, D_MODEL)
            sp0 = row(jnp.repeat(sgu_w_spatial[j][:, 0, 0], LANES))
            bsp0 = row(jnp.repeat(sgu_b_spatial[j][:, 0], LANES))
            xs, v_rows = _sgu(xs, g_mix, w_in, ln_g, ln_b, sp0, bsp0, w_out, tm=dec, sample=True)
            sgu_v_new.append(v_rows.reshape(dec, 1, D_MODEL))
        else:
            w_in = conv_w_in[j].astype(BF16)
            w_out = conv_w_out[j].astype(BF16)
            w_dw = conv_w_dw[j].astype(F32)
            b_dw = row(conv_b_dw[j])
            ln_g, ln_b = row(conv_ln_g[j]), row(conv_ln_b[j])
            w_dw8 = jnp.broadcast_to(w_dw[:, None, :], (CONV_WIDTH, SUBLANES, D_MODEL))
            b_dw8 = jnp.broadcast_to(b_dw, (SUBLANES, D_MODEL))
            xp, tail = _conv_prompt(xp, g_mix, w_in, w_dw8, b_dw8, ln_g, ln_b, w_out,
                                    tm=PROMPT_TILE)
            convp.append(tail[:, CONV_HALO - (CONV_WIDTH - 1):, :])
            a_s = _norm_proj(xs, g_mix, w_in, glu=True)
            c_s, st2 = _conv_sample_core(a_s, states, w_dw, b_dw, j, tb=SAMPLE_CONV_TILE)
            xs = _proj_res(xs, c_s, w_out, ln=(ln_g, ln_b))
            convs.append(jnp.transpose(st2, (1, 0, 2)))

        g_fin = row(norm_final) if i == depth - 1 else None
        xp, xs = _ffn(xp.reshape(batch * seq, D_MODEL), xs, row(norm_ffn[i]), ffn_w_up, ffn_w_down,
                      i, g_fin, tm=FFN_TILE)
        xp = xp.reshape(batch, seq, D_MODEL)

    y_prompt = xp
    y_sample = xs.reshape(dec, 1, D_MODEL)
    k_new, v_new = (caches_from_t(c) for c in new_caches)
    return (y_prompt, y_sample, jnp.stack(kp), jnp.stack(vp), k_new, v_new,
            jnp.stack(sgu_v_new), jnp.stack(convp), jnp.stack(convs))
```

```python
import functools
import math

import jax
import jax.numpy as jnp
from jax import lax
from jax.experimental import pallas as pl
from jax.experimental.pallas import tpu as pltpu

D_MODEL = 1024
HEAD_DIM = 64
N_HEADS = 16
N_KV_HEADS = 4
Q_PER_KV = 4
KV_DIM = N_KV_HEADS * HEAD_DIM
WINDOW = 128
ATTN_SCALE = HEAD_DIM ** -0.5
N_BUCKETS = 32
MAX_DISTANCE = 128
CHUNK = 128
SGU_GROUPS = 8
CONV_WIDTH = 31
CONV_HALO = 32
D_FF = 4 * D_MODEL
EPS = 1e-6
NEG_INF = -1e30
F32_MAX = float(jnp.finfo(jnp.float32).max)
INV_SQRT2 = 1.0 / math.sqrt(2.0)

LANES = 128
SUBLANES = 8
VMEM_LIMIT = 56 * 1024 * 1024

F32 = jnp.float32
BF16 = jnp.bfloat16


def _const_spec(shape):
    n = len(shape)
    return pl.BlockSpec(shape, lambda *_: (0,) * n, pipeline_mode=pl.Buffered(1))


def _params(*sem):
    return pltpu.CompilerParams(dimension_semantics=sem, vmem_limit_bytes=VMEM_LIMIT)


def _rms(x, g):
    return x * lax.rsqrt(jnp.mean(x * x, axis=-1, keepdims=True) + EPS) * g


def _layer_norm(x, g, b):
    mu = jnp.mean(x, axis=-1, keepdims=True)
    xc = x - mu
    var = jnp.mean(xc * xc, axis=-1, keepdims=True)
    return xc * lax.rsqrt(var + EPS) * g + b


def _dot(a, b):
    return jnp.dot(a, b, preferred_element_type=F32)


def _dot_nt(a, b):
    return lax.dot_general(a, b, (((1,), (1,)), ((), ())), preferred_element_type=F32)


FFN_CHUNK = 512
N_FFN_CHUNKS = D_FF // FFN_CHUNK


def _layer_spec(shape, layer):
    zeros = (0,) * (len(shape) - 1)
    return pl.BlockSpec((None,) + tuple(shape[1:]), lambda *_: (layer,) + zeros,
                        pipeline_mode=pl.Buffered(1))


def _ffn_kernel(xp_ref, xs_ref, g_ref, wu_ref, wd_ref, *rest, final, n_prompt_tiles):
    *rest, wu_scr, wd_scr = rest
    if final:
        gf_ref, op_ref, os_ref = rest
    else:
        op_ref, os_ref = rest

    def mlp(x):
        h = _rms(x, g_ref[...]).astype(BF16)
        y = x
        for c in range(N_FFN_CHUNKS):
            u = _dot(h, wu_scr[c])
            u = jnp.square(jnp.maximum(u, 0.0)).astype(BF16)
            y = y + _dot(u, wd_scr[c])
        return _rms(y, gf_ref[...]) if final else y

    i = pl.program_id(0)
    tile = i - N_FFN_CHUNKS

    @pl.when(i < N_FFN_CHUNKS)
    def _():
        wu_scr[i] = wu_ref[...].astype(BF16)
        wd_scr[i] = wd_ref[...].astype(BF16)

    @pl.when(jnp.logical_and(tile >= 0, tile < n_prompt_tiles - 1))
    def _():
        op_ref[...] = mlp(xp_ref[...])

    @pl.when(tile == n_prompt_tiles - 1)
    def _():
        n_rows = xp_ref.shape[0]
        y = mlp(jnp.concatenate([xp_ref[...], xs_ref[...]], axis=0))
        op_ref[...] = y[:n_rows]
        os_ref[...] = y[n_rows:]


def _ffn(xp, xs, g, w_up, w_down, layer, g_final=None, *, tm):
    n = xp.shape[0]
    nt = n // tm
    last_chunk = N_FFN_CHUNKS - 1
    row = pl.BlockSpec((tm, D_MODEL), lambda i: (jnp.clip(i - N_FFN_CHUNKS, 0, nt - 1), 0))
    in_specs = [row, _const_spec(xs.shape), _const_spec((1, D_MODEL)),
                pl.BlockSpec((None, D_MODEL, FFN_CHUNK),
                             lambda i: (layer, 0, jnp.minimum(i, last_chunk))),
                pl.BlockSpec((None, FFN_CHUNK, D_MODEL),
                             lambda i: (layer, jnp.minimum(i, last_chunk), 0))]
    args = [xp, xs, g, w_up, w_down]
    if g_final is not None:
        in_specs.append(_const_spec((1, D_MODEL)))
        args.append(g_final)
    return pl.pallas_call(
        functools.partial(_ffn_kernel, final=g_final is not None, n_prompt_tiles=nt),
        out_shape=(jax.ShapeDtypeStruct((n, D_MODEL), F32),
                   jax.ShapeDtypeStruct(xs.shape, F32)),
        grid=(N_FFN_CHUNKS + nt,),
        in_specs=in_specs,
        out_specs=(row, pl.BlockSpec(xs.shape, lambda i: (0, 0))),
        scratch_shapes=[pltpu.VMEM((N_FFN_CHUNKS, D_MODEL, FFN_CHUNK), BF16),
                        pltpu.VMEM((N_FFN_CHUNKS, FFN_CHUNK, D_MODEL), BF16)],
        compiler_params=_params("arbitrary"),
        name="ffn_final" if g_final is not None else "ffn",
    )(*args)


def _regroup_heads(t, group_major):
    blocks = range(N_HEADS)
    if group_major:
        order = [N_KV_HEADS * (blk % N_KV_HEADS) + blk // N_KV_HEADS for blk in blocks]
    else:
        order = [Q_PER_KV * (blk % Q_PER_KV) + blk // Q_PER_KV for blk in blocks]
    return jnp.concatenate([t[:, h * HEAD_DIM:(h + 1) * HEAD_DIM] for h in order], axis=1)


def _norm_proj_kernel(x_ref, g_ref, w_ref, o_ref, *rest, glu, kv_t):
    h = _rms(x_ref[...], g_ref[...]).astype(BF16)
    y = _dot(h, w_ref[...])
    if glu:
        half = y.shape[1] // 2
        y = y[:, :half] * jax.nn.sigmoid(y[:, half:])
    if kv_t:
        (t_ref,) = rest
        o_ref[:, :D_MODEL] = _regroup_heads(y[:, :D_MODEL], group_major=True)
        o_ref[:, D_MODEL:] = y[:, D_MODEL:]
        t_ref[...] = y[:, D_MODEL:].T
    else:
        o_ref[...] = y


def _weight_spec(w, layer):
    return _const_spec(w.shape) if layer is None else _layer_spec(w.shape, layer)


def _norm_proj(x, g, w, layer=None, *, glu=False, kv_t=False):
    n = x.shape[0]
    n_out = w.shape[-1] // 2 if glu else w.shape[-1]
    out_shape = [jax.ShapeDtypeStruct((n, n_out), F32)]
    if kv_t:
        out_shape.append(jax.ShapeDtypeStruct((n_out - D_MODEL, n), F32))
    out = pl.pallas_call(
        functools.partial(_norm_proj_kernel, glu=glu, kv_t=kv_t),
        out_shape=tuple(out_shape),
        grid=(1,),
        in_specs=[_const_spec(x.shape), _const_spec(g.shape), _weight_spec(w, layer)],
        out_specs=tuple(_const_spec(o.shape) for o in out_shape),
        compiler_params=_params("arbitrary"),
        name="norm_proj_glu" if glu else "norm_proj",
    )(x, g, w)
    return out if kv_t else out[0]


def _proj_res_kernel(x_ref, a_ref, w_ref, *rest, conv_tail, heads_grouped):
    a = a_ref[...]
    if conv_tail:
        lg_ref, lb_ref, o_ref = rest
        a = _layer_norm(a, lg_ref[...], lb_ref[...])
        a = a * jax.nn.sigmoid(a)
    else:
        (o_ref,) = rest
    if heads_grouped:
        a = _regroup_heads(a, group_major=False)
    o_ref[...] = x_ref[...] + _dot(a.astype(BF16), w_ref[...])


def _proj_res(x, a, w, layer=None, ln=None, heads_grouped=False):
    args = [x, a, w] + (list(ln) if ln is not None else [])
    in_specs = [_const_spec(t.shape) for t in args]
    in_specs[2] = _weight_spec(w, layer)
    return pl.pallas_call(
        functools.partial(_proj_res_kernel, conv_tail=ln is not None, heads_grouped=heads_grouped),
        out_shape=jax.ShapeDtypeStruct(x.shape, F32),
        grid=(1,),
        in_specs=in_specs,
        out_specs=_const_spec(x.shape),
        compiler_params=_params("arbitrary"),
        name="proj_res_ln" if ln is not None else "proj_res",
    )(*args)


def _t5_bucket(dist):
    n = jnp.maximum(dist, 0)
    max_exact = N_BUCKETS // 2
    nf = jnp.maximum(n, 1).astype(F32)
    large = max_exact + (jnp.log(nf / max_exact) / math.log(MAX_DISTANCE / max_exact)
                         * (N_BUCKETS - max_exact)).astype(jnp.int32)
    large = jnp.minimum(large, N_BUCKETS - 1)
    return jnp.where(n < max_exact, n, large)


def _distance_bias(rel_bias):
    buckets = _t5_bucket(jnp.arange(WINDOW + 1, dtype=jnp.int32))
    onehot = (buckets[:, None] == jnp.arange(N_BUCKETS, dtype=jnp.int32)[None, :]).astype(F32)
    return jnp.dot(onehot, rel_bias.astype(F32), precision=lax.Precision.HIGHEST)


def _prompt_bias_tables(dist_bias, sinks):
    period = 3 * WINDOW
    line = jnp.concatenate([jnp.broadcast_to(dist_bias[WINDOW:], (WINDOW - 1, N_HEADS)),
                            dist_bias[::-1],
                            jnp.broadcast_to(dist_bias[:1], (WINDOW, N_HEADS))]).T
    skew = jnp.tile(line, (1, WINDOW))[:, :WINDOW * (period - 1)]
    per_head = skew.reshape(N_HEADS, WINDOW, period - 1)[:, :, WINDOW - 1:period - 1]
    per_head = per_head.reshape(N_KV_HEADS, 2, 2, WINDOW, 2 * WINDOW)
    bias = per_head.transpose(0, 2, 4, 1, 3).reshape(N_KV_HEADS, 4 * WINDOW, 2 * WINDOW)

    qi = jnp.arange(WINDOW, dtype=jnp.int32)[None, :]
    kj = jnp.arange(2 * WINDOW, dtype=jnp.int32)[:, None]
    dist = qi - kj + WINDOW
    allowed = (dist >= 0) & (dist <= WINDOW)
    first = allowed & (kj >= WINDOW)
    cap = jnp.stack([jnp.where(allowed, F32_MAX, NEG_INF), jnp.where(first, F32_MAX, NEG_INF)])
    cap = jnp.tile(cap.astype(F32), (1, 2, 2))

    sink_rows = sinks.astype(F32).reshape(N_KV_HEADS, 2, 2).transpose(0, 2, 1).reshape(2 * N_KV_HEADS, 2)
    sink_rows = jnp.repeat(sink_rows, WINDOW, axis=1)
    return bias, cap, sink_rows


def _attn_prompt_kernel(x_ref, xres_ref, g_ref, wqkv_ref, wo_ref, bias_ref, cap_ref, sink_ref,
                        o_ref, kc_ref, vc_ref,
                        q0_scr, q1_scr, klo0_scr, klo1_scr, khi0_scr, khi1_scr, vt0_scr, vt1_scr,
                        a0_scr, a1_scr, *, tq, tiles_per_seq):
    s = pl.program_id(0)
    sets = ((q0_scr, klo0_scr, khi0_scr, vt0_scr, a0_scr),
            (q1_scr, klo1_scr, khi1_scr, vt1_scr, a1_scr))

    @pl.when(s == 0)
    def _():
        for scr in sets[0] + sets[1]:
            scr[...] = jnp.zeros(scr.shape, BF16)

    low = lax.broadcasted_iota(jnp.int32, (tq, LANES), 1) < HEAD_DIM
    n_col = D_MODEL // KV_DIM

    def step(new, old):
        q_new, klo_new, khi_new, vt_new, att_done = new
        q_old, klo_old, khi_old, vt_old, att_out = old

        starts_sequence = s % tiles_per_seq == 0
        klo_new[:, 0:WINDOW, :] = jnp.where(starts_sequence, 0.0, klo_old[:, tq:tq + WINDOW, :])
        khi_new[:, 0:WINDOW, :] = jnp.where(starts_sequence, 0.0, khi_old[:, tq:tq + WINDOW, :])
        vt_new[:, :, 0:WINDOW] = jnp.where(starts_sequence, 0.0, vt_old[:, :, tq:tq + WINDOW])
        h = _rms(x_ref[...], g_ref[...]).astype(BF16)

        def project_q(c):
            cols = slice(c * KV_DIM, (c + 1) * KV_DIM)
            q_new[cols, :] = (_dot(h, wqkv_ref[:, cols]) * ATTN_SCALE).T.astype(BF16)

        def project_k():
            k = _dot(h, wqkv_ref[:, D_MODEL:D_MODEL + KV_DIM])
            kc_ref[...] = k[tq - WINDOW:, :]
            for c in range(KV_DIM // LANES):
                kc = k[:, c * LANES:(c + 1) * LANES]
                kr = pltpu.roll(kc, HEAD_DIM, axis=1)
                klo_new[2 * c, WINDOW:, :] = jnp.where(low, kc, 0.0).astype(BF16)
                khi_new[2 * c, WINDOW:, :] = jnp.where(low, 0.0, kr).astype(BF16)
                klo_new[2 * c + 1, WINDOW:, :] = jnp.where(low, kr, 0.0).astype(BF16)
                khi_new[2 * c + 1, WINDOW:, :] = jnp.where(low, 0.0, kc).astype(BF16)

        def project_v():
            v = _dot(h, wqkv_ref[:, D_MODEL + KV_DIM:])
            vc_ref[...] = v[tq - WINDOW:, :]
            for c in range(KV_DIM // LANES):
                vt = v[:, c * LANES:(c + 1) * LANES].T.astype(BF16)
                vt_new[2 * c, :, WINDOW:] = vt[:HEAD_DIM]
                vt_new[2 * c + 1, :, WINDOW:] = vt[HEAD_DIM:]

        def project_out(c):
            cols = slice(c * KV_DIM, (c + 1) * KV_DIM)
            o_ref[:, cols] = xres_ref[:, cols] + _dot(att_done[...], wo_ref[:, cols])

        is_first = jnp.where((s - 1) % tiles_per_seq == 0, 1, 0)

        def scores(jb, kh):
            rows = slice(jb * WINDOW, (jb + 1) * WINDOW)
            band = slice(jb * WINDOW, (jb + 2) * WINDOW)
            cap = cap_ref[is_first] if jb == 0 else cap_ref[0]
            qst_t = jnp.concatenate([q_old[(2 * kh) * LANES:(2 * kh + 1) * LANES, rows],
                                     q_old[(2 * kh + 1) * LANES:(2 * kh + 2) * LANES, rows]], axis=1)
            kst = jnp.concatenate([klo_old[kh, band, :], khi_old[kh, band, :]], axis=0)
            return jnp.minimum(_dot(kst, qst_t) + bias_ref[kh], cap)

        def attend(jb, kh, sc):
            rows = slice(jb * WINDOW, (jb + 1) * WINDOW)
            vt = vt_old[kh, :, jb * WINDOW:(jb + 2) * WINDOW]
            halves = []
            for half in range(2):
                sh = sc[half * 2 * WINDOW:(half + 1) * 2 * WINDOW]
                sink = sink_ref[2 * kh + half:2 * kh + half + 1, :]
                m = jnp.maximum(jnp.max(sh, axis=0, keepdims=True), sink)
                p = jnp.exp(sh - m)
                denom = jnp.sum(p, axis=0, keepdims=True) + jnp.exp(sink - m)
                halves.append(_dot(vt, p.astype(BF16)) * (1.0 / denom))
            o = jnp.concatenate(halves, axis=0).T.astype(BF16)
            att_out[rows, (2 * kh) * LANES:(2 * kh + 1) * LANES] = o[:WINDOW]
            att_out[rows, (2 * kh + 1) * LANES:(2 * kh + 2) * LANES] = o[WINDOW:]

        pieces = [functools.partial(project_q, c) for c in range(n_col)] + [project_k, project_v]
        pieces += [functools.partial(project_out, c) for c in range(n_col)]
        steps = [(jb, kh) for jb in range(tq // WINDOW) for kh in range(N_KV_HEADS)]
        piece_at = {(n * len(steps)) // len(pieces): piece for n, piece in enumerate(pieces)}
        assert len(piece_at) == len(pieces)
        s_next = scores(*steps[0])
        for n, st in enumerate(steps):
            s_cur = s_next
            if n + 1 < len(steps):
                s_next = scores(*steps[n + 1])
            if n in piece_at:
                piece_at[n]()
            attend(*st, s_cur)

    @pl.when(s % 2 == 0)
    def _():
        step(sets[0], sets[1])

    @pl.when(s % 2 == 1)
    def _():
        step(sets[1], sets[0])


def _attn_prompt(x, g, w_qkv, w_o, layer, bias, cap, sink_rows, *, tq):
    b, seq, _ = x.shape
    tiles_per_seq = seq // tq
    n = b * tiles_per_seq
    x2 = x.reshape(b * seq, D_MODEL)
    lagged = pl.BlockSpec((tq, D_MODEL), lambda s: (jnp.maximum(s - 2, 0), 0))
    cache = pl.BlockSpec((None, WINDOW, KV_DIM),
                         lambda s: (jnp.minimum(s, n - 1) // tiles_per_seq, 0, 0))
    q_scr = pltpu.VMEM((D_MODEL, tq), BF16)
    a_scr = pltpu.VMEM((tq, D_MODEL), BF16)
    k_scr = pltpu.VMEM((N_KV_HEADS, WINDOW + tq, LANES), BF16)
    vt_scr = pltpu.VMEM((N_KV_HEADS, HEAD_DIM, WINDOW + tq), BF16)
    out, kc, vc = pl.pallas_call(
        functools.partial(_attn_prompt_kernel, tq=tq, tiles_per_seq=tiles_per_seq),
        out_shape=(jax.ShapeDtypeStruct((b * seq, D_MODEL), F32),
                   jax.ShapeDtypeStruct((b, WINDOW, KV_DIM), F32),
                   jax.ShapeDtypeStruct((b, WINDOW, KV_DIM), F32)),
        grid=(n + 2,),
        in_specs=[pl.BlockSpec((tq, D_MODEL), lambda s: (jnp.minimum(s, n - 1), 0)), lagged,
                  _const_spec((1, D_MODEL)),
                  _layer_spec(w_qkv.shape, layer), _layer_spec(w_o.shape, layer),
                  _const_spec(bias.shape), _const_spec(cap.shape), _const_spec(sink_rows.shape)],
        out_specs=(lagged, cache, cache),
        scratch_shapes=[q_scr, q_scr, k_scr, k_scr, k_scr, k_scr, vt_scr, vt_scr, a_scr, a_scr],
        compiler_params=_params("arbitrary"),
        name="attn_prompt",
    )(x2, x2, g, w_qkv, w_o, bias, cap, sink_rows)
    return out.reshape(b, seq, D_MODEL), kc, vc


ROWS_PAD = SUBLANES * Q_PER_KV
SAMPLE_GROUP = 16


def _sample_bias_tables(dist_bias, sinks):
    def rows(per_head):
        t = per_head.reshape(N_KV_HEADS, Q_PER_KV, -1).transpose(1, 0, 2)
        return jnp.pad(t, ((0, 0), (0, SUBLANES - N_KV_HEADS), (0, 0))).reshape(ROWS_PAD, -1)

    bias = rows(dist_bias[:0:-1].T)
    extra = rows(jnp.stack([dist_bias[0], sinks.astype(F32)], axis=1))
    return bias, jnp.pad(extra, ((0, 0), (0, LANES - 2)))


def _attn_sample_kernel(q_ref, kn_ref, vn_ref, kvt_ref, kc_ref, vc_ref, bias_ref, extra_ref,
                        *rest, tb, n_aliased, out_slot):
    o_ref, ko_ref, vo_ref = rest[n_aliased:]
    if out_slot is not None:
        for out_ref in (ko_ref, vo_ref):
            for slot in range(out_ref.shape[0]):
                if slot != out_slot:
                    out_ref[slot] = jnp.zeros(out_ref.shape[1:], F32)
        ko_ref, vo_ref = ko_ref.at[out_slot], vo_ref.at[out_slot]
    t = pl.program_id(0)
    sub = lax.broadcasted_iota(jnp.int32, (SUBLANES, KV_DIM), 0)
    lane_head = lax.broadcasted_iota(jnp.int32, (SUBLANES, KV_DIM), 1) // HEAD_DIM
    own = jnp.logical_and(sub < N_KV_HEADS, lane_head == sub)
    newest = lax.broadcasted_iota(jnp.int32, (KV_DIM, WINDOW), 1) == WINDOW - 1
    bias = bias_ref[...]
    bias_new = extra_ref[:, 0:1]
    sink = extra_ref[:, 1:2]

    def scores(group):
        s_old, s_new = [], []
        for bb in group:
            q_rows = []
            for gq in range(Q_PER_KV):
                q_g = q_ref[bb:bb + 1, gq * KV_DIM:(gq + 1) * KV_DIM] * ATTN_SCALE
                q_rows.append(jnp.where(own, jnp.broadcast_to(q_g, (SUBLANES, KV_DIM)), 0.0))
            q_blk = jnp.concatenate(q_rows, axis=0)
            s_old.append(_dot(q_blk.astype(BF16), kc_ref[bb].astype(BF16)) + bias)
            s_new.append(jnp.sum(q_blk * kn_ref[bb:bb + 1, :], axis=1, keepdims=True) + bias_new)
        return jnp.concatenate(s_old, axis=0), jnp.concatenate(s_new, axis=0)

    def attend(group, s_old, s_new):
        sinks = jnp.concatenate([sink] * len(group), axis=0)
        m = jnp.maximum(jnp.max(s_old, axis=1, keepdims=True), jnp.maximum(s_new, sinks))
        p_old = jnp.exp(s_old - m)
        p_new = jnp.exp(s_new - m)
        denom = jnp.sum(p_old, axis=1, keepdims=True) + p_new + jnp.exp(sinks - m)
        p_old = p_old.astype(BF16)
        for gi, bb in enumerate(group):
            rows = slice(gi * ROWS_PAD, (gi + 1) * ROWS_PAD)
            o = _dot_nt(p_old[rows], vc_ref[bb].astype(BF16)) + p_new[rows] * vn_ref[bb:bb + 1, :]
            o = o / denom[rows]
            for gq in range(Q_PER_KV):
                o_rows = o[SUBLANES * gq:SUBLANES * (gq + 1)]
                o_g = jnp.sum(jnp.where(own, o_rows, 0.0), axis=0, keepdims=True)
                o_ref[bb:bb + 1, gq * KV_DIM:(gq + 1) * KV_DIM] = o_g

    def shift(bb):
        to_last = (WINDOW - 1) - (t * tb + bb)
        for cache_ref, row0, out_ref in ((kc_ref, 0, ko_ref), (vc_ref, KV_DIM, vo_ref)):
            moved = pltpu.roll(cache_ref[bb], WINDOW - 1, axis=1)
            col = pltpu.roll(kvt_ref[row0:row0 + KV_DIM, :], to_last, axis=1)
            out_ref[bb] = jnp.where(newest, col, moved)

    for bb in range(tb):
        shift(bb)
    groups = [tuple(range(g, g + SAMPLE_GROUP)) for g in range(0, tb, SAMPLE_GROUP)]
    s_next = scores(groups[0])
    for n, group in enumerate(groups):
        s_cur = s_next
        if n + 1 < len(groups):
            s_next = scores(groups[n + 1])
        attend(group, *s_cur)


def _attn_sample_core(qkv, kv_t, k_caches, v_caches, bias, extra, layer, new_caches=None, *, tb):
    n = qkv.shape[0]
    cache = pl.BlockSpec((None, tb, KV_DIM, WINDOW), lambda t: (layer, t, 0, 0))
    kv_col = D_MODEL // KV_DIM
    args = [qkv, qkv, qkv, kv_t, k_caches, v_caches, bias, extra]
    in_specs = [pl.BlockSpec((tb, D_MODEL), lambda t: (t, 0)),
                pl.BlockSpec((tb, KV_DIM), lambda t: (t, kv_col)),
                pl.BlockSpec((tb, KV_DIM), lambda t: (t, kv_col + 1)),
                _const_spec(kv_t.shape), cache, cache,
                _const_spec(bias.shape), _const_spec(extra.shape)]
    if new_caches is None:
        aliases, out_slot = {}, layer
        cache_out = pl.BlockSpec((k_caches.shape[0], tb, KV_DIM, WINDOW), lambda t: (0, t, 0, 0))
    else:
        aliases, out_slot = {len(args): 1, len(args) + 1: 2}, None
        cache_out = cache
        args += list(new_caches)
        in_specs += [pl.BlockSpec(memory_space=pl.ANY)] * 2
    return pl.pallas_call(
        functools.partial(_attn_sample_kernel, tb=tb, n_aliased=len(aliases), out_slot=out_slot),
        out_shape=(jax.ShapeDtypeStruct((n, D_MODEL), F32),
                   jax.ShapeDtypeStruct(k_caches.shape, F32),
                   jax.ShapeDtypeStruct(v_caches.shape, F32)),
        grid=(n // tb,),
        in_specs=in_specs,
        out_specs=(pl.BlockSpec((tb, D_MODEL), lambda t: (t, 0)), cache_out, cache_out),
        input_output_aliases=aliases,
        compiler_params=_params("parallel"),
        name="attn_sample",
    )(*args)


def _sgu_kernel(x_ref, g_ref, win_ref, lg_ref, lb_ref, sp_ref, bsp_ref, wout_ref, *rest,
                tm, sample):
    def gelu_and_norm(z):
        z = 0.5 * z * (1.0 + lax.erf(z * INV_SQRT2))
        return z[:, :D_MODEL], _layer_norm(z[:, D_MODEL:], lg_ref[...], lb_ref[...])

    if sample:
        o_ref, v_ref = rest
        x = x_ref[...]
        u, v = gelu_and_norm(_dot(_rms(x, g_ref[...]).astype(BF16), win_ref[...]))
        v_ref[...] = v
        gated = (u * (v * sp_ref[...] + bsp_ref[...])).astype(BF16)
        o_ref[...] = x + _dot(gated, wout_ref[...])
        return

    o_ref, gated_scr = rest
    x = x_ref[...]
    h = _rms(x, g_ref[...]).astype(BF16)

    def gelu(z):
        return 0.5 * z * (1.0 + lax.erf(z * INV_SQRT2))

    z_v = _dot(h, win_ref[:, D_MODEL:])
    z_u = _dot(h, win_ref[:, :D_MODEL])
    vb = _layer_norm(gelu(z_v), lg_ref[...], lb_ref[...]).astype(BF16)
    u = gelu(z_u)
    for c in range(0, tm // CHUNK, 2):
        pair = [slice((c + j) * CHUNK, (c + j + 1) * CHUNK) for j in range(2)]
        for gi in range(SGU_GROUPS):
            cols = slice(gi * LANES, (gi + 1) * LANES)
            both = jnp.concatenate([vb[chunk, cols] for chunk in pair], axis=1)
            mixed = _dot(sp_ref[gi], both)
            for j, chunk in enumerate(pair):
                m_j = mixed[:, j * LANES:(j + 1) * LANES] + bsp_ref[gi]
                gated_scr[chunk, cols] = (u[chunk, cols] * m_j).astype(BF16)
    o_ref[...] = x + _dot(gated_scr[...], wout_ref[...])


def _sgu(x, g, w_in, ln_g, ln_b, sp, bsp, w_out, *, tm, sample):
    n = x.shape[0]
    row = pl.BlockSpec((tm, D_MODEL), lambda i: (i, 0))
    args = [x, g, w_in, ln_g, ln_b, sp, bsp, w_out]
    in_specs = [row] + [_const_spec(t.shape) for t in args[1:]]
    if sample:
        out_shape = (jax.ShapeDtypeStruct((n, D_MODEL), F32),) * 2
        out_specs = (row, row)
        scratch = []
    else:
        out_shape = jax.ShapeDtypeStruct((n, D_MODEL), F32)
        out_specs = row
        scratch = [pltpu.VMEM((tm, D_MODEL), BF16)]
    return pl.pallas_call(
        functools.partial(_sgu_kernel, tm=tm, sample=sample),
        out_shape=out_shape,
        grid=(n // tm,),
        in_specs=in_specs,
        out_specs=out_specs,
        scratch_shapes=scratch,
        compiler_params=_params("parallel"),
        name="sgu_sample" if sample else "sgu_prompt",
    )(*args)


CONV_ROWS = 64
CONV_LANES = 256
PROJ_LANES = 256


def _conv_prompt_kernel(x_ref, xprev_ref, g_ref, win_ref, wdw_ref, bdw_ref, lg_ref, lb_ref,
                        wout_ref, o_ref, tail_ref, a0_scr, a1_scr, sh_scr, c_scr,
                        *, tm, tiles_per_seq):
    s = pl.program_id(0)

    @pl.when(s == 0)
    def _():
        a0_scr[...] = jnp.zeros(a0_scr.shape, F32)
        a1_scr[...] = jnp.zeros(a1_scr.shape, F32)

    first_tap = CONV_HALO - (CONV_WIDTH - 1)
    reps = CONV_ROWS // SUBLANES

    def step(a_new, a_old):
        for r in range(1, SUBLANES):
            sh_scr[r - 1] = a_old[r:r + tm + CONV_HALO - SUBLANES, :]

        h = _rms(x_ref[...], g_ref[...]).astype(BF16)
        starts_sequence = s % tiles_per_seq == 0
        a_new[0:CONV_HALO, :] = jnp.where(starts_sequence, 0.0, a_old[tm:tm + CONV_HALO, :])

        def project(jc):
            cols = slice(jc * PROJ_LANES, (jc + 1) * PROJ_LANES)
            gate = slice(D_MODEL + jc * PROJ_LANES, D_MODEL + (jc + 1) * PROJ_LANES)
            a = _dot(h, win_ref[:, cols]) * jax.nn.sigmoid(_dot(h, win_ref[:, gate]))
            a_new[CONV_HALO:, cols] = a
            tail_ref[:, cols] = a[tm - CONV_HALO:, :]

        def convolve(rb):
            for lc in range(D_MODEL // CONV_LANES):
                cols = slice(lc * CONV_LANES, (lc + 1) * CONV_LANES)
                acc = jnp.concatenate([bdw_ref[:, cols]] * reps, axis=0)
                for kk in range(CONV_WIDTH):
                    whole, r = divmod(first_tap + kk, SUBLANES)
                    start = rb * CONV_ROWS + whole * SUBLANES
                    src = a_old if r == 0 else sh_scr.at[r - 1]
                    w = jnp.concatenate([wdw_ref[kk, :, cols]] * reps, axis=0)
                    acc = acc + src[start:start + CONV_ROWS, cols] * w
                c_scr[rb * CONV_ROWS:(rb + 1) * CONV_ROWS, cols] = acc

        n_proj = D_MODEL // PROJ_LANES
        n_conv = tm // CONV_ROWS
        for jc in range(n_proj):
            project(jc)
            for rb in range(jc * n_conv // n_proj, (jc + 1) * n_conv // n_proj):
                convolve(rb)
        c = _layer_norm(c_scr[...], lg_ref[...], lb_ref[...])
        c = (c * jax.nn.sigmoid(c)).astype(BF16)
        o_ref[...] = xprev_ref[...] + _dot(c, wout_ref[...])

    @pl.when(s % 2 == 0)
    def _():
        step(a0_scr, a1_scr)

    @pl.when(s % 2 == 1)
    def _():
        step(a1_scr, a0_scr)


def _conv_prompt(x, g, w_in, w_dw8, b_dw8, ln_g, ln_b, w_out, *, tm):
    b, seq, _ = x.shape
    tiles_per_seq = seq // tm
    n = b * tiles_per_seq
    consts = [g, w_in, w_dw8, b_dw8, ln_g, ln_b, w_out]
    a_scr = pltpu.VMEM((CONV_HALO + tm, D_MODEL), F32)
    x2 = x.reshape(b * seq, D_MODEL)
    out, tail = pl.pallas_call(
        functools.partial(_conv_prompt_kernel, tm=tm, tiles_per_seq=tiles_per_seq),
        out_shape=(jax.ShapeDtypeStruct((b * seq, D_MODEL), F32),
                   jax.ShapeDtypeStruct((b, CONV_HALO, D_MODEL), F32)),
        grid=(n + 1,),
        in_specs=[pl.BlockSpec((tm, D_MODEL), lambda s: (jnp.minimum(s, n - 1), 0)),
                  pl.BlockSpec((tm, D_MODEL), lambda s: (jnp.maximum(s - 1, 0), 0))]
                 + [_const_spec(t.shape) for t in consts],
        out_specs=(pl.BlockSpec((tm, D_MODEL), lambda s: (jnp.maximum(s - 1, 0), 0)),
                   pl.BlockSpec((None, CONV_HALO, D_MODEL),
                                lambda s: (jnp.minimum(s, n - 1) // tiles_per_seq, 0, 0))),
        scratch_shapes=[a_scr, a_scr,
                        pltpu.VMEM((SUBLANES - 1, CONV_HALO + tm - SUBLANES, D_MODEL), F32),
                        pltpu.VMEM((tm, D_MODEL), F32)],
        compiler_params=_params("arbitrary"),
        name="conv_prompt",
    )(x2, x2, *consts)
    return out.reshape(b, seq, D_MODEL), tail


def _conv_sample_kernel(a_ref, st_ref, wdw_ref, bdw_ref, c_ref, so_ref):
    n_hist = CONV_WIDTH - 1
    a = a_ref[...]
    acc = a * wdw_ref[n_hist:CONV_WIDTH, :] + bdw_ref[...]
    for k in range(n_hist):
        acc = acc + st_ref[k] * wdw_ref[k:k + 1, :]
    c_ref[...] = acc
    so_ref[0:n_hist - 1] = st_ref[1:n_hist]
    so_ref[n_hist - 1] = a


def _conv_sample_core(a, states, w_dw, b_dw, layer, *, tb):
    n = a.shape[0]
    row = pl.BlockSpec((tb, D_MODEL), lambda t: (t, 0))
    st_in = pl.BlockSpec((None, CONV_WIDTH - 1, tb, D_MODEL), lambda t: (layer, 0, t, 0))
    st_out = pl.BlockSpec((CONV_WIDTH - 1, tb, D_MODEL), lambda t: (0, t, 0))
    return pl.pallas_call(
        _conv_sample_kernel,
        out_shape=(jax.ShapeDtypeStruct((n, D_MODEL), F32),
                   jax.ShapeDtypeStruct(states.shape[1:], F32)),
        grid=(n // tb,),
        in_specs=[row, st_in, _const_spec(w_dw.shape), _const_spec(b_dw.shape)],
        out_specs=(row, st_out),
        compiler_params=_params("parallel"),
        name="conv_sample",
    )(a, states, w_dw, b_dw)


PROMPT_TILE = 512
SGU_TILE = 1024
FFN_TILE = 1024
SAMPLE_ATTN_TILE = 16
SAMPLE_CONV_TILE = 32


def kernel(x_prompt, x_sample, cache_swa_k, cache_swa_v, state_conv, rel_bias, norm_mix, norm_ffn, norm_final, attn_w_qkv, attn_w_o, attn_sinks, sgu_w_in, sgu_ln_g, sgu_ln_b, sgu_w_spatial, sgu_b_spatial, sgu_w_out, conv_w_in, conv_w_dw, conv_b_dw, conv_ln_g, conv_ln_b, conv_w_out, ffn_w_up, ffn_w_down):
    batch, seq, _ = x_prompt.shape
    dec = x_sample.shape[0]
    depth = norm_mix.shape[0]
    assert x_prompt.shape[2] == D_MODEL and x_sample.shape[1:] == (1, D_MODEL)
    assert seq % PROMPT_TILE == 0 and (batch * seq) % FFN_TILE == 0 and seq % SGU_TILE == 0
    assert dec % SAMPLE_ATTN_TILE == 0 and dec % SAMPLE_CONV_TILE == 0 and dec == WINDOW
    assert cache_swa_k.shape[2:] == (WINDOW, N_KV_HEADS, HEAD_DIM)
    assert state_conv.shape[2:] == (CONV_WIDTH - 1, D_MODEL)
    mixer_of_layer = tuple(i % 3 for i in range(depth))
    slot_of_layer = tuple(mixer_of_layer[:i].count(mixer_of_layer[i]) for i in range(depth))

    def row(v):
        return v.reshape(1, -1).astype(F32)

    n_attn = attn_w_qkv.shape[0]

    def caches_t(c):
        return jnp.transpose(c, (0, 1, 3, 4, 2)).reshape(n_attn, dec, KV_DIM, WINDOW)

    def caches_from_t(c):
        c = c.reshape(n_attn, dec, N_KV_HEADS, HEAD_DIM, WINDOW)
        return jnp.transpose(c, (0, 1, 4, 2, 3))

    xp = x_prompt
    xs = x_sample.reshape(dec, D_MODEL)
    dist_bias = _distance_bias(rel_bias)
    k_caches_t, v_caches_t = caches_t(cache_swa_k), caches_t(cache_swa_v)
    states = jnp.transpose(state_conv.astype(F32), (0, 2, 1, 3))
    w_qkv_all = attn_w_qkv.astype(BF16)
    w_o_all = attn_w_o.astype(BF16)

    kp, vp, sgu_v_new, convp, convs = [], [], [], [], []
    new_caches = None
    for i in range(depth):
        m, j = mixer_of_layer[i], slot_of_layer[i]
        g_mix = row(norm_mix[i])
        if m == 0:
            bias_p, cap_p, sink_rows = _prompt_bias_tables(dist_bias, attn_sinks[j])
            xp, k1, v1 = _attn_prompt(xp, g_mix, w_qkv_all, w_o_all, j, bias_p, cap_p, sink_rows,
                                      tq=PROMPT_TILE)
            kp.append(k1.reshape(batch, WINDOW, N_KV_HEADS, HEAD_DIM))
            vp.append(v1.reshape(batch, WINDOW, N_KV_HEADS, HEAD_DIM))

            bias_s, extra_s = _sample_bias_tables(dist_bias, attn_sinks[j])
            qkv_s, kv_t = _norm_proj(xs, g_mix, w_qkv_all, j, kv_t=True)
            o_s, *new_caches = _attn_sample_core(qkv_s, kv_t, k_caches_t, v_caches_t, bias_s,
                                                 extra_s, j, new_caches, tb=SAMPLE_ATTN_TILE)
            xs = _proj_res(xs, o_s, w_o_all, j, heads_grouped=True)
        elif m == 1:
            w_in = sgu_w_in[j].astype(BF16)
            w_out = sgu_w_out[j].astype(BF16)
            ln_g, ln_b = row(sgu_ln_g[j]), row(sgu_ln_b[j])
            sp = jnp.tril(sgu_w_spatial[j]).astype(BF16)
            bsp = jnp.broadcast_to(sgu_b_spatial[j].astype(F32)[:, :, None],
                                   (SGU_GROUPS, CHUNK, LANES))
            xp = _sgu(xp.reshape(batch * seq, D_MODEL), g_mix, w_in, ln_g, ln_b, sp, bsp, w_out,
                      tm=SGU_TILE, sample=False).reshape(batch, seq, D_MODEL)
            sp0 = row(jnp.repeat(sgu_w_spatial[j][:, 0, 0], LANES))
            bsp0 = row(jnp.repeat(sgu_b_spatial[j][:, 0], LANES))
            xs, v_rows = _sgu(xs, g_mix, w_in, ln_g, ln_b, sp0, bsp0, w_out, tm=dec, sample=True)
            sgu_v_new.append(v_rows.reshape(dec, 1, D_MODEL))
        else:
            w_in = conv_w_in[j].astype(BF16)
            w_out = conv_w_out[j].astype(BF16)
            w_dw = conv_w_dw[j].astype(F32)
            b_dw = row(conv_b_dw[j])
            ln_g, ln_b = row(conv_ln_g[j]), row(conv_ln_b[j])
            w_dw8 = jnp.broadcast_to(w_dw[:, None, :], (CONV_WIDTH, SUBLANES, D_MODEL))
            b_dw8 = jnp.broadcast_to(b_dw, (SUBLANES, D_MODEL))
            xp, tail = _conv_prompt(xp, g_mix, w_in, w_dw8, b_dw8, ln_g, ln_b, w_out,
                                    tm=PROMPT_TILE)
            convp.append(tail[:, CONV_HALO - (CONV_WIDTH - 1):, :])
            a_s = _norm_proj(xs, g_mix, w_in, glu=True)
            c_s, st2 = _conv_sample_core(a_s, states, w_dw, b_dw, j, tb=SAMPLE_CONV_TILE)
            xs = _proj_res(xs, c_s, w_out, ln=(ln_g, ln_b))
            convs.append(jnp.transpose(st2, (1, 0, 2)))

        g_fin = row(norm_final) if i == depth - 1 else None
        xp, xs = _ffn(xp.reshape(batch * seq, D_MODEL), xs, row(norm_ffn[i]), ffn_w_up, ffn_w_down,
                      i, g_fin, tm=FFN_TILE)
        xp = xp.reshape(batch, seq, D_MODEL)

    y_prompt = xp
    y_sample = xs.reshape(dec, 1, D_MODEL)
    k_new, v_new = (caches_from_t(c) for c in new_caches)
    return (y_prompt, y_sample, jnp.stack(kp), jnp.stack(vp), k_new, v_new,
            jnp.stack(sgu_v_new), jnp.stack(convp), jnp.stack(convs))
```
